```python
import math
import jax, jax.numpy as jnp
from jax import lax
import numpy as np

D_MODEL = 1024
BATCH = 16
SEQ = 2048
DEPTH = 2

D_MIX = D_MODEL
C_CONV = D_MIX // 4
C_POOL = D_MIX // 4
C_ATTN = D_MIX - C_CONV - C_POOL
DIFF_HEADS = 4
DIFF_HEAD_DIM = C_ATTN // (2 * DIFF_HEADS)
CONV_WIDTH = 31
POOL_WINDOWS = (2, 4, 8, 16)
POOL_GROUP = C_POOL // len(POOL_WINDOWS)
N_IN = 2 * C_CONV + C_POOL + 3 * C_ATTN
N_EXPERT_GROUPS = 4
EXPERTS_PER_GROUP = 4
N_EXPERTS = N_EXPERT_GROUPS * EXPERTS_PER_GROUP
D_EXPERT = D_MODEL // 4
MOE_TOP_K = 2
ATTN_BLOCK = 128
NORM_EPS = 1e-6

kernel_name = "hymba_conv_pool_diffattn_hmoe"


def rms_norm(x, g):
    xf = x.astype(jnp.float32)
    y = xf * lax.rsqrt(jnp.mean(xf * xf, axis=-1, keepdims=True) + NORM_EPS)
    return (y * g.astype(jnp.float32)).astype(x.dtype)


def layer_norm(x, g, b):
    xf = x.astype(jnp.float32)
    mu = jnp.mean(xf, axis=-1, keepdims=True)
    xc = xf - mu
    y = xc * lax.rsqrt(jnp.mean(xc * xc, axis=-1, keepdims=True) + NORM_EPS)
    return (y * g.astype(jnp.float32) + b.astype(jnp.float32)).astype(x.dtype)


def alibi_slopes(n_heads):
    start = 2.0 ** (-8.0 / n_heads)
    return jnp.asarray(np.array([start ** (i + 1) for i in range(n_heads)], dtype=np.float32))


def conv_mixer(a_val, a_gate, w_dw, b_dw, ln_g, ln_b, w_pw, b_pw):
    u = a_val * jax.nn.sigmoid(a_gate)
    y = lax.conv_general_dilated(
        u, w_dw[:, None, :].astype(u.dtype), window_strides=(1,),
        padding=[(CONV_WIDTH - 1, 0)],
        dimension_numbers=("NWC", "WIO", "NWC"),
        feature_group_count=C_CONV) + b_dw
    y = jax.nn.silu(layer_norm(y, ln_g, ln_b))
    return y @ w_pw + b_pw


def pool_mixer(u, w_pool, scale):
    s_len = u.shape[1]
    uf = u.astype(jnp.float32)
    c0 = jnp.pad(lax.cumsum(uf, axis=1), ((0, 0), (1, 0), (0, 0)))
    t = jnp.arange(s_len)
    outs = []
    for gi, win in enumerate(POOL_WINDOWS):
        sl = slice(gi * POOL_GROUP, (gi + 1) * POOL_GROUP)
        lo = jnp.maximum(t + 1 - win, 0)
        window_sum = c0[:, 1:, sl] - c0[:, lo, sl]
        count = jnp.minimum(t + 1, win).astype(jnp.float32)[None, :, None]
        d = (window_sum / count - uf[:, :, sl]).astype(u.dtype)
        outs.append(jnp.einsum("bsc,cd->bsd", d, w_pool[gi]))
    return jnp.concatenate(outs, axis=-1) * scale


def diff_attention(q, k, v, lam, slopes):
    s_len = q.shape[1]
    q = q.transpose(0, 2, 3, 1, 4)
    k = k.transpose(0, 2, 3, 1, 4)
    v = v.transpose(0, 2, 1, 3)
    scale = DIFF_HEAD_DIM ** -0.5
    outs = []
    for blk in range(s_len // ATTN_BLOCK):
        q0 = blk * ATTN_BLOCK
        end = q0 + ATTN_BLOCK
        qb = q[:, :, :, q0:end]
        kb = k[:, :, :, :end]
        vb = v[:, :, :end]
        s = jnp.einsum("bhmqd,bhmkd->bhmqk", qb, kb,
                       preferred_element_type=jnp.float32) * scale
        tq = jnp.arange(q0, end)[:, None]
        tk = jnp.arange(end)[None, :]
        dist = (tq - tk).astype(jnp.float32)
        s = s - slopes[None, :, None, None, None] * dist
        s = jnp.where(tk <= tq, s, -jnp.inf)
        p = jax.nn.softmax(s, axis=-1)
        a = p[:, :, 0] - lam * p[:, :, 1]
        outs.append(jnp.einsum("bhqk,bhkd->bhqd", a.astype(vb.dtype), vb))
    o = jnp.concatenate(outs, axis=2)
    return o.transpose(0, 2, 1, 3)


def hier_moe(h, wg, bg, we, be, w_gate, w_up, w_down):
    b_, s_, d_ = h.shape
    t = h.reshape(-1, d_)
    pg = jax.nn.softmax((t @ wg).astype(jnp.float32) + bg.astype(jnp.float32), axis=-1)
    g_idx = jnp.argmax(pg, axis=-1)
    g_p = jnp.take_along_axis(pg, g_idx[:, None], axis=1)[:, 0]
    le = ((t @ we).astype(jnp.float32) + be.astype(jnp.float32)).reshape(-1, N_EXPERT_GROUPS, EXPERTS_PER_GROUP)
    le = jnp.take_along_axis(le, g_idx[:, None, None], axis=1)[:, 0]
    pe = jax.nn.softmax(le, axis=-1)
    top_p, top_i = lax.top_k(pe, MOE_TOP_K)
    top_p = top_p / jnp.sum(top_p, axis=-1, keepdims=True)
    weights = g_p[:, None] * top_p
    eid = g_idx[:, None] * EXPERTS_PER_GROUP + top_i
    gates = jnp.sum(jax.nn.one_hot(eid, N_EXPERTS, dtype=jnp.float32) * weights[..., None], axis=1)
    gates = gates.astype(t.dtype)
    y = jnp.zeros_like(t)
    for e in range(N_EXPERTS):
        act = jax.nn.silu(t @ w_gate[e]) * (t @ w_up[e]) * gates[:, e:e + 1]
        y = y + act @ w_down[e]
    return y.reshape(b_, s_, d_)


def setup_inputs(seed: int = 0) -> dict:
    key = jax.random.key(seed)
    ks = iter(jax.random.split(key, 40))
    L = DEPTH

    def nrm(shape, scale):
        return jax.random.normal(next(ks), shape, jnp.float32) * scale

    def gain(shape):
        return 1.0 + nrm(shape, 0.02)

    return {
        "x": nrm((BATCH, SEQ, D_MODEL), 1.0),
        "attn_norm_g": gain((L, D_MODEL)),
        "w_in": nrm((L, D_MODEL, N_IN), D_MODEL ** -0.5),
        "conv_w": nrm((L, CONV_WIDTH, C_CONV), CONV_WIDTH ** -0.5),
        "conv_b": nrm((L, C_CONV), 0.02),
        "conv_ln_g": gain((L, C_CONV)),
        "conv_ln_b": nrm((L, C_CONV), 0.02),
        "conv_pw_w": nrm((L, C_CONV, C_CONV), C_CONV ** -0.5),
        "conv_pw_b": nrm((L, C_CONV), 0.02),
        "pool_w": nrm((L, len(POOL_WINDOWS), POOL_GROUP, POOL_GROUP), POOL_GROUP ** -0.5),
        "pool_scale": gain((L, C_POOL)),
        "q_norm_g": gain((L, DIFF_HEAD_DIM)),
        "k_norm_g": gain((L, DIFF_HEAD_DIM)),
        "lambda_q1": nrm((L, DIFF_HEAD_DIM), 0.1),
        "lambda_k1": nrm((L, DIFF_HEAD_DIM), 0.1),
        "lambda_q2": nrm((L, DIFF_HEAD_DIM), 0.1),
        "lambda_k2": nrm((L, DIFF_HEAD_DIM), 0.1),
        "attn_sub_norm_g": gain((L, 2 * DIFF_HEAD_DIM)),
        "w_out": nrm((L, D_MIX, D_MODEL), D_MIX ** -0.5),
        "ffn_norm_g": gain((L, D_MODEL)),
        "router_g_w": nrm((L, D_MODEL, N_EXPERT_GROUPS), D_MODEL ** -0.5),
        "router_g_b": nrm((L, N_EXPERT_GROUPS), 0.01),
        "router_e_w": nrm((L, D_MODEL, N_EXPERTS), D_MODEL ** -0.5),
        "router_e_b": nrm((L, N_EXPERTS), 0.01),
        "w_gate": nrm((L, N_EXPERTS, D_MODEL, D_EXPERT), D_MODEL ** -0.5),
        "w_up": nrm((L, N_EXPERTS, D_MODEL, D_EXPERT), D_MODEL ** -0.5),
        "w_down": nrm((L, N_EXPERTS, D_EXPERT, D_MODEL), D_EXPERT ** -0.5),
    }


def reference(x, attn_norm_g, w_in, conv_w, conv_b, conv_ln_g, conv_ln_b, conv_pw_w, conv_pw_b,
              pool_w, pool_scale, q_norm_g, k_norm_g, lambda_q1, lambda_k1, lambda_q2, lambda_k2,
              attn_sub_norm_g, w_out, ffn_norm_g, router_g_w, router_g_b, router_e_w, router_e_b,
              w_gate, w_up, w_down):
    b_, s_, _ = x.shape
    slopes = alibi_slopes(DIFF_HEADS)
    splits = [C_CONV, 2 * C_CONV, 2 * C_CONV + C_POOL,
              2 * C_CONV + C_POOL + C_ATTN, 2 * C_CONV + C_POOL + 2 * C_ATTN]
    for l in range(DEPTH):
        h = rms_norm(x, attn_norm_g[l])
        z = h @ w_in[l]
        a_val, a_gate, p_in, q, k, v = jnp.split(z, splits, axis=-1)

        y_conv = conv_mixer(a_val, a_gate, conv_w[l], conv_b[l], conv_ln_g[l], conv_ln_b[l],
                            conv_pw_w[l], conv_pw_b[l])
        y_pool = pool_mixer(p_in, pool_w[l], pool_scale[l])

        q = rms_norm(q.reshape(b_, s_, DIFF_HEADS, 2, DIFF_HEAD_DIM), q_norm_g[l])
        k = rms_norm(k.reshape(b_, s_, DIFF_HEADS, 2, DIFF_HEAD_DIM), k_norm_g[l])
        v = v.reshape(b_, s_, DIFF_HEADS, 2 * DIFF_HEAD_DIM)
        lam_init = 0.8 - 0.6 * math.exp(-0.3 * l)
        lam = (jnp.exp(jnp.sum(lambda_q1[l].astype(jnp.float32) * lambda_k1[l].astype(jnp.float32)))
               - jnp.exp(jnp.sum(lambda_q2[l].astype(jnp.float32) * lambda_k2[l].astype(jnp.float32)))
               + lam_init)
        o = diff_attention(q, k, v, lam, slopes)
        o = rms_norm(o, attn_sub_norm_g[l]) * (1.0 - lam_init)
        y_attn = o.reshape(b_, s_, C_ATTN)

        mix = jnp.concatenate([y_conv, y_pool, y_attn], axis=-1)
        x = x + mix @ w_out[l]

        hf = rms_norm(x, ffn_norm_g[l])
        x = x + hier_moe(hf, router_g_w[l], router_g_b[l], router_e_w[l], router_e_b[l],
                         w_gate[l], w_up[l], w_down[l])
    return x
```

```python
import functools
import math

import jax
import jax.numpy as jnp
import numpy as np
from jax import lax
from jax.experimental import pallas as pl
from jax.experimental.pallas import tpu as pltpu

D_MODEL = 1024
C_CONV = 256
C_POOL = 256
C_ATTN = 512
N_HEADS = 4
HEAD_DIM = 64
V_DIM = 2 * HEAD_DIM
CONV_WIDTH = 31
POOL_WINDOWS = (2, 4, 8, 16)
POOL_GROUP = C_POOL // len(POOL_WINDOWS)
N_IN = 2 * C_CONV + C_POOL + 3 * C_ATTN
N_GROUPS = 4
PER_GROUP = 4
N_EXPERTS = N_GROUPS * PER_GROUP
D_EXPERT = 256
EPS = 1e-6

LANES = 128
VMEM_LIMIT = 48 * 1024 * 1024

TM_PROJ = 512
TS_MIX = 512
HALO = 32
TQ = 256
TM_MOE = 1024

F32 = jnp.float32
BF16 = jnp.bfloat16


def _params(sem):
    return pltpu.CompilerParams(dimension_semantics=sem, vmem_limit_bytes=VMEM_LIMIT)


def _rms(x, g):
    return x * lax.rsqrt(jnp.mean(x * x, axis=-1, keepdims=True) + EPS) * g


def _in_proj_kernel(x_ref, g_ref, w_ref, qg_ref, kg_ref,
                    conv_ref, pool_ref, q_ref, k_ref, v_ref):
    h = _rms(x_ref[...], g_ref[...]).astype(BF16)

    def proj(c0, width):
        return jnp.dot(h, w_ref[:, c0:c0 + width], preferred_element_type=F32)

    conv_ref[...] = proj(0, 2 * C_CONV).astype(BF16)
    pool_ref[...] = proj(2 * C_CONV, C_POOL).astype(BF16)

    lane = lax.broadcasted_iota(jnp.int32, (1, LANES), 1)
    first = lane < HEAD_DIM

    def qk_norm(z, gain_ref, out_ref, post_scale):
        gain = gain_ref[...]
        for hd in range(N_HEADS):
            blk = z[:, hd * LANES:(hd + 1) * LANES]
            sq = blk * blk
            s_all = jnp.sum(sq, axis=-1, keepdims=True)
            s_lo = jnp.sum(jnp.where(first, sq, 0.0), axis=-1, keepdims=True)
            r_lo = lax.rsqrt(s_lo * (1.0 / HEAD_DIM) + EPS)
            r_hi = lax.rsqrt((s_all - s_lo) * (1.0 / HEAD_DIM) + EPS)
            r = jnp.where(first, r_lo, r_hi)
            out_ref[:, hd * LANES:(hd + 1) * LANES] = (blk * r * gain * post_scale).astype(BF16)

    c_q = 2 * C_CONV + C_POOL
    qk_norm(proj(c_q, C_ATTN), qg_ref, q_ref, HEAD_DIM ** -0.5)
    qk_norm(proj(c_q + C_ATTN, C_ATTN), kg_ref, k_ref, 1.0)
    v_ref[...] = proj(c_q + 2 * C_ATTN, C_ATTN).astype(BF16)


def _in_proj(x2, g, w_bf, qg, kg):
    t = x2.shape[0]
    tm = TM_PROJ
    row = lambda i: (i, 0)
    fixed = lambda i: (0, 0)
    outs = [jax.ShapeDtypeStruct((t, 2 * C_CONV), BF16),
            jax.ShapeDtypeStruct((t, C_POOL), BF16),
            jax.ShapeDtypeStruct((t, C_ATTN), BF16),
            jax.ShapeDtypeStruct((t, C_ATTN), BF16),
            jax.ShapeDtypeStruct((t, C_ATTN), BF16)]
    return pl.pallas_call(
        _in_proj_kernel,
        grid=(t // tm,),
        in_specs=[pl.BlockSpec((tm, D_MODEL), row),
                  pl.BlockSpec((1, D_MODEL), fixed),
                  pl.BlockSpec((D_MODEL, N_IN), fixed),
                  pl.BlockSpec((1, LANES), fixed),
                  pl.BlockSpec((1, LANES), fixed)],
        out_specs=[pl.BlockSpec((tm, 2 * C_CONV), row),
                   pl.BlockSpec((tm, C_POOL), row),
                   pl.BlockSpec((tm, C_ATTN), row),
                   pl.BlockSpec((tm, C_ATTN), row),
                   pl.BlockSpec((tm, C_ATTN), row)],
        out_shape=outs,
        compiler_params=_params(("parallel",)),
        name="in_proj",
    )(x2, g, w_bf, qg, kg)


def _conv_kernel(z_ref, halo_ref, w_ref, b_ref, lng_ref, lnb_ref, pw_ref, pwb_ref,
                 o_ref, ext_ref):
    ts = z_ref.shape[1]

    def glu(z):
        z = z.astype(F32)
        return z[:, :C_CONV] * jax.nn.sigmoid(z[:, C_CONV:])

    halo = glu(halo_ref[0])
    ext_ref[0:HALO, :] = jnp.where(pl.program_id(1) == 0, 0.0, halo)
    ext_ref[HALO:, :] = glu(z_ref[0])

    base = HALO - (CONV_WIDTH - 1)
    rows = 64
    for c in range(ts // rows):
        acc = jnp.broadcast_to(b_ref[...], (rows, C_CONV))
        for j in range(CONV_WIDTH):
            r0 = c * rows + base + j
            acc = acc + ext_ref[r0:r0 + rows, :] * w_ref[j:j + 1, :]
        mu = jnp.mean(acc, axis=-1, keepdims=True)
        xc = acc - mu
        y = xc * lax.rsqrt(jnp.mean(xc * xc, axis=-1, keepdims=True) + EPS)
        y = y * lng_ref[...] + lnb_ref[...]
        y = y * jax.nn.sigmoid(y)
        out = jnp.dot(y.astype(BF16), pw_ref[...], preferred_element_type=F32) + pwb_ref[...]
        o_ref[0, c * rows:(c + 1) * rows, :] = out.astype(BF16)


def _conv_mixer(z_conv, w, b, ln_g, ln_b, pw_bf, pw_b):
    bsz, s, _ = z_conv.shape
    ts = TS_MIX
    per = ts // HALO
    fixed = lambda b_, i: (0, 0)
    return pl.pallas_call(
        _conv_kernel,
        grid=(bsz, s // ts),
        in_specs=[pl.BlockSpec((1, ts, 2 * C_CONV), lambda b_, i: (b_, i, 0)),
                  pl.BlockSpec((1, HALO, 2 * C_CONV),
                               lambda b_, i: (b_, jnp.maximum(i * per - 1, 0), 0)),
                  pl.BlockSpec((CONV_WIDTH, C_CONV), fixed),
                  pl.BlockSpec((1, C_CONV), fixed),
                  pl.BlockSpec((1, C_CONV), fixed),
                  pl.BlockSpec((1, C_CONV), fixed),
                  pl.BlockSpec((C_CONV, C_CONV), fixed),
                  pl.BlockSpec((1, C_CONV), fixed)],
        out_specs=pl.BlockSpec((1, ts, C_CONV), lambda b_, i: (b_, i, 0)),
        out_shape=jax.ShapeDtypeStruct((bsz, s, C_CONV), BF16),
        scratch_shapes=[pltpu.VMEM((ts + HALO, C_CONV), F32)],
        compiler_params=_params(("parallel", "parallel")),
        name="conv_mixer",
    )(z_conv, z_conv, w, b, ln_g, ln_b, pw_bf, pw_b)


def _pool_kernel(u_ref, halo_ref, w_ref, scale_ref, o_ref, ext_ref):
    ts = u_ref.shape[1]
    i = pl.program_id(1)
    ext_ref[0:HALO, :] = jnp.where(i == 0, 0.0, halo_ref[0].astype(F32))
    u = u_ref[0].astype(F32)
    ext_ref[HALO:, :] = u

    lane = lax.broadcasted_iota(jnp.int32, (1, C_POOL), 1)
    win = jnp.left_shift(2, lane // POOL_GROUP)
    acc = u
    for d in range(1, max(POOL_WINDOWS)):
        shifted = ext_ref[HALO - d:HALO - d + ts, :]
        acc = acc + jnp.where(d < win, shifted, 0.0)
    pos = i * ts + lax.broadcasted_iota(jnp.int32, (ts, 1), 0)
    count = jnp.minimum(pos + 1, win).astype(F32)
    diff = acc / count - u
    out = jnp.dot(diff.astype(BF16), w_ref[...], preferred_element_type=F32) * scale_ref[...]
    o_ref[0] = out.astype(BF16)


def _pool_mixer(p_in, w_blockdiag_bf, scale):
    bsz, s, _ = p_in.shape
    ts = TS_MIX
    per = ts // HALO
    fixed = lambda b_, i: (0, 0)
    return pl.pallas_call(
        _pool_kernel,
        grid=(bsz, s // ts),
        in_specs=[pl.BlockSpec((1, ts, C_POOL), lambda b_, i: (b_, i, 0)),
                  pl.BlockSpec((1, HALO, C_POOL),
                               lambda b_, i: (b_, jnp.maximum(i * per - 1, 0), 0)),
                  pl.BlockSpec((C_POOL, C_POOL), fixed),
                  pl.BlockSpec((1, C_POOL), fixed)],
        out_specs=pl.BlockSpec((1, ts, C_POOL), lambda b_, i: (b_, i, 0)),
        out_shape=jax.ShapeDtypeStruct((bsz, s, C_POOL), BF16),
        scratch_shapes=[pltpu.VMEM((ts + HALO, C_POOL), F32)],
        compiler_params=_params(("parallel", "parallel")),
        name="pool_mixer",
    )(p_in, p_in, w_blockdiag_bf, scale)


def _attn_kernel(slopes_ref, q_ref, k_ref, v_ref, lq1_ref, lk1_ref, lq2_ref, lk2_ref, sg_ref,
                 o_ref, m_ref, l_ref, acc_ref, *, lam_init):
    hd = pl.program_id(1)
    qi = pl.program_id(2)
    ki = pl.program_id(3)
    tq = q_ref.shape[1]
    tk = k_ref.shape[1]

    @pl.when(ki == 0)
    def _():
        m_ref[...] = jnp.full(m_ref.shape, -jnp.inf, F32)
        l_ref[...] = jnp.zeros(l_ref.shape, F32)
        acc_ref[...] = jnp.zeros(acc_ref.shape, F32)

    @pl.when(ki <= qi)
    def _():
        q = q_ref[0]
        k = k_ref[0]
        v = v_ref[0]
        tq_pos = qi * tq + lax.broadcasted_iota(jnp.int32, (tq, tk), 0)
        tk_pos = ki * tk + lax.broadcasted_iota(jnp.int32, (tq, tk), 1)
        dist = (tq_pos - tk_pos).astype(F32)
        bias = jnp.where(tk_pos <= tq_pos, -slopes_ref[hd] * dist, -jnp.inf)
        for m in range(2):
            sl = slice(m * HEAD_DIM, (m + 1) * HEAD_DIM)
            s = lax.dot_general(q[:, sl], k[:, sl], (((1,), (1,)), ((), ())),
                                preferred_element_type=F32) + bias
            m_prev = m_ref[m]
            m_new = jnp.maximum(m_prev, jnp.max(s, axis=-1, keepdims=True))
            alpha = jnp.exp(m_prev - m_new)
            p = jnp.exp(s - m_new)
            l_ref[m] = alpha * l_ref[m] + jnp.sum(p, axis=-1, keepdims=True)
            acc_ref[m] = alpha * acc_ref[m] + jnp.dot(p.astype(BF16), v,
                                                       preferred_element_type=F32)
            m_ref[m] = m_new

    @pl.when(ki == qi)
    def _():
        lam = (jnp.exp(jnp.sum(lq1_ref[...] * lk1_ref[...], axis=-1, keepdims=True))
               - jnp.exp(jnp.sum(lq2_ref[...] * lk2_ref[...], axis=-1, keepdims=True))
               + lam_init)
        o = acc_ref[0] / l_ref[0] - lam * (acc_ref[1] / l_ref[1])
        o = _rms(o, sg_ref[...]) * (1.0 - lam_init)
        o_ref[0] = o.astype(BF16)


def _attention(q, k, v, slopes, lq1, lk1, lq2, lk2, sub_g, lam_init):
    bsz, s, _ = q.shape
    tq = TQ
    nq = s // tq
    fixed = lambda b_, h, i, j: (0, 0)
    kv_map = lambda b_, h, i, j: (b_, jnp.minimum(i, j), h)
    q_map = lambda b_, h, i, j: (b_, i, h)
    vec = pl.BlockSpec((1, HEAD_DIM), fixed)
    return pl.pallas_call(
        functools.partial(_attn_kernel, lam_init=lam_init),
        grid=(bsz, N_HEADS, nq, nq),
        in_specs=[pl.BlockSpec(memory_space=pltpu.SMEM),
                  pl.BlockSpec((1, tq, LANES), q_map),
                  pl.BlockSpec((1, tq, LANES), kv_map),
                  pl.BlockSpec((1, tq, LANES), kv_map),
                  vec, vec, vec, vec,
                  pl.BlockSpec((1, V_DIM), fixed)],
        out_specs=pl.BlockSpec((1, tq, LANES), q_map),
        out_shape=jax.ShapeDtypeStruct((bsz, s, C_ATTN), BF16),
        scratch_shapes=[pltpu.VMEM((2, tq, 1), F32),
                        pltpu.VMEM((2, tq, 1), F32),
                        pltpu.VMEM((2, tq, V_DIM), F32)],
        compiler_params=_params(("parallel", "parallel", "parallel", "arbitrary")),
        name="diff_attention",
    )(slopes, q, k, v, lq1, lk1, lq2, lk2, sub_g)


def _split_bf16(a):
    hi = a.astype(BF16)
    lo = (a - hi.astype(F32)).astype(BF16)
    return hi, lo


def _out_proj_kernel(x_ref, yc_ref, yp_ref, ya_ref, w_ref, g_ref, rw_hi_ref, rw_lo_ref, rb_ref,
                     xo_ref, hf_ref, gates_ref):
    x = x_ref[...]
    x = x + jnp.dot(yc_ref[...], w_ref[0:C_CONV, :], preferred_element_type=F32)
    x = x + jnp.dot(yp_ref[...], w_ref[C_CONV:C_CONV + C_POOL, :], preferred_element_type=F32)
    x = x + jnp.dot(ya_ref[...], w_ref[C_CONV + C_POOL:, :], preferred_element_type=F32)
    xo_ref[...] = x

    hf = _rms(x, g_ref[...])
    hf_hi, hf_lo = _split_bf16(hf)
    hf_ref[...] = hf_hi

    logits = (jnp.dot(hf_hi, rw_hi_ref[...], preferred_element_type=F32)
              + jnp.dot(hf_lo, rw_hi_ref[...], preferred_element_type=F32)
              + jnp.dot(hf_hi, rw_lo_ref[...], preferred_element_type=F32)) + rb_ref[...]
    tm = logits.shape[0]
    lane = lax.broadcasted_iota(jnp.int32, (tm, LANES), 1)
    neg = -jnp.inf

    def first_argmax(val, vmax):
        return jnp.min(jnp.where(val == vmax, lane, LANES), axis=-1, keepdims=True)

    lg = jnp.where(lane < N_GROUPS, logits, neg)
    g_max = jnp.max(lg, axis=-1, keepdims=True)
    g_idx = first_argmax(lg, g_max)
    g_p = 1.0 / jnp.sum(jnp.exp(lg - g_max), axis=-1, keepdims=True)

    e_lane = lane - N_GROUPS
    in_group = (e_lane >= g_idx * PER_GROUP) & (e_lane < (g_idx + 1) * PER_GROUP)
    le = jnp.where(in_group, logits, neg)
    e_max = jnp.max(le, axis=-1, keepdims=True)
    pe = jnp.exp(le - e_max)
    pe = pe / jnp.sum(pe, axis=-1, keepdims=True)
    p1 = jnp.max(pe, axis=-1, keepdims=True)
    i1 = first_argmax(jnp.where(in_group, pe, neg), p1)
    pe2 = jnp.where(in_group & (lane != i1), pe, neg)
    p2 = jnp.max(pe2, axis=-1, keepdims=True)
    i2 = first_argmax(pe2, p2)
    denom = p1 + p2
    gates = jnp.where(lane == i1, g_p * (p1 / denom),
                      jnp.where(lane == i2, g_p * (p2 / denom), 0.0))
    gates_ref[...] = gates


def _out_proj(x2, yc, yp, ya, w_bf, g, rw_hi, rw_lo, rb):
    t = x2.shape[0]
    tm = TM_PROJ
    row = lambda i: (i, 0)
    fixed = lambda i: (0, 0)
    return pl.pallas_call(
        _out_proj_kernel,
        grid=(t // tm,),
        in_specs=[pl.BlockSpec((tm, D_MODEL), row),
                  pl.BlockSpec((tm, C_CONV), row),
                  pl.BlockSpec((tm, C_POOL), row),
                  pl.BlockSpec((tm, C_ATTN), row),
                  pl.BlockSpec((D_MODEL, D_MODEL), fixed),
                  pl.BlockSpec((1, D_MODEL), fixed),
                  pl.BlockSpec((D_MODEL, LANES), fixed),
                  pl.BlockSpec((D_MODEL, LANES), fixed),
                  pl.BlockSpec((1, LANES), fixed)],
        out_specs=[pl.BlockSpec((tm, D_MODEL), row),
                   pl.BlockSpec((tm, D_MODEL), row),
                   pl.BlockSpec((tm, LANES), row)],
        out_shape=[jax.ShapeDtypeStruct((t, D_MODEL), F32),
                   jax.ShapeDtypeStruct((t, D_MODEL), BF16),
                   jax.ShapeDtypeStruct((t, LANES), F32)],
        compiler_params=_params(("parallel",)),
        name="out_proj_router",
    )(x2, yc, yp, ya, w_bf, g, rw_hi, rw_lo, rb)


def _moe_kernel(x_ref, hf_ref, gates_ref, wg_ref, wu_ref, wd_ref, o_ref, acc_ref):
    e = pl.program_id(1)

    @pl.when(e == 0)
    def _():
        acc_ref[...] = x_ref[...]

    hf = hf_ref[...]
    lane = lax.broadcasted_iota(jnp.int32, gates_ref.shape, 1)
    gate = jnp.sum(jnp.where(lane == e + N_GROUPS, gates_ref[...], 0.0), axis=-1, keepdims=True)
    a = jnp.dot(hf, wg_ref[0], preferred_element_type=F32)
    u = jnp.dot(hf, wu_ref[0], preferred_element_type=F32)
    act = a * jax.nn.sigmoid(a) * u * gate
    acc_ref[...] += jnp.dot(act.astype(BF16), wd_ref[0], preferred_element_type=F32)

    @pl.when(e == N_EXPERTS - 1)
    def _():
        o_ref[...] = acc_ref[...]


def _moe(x2, hf, gates, wg_bf, wu_bf, wd_bf):
    t = x2.shape[0]
    tm = TM_MOE
    row = lambda i, e: (i, 0)
    return pl.pallas_call(
        _moe_kernel,
        grid=(t // tm, N_EXPERTS),
        in_specs=[pl.BlockSpec((tm, D_MODEL), row),
                  pl.BlockSpec((tm, D_MODEL), row),
                  pl.BlockSpec((tm, LANES), row),
                  pl.BlockSpec((1, D_MODEL, D_EXPERT), lambda i, e: (e, 0, 0)),
                  pl.BlockSpec((1, D_MODEL, D_EXPERT), lambda i, e: (e, 0, 0)),
                  pl.BlockSpec((1, D_EXPERT, D_MODEL), lambda i, e: (e, 0, 0))],
        out_specs=pl.BlockSpec((tm, D_MODEL), row),
        out_shape=jax.ShapeDtypeStruct((t, D_MODEL), F32),
        scratch_shapes=[pltpu.VMEM((tm, D_MODEL), F32)],
        compiler_params=_params(("parallel", "arbitrary")),
        name="moe",
    )(x2, hf, gates, wg_bf, wu_bf, wd_bf)


def _alibi_slopes():
    start = 2.0 ** (-8.0 / N_HEADS)
    return jnp.asarray(np.array([start ** (i + 1) for i in range(N_HEADS)], dtype=np.float32))


def _block_diag(w):
    n = w.shape[0]
    eye = jnp.eye(n, dtype=w.dtype)
    return jnp.einsum("gcd,gh->gchd", w, eye).reshape(n * POOL_GROUP, n * POOL_GROUP)


def kernel(x, attn_norm_g, w_in, conv_w, conv_b, conv_ln_g, conv_ln_b, conv_pw_w, conv_pw_b,
           pool_w, pool_scale, q_norm_g, k_norm_g, lambda_q1, lambda_k1, lambda_q2, lambda_k2,
           attn_sub_norm_g, w_out, ffn_norm_g, router_g_w, router_g_b, router_e_w, router_e_b,
           w_gate, w_up, w_down):
    bsz, s, d = x.shape
    depth = w_in.shape[0]
    t = bsz * s
    slopes = _alibi_slopes()
    row = lambda a: a.reshape(1, -1)

    x2 = x.reshape(t, d)
    for l in range(depth):
        lam_init = 0.8 - 0.6 * math.exp(-0.3 * l)
        z_conv, p_in, q, k, v = _in_proj(
            x2, row(attn_norm_g[l]), w_in[l].astype(BF16),
            row(jnp.tile(q_norm_g[l], 2)), row(jnp.tile(k_norm_g[l], 2)))

        y_conv = _conv_mixer(z_conv.reshape(bsz, s, -1), conv_w[l], row(conv_b[l]),
                             row(conv_ln_g[l]), row(conv_ln_b[l]),
                             conv_pw_w[l].astype(BF16), row(conv_pw_b[l]))
        y_pool = _pool_mixer(p_in.reshape(bsz, s, -1), _block_diag(pool_w[l]).astype(BF16),
                             row(pool_scale[l]))
        y_attn = _attention(q.reshape(bsz, s, -1), k.reshape(bsz, s, -1), v.reshape(bsz, s, -1),
                            slopes, row(lambda_q1[l]), row(lambda_k1[l]), row(lambda_q2[l]),
                            row(lambda_k2[l]), row(attn_sub_norm_g[l]), lam_init)

        pad = LANES - N_GROUPS - N_EXPERTS
        rw = jnp.pad(jnp.concatenate([router_g_w[l], router_e_w[l]], axis=1), ((0, 0), (0, pad)))
        rb = jnp.pad(jnp.concatenate([router_g_b[l], router_e_b[l]]), (0, pad))
        rw_hi = rw.astype(BF16)
        rw_lo = (rw - rw_hi.astype(F32)).astype(BF16)
        x2, hf, gates = _out_proj(
            x2, y_conv.reshape(t, -1), y_pool.reshape(t, -1), y_attn.reshape(t, -1),
            w_out[l].astype(BF16), row(ffn_norm_g[l]), rw_hi, rw_lo, row(rb))

        x2 = _moe(x2, hf, gates, w_gate[l].astype(BF16), w_up[l].astype(BF16),
                  w_down[l].astype(BF16))
    return x2.reshape(bsz, s, d)
```

```python
import functools
import math

import jax
import jax.numpy as jnp
import numpy as np
from jax import lax
from jax.experimental import pallas as pl
from jax.experimental.pallas import tpu as pltpu

D_MODEL = 1024
C_CONV = 256
C_POOL = 256
C_ATTN = 512
N_HEADS = 4
HEAD_DIM = 64
V_DIM = 2 * HEAD_DIM
CONV_WIDTH = 31
POOL_WINDOWS = (2, 4, 8, 16)
POOL_GROUP = C_POOL // len(POOL_WINDOWS)
N_IN = 2 * C_CONV + C_POOL + 3 * C_ATTN
N_GROUPS = 4
PER_GROUP = 4
N_EXPERTS = N_GROUPS * PER_GROUP
D_EXPERT = 256
EPS = 1e-6

LANES = 128
VMEM_LIMIT = 48 * 1024 * 1024

TM_PROJ = 512
TS_MIX = 512
HALO = 32
TQ = 512
TK = 256
TM_MOE = 1024

F32 = jnp.float32
BF16 = jnp.bfloat16


def _params(sem):
    return pltpu.CompilerParams(dimension_semantics=sem, vmem_limit_bytes=VMEM_LIMIT)


def _rms(x, g):
    return x * lax.rsqrt(jnp.mean(x * x, axis=-1, keepdims=True) + EPS) * g


def _in_proj_kernel(x_ref, g_ref, w_ref, qg_ref, kg_ref,
                    conv_ref, pool_ref, q_ref, k_ref, v_ref):
    h = _rms(x_ref[...], g_ref[...]).astype(BF16)

    def proj(c0, width):
        return jnp.dot(h, w_ref[:, c0:c0 + width], preferred_element_type=F32)

    conv_ref[...] = proj(0, 2 * C_CONV).astype(BF16)
    pool_ref[...] = proj(2 * C_CONV, C_POOL).astype(BF16)

    lane = lax.broadcasted_iota(jnp.int32, (1, LANES), 1)
    first = lane < HEAD_DIM

    def qk_norm(z, gain_ref, out_ref, post_scale):
        gain = gain_ref[...]
        for hd in range(N_HEADS):
            blk = z[:, hd * LANES:(hd + 1) * LANES]
            sq = blk * blk
            s_all = jnp.sum(sq, axis=-1, keepdims=True)
            s_lo = jnp.sum(jnp.where(first, sq, 0.0), axis=-1, keepdims=True)
            r_lo = lax.rsqrt(s_lo * (1.0 / HEAD_DIM) + EPS)
            r_hi = lax.rsqrt((s_all - s_lo) * (1.0 / HEAD_DIM) + EPS)
            r = jnp.where(first, r_lo, r_hi)
            out_ref[:, hd * LANES:(hd + 1) * LANES] = (blk * r * gain * post_scale).astype(BF16)

    c_q = 2 * C_CONV + C_POOL
    qk_norm(proj(c_q, C_ATTN), qg_ref, q_ref, HEAD_DIM ** -0.5 * math.log2(math.e))
    qk_norm(proj(c_q + C_ATTN, C_ATTN), kg_ref, k_ref, 1.0)
    v_ref[...] = proj(c_q + 2 * C_ATTN, C_ATTN).astype(BF16)


def _in_proj(x2, g, w_bf, qg, kg):
    t = x2.shape[0]
    tm = TM_PROJ
    row = lambda i: (i, 0)
    fixed = lambda i: (0, 0)
    outs = [jax.ShapeDtypeStruct((t, 2 * C_CONV), BF16),
            jax.ShapeDtypeStruct((t, C_POOL), BF16),
            jax.ShapeDtypeStruct((t, C_ATTN), BF16),
            jax.ShapeDtypeStruct((t, C_ATTN), BF16),
            jax.ShapeDtypeStruct((t, C_ATTN), BF16)]
    return pl.pallas_call(
        _in_proj_kernel,
        grid=(t // tm,),
        in_specs=[pl.BlockSpec((tm, D_MODEL), row),
                  pl.BlockSpec((1, D_MODEL), fixed),
                  pl.BlockSpec((D_MODEL, N_IN), fixed),
                  pl.BlockSpec((1, LANES), fixed),
                  pl.BlockSpec((1, LANES), fixed)],
        out_specs=[pl.BlockSpec((tm, 2 * C_CONV), row),
                   pl.BlockSpec((tm, C_POOL), row),
                   pl.BlockSpec((tm, C_ATTN), row),
                   pl.BlockSpec((tm, C_ATTN), row),
                   pl.BlockSpec((tm, C_ATTN), row)],
        out_shape=outs,
        compiler_params=_params(("parallel",)),
        name="in_proj",
    )(x2, g, w_bf, qg, kg)


def _conv_kernel(z_ref, halo_ref, w_ref, b_ref, lng_ref, lnb_ref, pw_ref, pwb_ref,
                 o_ref, ext_ref):
    ts = z_ref.shape[1]

    def glu(z):
        z = z.astype(F32)
        return z[:, :C_CONV] * jax.nn.sigmoid(z[:, C_CONV:])

    halo = glu(halo_ref[0])
    ext_ref[0:HALO, :] = jnp.where(pl.program_id(1) == 0, 0.0, halo)
    ext_ref[HALO:, :] = glu(z_ref[0])

    base = HALO - (CONV_WIDTH - 1)
    rows = 64
    for c in range(ts // rows):
        acc = jnp.broadcast_to(b_ref[...], (rows, C_CONV))
        for j in range(CONV_WIDTH):
            r0 = c * rows + base + j
            acc = acc + ext_ref[r0:r0 + rows, :] * w_ref[j:j + 1, :]
        mu = jnp.mean(acc, axis=-1, keepdims=True)
        xc = acc - mu
        y = xc * lax.rsqrt(jnp.mean(xc * xc, axis=-1, keepdims=True) + EPS)
        y = y * lng_ref[...] + lnb_ref[...]
        y = y * jax.nn.sigmoid(y)
        out = jnp.dot(y.astype(BF16), pw_ref[...], preferred_element_type=F32) + pwb_ref[...]
        o_ref[0, c * rows:(c + 1) * rows, :] = out.astype(BF16)


def _conv_mixer(z_conv, w, b, ln_g, ln_b, pw_bf, pw_b):
    bsz, s, _ = z_conv.shape
    ts = TS_MIX
    per = ts // HALO
    fixed = lambda b_, i: (0, 0)
    return pl.pallas_call(
        _conv_kernel,
        grid=(bsz, s // ts),
        in_specs=[pl.BlockSpec((1, ts, 2 * C_CONV), lambda b_, i: (b_, i, 0)),
                  pl.BlockSpec((1, HALO, 2 * C_CONV),
                               lambda b_, i: (b_, jnp.maximum(i * per - 1, 0), 0)),
                  pl.BlockSpec((CONV_WIDTH, C_CONV), fixed),
                  pl.BlockSpec((1, C_CONV), fixed),
                  pl.BlockSpec((1, C_CONV), fixed),
                  pl.BlockSpec((1, C_CONV), fixed),
                  pl.BlockSpec((C_CONV, C_CONV), fixed),
                  pl.BlockSpec((1, C_CONV), fixed)],
        out_specs=pl.BlockSpec((1, ts, C_CONV), lambda b_, i: (b_, i, 0)),
        out_shape=jax.ShapeDtypeStruct((bsz, s, C_CONV), BF16),
        scratch_shapes=[pltpu.VMEM((ts + HALO, C_CONV), F32)],
        compiler_params=_params(("parallel", "parallel")),
        name="conv_mixer",
    )(z_conv, z_conv, w, b, ln_g, ln_b, pw_bf, pw_b)


def _pool_kernel(u_ref, halo_ref, w_ref, scale_ref, o_ref, ext_ref):
    ts = u_ref.shape[1]
    i = pl.program_id(1)
    ext_ref[0:HALO, :] = jnp.where(i == 0, 0.0, halo_ref[0].astype(F32))
    u = u_ref[0].astype(F32)
    ext_ref[HALO:, :] = u

    lane = lax.broadcasted_iota(jnp.int32, (1, C_POOL), 1)
    win = jnp.left_shift(2, lane // POOL_GROUP)
    acc = u
    for d in range(1, max(POOL_WINDOWS)):
        shifted = ext_ref[HALO - d:HALO - d + ts, :]
        acc = acc + jnp.where(d < win, shifted, 0.0)
    pos = i * ts + lax.broadcasted_iota(jnp.int32, (ts, 1), 0)
    count = jnp.minimum(pos + 1, win).astype(F32)
    diff = acc / count - u
    out = jnp.dot(diff.astype(BF16), w_ref[...], preferred_element_type=F32) * scale_ref[...]
    o_ref[0] = out.astype(BF16)


def _pool_mixer(p_in, w_blockdiag_bf, scale):
    bsz, s, _ = p_in.shape
    ts = TS_MIX
    per = ts // HALO
    fixed = lambda b_, i: (0, 0)
    return pl.pallas_call(
        _pool_kernel,
        grid=(bsz, s // ts),
        in_specs=[pl.BlockSpec((1, ts, C_POOL), lambda b_, i: (b_, i, 0)),
                  pl.BlockSpec((1, HALO, C_POOL),
                               lambda b_, i: (b_, jnp.maximum(i * per - 1, 0), 0)),
                  pl.BlockSpec((C_POOL, C_POOL), fixed),
                  pl.BlockSpec((1, C_POOL), fixed)],
        out_specs=pl.BlockSpec((1, ts, C_POOL), lambda b_, i: (b_, i, 0)),
        out_shape=jax.ShapeDtypeStruct((bsz, s, C_POOL), BF16),
        scratch_shapes=[pltpu.VMEM((ts + HALO, C_POOL), F32)],
        compiler_params=_params(("parallel", "parallel")),
        name="pool_mixer",
    )(p_in, p_in, w_blockdiag_bf, scale)


def _attn_kernel(q_ref, k_ref, v_ref, qaug_ref, kaug_ref, lq1_ref, lk1_ref, lq2_ref, lk2_ref,
                 sg_ref, o_ref, kx_ref, q2_all, m_all, l_all, acc_all, *, lam_init):
    qi = pl.program_id(1)
    half = TQ // 2
    assert half == TK

    @pl.when(qi == 0)
    def _():
        for hd in range(N_HEADS):
            kx_ref[hd, :, 0:LANES] = k_ref[0, :, hd * LANES:(hd + 1) * LANES]
            kx_ref[hd, :, LANES:] = kaug_ref[hd]

    lam = (jnp.exp(jnp.sum(lq1_ref[...] * lk1_ref[...], axis=-1, keepdims=True))
           - jnp.exp(jnp.sum(lq2_ref[...] * lk2_ref[...], axis=-1, keepdims=True))
           + lam_init)
    first = lax.broadcasted_iota(jnp.int32, (1, LANES), 1) < HEAD_DIM
    row = lax.broadcasted_iota(jnp.int32, (2 * half, TK), 0)
    col = lax.broadcasted_iota(jnp.int32, (2 * half, TK), 1)
    tri = col <= (row & (half - 1))

    def step(hd, r0, nr, k0, width, masked_rows):
        q2_ref, m_ref, l_ref, acc_ref = q2_all.at[hd], m_all.at[hd], l_all.at[hd], acc_all.at[hd]
        kv_rows = pl.ds(pl.multiple_of(k0, TK), width)
        kb = kx_ref[hd, kv_rows, :]
        vb = v_ref[0, kv_rows, hd * LANES:(hd + 1) * LANES]
        s = lax.dot_general(q2_ref[r0:r0 + nr, :], kb, (((1,), (1,)), ((), ())),
                            preferred_element_type=F32)
        if masked_rows:
            s_m = jnp.where(tri, s[:masked_rows], -jnp.inf)
            s = s_m if masked_rows == nr else jnp.concatenate([s_m, s[masked_rows:]], axis=0)
        m_prev = m_ref[r0:r0 + nr]
        m_new = jnp.maximum(m_prev, jnp.max(s, axis=-1, keepdims=True))
        alpha = jnp.exp2(m_prev - m_new)
        p = jnp.exp2(s - jnp.concatenate([m_new] * (width // LANES), axis=1))
        l_ref[r0:r0 + nr] = alpha * l_ref[r0:r0 + nr] + jnp.sum(p, axis=-1, keepdims=True)
        acc_ref[r0:r0 + nr] = alpha * acc_ref[r0:r0 + nr] + jnp.dot(
            p.astype(BF16), vb, preferred_element_type=F32)
        m_ref[r0:r0 + nr] = m_new

    for hd in range(N_HEADS):
        q2_ref, m_ref, l_ref, acc_ref = q2_all.at[hd], m_all.at[hd], l_all.at[hd], acc_all.at[hd]
        qb = q_ref[0, :, hd * LANES:(hd + 1) * LANES]
        zero = jnp.zeros_like(qb)
        q_maps = (jnp.where(first, qb, zero), jnp.where(first, zero, qb))
        qa = qaug_ref[hd]
        for part in range(2):
            rows = slice(part * half, (part + 1) * half)
            for m in range(2):
                dst = slice((2 * part + m) * half, (2 * part + m + 1) * half)
                q2_ref[dst, 0:LANES] = q_maps[m][rows]
                q2_ref[dst, LANES:] = qa[rows]
        m_ref[...] = jnp.full(m_ref.shape, -jnp.inf, F32)
        l_ref[...] = jnp.zeros(l_ref.shape, F32)
        acc_ref[...] = jnp.zeros(acc_ref.shape, F32)

        def body(j, carry, hd=hd):
            step(hd, 0, 4 * half, j * TQ, TQ, 0)
            return carry

        lax.fori_loop(0, qi, body, 0)
        step(hd, 0, 4 * half, qi * TQ, TK, 2 * half)
        step(hd, 2 * half, 2 * half, qi * TQ + TK, TK, 2 * half)

        for part in range(2):
            r = 2 * part * half
            o = (acc_ref[r:r + half] / l_ref[r:r + half]
                 - lam * (acc_ref[r + half:r + 2 * half] / l_ref[r + half:r + 2 * half]))
            o = _rms(o, sg_ref[...]) * (1.0 - lam_init)
            o_ref[0, part * half:(part + 1) * half, hd * LANES:(hd + 1) * LANES] = o.astype(BF16)


def _alibi_aug(s):
    start = 2.0 ** (-8.0 / N_HEADS)
    slopes = np.array([start ** (i + 1) for i in range(N_HEADS)], dtype=np.float32)
    c = math.log2(math.e) * slopes.astype(np.float64)[:, None] * np.arange(s)[None, :]
    pieces = []
    rem = c
    for _ in range(3):
        piece = rem.astype(jnp.bfloat16)
        pieces.append(piece)
        rem = rem - piece.astype(np.float64)
    one = np.ones_like(pieces[0])
    zero = np.zeros((N_HEADS, s, LANES - 6), dtype=jnp.bfloat16)
    kaug = np.concatenate([np.stack(pieces + [one] * 3, axis=-1), zero], axis=-1)
    qaug = np.concatenate([np.stack([one] * 3 + [-p for p in pieces], axis=-1), zero], axis=-1)
    return jnp.asarray(qaug), jnp.asarray(kaug)


def _attention(q, k, v, qaug, kaug, lq1, lk1, lq2, lk2, sub_g, lam_init):
    bsz, s, _ = q.shape
    nq = s // TQ
    fixed2 = lambda b_, i: (0, 0)
    seq = lambda b_, i: (b_, 0, 0)
    vec = pl.BlockSpec((1, HEAD_DIM), fixed2)
    return pl.pallas_call(
        functools.partial(_attn_kernel, lam_init=lam_init),
        grid=(bsz, nq),
        in_specs=[pl.BlockSpec((1, TQ, C_ATTN), lambda b_, i: (b_, i, 0)),
                  pl.BlockSpec((1, s, C_ATTN), seq),
                  pl.BlockSpec((1, s, C_ATTN), seq),
                  pl.BlockSpec((N_HEADS, TQ, LANES), lambda b_, i: (0, i, 0)),
                  pl.BlockSpec((N_HEADS, s, LANES), lambda b_, i: (0, 0, 0)),
                  vec, vec, vec, vec,
                  pl.BlockSpec((1, V_DIM), fixed2)],
        out_specs=pl.BlockSpec((1, TQ, C_ATTN), lambda b_, i: (b_, i, 0)),
        out_shape=jax.ShapeDtypeStruct((bsz, s, C_ATTN), BF16),
        scratch_shapes=[pltpu.VMEM((N_HEADS, s, 2 * LANES), BF16),
                        pltpu.VMEM((N_HEADS, 2 * TQ, 2 * LANES), BF16),
                        pltpu.VMEM((N_HEADS, 2 * TQ, LANES), F32),
                        pltpu.VMEM((N_HEADS, 2 * TQ, LANES), F32),
                        pltpu.VMEM((N_HEADS, 2 * TQ, V_DIM), F32)],
        compiler_params=_params(("parallel", "arbitrary")),
        name="diff_attention",
    )(q, k, v, qaug, kaug, lq1, lk1, lq2, lk2, sub_g)


def _split_bf16(a):
    hi = a.astype(BF16)
    lo = (a - hi.astype(F32)).astype(BF16)
    return hi, lo


def _out_proj_kernel(x_ref, yc_ref, yp_ref, ya_ref, w_ref, g_ref, rw_hi_ref, rw_lo_ref, rb_ref,
                     xo_ref, hf_ref, gates_ref):
    x = x_ref[...]
    x = x + jnp.dot(yc_ref[...], w_ref[0:C_CONV, :], preferred_element_type=F32)
    x = x + jnp.dot(yp_ref[...], w_ref[C_CONV:C_CONV + C_POOL, :], preferred_element_type=F32)
    x = x + jnp.dot(ya_ref[...], w_ref[C_CONV + C_POOL:, :], preferred_element_type=F32)
    xo_ref[...] = x

    hf = _rms(x, g_ref[...])
    hf_hi, hf_lo = _split_bf16(hf)
    hf_ref[...] = hf_hi

    logits = (jnp.dot(hf_hi, rw_hi_ref[...], preferred_element_type=F32)
              + jnp.dot(hf_lo, rw_hi_ref[...], preferred_element_type=F32)
              + jnp.dot(hf_hi, rw_lo_ref[...], preferred_element_type=F32)) + rb_ref[...]
    tm = logits.shape[0]
    lane = lax.broadcasted_iota(jnp.int32, (tm, LANES), 1)
    neg = -jnp.inf

    def first_argmax(val, vmax):
        return jnp.min(jnp.where(val == vmax, lane, LANES), axis=-1, keepdims=True)

    lg = jnp.where(lane < N_GROUPS, logits, neg)
    g_max = jnp.max(lg, axis=-1, keepdims=True)
    g_idx = first_argmax(lg, g_max)
    g_p = 1.0 / jnp.sum(jnp.exp(lg - g_max), axis=-1, keepdims=True)

    e_lane = lane - N_GROUPS
    in_group = (e_lane >= g_idx * PER_GROUP) & (e_lane < (g_idx + 1) * PER_GROUP)
    le = jnp.where(in_group, logits, neg)
    e_max = jnp.max(le, axis=-1, keepdims=True)
    pe = jnp.exp(le - e_max)
    pe = pe / jnp.sum(pe, axis=-1, keepdims=True)
    p1 = jnp.max(pe, axis=-1, keepdims=True)
    i1 = first_argmax(jnp.where(in_group, pe, neg), p1)
    pe2 = jnp.where(in_group & (lane != i1), pe, neg)
    p2 = jnp.max(pe2, axis=-1, keepdims=True)
    i2 = first_argmax(pe2, p2)
    denom = p1 + p2
    gates = jnp.where(lane == i1, g_p * (p1 / denom),
                      jnp.where(lane == i2, g_p * (p2 / denom), 0.0))
    gates_ref[...] = gates


def _out_proj(x2, yc, yp, ya, w_bf, g, rw_hi, rw_lo, rb):
    t = x2.shape[0]
    tm = TM_PROJ
    row = lambda i: (i, 0)
    fixed = lambda i: (0, 0)
    return pl.pallas_call(
        _out_proj_kernel,
        grid=(t // tm,),
        in_specs=[pl.BlockSpec((tm, D_MODEL), row),
                  pl.BlockSpec((tm, C_CONV), row),
                  pl.BlockSpec((tm, C_POOL), row),
                  pl.BlockSpec((tm, C_ATTN), row),
                  pl.BlockSpec((D_MODEL, D_MODEL), fixed),
                  pl.BlockSpec((1, D_MODEL), fixed),
                  pl.BlockSpec((D_MODEL, LANES), fixed),
                  pl.BlockSpec((D_MODEL, LANES), fixed),
                  pl.BlockSpec((1, LANES), fixed)],
        out_specs=[pl.BlockSpec((tm, D_MODEL), row),
                   pl.BlockSpec((tm, D_MODEL), row),
                   pl.BlockSpec((tm, LANES), row)],
        out_shape=[jax.ShapeDtypeStruct((t, D_MODEL), F32),
                   jax.ShapeDtypeStruct((t, D_MODEL), BF16),
                   jax.ShapeDtypeStruct((t, LANES), F32)],
        compiler_params=_params(("parallel",)),
        name="out_proj_router",
    )(x2, yc, yp, ya, w_bf, g, rw_hi, rw_lo, rb)


def _moe_kernel(x_ref, hf_ref, gates_ref, wg_ref, wu_ref, wd_ref, o_ref, acc_ref):
    e = pl.program_id(1)

    @pl.when(e == 0)
    def _():
        acc_ref[...] = x_ref[...]

    hf = hf_ref[...]
    lane = lax.broadcasted_iota(jnp.int32, gates_ref.shape, 1)
    gate = jnp.sum(jnp.where(lane == e + N_GROUPS, gates_ref[...], 0.0), axis=-1, keepdims=True)
    a = jnp.dot(hf, wg_ref[0], preferred_element_type=F32)
    u = jnp.dot(hf, wu_ref[0], preferred_element_type=F32)
    act = a * jax.nn.sigmoid(a) * u * gate
    acc_ref[...] += jnp.dot(act.astype(BF16), wd_ref[0], preferred_element_type=F32)

    @pl.when(e == N_EXPERTS - 1)
    def _():
        o_ref[...] = acc_ref[...]


def _moe(x2, hf, gates, wg_bf, wu_bf, wd_bf):
    t = x2.shape[0]
    tm = TM_MOE
    row = lambda i, e: (i, 0)
    return pl.pallas_call(
        _moe_kernel,
        grid=(t // tm, N_EXPERTS),
        in_specs=[pl.BlockSpec((tm, D_MODEL), row),
                  pl.BlockSpec((tm, D_MODEL), row),
                  pl.BlockSpec((tm, LANES), row),
                  pl.BlockSpec((1, D_MODEL, D_EXPERT), lambda i, e: (e, 0, 0)),
                  pl.BlockSpec((1, D_MODEL, D_EXPERT), lambda i, e: (e, 0, 0)),
                  pl.BlockSpec((1, D_EXPERT, D_MODEL), lambda i, e: (e, 0, 0))],
        out_specs=pl.BlockSpec((tm, D_MODEL), row),
        out_shape=jax.ShapeDtypeStruct((t, D_MODEL), F32),
        scratch_shapes=[pltpu.VMEM((tm, D_MODEL), F32)],
        compiler_params=_params(("parallel", "arbitrary")),
        name="moe",
    )(x2, hf, gates, wg_bf, wu_bf, wd_bf)


def _block_diag(w):
    n = w.shape[0]
    eye = jnp.eye(n, dtype=w.dtype)
    return jnp.einsum("gcd,gh->gchd", w, eye).reshape(n * POOL_GROUP, n * POOL_GROUP)


def kernel(x, attn_norm_g, w_in, conv_w, conv_b, conv_ln_g, conv_ln_b, conv_pw_w, conv_pw_b,
           pool_w, pool_scale, q_norm_g, k_norm_g, lambda_q1, lambda_k1, lambda_q2, lambda_k2,
           attn_sub_norm_g, w_out, ffn_norm_g, router_g_w, router_g_b, router_e_w, router_e_b,
           w_gate, w_up, w_down):
    bsz, s, d = x.shape
    depth = w_in.shape[0]
    t = bsz * s
    qaug, kaug = _alibi_aug(s)
    row = lambda a: a.reshape(1, -1)

    x2 = x.reshape(t, d)
    for l in range(depth):
        lam_init = 0.8 - 0.6 * math.exp(-0.3 * l)
        z_conv, p_in, q, k, v = _in_proj(
            x2, row(attn_norm_g[l]), w_in[l].astype(BF16),
            row(jnp.tile(q_norm_g[l], 2)), row(jnp.tile(k_norm_g[l], 2)))

        y_conv = _conv_mixer(z_conv.reshape(bsz, s, -1), conv_w[l], row(conv_b[l]),
                             row(conv_ln_g[l]), row(conv_ln_b[l]),
                             conv_pw_w[l].astype(BF16), row(conv_pw_b[l]))
        y_pool = _pool_mixer(p_in.reshape(bsz, s, -1), _block_diag(pool_w[l]).astype(BF16),
                             row(pool_scale[l]))
        y_attn = _attention(q.reshape(bsz, s, -1), k.reshape(bsz, s, -1), v.reshape(bsz, s, -1),
                            qaug, kaug, row(lambda_q1[l]), row(lambda_k1[l]), row(lambda_q2[l]),
                            row(lambda_k2[l]), row(attn_sub_norm_g[l]), lam_init)

        pad = LANES - N_GROUPS - N_EXPERTS
        rw = jnp.pad(jnp.concatenate([router_g_w[l], router_e_w[l]], axis=1), ((0, 0), (0, pad)))
        rb = jnp.pad(jnp.concatenate([router_g_b[l], router_e_b[l]]), (0, pad))
        rw_hi = rw.astype(BF16)
        rw_lo = (rw - rw_hi.astype(F32)).astype(BF16)
        x2, hf, gates = _out_proj(
            x2, y_conv.reshape(t, -1), y_pool.reshape(t, -1), y_attn.reshape(t, -1),
            w_out[l].astype(BF16), row(ffn_norm_g[l]), rw_hi, rw_lo, row(rb))

        x2 = _moe(x2, hf, gates, w_gate[l].astype(BF16), w_up[l].astype(BF16),
                  w_down[l].astype(BF16))
    return x2.reshape(bsz, s, d)
```

```python
import functools
import math

import jax
import jax.numpy as jnp
import numpy as np
from jax import lax
from jax.experimental import pallas as pl
from jax.experimental.pallas import tpu as pltpu

D_MODEL = 1024
C_CONV = 256
C_POOL = 256
C_ATTN = 512
N_HEADS = 4
HEAD_DIM = 64
V_DIM = 2 * HEAD_DIM
CONV_WIDTH = 31
POOL_WINDOWS = (2, 4, 8, 16)
POOL_GROUP = C_POOL // len(POOL_WINDOWS)
N_IN = 2 * C_CONV + C_POOL + 3 * C_ATTN
N_GROUPS = 4
PER_GROUP = 4
N_EXPERTS = N_GROUPS * PER_GROUP
D_EXPERT = 256
EPS = 1e-6

LANES = 128
VMEM_LIMIT = 48 * 1024 * 1024

TM_PROJ = 512
TS_MIX = 512
HALO = 32
TQ = 512
TK = 256
TM_MOE = 256
N_PAIRS = 6
X_ROWS = D_MODEL // LANES
XM_ROWS = X_ROWS + 1
META_GATE_A, META_GATE_B, META_BUCKET = 0, 1, 2
DMA_UNROLL = 8

F32 = jnp.float32
BF16 = jnp.bfloat16


def _params(sem):
    return pltpu.CompilerParams(dimension_semantics=sem, vmem_limit_bytes=VMEM_LIMIT)


def _rms(x, g):
    return x * lax.rsqrt(jnp.mean(x * x, axis=-1, keepdims=True) + EPS) * g


def _in_proj_kernel(x_ref, g_ref, w_ref, qg_ref, kg_ref,
                    conv_ref, pool_ref, q_ref, k_ref, v_ref):
    h = _rms(x_ref[...], g_ref[...]).astype(BF16)

    def proj(c0, width):
        return jnp.dot(h, w_ref[:, c0:c0 + width], preferred_element_type=F32)

    conv_ref[...] = proj(0, 2 * C_CONV).astype(BF16)
    pool_ref[...] = proj(2 * C_CONV, C_POOL).astype(BF16)

    lane = lax.broadcasted_iota(jnp.int32, (1, LANES), 1)
    first = lane < HEAD_DIM

    def qk_norm(z, gain_ref, out_ref, post_scale):
        gain = gain_ref[...]
        for hd in range(N_HEADS):
            blk = z[:, hd * LANES:(hd + 1) * LANES]
            sq = blk * blk
            s_all = jnp.sum(sq, axis=-1, keepdims=True)
            s_lo = jnp.sum(jnp.where(first, sq, 0.0), axis=-1, keepdims=True)
            r_lo = lax.rsqrt(s_lo * (1.0 / HEAD_DIM) + EPS)
            r_hi = lax.rsqrt((s_all - s_lo) * (1.0 / HEAD_DIM) + EPS)
            r = jnp.where(first, r_lo, r_hi)
            out_ref[:, hd * LANES:(hd + 1) * LANES] = (blk * r * gain * post_scale).astype(BF16)

    c_q = 2 * C_CONV + C_POOL
    qk_norm(proj(c_q, C_ATTN), qg_ref, q_ref, HEAD_DIM ** -0.5 * math.log2(math.e))
    qk_norm(proj(c_q + C_ATTN, C_ATTN), kg_ref, k_ref, 1.0)
    v_ref[...] = proj(c_q + 2 * C_ATTN, C_ATTN).astype(BF16)


def _in_proj(x2, g, w_bf, qg, kg):
    t = x2.shape[0]
    tm = TM_PROJ
    row = lambda i: (i, 0)
    fixed = lambda i: (0, 0)
    outs = [jax.ShapeDtypeStruct((t, 2 * C_CONV), BF16),
            jax.ShapeDtypeStruct((t, C_POOL), BF16),
            jax.ShapeDtypeStruct((t, C_ATTN), BF16),
            jax.ShapeDtypeStruct((t, C_ATTN), BF16),
            jax.ShapeDtypeStruct((t, C_ATTN), BF16)]
    return pl.pallas_call(
        _in_proj_kernel,
        grid=(t // tm,),
        in_specs=[pl.BlockSpec((tm, D_MODEL), row),
                  pl.BlockSpec((1, D_MODEL), fixed),
                  pl.BlockSpec((D_MODEL, N_IN), fixed),
                  pl.BlockSpec((1, LANES), fixed),
                  pl.BlockSpec((1, LANES), fixed)],
        out_specs=[pl.BlockSpec((tm, 2 * C_CONV), row),
                   pl.BlockSpec((tm, C_POOL), row),
                   pl.BlockSpec((tm, C_ATTN), row),
                   pl.BlockSpec((tm, C_ATTN), row),
                   pl.BlockSpec((tm, C_ATTN), row)],
        out_shape=outs,
        compiler_params=_params(("parallel",)),
        name="in_proj",
    )(x2, g, w_bf, qg, kg)


def _conv_kernel(z_ref, halo_ref, w_ref, b_ref, lng_ref, lnb_ref, pw_ref, pwb_ref,
                 o_ref, ext_ref):
    ts = z_ref.shape[1]

    def glu(z):
        z = z.astype(F32)
        return z[:, :C_CONV] * jax.nn.sigmoid(z[:, C_CONV:])

    halo = glu(halo_ref[0])
    ext_ref[0:HALO, :] = jnp.where(pl.program_id(1) == 0, 0.0, halo)
    ext_ref[HALO:, :] = glu(z_ref[0])

    base = HALO - (CONV_WIDTH - 1)
    rows = 64
    for c in range(ts // rows):
        acc = jnp.broadcast_to(b_ref[...], (rows, C_CONV))
        for j in range(CONV_WIDTH):
            r0 = c * rows + base + j
            acc = acc + ext_ref[r0:r0 + rows, :] * w_ref[j:j + 1, :]
        mu = jnp.mean(acc, axis=-1, keepdims=True)
        xc = acc - mu
        y = xc * lax.rsqrt(jnp.mean(xc * xc, axis=-1, keepdims=True) + EPS)
        y = y * lng_ref[...] + lnb_ref[...]
        y = y * jax.nn.sigmoid(y)
        out = jnp.dot(y.astype(BF16), pw_ref[...], preferred_element_type=F32) + pwb_ref[...]
        o_ref[0, c * rows:(c + 1) * rows, :] = out.astype(BF16)


def _conv_mixer(z_conv, w, b, ln_g, ln_b, pw_bf, pw_b):
    bsz, s, _ = z_conv.shape
    ts = TS_MIX
    per = ts // HALO
    fixed = lambda b_, i: (0, 0)
    return pl.pallas_call(
        _conv_kernel,
        grid=(bsz, s // ts),
        in_specs=[pl.BlockSpec((1, ts, 2 * C_CONV), lambda b_, i: (b_, i, 0)),
                  pl.BlockSpec((1, HALO, 2 * C_CONV),
                               lambda b_, i: (b_, jnp.maximum(i * per - 1, 0), 0)),
                  pl.BlockSpec((CONV_WIDTH, C_CONV), fixed),
                  pl.BlockSpec((1, C_CONV), fixed),
                  pl.BlockSpec((1, C_CONV), fixed),
                  pl.BlockSpec((1, C_CONV), fixed),
                  pl.BlockSpec((C_CONV, C_CONV), fixed),
                  pl.BlockSpec((1, C_CONV), fixed)],
        out_specs=pl.BlockSpec((1, ts, C_CONV), lambda b_, i: (b_, i, 0)),
        out_shape=jax.ShapeDtypeStruct((bsz, s, C_CONV), BF16),
        scratch_shapes=[pltpu.VMEM((ts + HALO, C_CONV), F32)],
        compiler_params=_params(("parallel", "parallel")),
        name="conv_mixer",
    )(z_conv, z_conv, w, b, ln_g, ln_b, pw_bf, pw_b)


def _pool_kernel(u_ref, halo_ref, w_ref, scale_ref, o_ref, ext_ref):
    ts = u_ref.shape[1]
    i = pl.program_id(1)
    ext_ref[0:HALO, :] = jnp.where(i == 0, 0.0, halo_ref[0].astype(F32))
    u = u_ref[0].astype(F32)
    ext_ref[HALO:, :] = u

    lane = lax.broadcasted_iota(jnp.int32, (1, C_POOL), 1)
    win = jnp.left_shift(2, lane // POOL_GROUP)
    acc = u
    for d in range(1, max(POOL_WINDOWS)):
        shifted = ext_ref[HALO - d:HALO - d + ts, :]
        acc = acc + jnp.where(d < win, shifted, 0.0)
    pos = i * ts + lax.broadcasted_iota(jnp.int32, (ts, 1), 0)
    count = jnp.minimum(pos + 1, win).astype(F32)
    diff = acc / count - u
    out = jnp.dot(diff.astype(BF16), w_ref[...], preferred_element_type=F32) * scale_ref[...]
    o_ref[0] = out.astype(BF16)


def _pool_mixer(p_in, w_blockdiag_bf, scale):
    bsz, s, _ = p_in.shape
    ts = TS_MIX
    per = ts // HALO
    fixed = lambda b_, i: (0, 0)
    return pl.pallas_call(
        _pool_kernel,
        grid=(bsz, s // ts),
        in_specs=[pl.BlockSpec((1, ts, C_POOL), lambda b_, i: (b_, i, 0)),
                  pl.BlockSpec((1, HALO, C_POOL),
                               lambda b_, i: (b_, jnp.maximum(i * per - 1, 0), 0)),
                  pl.BlockSpec((C_POOL, C_POOL), fixed),
                  pl.BlockSpec((1, C_POOL), fixed)],
        out_specs=pl.BlockSpec((1, ts, C_POOL), lambda b_, i: (b_, i, 0)),
        out_shape=jax.ShapeDtypeStruct((bsz, s, C_POOL), BF16),
        scratch_shapes=[pltpu.VMEM((ts + HALO, C_POOL), F32)],
        compiler_params=_params(("parallel", "parallel")),
        name="pool_mixer",
    )(p_in, p_in, w_blockdiag_bf, scale)


def _attn_kernel(q_ref, k_ref, v_ref, qaug_ref, kaug_ref, lq1_ref, lk1_ref, lq2_ref, lk2_ref,
                 sg_ref, o_ref, kx_ref, q2_all, m_all, l_all, acc_all, *, lam_init):
    qi = pl.program_id(1)
    half = TQ // 2
    assert half == TK

    @pl.when(qi == 0)
    def _():
        for hd in range(N_HEADS):
            kx_ref[hd, :, 0:LANES] = k_ref[0, :, hd * LANES:(hd + 1) * LANES]
            kx_ref[hd, :, LANES:] = kaug_ref[hd]

    lam = (jnp.exp(jnp.sum(lq1_ref[...] * lk1_ref[...], axis=-1, keepdims=True))
           - jnp.exp(jnp.sum(lq2_ref[...] * lk2_ref[...], axis=-1, keepdims=True))
           + lam_init)
    first = lax.broadcasted_iota(jnp.int32, (1, LANES), 1) < HEAD_DIM
    row = lax.broadcasted_iota(jnp.int32, (2 * half, TK), 0)
    col = lax.broadcasted_iota(jnp.int32, (2 * half, TK), 1)
    tri = col <= (row & (half - 1))

    def step(hd, r0, nr, k0, width, masked_rows):
        q2_ref, m_ref, l_ref, acc_ref = q2_all.at[hd], m_all.at[hd], l_all.at[hd], acc_all.at[hd]
        kv_rows = pl.ds(pl.multiple_of(k0, TK), width)
        kb = kx_ref[hd, kv_rows, :]
        vb = v_ref[0, kv_rows, hd * LANES:(hd + 1) * LANES]
        s = lax.dot_general(q2_ref[r0:r0 + nr, :], kb, (((1,), (1,)), ((), ())),
                            preferred_element_type=F32)
        if masked_rows:
            s_m = jnp.where(tri, s[:masked_rows], -jnp.inf)
            s = s_m if masked_rows == nr else jnp.concatenate([s_m, s[masked_rows:]], axis=0)
        m_prev = m_ref[r0:r0 + nr]
        m_new = jnp.maximum(m_prev, jnp.max(s, axis=-1, keepdims=True))
        alpha = jnp.exp2(m_prev - m_new)
        p = jnp.exp2(s - jnp.concatenate([m_new] * (width // LANES), axis=1))
        l_ref[r0:r0 + nr] = alpha * l_ref[r0:r0 + nr] + jnp.sum(p, axis=-1, keepdims=True)
        acc_ref[r0:r0 + nr] = alpha * acc_ref[r0:r0 + nr] + jnp.dot(
            p.astype(BF16), vb, preferred_element_type=F32)
        m_ref[r0:r0 + nr] = m_new

    for hd in range(N_HEADS):
        q2_ref, m_ref, l_ref, acc_ref = q2_all.at[hd], m_all.at[hd], l_all.at[hd], acc_all.at[hd]
        qb = q_ref[0, :, hd * LANES:(hd + 1) * LANES]
        zero = jnp.zeros_like(qb)
        q_maps = (jnp.where(first, qb, zero), jnp.where(first, zero, qb))
        qa = qaug_ref[hd]
        for part in range(2):
            rows = slice(part * half, (part + 1) * half)
            for m in range(2):
                dst = slice((2 * part + m) * half, (2 * part + m + 1) * half)
                q2_ref[dst, 0:LANES] = q_maps[m][rows]
                q2_ref[dst, LANES:] = qa[rows]
        m_ref[...] = jnp.full(m_ref.shape, -jnp.inf, F32)
        l_ref[...] = jnp.zeros(l_ref.shape, F32)
        acc_ref[...] = jnp.zeros(acc_ref.shape, F32)

        def body(j, carry, hd=hd):
            step(hd, 0, 4 * half, j * TQ, TQ, 0)
            return carry

        lax.fori_loop(0, qi, body, 0)
        step(hd, 0, 4 * half, qi * TQ, TK, 2 * half)
        step(hd, 2 * half, 2 * half, qi * TQ + TK, TK, 2 * half)

        for part in range(2):
            r = 2 * part * half
            o = (acc_ref[r:r + half] / l_ref[r:r + half]
                 - lam * (acc_ref[r + half:r + 2 * half] / l_ref[r + half:r + 2 * half]))
            o = _rms(o, sg_ref[...]) * (1.0 - lam_init)
            o_ref[0, part * half:(part + 1) * half, hd * LANES:(hd + 1) * LANES] = o.astype(BF16)


def _alibi_aug(s):
    start = 2.0 ** (-8.0 / N_HEADS)
    slopes = np.array([start ** (i + 1) for i in range(N_HEADS)], dtype=np.float32)
    c = math.log2(math.e) * slopes.astype(np.float64)[:, None] * np.arange(s)[None, :]
    pieces = []
    rem = c
    for _ in range(3):
        piece = rem.astype(jnp.bfloat16)
        pieces.append(piece)
        rem = rem - piece.astype(np.float64)
    one = np.ones_like(pieces[0])
    zero = np.zeros((N_HEADS, s, LANES - 6), dtype=jnp.bfloat16)
    kaug = np.concatenate([np.stack(pieces + [one] * 3, axis=-1), zero], axis=-1)
    qaug = np.concatenate([np.stack([one] * 3 + [-p for p in pieces], axis=-1), zero], axis=-1)
    return jnp.asarray(qaug), jnp.asarray(kaug)


def _attention(q, k, v, qaug, kaug, lq1, lk1, lq2, lk2, sub_g, lam_init):
    bsz, s, _ = q.shape
    nq = s // TQ
    fixed2 = lambda b_, i: (0, 0)
    seq = lambda b_, i: (b_, 0, 0)
    vec = pl.BlockSpec((1, HEAD_DIM), fixed2)
    return pl.pallas_call(
        functools.partial(_attn_kernel, lam_init=lam_init),
        grid=(bsz, nq),
        in_specs=[pl.BlockSpec((1, TQ, C_ATTN), lambda b_, i: (b_, i, 0)),
                  pl.BlockSpec((1, s, C_ATTN), seq),
                  pl.BlockSpec((1, s, C_ATTN), seq),
                  pl.BlockSpec((N_HEADS, TQ, LANES), lambda b_, i: (0, i, 0)),
                  pl.BlockSpec((N_HEADS, s, LANES), lambda b_, i: (0, 0, 0)),
                  vec, vec, vec, vec,
                  pl.BlockSpec((1, V_DIM), fixed2)],
        out_specs=pl.BlockSpec((1, TQ, C_ATTN), lambda b_, i: (b_, i, 0)),
        out_shape=jax.ShapeDtypeStruct((bsz, s, C_ATTN), BF16),
        scratch_shapes=[pltpu.VMEM((N_HEADS, s, 2 * LANES), BF16),
                        pltpu.VMEM((N_HEADS, 2 * TQ, 2 * LANES), BF16),
                        pltpu.VMEM((N_HEADS, 2 * TQ, LANES), F32),
                        pltpu.VMEM((N_HEADS, 2 * TQ, LANES), F32),
                        pltpu.VMEM((N_HEADS, 2 * TQ, V_DIM), F32)],
        compiler_params=_params(("parallel", "arbitrary")),
        name="diff_attention",
    )(q, k, v, qaug, kaug, lq1, lk1, lq2, lk2, sub_g)


def _split_bf16(a):
    hi = a.astype(BF16)
    lo = (a - hi.astype(F32)).astype(BF16)
    return hi, lo


def _out_proj_kernel(x_ref, yc_ref, yp_ref, ya_ref, w_ref, g_ref, rw_hi_ref, rw_lo_ref, rb_ref,
                     xm_ref):
    x = x_ref[...]
    x = x + jnp.dot(yc_ref[...], w_ref[0:C_CONV, :], preferred_element_type=F32)
    x = x + jnp.dot(yp_ref[...], w_ref[C_CONV:C_CONV + C_POOL, :], preferred_element_type=F32)
    x = x + jnp.dot(ya_ref[...], w_ref[C_CONV + C_POOL:, :], preferred_element_type=F32)
    tm = x.shape[0]
    for c in range(X_ROWS):
        xm_ref[pl.ds(c, tm, stride=XM_ROWS), :] = x[:, c * LANES:(c + 1) * LANES]

    hf_hi, hf_lo = _split_bf16(_rms(x, g_ref[...]))

    logits = (jnp.dot(hf_hi, rw_hi_ref[...], preferred_element_type=F32)
              + jnp.dot(hf_lo, rw_hi_ref[...], preferred_element_type=F32)
              + jnp.dot(hf_hi, rw_lo_ref[...], preferred_element_type=F32)) + rb_ref[...]
    lane = lax.broadcasted_iota(jnp.int32, (tm, LANES), 1)
    neg = -jnp.inf

    def first_argmax(val, vmax):
        return jnp.min(jnp.where(val == vmax, lane, LANES), axis=-1, keepdims=True)

    lg = jnp.where(lane < N_GROUPS, logits, neg)
    g_max = jnp.max(lg, axis=-1, keepdims=True)
    g_idx = first_argmax(lg, g_max)
    g_p = 1.0 / jnp.sum(jnp.exp(lg - g_max), axis=-1, keepdims=True)

    e_lane = lane - N_GROUPS
    in_group = (e_lane >= g_idx * PER_GROUP) & (e_lane < (g_idx + 1) * PER_GROUP)
    le = jnp.where(in_group, logits, neg)
    e_max = jnp.max(le, axis=-1, keepdims=True)
    pe = jnp.exp(le - e_max)
    pe = pe / jnp.sum(pe, axis=-1, keepdims=True)
    p1 = jnp.max(pe, axis=-1, keepdims=True)
    i1 = first_argmax(jnp.where(in_group, pe, neg), p1)
    pe2 = jnp.where(in_group & (lane != i1), pe, neg)
    p2 = jnp.max(pe2, axis=-1, keepdims=True)
    i2 = first_argmax(pe2, p2)
    denom = p1 + p2
    gate1 = g_p * (p1 / denom)
    gate2 = g_p * (p2 / denom)

    loc1 = i1 - N_GROUPS - g_idx * PER_GROUP
    loc2 = i2 - N_GROUPS - g_idx * PER_GROUP
    a = jnp.minimum(loc1, loc2)
    b = jnp.maximum(loc1, loc2)
    pair = jnp.where(a == 0, 0, jnp.where(a == 1, 3, 5)) + (b - a - 1)
    bucket = (g_idx * N_PAIRS + pair).astype(F32)
    gate_a = jnp.where(loc1 < loc2, gate1, gate2)
    gate_b = jnp.where(loc1 < loc2, gate2, gate1)
    xm_ref[pl.ds(X_ROWS, tm, stride=XM_ROWS), :] = jnp.where(
        lane == META_GATE_A, gate_a,
        jnp.where(lane == META_GATE_B, gate_b, jnp.where(lane == META_BUCKET, bucket, 0.0)))


def _out_proj(x2, yc, yp, ya, w_bf, g, rw_hi, rw_lo, rb):
    t = x2.shape[0]
    tm = TM_PROJ
    row = lambda i: (i, 0)
    fixed = lambda i: (0, 0)
    return pl.pallas_call(
        _out_proj_kernel,
        grid=(t // tm,),
        in_specs=[pl.BlockSpec((tm, D_MODEL), row),
                  pl.BlockSpec((tm, C_CONV), row),
                  pl.BlockSpec((tm, C_POOL), row),
                  pl.BlockSpec((tm, C_ATTN), row),
                  pl.BlockSpec((D_MODEL, D_MODEL), fixed),
                  pl.BlockSpec((1, D_MODEL), fixed),
                  pl.BlockSpec((D_MODEL, LANES), fixed),
                  pl.BlockSpec((D_MODEL, LANES), fixed),
                  pl.BlockSpec((1, LANES), fixed)],
        out_specs=pl.BlockSpec((tm * XM_ROWS, LANES), row),
        out_shape=jax.ShapeDtypeStruct((t * XM_ROWS, LANES), F32),
        compiler_params=_params(("parallel",)),
        name="out_proj_router",
    )(x2, yc, yp, ya, w_bf, g, rw_hi, rw_lo, rb)


def _moe_kernel(src_ref, te1_ref, te2_ref, nv_ref,
                xm_hbm, g_ref, wg1_ref, wu1_ref, wd1_ref, wg2_ref, wu2_ref, wd2_ref,
                out_hbm, xbuf, ybuf, gsem, ssem):
    i = pl.program_id(0)
    n = pl.num_programs(0)
    tm = TM_MOE
    slot = i % 2

    def gather_copy(buf_slot, r, tok):
        return pltpu.make_async_copy(xm_hbm.at[pl.ds(tok * XM_ROWS, XM_ROWS)],
                                     xbuf.at[buf_slot, pl.ds(r * XM_ROWS, XM_ROWS)],
                                     gsem.at[buf_slot])

    def scatter_copy(r, tok):
        return pltpu.make_async_copy(ybuf.at[pl.ds(r, 1)], out_hbm.at[pl.ds(tok, 1)], ssem.at[0])

    def for_all_rows(fn):
        def body(blk, carry):
            for u in range(DMA_UNROLL):
                fn(blk * DMA_UNROLL + u)
            return carry
        lax.fori_loop(0, tm // DMA_UNROLL, body, 0)

    def for_valid_rows(tile, fn):
        def body(r, carry):
            fn(r)
            return carry
        lax.fori_loop(0, nv_ref[tile], body, 0)

    def start_gather(tile, buf_slot):
        for_all_rows(lambda r: gather_copy(buf_slot, r, src_ref[tile * tm + r]).start())

    def wait_gather(buf_slot):
        pltpu.make_async_copy(xm_hbm.at[pl.ds(0, tm * XM_ROWS)], xbuf.at[buf_slot],
                              gsem.at[buf_slot]).wait()

    def start_scatter(tile):
        @pl.when(nv_ref[tile] == tm)
        def _():
            for_all_rows(lambda r: scatter_copy(r, src_ref[tile * tm + r]).start())

        @pl.when(nv_ref[tile] < tm)
        def _():
            for_valid_rows(tile, lambda r: scatter_copy(r, src_ref[tile * tm + r]).start())

    def wait_scatter(tile):
        @pl.when(nv_ref[tile] == tm)
        def _():
            pltpu.make_async_copy(ybuf, out_hbm.at[pl.ds(0, tm)], ssem.at[0]).wait()

        @pl.when(nv_ref[tile] < tm)
        def _():
            for_valid_rows(tile, lambda r: scatter_copy(r, 0).wait())

    @pl.when(i == 0)
    def _():
        start_gather(0, 0)

    @pl.when(jnp.logical_and(i + 1 < n, nv_ref[jnp.minimum(i + 1, n - 1)] > 0))
    def _():
        start_gather(i + 1, 1 - slot)

    @pl.when(i > 0)
    def _():
        wait_scatter(i - 1)

    @pl.when(nv_ref[i] > 0)
    def _():
        wait_gather(slot)
        xs = xbuf.at[slot]
        x = jnp.concatenate([xs[pl.ds(c, tm, stride=XM_ROWS), :] for c in range(X_ROWS)], axis=1)
        meta = xs[pl.ds(X_ROWS, tm, stride=XM_ROWS), :]
        hf = _rms(x, g_ref[...]).astype(BF16)

        def expert(wg_ref, wu_ref, wd_ref, gate):
            a = jnp.dot(hf, wg_ref[0], preferred_element_type=F32)
            u = jnp.dot(hf, wu_ref[0], preferred_element_type=F32)
            act = a * jax.nn.sigmoid(a) * u * gate
            return jnp.dot(act.astype(BF16), wd_ref[0], preferred_element_type=F32)

        y = expert(wg1_ref, wu1_ref, wd1_ref, meta[:, META_GATE_A:META_GATE_A + 1])
        y = y + expert(wg2_ref, wu2_ref, wd2_ref, meta[:, META_GATE_B:META_GATE_B + 1])
        ybuf[...] = x + y
        start_scatter(i)

    @pl.when(i == n - 1)
    def _():
        wait_scatter(i)


def _moe(xm, g, src, te1, te2, nv, wg_bf, wu_bf, wd_bf):
    t = xm.shape[0] // XM_ROWS
    tm = TM_MOE
    n_tiles = nv.shape[0]
    fixed = lambda i, src, te1, te2, nv: (0, 0)
    expert1 = lambda i, src, te1, te2, nv: (te1[i], 0, 0)
    expert2 = lambda i, src, te1, te2, nv: (te2[i], 0, 0)
    w_in = lambda index_map: pl.BlockSpec((1, D_MODEL, D_EXPERT), index_map)
    w_out = lambda index_map: pl.BlockSpec((1, D_EXPERT, D_MODEL), index_map)
    grid_spec = pltpu.PrefetchScalarGridSpec(
        num_scalar_prefetch=4,
        grid=(n_tiles,),
        in_specs=[pl.BlockSpec(memory_space=pl.ANY),
                  pl.BlockSpec((1, D_MODEL), fixed),
                  w_in(expert1), w_in(expert1), w_out(expert1),
                  w_in(expert2), w_in(expert2), w_out(expert2)],
        out_specs=pl.BlockSpec(memory_space=pl.ANY),
        scratch_shapes=[pltpu.VMEM((2, tm * XM_ROWS, LANES), F32),
                        pltpu.VMEM((tm, D_MODEL), F32),
                        pltpu.SemaphoreType.DMA((2,)),
                        pltpu.SemaphoreType.DMA((1,))])
    return pl.pallas_call(
        _moe_kernel,
        grid_spec=grid_spec,
        out_shape=jax.ShapeDtypeStruct((t, D_MODEL), F32),
        compiler_params=_params(("arbitrary",)),
        name="moe",
    )(src, te1, te2, nv, xm, g, wg_bf, wu_bf, wd_bf, wg_bf, wu_bf, wd_bf)


def _route(bucket, tm):
    t = bucket.shape[0]
    nb = N_GROUPS * N_PAIRS
    n_tiles = t // tm + nb
    order = jnp.argsort(bucket, stable=True).astype(jnp.int32)
    ids = jnp.arange(nb, dtype=jnp.int32)
    counts = jnp.sum((bucket[:, None] == ids[None, :]).astype(jnp.int32), axis=0)
    tiles = (counts + tm - 1) // tm
    tile_end = jnp.cumsum(tiles)
    tile_start = tile_end - tiles
    row_start = jnp.cumsum(counts) - counts
    used = tile_end[-1]
    ti = jnp.minimum(jnp.arange(n_tiles, dtype=jnp.int32), used - 1)
    tb = jnp.searchsorted(tile_end, ti, side="right").astype(jnp.int32)
    local = ti - tile_start[tb]
    nv = jnp.where(jnp.arange(n_tiles) < used, jnp.clip(counts[tb] - local * tm, 0, tm), 0)
    rows = row_start[tb][:, None] + local[:, None] * tm + jnp.arange(tm, dtype=jnp.int32)[None, :]
    src = order[jnp.clip(rows, 0, t - 1)].reshape(-1)
    pair_a = jnp.asarray([0, 0, 0, 1, 1, 2], jnp.int32)
    pair_b = jnp.asarray([1, 2, 3, 2, 3, 3], jnp.int32)
    te1 = (tb // N_PAIRS) * PER_GROUP + pair_a[tb % N_PAIRS]
    te2 = (tb // N_PAIRS) * PER_GROUP + pair_b[tb % N_PAIRS]
    return src, te1.astype(jnp.int32), te2.astype(jnp.int32), nv.astype(jnp.int32)


def _block_diag(w):
    n = w.shape[0]
    eye = jnp.eye(n, dtype=w.dtype)
    return jnp.einsum("gcd,gh->gchd", w, eye).reshape(n * POOL_GROUP, n * POOL_GROUP)


def kernel(x, attn_norm_g, w_in, conv_w, conv_b, conv_ln_g, conv_ln_b, conv_pw_w, conv_pw_b,
           pool_w, pool_scale, q_norm_g, k_norm_g, lambda_q1, lambda_k1, lambda_q2, lambda_k2,
           attn_sub_norm_g, w_out, ffn_norm_g, router_g_w, router_g_b, router_e_w, router_e_b,
           w_gate, w_up, w_down):
    bsz, s, d = x.shape
    depth = w_in.shape[0]
    t = bsz * s
    qaug, kaug = _alibi_aug(s)
    row = lambda a: a.reshape(1, -1)

    x2 = x.reshape(t, d)
    for l in range(depth):
        lam_init = 0.8 - 0.6 * math.exp(-0.3 * l)
        z_conv, p_in, q, k, v = _in_proj(
            x2, row(attn_norm_g[l]), w_in[l].astype(BF16),
            row(jnp.tile(q_norm_g[l], 2)), row(jnp.tile(k_norm_g[l], 2)))

        y_conv = _conv_mixer(z_conv.reshape(bsz, s, -1), conv_w[l], row(conv_b[l]),
                             row(conv_ln_g[l]), row(conv_ln_b[l]),
                             conv_pw_w[l].astype(BF16), row(conv_pw_b[l]))
        y_pool = _pool_mixer(p_in.reshape(bsz, s, -1), _block_diag(pool_w[l]).astype(BF16),
                             row(pool_scale[l]))
        y_attn = _attention(q.reshape(bsz, s, -1), k.reshape(bsz, s, -1), v.reshape(bsz, s, -1),
                            qaug, kaug, row(lambda_q1[l]), row(lambda_k1[l]), row(lambda_q2[l]),
                            row(lambda_k2[l]), row(attn_sub_norm_g[l]), lam_init)

        pad = LANES - N_GROUPS - N_EXPERTS
        rw = jnp.pad(jnp.concatenate([router_g_w[l], router_e_w[l]], axis=1), ((0, 0), (0, pad)))
        rb = jnp.pad(jnp.concatenate([router_g_b[l], router_e_b[l]]), (0, pad))
        rw_hi = rw.astype(BF16)
        rw_lo = (rw - rw_hi.astype(F32)).astype(BF16)
        xm = _out_proj(
            x2, y_conv.reshape(t, -1), y_pool.reshape(t, -1), y_attn.reshape(t, -1),
            w_out[l].astype(BF16), row(ffn_norm_g[l]), rw_hi, rw_lo, row(rb))

        bucket = xm[X_ROWS::XM_ROWS, META_BUCKET].astype(jnp.int32)
        src, te1, te2, nv = _route(bucket, TM_MOE)
        x2 = _moe(xm, row(ffn_norm_g[l]), src, te1, te2, nv, w_gate[l].astype(BF16),
                  w_up[l].astype(BF16), w_down[l].astype(BF16))
    return x2.reshape(bsz, s, d)
```

```python
import functools
import math

import jax
import jax.numpy as jnp
import numpy as np
from jax import lax
from jax.experimental import pallas as pl
from jax.experimental.pallas import tpu as pltpu

D_MODEL = 1024
C_CONV = 256
C_POOL = 256
C_ATTN = 512
N_HEADS = 4
HEAD_DIM = 64
V_DIM = 2 * HEAD_DIM
CONV_WIDTH = 31
POOL_WINDOWS = (2, 4, 8, 16)
POOL_GROUP = C_POOL // len(POOL_WINDOWS)
N_IN = 2 * C_CONV + C_POOL + 3 * C_ATTN
N_GROUPS = 4
PER_GROUP = 4
N_EXPERTS = N_GROUPS * PER_GROUP
D_EXPERT = 256
EPS = 1e-6

LANES = 128
VMEM_LIMIT = 48 * 1024 * 1024

TM_PROJ = 512
TS_MIX = 512
HALO = 32
TQ = 512
TK = 256
TM_MOE = 256
N_PAIRS = 6
X_ROWS = D_MODEL // LANES
XM_ROWS = X_ROWS + 1
META_GATE_A, META_GATE_B, META_BUCKET, META_RANK = 0, 1, 2, 3
META_ROWS = 8
DMA_UNROLL = 8
COPY_CHUNK = 512

F32 = jnp.float32
BF16 = jnp.bfloat16


def _params(sem):
    return pltpu.CompilerParams(dimension_semantics=sem, vmem_limit_bytes=VMEM_LIMIT)


def _rms(x, g):
    return x * lax.rsqrt(jnp.mean(x * x, axis=-1, keepdims=True) + EPS) * g


def _load_tokens(x_ref, rows_per_token=X_ROWS):
    if x_ref.shape[-1] != LANES:
        return x_ref[...]
    tm = x_ref.shape[0] // rows_per_token
    return jnp.concatenate(
        [x_ref[pl.ds(c, tm, stride=rows_per_token), :] for c in range(X_ROWS)], axis=1)


def _store_tokens(o_ref, x, rows_per_token=X_ROWS):
    tm = x.shape[0]
    for c in range(X_ROWS):
        o_ref[pl.ds(c, tm, stride=rows_per_token), :] = x[:, c * LANES:(c + 1) * LANES]


def _in_proj_kernel(x_ref, g_ref, w_ref, qg_ref, kg_ref,
                    conv_ref, pool_ref, q_ref, k_ref, v_ref):
    h = _rms(_load_tokens(x_ref), g_ref[...]).astype(BF16)

    def proj(c0, width):
        return jnp.dot(h, w_ref[:, c0:c0 + width], preferred_element_type=F32)

    conv_ref[...] = proj(0, 2 * C_CONV).astype(BF16)
    pool_ref[...] = proj(2 * C_CONV, C_POOL).astype(BF16)

    lane = lax.broadcasted_iota(jnp.int32, (1, LANES), 1)
    first = lane < HEAD_DIM

    def qk_norm(z, gain_ref, out_ref, post_scale):
        gain = gain_ref[...]
        for hd in range(N_HEADS):
            blk = z[:, hd * LANES:(hd + 1) * LANES]
            sq = blk * blk
            s_all = jnp.sum(sq, axis=-1, keepdims=True)
            s_lo = jnp.sum(jnp.where(first, sq, 0.0), axis=-1, keepdims=True)
            r_lo = lax.rsqrt(s_lo * (1.0 / HEAD_DIM) + EPS)
            r_hi = lax.rsqrt((s_all - s_lo) * (1.0 / HEAD_DIM) + EPS)
            r = jnp.where(first, r_lo, r_hi)
            out_ref[:, hd * LANES:(hd + 1) * LANES] = (blk * r * gain * post_scale).astype(BF16)

    c_q = 2 * C_CONV + C_POOL
    qk_norm(proj(c_q, C_ATTN), qg_ref, q_ref, HEAD_DIM ** -0.5 * math.log2(math.e))
    qk_norm(proj(c_q + C_ATTN, C_ATTN), kg_ref, k_ref, 1.0)
    v_ref[...] = proj(c_q + 2 * C_ATTN, C_ATTN).astype(BF16)


def _in_proj(x2, g, w_bf, qg, kg):
    token_major = x2.shape[1] == LANES
    t = x2.shape[0] // X_ROWS if token_major else x2.shape[0]
    tm = TM_PROJ
    row = lambda i: (i, 0)
    fixed = lambda i: (0, 0)
    x_spec = pl.BlockSpec((tm * X_ROWS, LANES) if token_major else (tm, D_MODEL), row)
    outs = [jax.ShapeDtypeStruct((t, 2 * C_CONV), BF16),
            jax.ShapeDtypeStruct((t, C_POOL), BF16),
            jax.ShapeDtypeStruct((t, C_ATTN), BF16),
            jax.ShapeDtypeStruct((t, C_ATTN), BF16),
            jax.ShapeDtypeStruct((t, C_ATTN), BF16)]
    return pl.pallas_call(
        _in_proj_kernel,
        grid=(t // tm,),
        in_specs=[x_spec,
                  pl.BlockSpec((1, D_MODEL), fixed),
                  pl.BlockSpec((D_MODEL, N_IN), fixed),
                  pl.BlockSpec((1, LANES), fixed),
                  pl.BlockSpec((1, LANES), fixed)],
        out_specs=[pl.BlockSpec((tm, 2 * C_CONV), row),
                   pl.BlockSpec((tm, C_POOL), row),
                   pl.BlockSpec((tm, C_ATTN), row),
                   pl.BlockSpec((tm, C_ATTN), row),
                   pl.BlockSpec((tm, C_ATTN), row)],
        out_shape=outs,
        compiler_params=_params(("parallel",)),
        name="in_proj",
    )(x2, g, w_bf, qg, kg)


def _conv_kernel(z_ref, halo_ref, w_ref, b_ref, lng_ref, lnb_ref, pw_ref, pwb_ref,
                 o_ref, ext_ref):
    ts = z_ref.shape[1]

    def glu(z):
        z = z.astype(F32)
        return z[:, :C_CONV] * jax.nn.sigmoid(z[:, C_CONV:])

    halo = glu(halo_ref[0])
    ext_ref[0:HALO, :] = jnp.where(pl.program_id(1) == 0, 0.0, halo)
    ext_ref[HALO:, :] = glu(z_ref[0])

    base = HALO - (CONV_WIDTH - 1)
    rows = 64
    for c in range(ts // rows):
        acc = jnp.broadcast_to(b_ref[...], (rows, C_CONV))
        for j in range(CONV_WIDTH):
            r0 = c * rows + base + j
            acc = acc + ext_ref[r0:r0 + rows, :] * w_ref[j:j + 1, :]
        mu = jnp.mean(acc, axis=-1, keepdims=True)
        xc = acc - mu
        y = xc * lax.rsqrt(jnp.mean(xc * xc, axis=-1, keepdims=True) + EPS)
        y = y * lng_ref[...] + lnb_ref[...]
        y = y * jax.nn.sigmoid(y)
        out = jnp.dot(y.astype(BF16), pw_ref[...], preferred_element_type=F32) + pwb_ref[...]
        o_ref[0, c * rows:(c + 1) * rows, :] = out.astype(BF16)


def _conv_mixer(z_conv, w, b, ln_g, ln_b, pw_bf, pw_b):
    bsz, s, _ = z_conv.shape
    ts = TS_MIX
    per = ts // HALO
    fixed = lambda b_, i: (0, 0)
    return pl.pallas_call(
        _conv_kernel,
        grid=(bsz, s // ts),
        in_specs=[pl.BlockSpec((1, ts, 2 * C_CONV), lambda b_, i: (b_, i, 0)),
                  pl.BlockSpec((1, HALO, 2 * C_CONV),
                               lambda b_, i: (b_, jnp.maximum(i * per - 1, 0), 0)),
                  pl.BlockSpec((CONV_WIDTH, C_CONV), fixed),
                  pl.BlockSpec((1, C_CONV), fixed),
                  pl.BlockSpec((1, C_CONV), fixed),
                  pl.BlockSpec((1, C_CONV), fixed),
                  pl.BlockSpec((C_CONV, C_CONV), fixed),
                  pl.BlockSpec((1, C_CONV), fixed)],
        out_specs=pl.BlockSpec((1, ts, C_CONV), lambda b_, i: (b_, i, 0)),
        out_shape=jax.ShapeDtypeStruct((bsz, s, C_CONV), BF16),
        scratch_shapes=[pltpu.VMEM((ts + HALO, C_CONV), F32)],
        compiler_params=_params(("parallel", "parallel")),
        name="conv_mixer",
    )(z_conv, z_conv, w, b, ln_g, ln_b, pw_bf, pw_b)


def _pool_kernel(u_ref, halo_ref, w_ref, scale_ref, o_ref, ext_ref):
    ts = u_ref.shape[1]
    i = pl.program_id(1)
    ext_ref[0:HALO, :] = jnp.where(i == 0, 0.0, halo_ref[0].astype(F32))
    u = u_ref[0].astype(F32)
    ext_ref[HALO:, :] = u

    lane = lax.broadcasted_iota(jnp.int32, (1, C_POOL), 1)
    win = jnp.left_shift(2, lane // POOL_GROUP)
    acc = u
    for d in range(1, max(POOL_WINDOWS)):
        shifted = ext_ref[HALO - d:HALO - d + ts, :]
        acc = acc + jnp.where(d < win, shifted, 0.0)
    pos = i * ts + lax.broadcasted_iota(jnp.int32, (ts, 1), 0)
    count = jnp.minimum(pos + 1, win).astype(F32)
    diff = acc / count - u
    out = jnp.dot(diff.astype(BF16), w_ref[...], preferred_element_type=F32) * scale_ref[...]
    o_ref[0] = out.astype(BF16)


def _pool_mixer(p_in, w_blockdiag_bf, scale):
    bsz, s, _ = p_in.shape
    ts = TS_MIX
    per = ts // HALO
    fixed = lambda b_, i: (0, 0)
    return pl.pallas_call(
        _pool_kernel,
        grid=(bsz, s // ts),
        in_specs=[pl.BlockSpec((1, ts, C_POOL), lambda b_, i: (b_, i, 0)),
                  pl.BlockSpec((1, HALO, C_POOL),
                               lambda b_, i: (b_, jnp.maximum(i * per - 1, 0), 0)),
                  pl.BlockSpec((C_POOL, C_POOL), fixed),
                  pl.BlockSpec((1, C_POOL), fixed)],
        out_specs=pl.BlockSpec((1, ts, C_POOL), lambda b_, i: (b_, i, 0)),
        out_shape=jax.ShapeDtypeStruct((bsz, s, C_POOL), BF16),
        scratch_shapes=[pltpu.VMEM((ts + HALO, C_POOL), F32)],
        compiler_params=_params(("parallel", "parallel")),
        name="pool_mixer",
    )(p_in, p_in, w_blockdiag_bf, scale)


def _attn_kernel(q_ref, k_ref, v_ref, qaug_ref, kaug_ref, lq1_ref, lk1_ref, lq2_ref, lk2_ref,
                 sg_ref, o_ref, kx_ref, q2_all, m_all, l_all, acc_all, *, lam_init):
    qi = pl.program_id(1)
    half = TQ // 2
    assert half == TK

    @pl.when(qi == 0)
    def _():
        for hd in range(N_HEADS):
            kx_ref[hd, :, 0:LANES] = k_ref[0, :, hd * LANES:(hd + 1) * LANES]
            kx_ref[hd, :, LANES:] = kaug_ref[hd]

    lam = (jnp.exp(jnp.sum(lq1_ref[...] * lk1_ref[...], axis=-1, keepdims=True))
           - jnp.exp(jnp.sum(lq2_ref[...] * lk2_ref[...], axis=-1, keepdims=True))
           + lam_init)
    first = lax.broadcasted_iota(jnp.int32, (1, LANES), 1) < HEAD_DIM
    row = lax.broadcasted_iota(jnp.int32, (2 * half, TK), 0)
    col = lax.broadcasted_iota(jnp.int32, (2 * half, TK), 1)
    tri = col <= (row & (half - 1))

    def step(hd, r0, nr, k0, width, masked_rows):
        q2_ref, m_ref, l_ref, acc_ref = q2_all.at[hd], m_all.at[hd], l_all.at[hd], acc_all.at[hd]
        kv_rows = pl.ds(pl.multiple_of(k0, TK), width)
        kb = kx_ref[hd, kv_rows, :]
        vb = v_ref[0, kv_rows, hd * LANES:(hd + 1) * LANES]
        s = lax.dot_general(q2_ref[r0:r0 + nr, :], kb, (((1,), (1,)), ((), ())),
                            preferred_element_type=F32)
        if masked_rows:
            s_m = jnp.where(tri, s[:masked_rows], -jnp.inf)
            s = s_m if masked_rows == nr else jnp.concatenate([s_m, s[masked_rows:]], axis=0)
        m_prev = m_ref[r0:r0 + nr]
        m_new = jnp.maximum(m_prev, jnp.max(s, axis=-1, keepdims=True))
        alpha = jnp.exp2(m_prev - m_new)
        p = jnp.exp2(s - jnp.concatenate([m_new] * (width // LANES), axis=1))
        l_ref[r0:r0 + nr] = alpha * l_ref[r0:r0 + nr] + jnp.sum(p, axis=-1, keepdims=True)
        acc_ref[r0:r0 + nr] = alpha * acc_ref[r0:r0 + nr] + jnp.dot(
            p.astype(BF16), vb, preferred_element_type=F32)
        m_ref[r0:r0 + nr] = m_new

    for hd in range(N_HEADS):
        q2_ref, m_ref, l_ref, acc_ref = q2_all.at[hd], m_all.at[hd], l_all.at[hd], acc_all.at[hd]
        qb = q_ref[0, :, hd * LANES:(hd + 1) * LANES]
        zero = jnp.zeros_like(qb)
        q_maps = (jnp.where(first, qb, zero), jnp.where(first, zero, qb))
        qa = qaug_ref[hd]
        for part in range(2):
            rows = slice(part * half, (part + 1) * half)
            for m in range(2):
                dst = slice((2 * part + m) * half, (2 * part + m + 1) * half)
                q2_ref[dst, 0:LANES] = q_maps[m][rows]
                q2_ref[dst, LANES:] = qa[rows]
        m_ref[...] = jnp.full(m_ref.shape, -jnp.inf, F32)
        l_ref[...] = jnp.zeros(l_ref.shape, F32)
        acc_ref[...] = jnp.zeros(acc_ref.shape, F32)

        def body(j, carry, hd=hd):
            step(hd, 0, 4 * half, j * TQ, TQ, 0)
            return carry

        lax.fori_loop(0, qi, body, 0)
        step(hd, 0, 4 * half, qi * TQ, TK, 2 * half)
        step(hd, 2 * half, 2 * half, qi * TQ + TK, TK, 2 * half)

        for part in range(2):
            r = 2 * part * half
            o = (acc_ref[r:r + half] / l_ref[r:r + half]
                 - lam * (acc_ref[r + half:r + 2 * half] / l_ref[r + half:r + 2 * half]))
            o = _rms(o, sg_ref[...]) * (1.0 - lam_init)
            o_ref[0, part * half:(part + 1) * half, hd * LANES:(hd + 1) * LANES] = o.astype(BF16)


def _alibi_aug(s):
    start = 2.0 ** (-8.0 / N_HEADS)
    slopes = np.array([start ** (i + 1) for i in range(N_HEADS)], dtype=np.float32)
    c = math.log2(math.e) * slopes.astype(np.float64)[:, None] * np.arange(s)[None, :]
    pieces = []
    rem = c
    for _ in range(3):
        piece = rem.astype(jnp.bfloat16)
        pieces.append(piece)
        rem = rem - piece.astype(np.float64)
    one = np.ones_like(pieces[0])
    zero = np.zeros((N_HEADS, s, LANES - 6), dtype=jnp.bfloat16)
    kaug = np.concatenate([np.stack(pieces + [one] * 3, axis=-1), zero], axis=-1)
    qaug = np.concatenate([np.stack([one] * 3 + [-p for p in pieces], axis=-1), zero], axis=-1)
    return jnp.asarray(qaug), jnp.asarray(kaug)


def _attention(q, k, v, qaug, kaug, lq1, lk1, lq2, lk2, sub_g, lam_init):
    bsz, s, _ = q.shape
    nq = s // TQ
    fixed2 = lambda b_, i: (0, 0)
    seq = lambda b_, i: (b_, 0, 0)
    vec = pl.BlockSpec((1, HEAD_DIM), fixed2)
    return pl.pallas_call(
        functools.partial(_attn_kernel, lam_init=lam_init),
        grid=(bsz, nq),
        in_specs=[pl.BlockSpec((1, TQ, C_ATTN), lambda b_, i: (b_, i, 0)),
                  pl.BlockSpec((1, s, C_ATTN), seq),
                  pl.BlockSpec((1, s, C_ATTN), seq),
                  pl.BlockSpec((N_HEADS, TQ, LANES), lambda b_, i: (0, i, 0)),
                  pl.BlockSpec((N_HEADS, s, LANES), lambda b_, i: (0, 0, 0)),
                  vec, vec, vec, vec,
                  pl.BlockSpec((1, V_DIM), fixed2)],
        out_specs=pl.BlockSpec((1, TQ, C_ATTN), lambda b_, i: (b_, i, 0)),
        out_shape=jax.ShapeDtypeStruct((bsz, s, C_ATTN), BF16),
        scratch_shapes=[pltpu.VMEM((N_HEADS, s, 2 * LANES), BF16),
                        pltpu.VMEM((N_HEADS, 2 * TQ, 2 * LANES), BF16),
                        pltpu.VMEM((N_HEADS, 2 * TQ, LANES), F32),
                        pltpu.VMEM((N_HEADS, 2 * TQ, LANES), F32),
                        pltpu.VMEM((N_HEADS, 2 * TQ, V_DIM), F32)],
        compiler_params=_params(("parallel", "arbitrary")),
        name="diff_attention",
    )(q, k, v, qaug, kaug, lq1, lk1, lq2, lk2, sub_g)


def _split_bf16(a):
    hi = a.astype(BF16)
    lo = (a - hi.astype(F32)).astype(BF16)
    return hi, lo


def _out_proj_kernel(x_ref, yc_ref, yp_ref, ya_ref, w_ref, g_ref, rw_hi_ref, rw_lo_ref, rb_ref,
                     xm_ref, meta_t_ref, counts_ref, cnt_ref):
    x = _load_tokens(x_ref)
    x = x + jnp.dot(yc_ref[...], w_ref[0:C_CONV, :], preferred_element_type=F32)
    x = x + jnp.dot(yp_ref[...], w_ref[C_CONV:C_CONV + C_POOL, :], preferred_element_type=F32)
    x = x + jnp.dot(ya_ref[...], w_ref[C_CONV + C_POOL:, :], preferred_element_type=F32)
    tm = x.shape[0]
    _store_tokens(xm_ref, x, XM_ROWS)

    hf_hi, hf_lo = _split_bf16(_rms(x, g_ref[...]))

    logits = (jnp.dot(hf_hi, rw_hi_ref[...], preferred_element_type=F32)
              + jnp.dot(hf_lo, rw_hi_ref[...], preferred_element_type=F32)
              + jnp.dot(hf_hi, rw_lo_ref[...], preferred_element_type=F32)) + rb_ref[...]
    lane = lax.broadcasted_iota(jnp.int32, (tm, LANES), 1)
    neg = -jnp.inf

    def first_argmax(val, vmax):
        return jnp.min(jnp.where(val == vmax, lane, LANES), axis=-1, keepdims=True)

    lg = jnp.where(lane < N_GROUPS, logits, neg)
    g_max = jnp.max(lg, axis=-1, keepdims=True)
    g_idx = first_argmax(lg, g_max)
    g_p = 1.0 / jnp.sum(jnp.exp(lg - g_max), axis=-1, keepdims=True)

    e_lane = lane - N_GROUPS
    in_group = (e_lane >= g_idx * PER_GROUP) & (e_lane < (g_idx + 1) * PER_GROUP)
    le = jnp.where(in_group, logits, neg)
    e_max = jnp.max(le, axis=-1, keepdims=True)
    pe = jnp.exp(le - e_max)
    pe = pe / jnp.sum(pe, axis=-1, keepdims=True)
    p1 = jnp.max(pe, axis=-1, keepdims=True)
    i1 = first_argmax(jnp.where(in_group, pe, neg), p1)
    pe2 = jnp.where(in_group & (lane != i1), pe, neg)
    p2 = jnp.max(pe2, axis=-1, keepdims=True)
    i2 = first_argmax(pe2, p2)
    denom = p1 + p2
    gate1 = g_p * (p1 / denom)
    gate2 = g_p * (p2 / denom)

    loc1 = i1 - N_GROUPS - g_idx * PER_GROUP
    loc2 = i2 - N_GROUPS - g_idx * PER_GROUP
    a = jnp.minimum(loc1, loc2)
    b = jnp.maximum(loc1, loc2)
    pair = jnp.where(a == 0, 0, jnp.where(a == 1, 3, 5)) + (b - a - 1)
    bucket = g_idx * N_PAIRS + pair
    gate_a = jnp.where(loc1 < loc2, gate1, gate2)
    gate_b = jnp.where(loc1 < loc2, gate2, gate1)

    @pl.when(pl.program_id(0) == 0)
    def _():
        cnt_ref[...] = jnp.zeros(cnt_ref.shape, F32)

    in_bucket = lane == bucket
    earlier = (lax.broadcasted_iota(jnp.int32, (tm, tm), 1)
               < lax.broadcasted_iota(jnp.int32, (tm, tm), 0)).astype(BF16)
    before = jnp.dot(earlier, in_bucket.astype(BF16), preferred_element_type=F32) + cnt_ref[...]
    rank = jnp.sum(jnp.where(in_bucket, before, 0.0), axis=-1, keepdims=True)
    cnt_ref[...] += jnp.sum(in_bucket.astype(F32), axis=0, keepdims=True)
    counts_ref[...] = cnt_ref[...]

    meta = jnp.where(lane == META_GATE_A, gate_a,
                     jnp.where(lane == META_GATE_B, gate_b,
                               jnp.where(lane == META_BUCKET, bucket.astype(F32),
                                         jnp.where(lane == META_RANK, rank, 0.0))))
    xm_ref[pl.ds(X_ROWS, tm, stride=XM_ROWS), :] = meta
    meta_t_ref[...] = meta.T[0:META_ROWS, :]


def _out_proj(x2, yc, yp, ya, w_bf, g, rw_hi, rw_lo, rb):
    token_major = x2.shape[1] == LANES
    t = x2.shape[0] // X_ROWS if token_major else x2.shape[0]
    tm = TM_PROJ
    row = lambda i: (i, 0)
    fixed = lambda i: (0, 0)
    x_spec = pl.BlockSpec((tm * X_ROWS, LANES) if token_major else (tm, D_MODEL), row)
    return pl.pallas_call(
        _out_proj_kernel,
        grid=(t // tm,),
        in_specs=[x_spec,
                  pl.BlockSpec((tm, C_CONV), row),
                  pl.BlockSpec((tm, C_POOL), row),
                  pl.BlockSpec((tm, C_ATTN), row),
                  pl.BlockSpec((D_MODEL, D_MODEL), fixed),
                  pl.BlockSpec((1, D_MODEL), fixed),
                  pl.BlockSpec((D_MODEL, LANES), fixed),
                  pl.BlockSpec((D_MODEL, LANES), fixed),
                  pl.BlockSpec((1, LANES), fixed)],
        out_specs=[pl.BlockSpec((tm * XM_ROWS, LANES), row),
                   pl.BlockSpec((META_ROWS, tm), row),
                   pl.BlockSpec((1, LANES), fixed)],
        out_shape=[jax.ShapeDtypeStruct((t * XM_ROWS, LANES), F32),
                   jax.ShapeDtypeStruct((t // tm * META_ROWS, tm), F32),
                   jax.ShapeDtypeStruct((1, LANES), F32)],
        scratch_shapes=[pltpu.VMEM((1, LANES), F32)],
        compiler_params=_params(("arbitrary",)),
        name="out_proj_router",
    )(x2, yc, yp, ya, w_bf, g, rw_hi, rw_lo, rb)


def _copy_rows_kernel(dest_ref, src_hbm, out_hbm, sem, *, rows, gather, n_tokens):
    base = pl.program_id(0) * COPY_CHUNK

    def row_copy(tok, slot):
        a, b = (slot, tok) if gather else (tok, slot)
        return pltpu.make_async_copy(src_hbm.at[pl.ds(a * rows, rows)],
                                     out_hbm.at[pl.ds(b * rows, rows)], sem.at[0])

    def body(blk, carry):
        toks = [base + blk * DMA_UNROLL + u for u in range(DMA_UNROLL)]
        slots = [dest_ref[tok] for tok in toks]
        for tok, slot in zip(toks, slots):
            row_copy(tok if gather else jnp.where(tok < n_tokens, tok, 0), slot).start()
        return carry

    lax.fori_loop(0, COPY_CHUNK // DMA_UNROLL, body, 0)
    pltpu.make_async_copy(src_hbm.at[pl.ds(0, COPY_CHUNK * rows)],
                          out_hbm.at[pl.ds(0, COPY_CHUNK * rows)], sem.at[0]).wait()


def _copy_rows(src, dest, n_tokens, out_tokens, rows, gather, name):
    n_copies = n_tokens if gather else dest.shape[0]
    assert n_copies % COPY_CHUNK == 0
    grid_spec = pltpu.PrefetchScalarGridSpec(
        num_scalar_prefetch=1,
        grid=(n_copies // COPY_CHUNK,),
        in_specs=[pl.BlockSpec(memory_space=pl.ANY)],
        out_specs=pl.BlockSpec(memory_space=pl.ANY),
        scratch_shapes=[pltpu.SemaphoreType.DMA((1,))])
    return pl.pallas_call(
        functools.partial(_copy_rows_kernel, rows=rows, gather=gather, n_tokens=n_tokens),
        grid_spec=grid_spec,
        out_shape=jax.ShapeDtypeStruct((out_tokens * rows, LANES), F32),
        compiler_params=_params(("arbitrary",)),
        name=name,
    )(dest, src)


def _moe_kernel(te1_ref, te2_ref, nv_ref, used_ref,
                xs_ref, g_ref, wg1_ref, wu1_ref, wd1_ref, wg2_ref, wu2_ref, wd2_ref, ys_ref):
    i = pl.program_id(0)

    @pl.when(nv_ref[i] > 0)
    def _():
        x = _load_tokens(xs_ref, XM_ROWS)
        meta = xs_ref[pl.ds(X_ROWS, TM_MOE, stride=XM_ROWS), :]
        hf = _rms(x, g_ref[...]).astype(BF16)

        def expert(wg_ref, wu_ref, wd_ref, gate):
            a = jnp.dot(hf, wg_ref[0], preferred_element_type=F32)
            u = jnp.dot(hf, wu_ref[0], preferred_element_type=F32)
            act = a * jax.nn.sigmoid(a) * u * gate
            return jnp.dot(act.astype(BF16), wd_ref[0], preferred_element_type=F32)

        y = expert(wg1_ref, wu1_ref, wd1_ref, meta[:, META_GATE_A:META_GATE_A + 1])
        y = y + expert(wg2_ref, wu2_ref, wd2_ref, meta[:, META_GATE_B:META_GATE_B + 1])
        _store_tokens(ys_ref, x + y)

    @pl.when(nv_ref[i] == 0)
    def _():
        ys_ref[...] = jnp.zeros(ys_ref.shape, F32)


def _moe(xs, g, te1, te2, nv, used, wg_bf, wu_bf, wd_bf):
    tm = TM_MOE
    n_tiles = nv.shape[0]
    fixed = lambda i, te1, te2, nv, used: (0, 0)
    tile_in = lambda i, te1, te2, nv, used: (jnp.minimum(i, used[0] - 1), 0)
    tile_out = lambda i, te1, te2, nv, used: (i, 0)
    expert1 = lambda i, te1, te2, nv, used: (te1[i], 0, 0)
    expert2 = lambda i, te1, te2, nv, used: (te2[i], 0, 0)
    w_in = lambda index_map: pl.BlockSpec((1, D_MODEL, D_EXPERT), index_map)
    w_out = lambda index_map: pl.BlockSpec((1, D_EXPERT, D_MODEL), index_map)
    grid_spec = pltpu.PrefetchScalarGridSpec(
        num_scalar_prefetch=4,
        grid=(n_tiles,),
        in_specs=[pl.BlockSpec((tm * XM_ROWS, LANES), tile_in),
                  pl.BlockSpec((1, D_MODEL), fixed),
                  w_in(expert1), w_in(expert1), w_out(expert1),
                  w_in(expert2), w_in(expert2), w_out(expert2)],
        out_specs=pl.BlockSpec((tm * X_ROWS, LANES), tile_out))
    return pl.pallas_call(
        _moe_kernel,
        grid_spec=grid_spec,
        out_shape=jax.ShapeDtypeStruct((n_tiles * tm * X_ROWS, LANES), F32),
        compiler_params=_params(("arbitrary",)),
        name="moe",
    )(te1, te2, nv, used, xs, g, wg_bf, wu_bf, wd_bf, wg_bf, wu_bf, wd_bf)


def _route(meta_t, counts, t):
    tm = TM_MOE
    nb = N_GROUPS * N_PAIRS
    n_tiles = t // tm + nb
    meta = meta_t.reshape(t // TM_PROJ, META_ROWS, TM_PROJ)
    bucket = meta[:, META_BUCKET, :].reshape(t).astype(jnp.int32)
    rank = meta[:, META_RANK, :].reshape(t).astype(jnp.int32)
    counts = counts[0, :nb].astype(jnp.int32)
    ids = jnp.arange(nb, dtype=jnp.int32)

    tiles = (counts + tm - 1) // tm
    tile_end = jnp.cumsum(tiles)
    tile_start = tile_end - tiles
    used = tile_end[-1]
    row0 = tile_start * tm
    dest = rank + jnp.sum(jnp.where(bucket[:, None] == ids[None, :], row0[None, :], 0), axis=1)

    ti = jnp.arange(n_tiles, dtype=jnp.int32)
    tic = jnp.minimum(ti, used - 1)
    tb = jnp.sum((tile_end[None, :] <= tic[:, None]).astype(jnp.int32), axis=1)
    onehot_tb = tb[:, None] == ids[None, :]
    pick = lambda table: jnp.sum(jnp.where(onehot_tb, table[None, :], 0), axis=1)
    nv = jnp.where(ti < used, jnp.clip(pick(counts) - (tic - pick(tile_start)) * tm, 0, tm), 0)
    pair_a = jnp.asarray([0, 0, 0, 1, 1, 2], jnp.int32)
    pair_b = jnp.asarray([1, 2, 3, 2, 3, 3], jnp.int32)
    te1 = pick((ids // N_PAIRS) * PER_GROUP + pair_a[ids % N_PAIRS])
    te2 = pick((ids // N_PAIRS) * PER_GROUP + pair_b[ids % N_PAIRS])

    pads = tiles * tm - counts
    pad_end = jnp.cumsum(pads)
    k = jnp.arange(nb * tm, dtype=jnp.int32)
    kb = jnp.minimum(jnp.sum((pad_end[None, :] <= k[:, None]).astype(jnp.int32), axis=1), nb - 1)
    onehot_kb = kb[:, None] == ids[None, :]
    pick_k = lambda table: jnp.sum(jnp.where(onehot_kb, table[None, :], 0), axis=1)
    in_pad = pick_k(row0 + counts) + (k - pick_k(pad_end - pads))
    free_rows = jnp.where(k < pad_end[-1], in_pad, used * tm + (k - pad_end[-1]))
    i32 = lambda a: a.astype(jnp.int32)
    return (i32(jnp.concatenate([dest, free_rows])), i32(te1), i32(te2), i32(nv),
            i32(used).reshape(1))


def _to_rows_kernel(x_ref, o_ref):
    o_ref[...] = _load_tokens(x_ref)


def _to_rows(x_tm):
    t = x_tm.shape[0] // X_ROWS
    tm = TM_PROJ
    return pl.pallas_call(
        _to_rows_kernel,
        grid=(t // tm,),
        in_specs=[pl.BlockSpec((tm * X_ROWS, LANES), lambda i: (i, 0))],
        out_specs=pl.BlockSpec((tm, D_MODEL), lambda i: (i, 0)),
        out_shape=jax.ShapeDtypeStruct((t, D_MODEL), F32),
        compiler_params=_params(("parallel",)),
        name="to_rows",
    )(x_tm)


def _block_diag(w):
    n = w.shape[0]
    eye = jnp.eye(n, dtype=w.dtype)
    return jnp.einsum("gcd,gh->gchd", w, eye).reshape(n * POOL_GROUP, n * POOL_GROUP)


def kernel(x, attn_norm_g, w_in, conv_w, conv_b, conv_ln_g, conv_ln_b, conv_pw_w, conv_pw_b,
           pool_w, pool_scale, q_norm_g, k_norm_g, lambda_q1, lambda_k1, lambda_q2, lambda_k2,
           attn_sub_norm_g, w_out, ffn_norm_g, router_g_w, router_g_b, router_e_w, router_e_b,
           w_gate, w_up, w_down):
    bsz, s, d = x.shape
    depth = w_in.shape[0]
    t = bsz * s
    qaug, kaug = _alibi_aug(s)
    row = lambda a: a.reshape(1, -1)

    x2 = x.reshape(t, d)
    for l in range(depth):
        lam_init = 0.8 - 0.6 * math.exp(-0.3 * l)
        z_conv, p_in, q, k, v = _in_proj(
            x2, row(attn_norm_g[l]), w_in[l].astype(BF16),
            row(jnp.tile(q_norm_g[l], 2)), row(jnp.tile(k_norm_g[l], 2)))

        y_conv = _conv_mixer(z_conv.reshape(bsz, s, -1), conv_w[l], row(conv_b[l]),
                             row(conv_ln_g[l]), row(conv_ln_b[l]),
                             conv_pw_w[l].astype(BF16), row(conv_pw_b[l]))
        y_pool = _pool_mixer(p_in.reshape(bsz, s, -1), _block_diag(pool_w[l]).astype(BF16),
                             row(pool_scale[l]))
        y_attn = _attention(q.reshape(bsz, s, -1), k.reshape(bsz, s, -1), v.reshape(bsz, s, -1),
                            qaug, kaug, row(lambda_q1[l]), row(lambda_k1[l]), row(lambda_q2[l]),
                            row(lambda_k2[l]), row(attn_sub_norm_g[l]), lam_init)

        pad = LANES - N_GROUPS - N_EXPERTS
        rw = jnp.pad(jnp.concatenate([router_g_w[l], router_e_w[l]], axis=1), ((0, 0), (0, pad)))
        rb = jnp.pad(jnp.concatenate([router_g_b[l], router_e_b[l]]), (0, pad))
        rw_hi = rw.astype(BF16)
        rw_lo = (rw - rw_hi.astype(F32)).astype(BF16)
        xm, meta_t, counts = _out_proj(
            x2, y_conv.reshape(t, -1), y_pool.reshape(t, -1), y_attn.reshape(t, -1),
            w_out[l].astype(BF16), row(ffn_norm_g[l]), rw_hi, rw_lo, row(rb))

        dest, te1, te2, nv, used = _route(meta_t, counts, t)
        xs = _copy_rows(xm, dest, t, dest.shape[0], XM_ROWS, False, "moe_dispatch")
        ys = _moe(xs, row(ffn_norm_g[l]), te1, te2, nv, used, w_gate[l].astype(BF16),
                  w_up[l].astype(BF16), w_down[l].astype(BF16))
        x2 = _copy_rows(ys, dest, t, t, X_ROWS, True, "moe_combine")
    return _to_rows(x2).reshape(bsz, s, d)
```

```python
import functools
import math

import jax
import jax.numpy as jnp
import numpy as np
from jax import lax
from jax.experimental import pallas as pl
from jax.experimental.pallas import tpu as pltpu

D_MODEL = 1024
C_CONV = 256
C_POOL = 256
C_ATTN = 512
N_HEADS = 4
HEAD_DIM = 64
V_DIM = 2 * HEAD_DIM
CONV_WIDTH = 31
POOL_WINDOWS = (2, 4, 8, 16)
POOL_GROUP = C_POOL // len(POOL_WINDOWS)
N_IN = 2 * C_CONV + C_POOL + 3 * C_ATTN
N_GROUPS = 4
PER_GROUP = 4
N_EXPERTS = N_GROUPS * PER_GROUP
D_EXPERT = 256
EPS = 1e-6

LANES = 128
VMEM_LIMIT = 48 * 1024 * 1024

TM_PROJ = 512
TS_MIX = 512
HALO = 32
TQ = 512
TK = 256
TM_MOE = 256
N_PAIRS = 6
X_ROWS = D_MODEL // LANES
XM_ROWS = X_ROWS + 1
META_GATE_A, META_GATE_B, META_BUCKET, META_RANK = 0, 1, 2, 3
META_ROWS = 8
DMA_UNROLL = 8
COPY_CHUNK = 2048
COPY_WAIT = 512

F32 = jnp.float32
BF16 = jnp.bfloat16


def _params(sem):
    return pltpu.CompilerParams(dimension_semantics=sem, vmem_limit_bytes=VMEM_LIMIT)


def _rms(x, g):
    return x * lax.rsqrt(jnp.mean(x * x, axis=-1, keepdims=True) + EPS) * g


def _load_tokens(x_ref, rows_per_token=X_ROWS):
    if x_ref.shape[-1] != LANES:
        return x_ref[...]
    tm = x_ref.shape[0] // rows_per_token
    return jnp.concatenate(
        [x_ref[pl.ds(c, tm, stride=rows_per_token), :] for c in range(X_ROWS)], axis=1)


def _store_tokens(o_ref, x, rows_per_token=X_ROWS):
    tm = x.shape[0]
    for c in range(X_ROWS):
        o_ref[pl.ds(c, tm, stride=rows_per_token), :] = x[:, c * LANES:(c + 1) * LANES]


def _in_proj_kernel(x_ref, g_ref, w_ref, qg_ref, kg_ref,
                    conv_ref, pool_ref, q_ref, k_ref, v_ref):
    h = _rms(_load_tokens(x_ref), g_ref[...]).astype(BF16)

    def proj(c0, width):
        return jnp.dot(h, w_ref[:, c0:c0 + width], preferred_element_type=F32)

    conv_ref[...] = proj(0, 2 * C_CONV).astype(BF16)
    pool_ref[...] = proj(2 * C_CONV, C_POOL).astype(BF16)

    lane = lax.broadcasted_iota(jnp.int32, (1, LANES), 1)
    first = lane < HEAD_DIM

    def qk_norm(z, gain_ref, out_ref, post_scale):
        gain = gain_ref[...]
        for hd in range(N_HEADS):
            blk = z[:, hd * LANES:(hd + 1) * LANES]
            sq = blk * blk
            s_all = jnp.sum(sq, axis=-1, keepdims=True)
            s_lo = jnp.sum(jnp.where(first, sq, 0.0), axis=-1, keepdims=True)
            r_lo = lax.rsqrt(s_lo * (1.0 / HEAD_DIM) + EPS)
            r_hi = lax.rsqrt((s_all - s_lo) * (1.0 / HEAD_DIM) + EPS)
            r = jnp.where(first, r_lo, r_hi)
            out_ref[:, hd * LANES:(hd + 1) * LANES] = (blk * r * gain * post_scale).astype(BF16)

    c_q = 2 * C_CONV + C_POOL
    qk_norm(proj(c_q, C_ATTN), qg_ref, q_ref, HEAD_DIM ** -0.5 * math.log2(math.e))
    qk_norm(proj(c_q + C_ATTN, C_ATTN), kg_ref, k_ref, 1.0)
    v_ref[...] = proj(c_q + 2 * C_ATTN, C_ATTN).astype(BF16)


def _in_proj(x2, g, w_bf, qg, kg):
    token_major = x2.shape[1] == LANES
    t = x2.shape[0] // X_ROWS if token_major else x2.shape[0]
    tm = TM_PROJ
    row = lambda i: (i, 0)
    fixed = lambda i: (0, 0)
    x_spec = pl.BlockSpec((tm * X_ROWS, LANES) if token_major else (tm, D_MODEL), row)
    outs = [jax.ShapeDtypeStruct((t, 2 * C_CONV), BF16),
            jax.ShapeDtypeStruct((t, C_POOL), BF16),
            jax.ShapeDtypeStruct((t, C_ATTN), BF16),
            jax.ShapeDtypeStruct((t, C_ATTN), BF16),
            jax.ShapeDtypeStruct((t, C_ATTN), BF16)]
    return pl.pallas_call(
        _in_proj_kernel,
        grid=(t // tm,),
        in_specs=[x_spec,
                  pl.BlockSpec((1, D_MODEL), fixed),
                  pl.BlockSpec((D_MODEL, N_IN), fixed),
                  pl.BlockSpec((1, LANES), fixed),
                  pl.BlockSpec((1, LANES), fixed)],
        out_specs=[pl.BlockSpec((tm, 2 * C_CONV), row),
                   pl.BlockSpec((tm, C_POOL), row),
                   pl.BlockSpec((tm, C_ATTN), row),
                   pl.BlockSpec((tm, C_ATTN), row),
                   pl.BlockSpec((tm, C_ATTN), row)],
        out_shape=outs,
        compiler_params=_params(("parallel",)),
        name="in_proj",
    )(x2, g, w_bf, qg, kg)


def _conv_kernel(z_ref, halo_ref, w_ref, b_ref, lng_ref, lnb_ref, pw_ref, pwb_ref,
                 o_ref, ext_ref):
    ts = z_ref.shape[1]

    def glu(z):
        z = z.astype(F32)
        return z[:, :C_CONV] * jax.nn.sigmoid(z[:, C_CONV:])

    halo = glu(halo_ref[0])
    ext_ref[0:HALO, :] = jnp.where(pl.program_id(1) == 0, 0.0, halo)
    ext_ref[HALO:, :] = glu(z_ref[0])

    base = HALO - (CONV_WIDTH - 1)
    rows = 64
    for c in range(ts // rows):
        acc = jnp.broadcast_to(b_ref[...], (rows, C_CONV))
        for j in range(CONV_WIDTH):
            r0 = c * rows + base + j
            acc = acc + ext_ref[r0:r0 + rows, :] * w_ref[j:j + 1, :]
        mu = jnp.mean(acc, axis=-1, keepdims=True)
        xc = acc - mu
        y = xc * lax.rsqrt(jnp.mean(xc * xc, axis=-1, keepdims=True) + EPS)
        y = y * lng_ref[...] + lnb_ref[...]
        y = y * jax.nn.sigmoid(y)
        out = jnp.dot(y.astype(BF16), pw_ref[...], preferred_element_type=F32) + pwb_ref[...]
        o_ref[0, c * rows:(c + 1) * rows, :] = out.astype(BF16)


def _conv_mixer(z_conv, w, b, ln_g, ln_b, pw_bf, pw_b):
    bsz, s, _ = z_conv.shape
    ts = TS_MIX
    per = ts // HALO
    fixed = lambda b_, i: (0, 0)
    return pl.pallas_call(
        _conv_kernel,
        grid=(bsz, s // ts),
        in_specs=[pl.BlockSpec((1, ts, 2 * C_CONV), lambda b_, i: (b_, i, 0)),
                  pl.BlockSpec((1, HALO, 2 * C_CONV),
                               lambda b_, i: (b_, jnp.maximum(i * per - 1, 0), 0)),
                  pl.BlockSpec((CONV_WIDTH, C_CONV), fixed),
                  pl.BlockSpec((1, C_CONV), fixed),
                  pl.BlockSpec((1, C_CONV), fixed),
                  pl.BlockSpec((1, C_CONV), fixed),
                  pl.BlockSpec((C_CONV, C_CONV), fixed),
                  pl.BlockSpec((1, C_CONV), fixed)],
        out_specs=pl.BlockSpec((1, ts, C_CONV), lambda b_, i: (b_, i, 0)),
        out_shape=jax.ShapeDtypeStruct((bsz, s, C_CONV), BF16),
        scratch_shapes=[pltpu.VMEM((ts + HALO, C_CONV), F32)],
        compiler_params=_params(("parallel", "parallel")),
        name="conv_mixer",
    )(z_conv, z_conv, w, b, ln_g, ln_b, pw_bf, pw_b)


def _pool_kernel(u_ref, halo_ref, w_ref, scale_ref, o_ref, ext_ref):
    ts = u_ref.shape[1]
    i = pl.program_id(1)
    ext_ref[0:HALO, :] = jnp.where(i == 0, 0.0, halo_ref[0].astype(F32))
    u = u_ref[0].astype(F32)
    ext_ref[HALO:, :] = u

    lane = lax.broadcasted_iota(jnp.int32, (1, C_POOL), 1)
    win = jnp.left_shift(2, lane // POOL_GROUP)
    acc = u
    for d in range(1, max(POOL_WINDOWS)):
        shifted = ext_ref[HALO - d:HALO - d + ts, :]
        acc = acc + jnp.where(d < win, shifted, 0.0)
    pos = i * ts + lax.broadcasted_iota(jnp.int32, (ts, 1), 0)
    count = jnp.minimum(pos + 1, win).astype(F32)
    diff = acc / count - u
    out = jnp.dot(diff.astype(BF16), w_ref[...], preferred_element_type=F32) * scale_ref[...]
    o_ref[0] = out.astype(BF16)


def _pool_mixer(p_in, w_blockdiag_bf, scale):
    bsz, s, _ = p_in.shape
    ts = TS_MIX
    per = ts // HALO
    fixed = lambda b_, i: (0, 0)
    return pl.pallas_call(
        _pool_kernel,
        grid=(bsz, s // ts),
        in_specs=[pl.BlockSpec((1, ts, C_POOL), lambda b_, i: (b_, i, 0)),
                  pl.BlockSpec((1, HALO, C_POOL),
                               lambda b_, i: (b_, jnp.maximum(i * per - 1, 0), 0)),
                  pl.BlockSpec((C_POOL, C_POOL), fixed),
                  pl.BlockSpec((1, C_POOL), fixed)],
        out_specs=pl.BlockSpec((1, ts, C_POOL), lambda b_, i: (b_, i, 0)),
        out_shape=jax.ShapeDtypeStruct((bsz, s, C_POOL), BF16),
        scratch_shapes=[pltpu.VMEM((ts + HALO, C_POOL), F32)],
        compiler_params=_params(("parallel", "parallel")),
        name="pool_mixer",
    )(p_in, p_in, w_blockdiag_bf, scale)


def _attn_kernel(q_ref, k_ref, v_ref, qaug_ref, kaug_ref, lq1_ref, lk1_ref, lq2_ref, lk2_ref,
                 sg_ref, o_ref, kx_ref, q2_all, m_all, l_all, acc_all, *, lam_init):
    qi = pl.program_id(1)
    half = TQ // 2
    assert half == TK

    @pl.when(qi == 0)
    def _():
        for hd in range(N_HEADS):
            kx_ref[hd, :, 0:LANES] = k_ref[0, :, hd * LANES:(hd + 1) * LANES]
            kx_ref[hd, :, LANES:] = kaug_ref[hd]

    lam = (jnp.exp(jnp.sum(lq1_ref[...] * lk1_ref[...], axis=-1, keepdims=True))
           - jnp.exp(jnp.sum(lq2_ref[...] * lk2_ref[...], axis=-1, keepdims=True))
           + lam_init)
    first = lax.broadcasted_iota(jnp.int32, (1, LANES), 1) < HEAD_DIM
    row = lax.broadcasted_iota(jnp.int32, (2 * half, TK), 0)
    col = lax.broadcasted_iota(jnp.int32, (2 * half, TK), 1)
    tri = col <= (row & (half - 1))

    def step(hd, r0, nr, k0, width, masked_rows):
        q2_ref, m_ref, l_ref, acc_ref = q2_all.at[hd], m_all.at[hd], l_all.at[hd], acc_all.at[hd]
        kv_rows = pl.ds(pl.multiple_of(k0, TK), width)
        kb = kx_ref[hd, kv_rows, :]
        vb = v_ref[0, kv_rows, hd * LANES:(hd + 1) * LANES]
        s = lax.dot_general(q2_ref[r0:r0 + nr, :], kb, (((1,), (1,)), ((), ())),
                            preferred_element_type=F32)
        if masked_rows:
            s_m = jnp.where(tri, s[:masked_rows], -jnp.inf)
            s = s_m if masked_rows == nr else jnp.concatenate([s_m, s[masked_rows:]], axis=0)
        m_prev = m_ref[r0:r0 + nr]
        m_new = jnp.maximum(m_prev, jnp.max(s, axis=-1, keepdims=True))
        alpha = jnp.exp2(m_prev - m_new)
        p = jnp.exp2(s - jnp.concatenate([m_new] * (width // LANES), axis=1))
        l_ref[r0:r0 + nr] = alpha * l_ref[r0:r0 + nr] + jnp.sum(p, axis=-1, keepdims=True)
        acc_ref[r0:r0 + nr] = alpha * acc_ref[r0:r0 + nr] + jnp.dot(
            p.astype(BF16), vb, preferred_element_type=F32)
        m_ref[r0:r0 + nr] = m_new

    for hd in range(N_HEADS):
        q2_ref, m_ref, l_ref, acc_ref = q2_all.at[hd], m_all.at[hd], l_all.at[hd], acc_all.at[hd]
        qb = q_ref[0, :, hd * LANES:(hd + 1) * LANES]
        zero = jnp.zeros_like(qb)
        q_maps = (jnp.where(first, qb, zero), jnp.where(first, zero, qb))
        qa = qaug_ref[hd]
        for part in range(2):
            rows = slice(part * half, (part + 1) * half)
            for m in range(2):
                dst = slice((2 * part + m) * half, (2 * part + m + 1) * half)
                q2_ref[dst, 0:LANES] = q_maps[m][rows]
                q2_ref[dst, LANES:] = qa[rows]
        m_ref[...] = jnp.full(m_ref.shape, -jnp.inf, F32)
        l_ref[...] = jnp.zeros(l_ref.shape, F32)
        acc_ref[...] = jnp.zeros(acc_ref.shape, F32)

        def body(j, carry, hd=hd):
            step(hd, 0, 4 * half, j * TQ, TQ, 0)
            return carry

        lax.fori_loop(0, qi, body, 0)
        step(hd, 0, 4 * half, qi * TQ, TK, 2 * half)
        step(hd, 2 * half, 2 * half, qi * TQ + TK, TK, 2 * half)

        for part in range(2):
            r = 2 * part * half
            o = (acc_ref[r:r + half] / l_ref[r:r + half]
                 - lam * (acc_ref[r + half:r + 2 * half] / l_ref[r + half:r + 2 * half]))
            o = _rms(o, sg_ref[...]) * (1.0 - lam_init)
            o_ref[0, part * half:(part + 1) * half, hd * LANES:(hd + 1) * LANES] = o.astype(BF16)


def _alibi_aug(s):
    start = 2.0 ** (-8.0 / N_HEADS)
    slopes = np.array([start ** (i + 1) for i in range(N_HEADS)], dtype=np.float32)
    c = math.log2(math.e) * slopes.astype(np.float64)[:, None] * np.arange(s)[None, :]
    pieces = []
    rem = c
    for _ in range(3):
        piece = rem.astype(jnp.bfloat16)
        pieces.append(piece)
        rem = rem - piece.astype(np.float64)
    one = np.ones_like(pieces[0])
    zero = np.zeros((N_HEADS, s, LANES - 6), dtype=jnp.bfloat16)
    kaug = np.concatenate([np.stack(pieces + [one] * 3, axis=-1), zero], axis=-1)
    qaug = np.concatenate([np.stack([one] * 3 + [-p for p in pieces], axis=-1), zero], axis=-1)
    return jnp.asarray(qaug), jnp.asarray(kaug)


def _attention(q, k, v, qaug, kaug, lq1, lk1, lq2, lk2, sub_g, lam_init):
    bsz, s, _ = q.shape
    nq = s // TQ
    fixed2 = lambda b_, i: (0, 0)
    seq = lambda b_, i: (b_, 0, 0)
    vec = pl.BlockSpec((1, HEAD_DIM), fixed2)
    return pl.pallas_call(
        functools.partial(_attn_kernel, lam_init=lam_init),
        grid=(bsz, nq),
        in_specs=[pl.BlockSpec((1, TQ, C_ATTN), lambda b_, i: (b_, i, 0)),
                  pl.BlockSpec((1, s, C_ATTN), seq),
                  pl.BlockSpec((1, s, C_ATTN), seq),
                  pl.BlockSpec((N_HEADS, TQ, LANES), lambda b_, i: (0, i, 0)),
                  pl.BlockSpec((N_HEADS, s, LANES), lambda b_, i: (0, 0, 0)),
                  vec, vec, vec, vec,
                  pl.BlockSpec((1, V_DIM), fixed2)],
        out_specs=pl.BlockSpec((1, TQ, C_ATTN), lambda b_, i: (b_, i, 0)),
        out_shape=jax.ShapeDtypeStruct((bsz, s, C_ATTN), BF16),
        scratch_shapes=[pltpu.VMEM((N_HEADS, s, 2 * LANES), BF16),
                        pltpu.VMEM((N_HEADS, 2 * TQ, 2 * LANES), BF16),
                        pltpu.VMEM((N_HEADS, 2 * TQ, LANES), F32),
                        pltpu.VMEM((N_HEADS, 2 * TQ, LANES), F32),
                        pltpu.VMEM((N_HEADS, 2 * TQ, V_DIM), F32)],
        compiler_params=_params(("parallel", "arbitrary")),
        name="diff_attention",
    )(q, k, v, qaug, kaug, lq1, lk1, lq2, lk2, sub_g)


def _split_bf16(a):
    hi = a.astype(BF16)
    lo = (a - hi.astype(F32)).astype(BF16)
    return hi, lo


def _out_proj_kernel(x_ref, yc_ref, yp_ref, ya_ref, w_ref, g_ref, rw_hi_ref, rw_lo_ref, rb_ref,
                     xm_ref, meta_t_ref, counts_ref, cnt_ref):
    x = _load_tokens(x_ref)
    x = x + jnp.dot(yc_ref[...], w_ref[0:C_CONV, :], preferred_element_type=F32)
    x = x + jnp.dot(yp_ref[...], w_ref[C_CONV:C_CONV + C_POOL, :], preferred_element_type=F32)
    x = x + jnp.dot(ya_ref[...], w_ref[C_CONV + C_POOL:, :], preferred_element_type=F32)
    tm = x.shape[0]
    _store_tokens(xm_ref, x, XM_ROWS)

    hf_hi, hf_lo = _split_bf16(_rms(x, g_ref[...]))

    logits = (jnp.dot(hf_hi, rw_hi_ref[...], preferred_element_type=F32)
              + jnp.dot(hf_lo, rw_hi_ref[...], preferred_element_type=F32)
              + jnp.dot(hf_hi, rw_lo_ref[...], preferred_element_type=F32)) + rb_ref[...]
    lane = lax.broadcasted_iota(jnp.int32, (tm, LANES), 1)
    neg = -jnp.inf

    def first_argmax(val, vmax):
        return jnp.min(jnp.where(val == vmax, lane, LANES), axis=-1, keepdims=True)

    lg = jnp.where(lane < N_GROUPS, logits, neg)
    g_max = jnp.max(lg, axis=-1, keepdims=True)
    g_idx = first_argmax(lg, g_max)
    g_p = 1.0 / jnp.sum(jnp.exp(lg - g_max), axis=-1, keepdims=True)

    e_lane = lane - N_GROUPS
    in_group = (e_lane >= g_idx * PER_GROUP) & (e_lane < (g_idx + 1) * PER_GROUP)
    le = jnp.where(in_group, logits, neg)
    e_max = jnp.max(le, axis=-1, keepdims=True)
    pe = jnp.exp(le - e_max)
    pe = pe / jnp.sum(pe, axis=-1, keepdims=True)
    p1 = jnp.max(pe, axis=-1, keepdims=True)
    i1 = first_argmax(jnp.where(in_group, pe, neg), p1)
    pe2 = jnp.where(in_group & (lane != i1), pe, neg)
    p2 = jnp.max(pe2, axis=-1, keepdims=True)
    i2 = first_argmax(pe2, p2)
    denom = p1 + p2
    gate1 = g_p * (p1 / denom)
    gate2 = g_p * (p2 / denom)

    loc1 = i1 - N_GROUPS - g_idx * PER_GROUP
    loc2 = i2 - N_GROUPS - g_idx * PER_GROUP
    a = jnp.minimum(loc1, loc2)
    b = jnp.maximum(loc1, loc2)
    pair = jnp.where(a == 0, 0, jnp.where(a == 1, 3, 5)) + (b - a - 1)
    bucket = g_idx * N_PAIRS + pair
    gate_a = jnp.where(loc1 < loc2, gate1, gate2)
    gate_b = jnp.where(loc1 < loc2, gate2, gate1)

    @pl.when(pl.program_id(0) == 0)
    def _():
        cnt_ref[...] = jnp.zeros(cnt_ref.shape, F32)

    in_bucket = lane == bucket
    earlier = (lax.broadcasted_iota(jnp.int32, (tm, tm), 1)
               < lax.broadcasted_iota(jnp.int32, (tm, tm), 0)).astype(BF16)
    before = jnp.dot(earlier, in_bucket.astype(BF16), preferred_element_type=F32) + cnt_ref[...]
    rank = jnp.sum(jnp.where(in_bucket, before, 0.0), axis=-1, keepdims=True)
    cnt_ref[...] += jnp.sum(in_bucket.astype(F32), axis=0, keepdims=True)
    counts_ref[...] = cnt_ref[...]

    meta = jnp.where(lane == META_GATE_A, gate_a,
                     jnp.where(lane == META_GATE_B, gate_b,
                               jnp.where(lane == META_BUCKET, bucket.astype(F32),
                                         jnp.where(lane == META_RANK, rank, 0.0))))
    xm_ref[pl.ds(X_ROWS, tm, stride=XM_ROWS), :] = meta
    meta_t_ref[...] = meta.T[0:META_ROWS, :]


def _out_proj(x2, yc, yp, ya, w_bf, g, rw_hi, rw_lo, rb):
    token_major = x2.shape[1] == LANES
    t = x2.shape[0] // X_ROWS if token_major else x2.shape[0]
    tm = TM_PROJ
    row = lambda i: (i, 0)
    fixed = lambda i: (0, 0)
    x_spec = pl.BlockSpec((tm * X_ROWS, LANES) if token_major else (tm, D_MODEL), row)
    return pl.pallas_call(
        _out_proj_kernel,
        grid=(t // tm,),
        in_specs=[x_spec,
                  pl.BlockSpec((tm, C_CONV), row),
                  pl.BlockSpec((tm, C_POOL), row),
                  pl.BlockSpec((tm, C_ATTN), row),
                  pl.BlockSpec((D_MODEL, D_MODEL), fixed),
                  pl.BlockSpec((1, D_MODEL), fixed),
                  pl.BlockSpec((D_MODEL, LANES), fixed),
                  pl.BlockSpec((D_MODEL, LANES), fixed),
                  pl.BlockSpec((1, LANES), fixed)],
        out_specs=[pl.BlockSpec((tm * XM_ROWS, LANES), row),
                   pl.BlockSpec((META_ROWS, tm), row),
                   pl.BlockSpec((1, LANES), fixed)],
        out_shape=[jax.ShapeDtypeStruct((t * XM_ROWS, LANES), F32),
                   jax.ShapeDtypeStruct((t // tm * META_ROWS, tm), F32),
                   jax.ShapeDtypeStruct((1, LANES), F32)],
        scratch_shapes=[pltpu.VMEM((1, LANES), F32)],
        compiler_params=_params(("arbitrary",)),
        name="out_proj_router",
    )(x2, yc, yp, ya, w_bf, g, rw_hi, rw_lo, rb)


def _for_chunk_rows(dest_ref, start_copy, wait_copy):
    base = pl.program_id(0) * COPY_CHUNK

    def body(blk, carry):
        rows = [blk * DMA_UNROLL + u for u in range(DMA_UNROLL)]
        slots = [dest_ref[base + r] for r in rows]
        for r, slot in zip(rows, slots):
            start_copy(r, slot)
        return carry

    lax.fori_loop(0, COPY_CHUNK // DMA_UNROLL, body, 0)
    for piece in range(COPY_CHUNK // COPY_WAIT):
        wait_copy(piece)


def _dispatch_kernel(dest_ref, x_ref, out_hbm, sem):
    def start_copy(r, slot):
        pltpu.make_async_copy(x_ref.at[pl.ds(r * XM_ROWS, XM_ROWS)],
                              out_hbm.at[pl.ds(slot * XM_ROWS, XM_ROWS)], sem.at[0]).start()

    def wait_copy(piece):
        n = COPY_WAIT * XM_ROWS
        pltpu.make_async_copy(x_ref.at[pl.ds(piece * n, n)], out_hbm.at[pl.ds(0, n)],
                              sem.at[0]).wait()

    _for_chunk_rows(dest_ref, start_copy, wait_copy)


def _dispatch(xm, dest):
    n_tokens = xm.shape[0] // XM_ROWS
    n_rows = dest.shape[0]
    assert n_tokens % COPY_CHUNK == 0 and n_rows % COPY_CHUNK == 0
    last = n_tokens // COPY_CHUNK - 1
    grid_spec = pltpu.PrefetchScalarGridSpec(
        num_scalar_prefetch=1,
        grid=(n_rows // COPY_CHUNK,),
        in_specs=[pl.BlockSpec((COPY_CHUNK * XM_ROWS, LANES),
                               lambda i, dest: (jnp.minimum(i, last), 0))],
        out_specs=pl.BlockSpec(memory_space=pl.ANY),
        scratch_shapes=[pltpu.SemaphoreType.DMA((1,))])
    return pl.pallas_call(
        _dispatch_kernel,
        grid_spec=grid_spec,
        out_shape=jax.ShapeDtypeStruct((n_rows * XM_ROWS, LANES), F32),
        compiler_params=_params(("arbitrary",)),
        name="moe_dispatch",
    )(dest, xm)


def _combine_kernel(dest_ref, ys_hbm, o_ref, *scratch, to_rows):
    buf = scratch[0] if to_rows else o_ref
    sem = scratch[-1]

    def start_copy(r, slot):
        pltpu.make_async_copy(ys_hbm.at[pl.ds(slot * X_ROWS, X_ROWS)],
                              buf.at[pl.ds(r * X_ROWS, X_ROWS)], sem.at[0]).start()

    def wait_copy(piece):
        n = COPY_WAIT * X_ROWS
        pltpu.make_async_copy(ys_hbm.at[pl.ds(0, n)], buf.at[pl.ds(piece * n, n)],
                              sem.at[0]).wait()

    _for_chunk_rows(dest_ref, start_copy, wait_copy)
    if to_rows:
        o_ref[...] = _load_tokens(buf)


def _combine(ys, dest, n_tokens, to_rows):
    assert n_tokens % COPY_CHUNK == 0
    chunk_rows = COPY_CHUNK * X_ROWS
    if to_rows:
        out_spec = pl.BlockSpec((COPY_CHUNK, D_MODEL), lambda i, dest: (i, 0))
        out_shape = jax.ShapeDtypeStruct((n_tokens, D_MODEL), F32)
        scratch = [pltpu.VMEM((chunk_rows, LANES), F32)]
    else:
        out_spec = pl.BlockSpec((chunk_rows, LANES), lambda i, dest: (i, 0))
        out_shape = jax.ShapeDtypeStruct((n_tokens * X_ROWS, LANES), F32)
        scratch = []
    grid_spec = pltpu.PrefetchScalarGridSpec(
        num_scalar_prefetch=1,
        grid=(n_tokens // COPY_CHUNK,),
        in_specs=[pl.BlockSpec(memory_space=pl.ANY)],
        out_specs=out_spec,
        scratch_shapes=scratch + [pltpu.SemaphoreType.DMA((1,))])
    return pl.pallas_call(
        functools.partial(_combine_kernel, to_rows=to_rows),
        grid_spec=grid_spec,
        out_shape=out_shape,
        compiler_params=_params(("arbitrary",)),
        name="moe_combine",
    )(dest, ys)


def _moe_kernel(te1_ref, te2_ref, nv_ref, used_ref,
                xs_ref, g_ref, wg1_ref, wu1_ref, wd1_ref, wg2_ref, wu2_ref, wd2_ref, ys_ref):
    i = pl.program_id(0)

    @pl.when(nv_ref[i] > 0)
    def _():
        x = _load_tokens(xs_ref, XM_ROWS)
        meta = xs_ref[pl.ds(X_ROWS, TM_MOE, stride=XM_ROWS), :]
        hf = _rms(x, g_ref[...]).astype(BF16)

        def expert(wg_ref, wu_ref, wd_ref, gate):
            a = jnp.dot(hf, wg_ref[0], preferred_element_type=F32)
            u = jnp.dot(hf, wu_ref[0], preferred_element_type=F32)
            act = a * jax.nn.sigmoid(a) * u * gate
            return jnp.dot(act.astype(BF16), wd_ref[0], preferred_element_type=F32)

        y = expert(wg1_ref, wu1_ref, wd1_ref, meta[:, META_GATE_A:META_GATE_A + 1])
        y = y + expert(wg2_ref, wu2_ref, wd2_ref, meta[:, META_GATE_B:META_GATE_B + 1])
        _store_tokens(ys_ref, x + y)

    @pl.when(nv_ref[i] == 0)
    def _():
        ys_ref[...] = jnp.zeros(ys_ref.shape, F32)


def _moe(xs, g, te1, te2, nv, used, wg_bf, wu_bf, wd_bf):
    tm = TM_MOE
    n_tiles = nv.shape[0]
    fixed = lambda i, te1, te2, nv, used: (0, 0)
    tile_in = lambda i, te1, te2, nv, used: (jnp.minimum(i, used[0] - 1), 0)
    tile_out = lambda i, te1, te2, nv, used: (i, 0)
    expert1 = lambda i, te1, te2, nv, used: (te1[i], 0, 0)
    expert2 = lambda i, te1, te2, nv, used: (te2[i], 0, 0)
    w_in = lambda index_map: pl.BlockSpec((1, D_MODEL, D_EXPERT), index_map)
    w_out = lambda index_map: pl.BlockSpec((1, D_EXPERT, D_MODEL), index_map)
    grid_spec = pltpu.PrefetchScalarGridSpec(
        num_scalar_prefetch=4,
        grid=(n_tiles,),
        in_specs=[pl.BlockSpec((tm * XM_ROWS, LANES), tile_in),
                  pl.BlockSpec((1, D_MODEL), fixed),
                  w_in(expert1), w_in(expert1), w_out(expert1),
                  w_in(expert2), w_in(expert2), w_out(expert2)],
        out_specs=pl.BlockSpec((tm * X_ROWS, LANES), tile_out))
    return pl.pallas_call(
        _moe_kernel,
        grid_spec=grid_spec,
        out_shape=jax.ShapeDtypeStruct((n_tiles * tm * X_ROWS, LANES), F32),
        compiler_params=_params(("arbitrary",)),
        name="moe",
    )(te1, te2, nv, used, xs, g, wg_bf, wu_bf, wd_bf, wg_bf, wu_bf, wd_bf)


def _route(meta_t, counts, t):
    tm = TM_MOE
    nb = N_GROUPS * N_PAIRS
    n_tiles = t // tm + nb
    meta = meta_t.reshape(t // TM_PROJ, META_ROWS, TM_PROJ)
    bucket = meta[:, META_BUCKET, :].reshape(t).astype(jnp.int32)
    rank = meta[:, META_RANK, :].reshape(t).astype(jnp.int32)
    counts = counts[0, :nb].astype(jnp.int32)
    ids = jnp.arange(nb, dtype=jnp.int32)

    tiles = (counts + tm - 1) // tm
    tile_end = jnp.cumsum(tiles)
    tile_start = tile_end - tiles
    used = tile_end[-1]
    row0 = tile_start * tm
    dest = rank + jnp.sum(jnp.where(bucket[:, None] == ids[None, :], row0[None, :], 0), axis=1)

    ti = jnp.arange(n_tiles, dtype=jnp.int32)
    tic = jnp.minimum(ti, used - 1)
    tb = jnp.sum((tile_end[None, :] <= tic[:, None]).astype(jnp.int32), axis=1)
    onehot_tb = tb[:, None] == ids[None, :]
    pick = lambda table: jnp.sum(jnp.where(onehot_tb, table[None, :], 0), axis=1)
    nv = jnp.where(ti < used, jnp.clip(pick(counts) - (tic - pick(tile_start)) * tm, 0, tm), 0)
    pair_a = jnp.asarray([0, 0, 0, 1, 1, 2], jnp.int32)
    pair_b = jnp.asarray([1, 2, 3, 2, 3, 3], jnp.int32)
    te1 = pick((ids // N_PAIRS) * PER_GROUP + pair_a[ids % N_PAIRS])
    te2 = pick((ids // N_PAIRS) * PER_GROUP + pair_b[ids % N_PAIRS])

    pads = tiles * tm - counts
    pad_end = jnp.cumsum(pads)
    k = jnp.arange(nb * tm, dtype=jnp.int32)
    kb = jnp.minimum(jnp.sum((pad_end[None, :] <= k[:, None]).astype(jnp.int32), axis=1), nb - 1)
    onehot_kb = kb[:, None] == ids[None, :]
    pick_k = lambda table: jnp.sum(jnp.where(onehot_kb, table[None, :], 0), axis=1)
    in_pad = pick_k(row0 + counts) + (k - pick_k(pad_end - pads))
    free_rows = jnp.where(k < pad_end[-1], in_pad, used * tm + (k - pad_end[-1]))
    i32 = lambda a: a.astype(jnp.int32)
    return (i32(jnp.concatenate([dest, free_rows])), i32(te1), i32(te2), i32(nv),
            i32(used).reshape(1))


def _block_diag(w):
    n = w.shape[0]
    eye = jnp.eye(n, dtype=w.dtype)
    return jnp.einsum("gcd,gh->gchd", w, eye).reshape(n * POOL_GROUP, n * POOL_GROUP)


def kernel(x, attn_norm_g, w_in, conv_w, conv_b, conv_ln_g, conv_ln_b, conv_pw_w, conv_pw_b,
           pool_w, pool_scale, q_norm_g, k_norm_g, lambda_q1, lambda_k1, lambda_q2, lambda_k2,
           attn_sub_norm_g, w_out, ffn_norm_g, router_g_w, router_g_b, router_e_w, router_e_b,
           w_gate, w_up, w_down):
    bsz, s, d = x.shape
    depth = w_in.shape[0]
    t = bsz * s
    qaug, kaug = _alibi_aug(s)
    row = lambda a: a.reshape(1, -1)

    x2 = x.reshape(t, d)
    for l in range(depth):
        lam_init = 0.8 - 0.6 * math.exp(-0.3 * l)
        z_conv, p_in, q, k, v = _in_proj(
            x2, row(attn_norm_g[l]), w_in[l].astype(BF16),
            row(jnp.tile(q_norm_g[l], 2)), row(jnp.tile(k_norm_g[l], 2)))

        y_conv = _conv_mixer(z_conv.reshape(bsz, s, -1), conv_w[l], row(conv_b[l]),
                             row(conv_ln_g[l]), row(conv_ln_b[l]),
                             conv_pw_w[l].astype(BF16), row(conv_pw_b[l]))
        y_pool = _pool_mixer(p_in.reshape(bsz, s, -1), _block_diag(pool_w[l]).astype(BF16),
                             row(pool_scale[l]))
        y_attn = _attention(q.reshape(bsz, s, -1), k.reshape(bsz, s, -1), v.reshape(bsz, s, -1),
                            qaug, kaug, row(lambda_q1[l]), row(lambda_k1[l]), row(lambda_q2[l]),
                            row(lambda_k2[l]), row(attn_sub_norm_g[l]), lam_init)

        pad = LANES - N_GROUPS - N_EXPERTS
        rw = jnp.pad(jnp.concatenate([router_g_w[l], router_e_w[l]], axis=1), ((0, 0), (0, pad)))
        rb = jnp.pad(jnp.concatenate([router_g_b[l], router_e_b[l]]), (0, pad))
        rw_hi = rw.astype(BF16)
        rw_lo = (rw - rw_hi.astype(F32)).astype(BF16)
        xm, meta_t, counts = _out_proj(
            x2, y_conv.reshape(t, -1), y_pool.reshape(t, -1), y_attn.reshape(t, -1),
            w_out[l].astype(BF16), row(ffn_norm_g[l]), rw_hi, rw_lo, row(rb))

        dest, te1, te2, nv, used = _route(meta_t, counts, t)
        xs = _dispatch(xm, dest)
        ys = _moe(xs, row(ffn_norm_g[l]), te1, te2, nv, used, w_gate[l].astype(BF16),
                  w_up[l].astype(BF16), w_down[l].astype(BF16))
        x2 = _combine(ys, dest, t, to_rows=l == depth - 1)
    return x2.reshape(bsz, s, d)
```

```python
import functools
import math

import jax
import jax.numpy as jnp
import numpy as np
from jax import lax
from jax.experimental import pallas as pl
from jax.experimental.pallas import tpu as pltpu

D_MODEL = 1024
C_CONV = 256
C_POOL = 256
C_ATTN = 512
N_HEADS = 4
HEAD_DIM = 64
V_DIM = 2 * HEAD_DIM
CONV_WIDTH = 31
POOL_WINDOWS = (2, 4, 8, 16)
POOL_GROUP = C_POOL // len(POOL_WINDOWS)
N_IN = 2 * C_CONV + C_POOL + 3 * C_ATTN
N_GROUPS = 4
PER_GROUP = 4
N_EXPERTS = N_GROUPS * PER_GROUP
D_EXPERT = 256
EPS = 1e-6

LANES = 128
SUBLANES = 8
VMEM_LIMIT = 48 * 1024 * 1024

TM_PROJ = 512
TS_MIX = 512
HALO = 32
TQ = 512
TK = 256
TM_MOE = 256
N_PAIRS = 6
X_ROWS = D_MODEL // LANES
XM_ROWS = X_ROWS + 1
META_GATE_A, META_GATE_B, META_BUCKET, META_RANK = 0, 1, 2, 3
META_ROWS = 8
DMA_UNROLL = 8
COPY_CHUNK = 2048
COPY_WAIT = 512

F32 = jnp.float32
BF16 = jnp.bfloat16


def _params(sem):
    return pltpu.CompilerParams(dimension_semantics=sem, vmem_limit_bytes=VMEM_LIMIT)


def _rms(x, g):
    return x * lax.rsqrt(jnp.mean(x * x, axis=-1, keepdims=True) + EPS) * g


def _load_tokens(x_ref, rows_per_token=X_ROWS):
    if x_ref.shape[-1] != LANES:
        return x_ref[...]
    tm = x_ref.shape[0] // rows_per_token
    return jnp.concatenate(
        [x_ref[pl.ds(c, tm, stride=rows_per_token), :] for c in range(X_ROWS)], axis=1)


def _store_tokens(o_ref, x, rows_per_token=X_ROWS):
    tm = x.shape[0]
    for c in range(X_ROWS):
        o_ref[pl.ds(c, tm, stride=rows_per_token), :] = x[:, c * LANES:(c + 1) * LANES]


def _in_proj_kernel(x_ref, g_ref, w_ref, qg_ref, kg_ref,
                    conv_ref, pool_ref, q_ref, k_ref, v_ref):
    h = _rms(_load_tokens(x_ref), g_ref[...]).astype(BF16)

    def proj(c0, width):
        return jnp.dot(h, w_ref[:, c0:c0 + width], preferred_element_type=F32)

    conv_ref[...] = proj(0, 2 * C_CONV).astype(BF16)
    pool_ref[...] = proj(2 * C_CONV, C_POOL).astype(BF16)

    lane = lax.broadcasted_iota(jnp.int32, (1, LANES), 1)
    first = lane < HEAD_DIM

    def qk_norm(z, gain_ref, out_ref, post_scale):
        gain = gain_ref[...]
        for hd in range(N_HEADS):
            blk = z[:, hd * LANES:(hd + 1) * LANES]
            sq = blk * blk
            s_all = jnp.sum(sq, axis=-1, keepdims=True)
            s_lo = jnp.sum(jnp.where(first, sq, 0.0), axis=-1, keepdims=True)
            r_lo = lax.rsqrt(s_lo * (1.0 / HEAD_DIM) + EPS)
            r_hi = lax.rsqrt((s_all - s_lo) * (1.0 / HEAD_DIM) + EPS)
            r = jnp.where(first, r_lo, r_hi)
            out_ref[:, hd * LANES:(hd + 1) * LANES] = (blk * r * gain * post_scale).astype(BF16)

    c_q = 2 * C_CONV + C_POOL
    qk_norm(proj(c_q, C_ATTN), qg_ref, q_ref, HEAD_DIM ** -0.5 * math.log2(math.e))
    qk_norm(proj(c_q + C_ATTN, C_ATTN), kg_ref, k_ref, 1.0)
    v_ref[...] = proj(c_q + 2 * C_ATTN, C_ATTN).astype(BF16)


def _in_proj(x2, g, w_bf, qg, kg):
    token_major = x2.shape[1] == LANES
    t = x2.shape[0] // X_ROWS if token_major else x2.shape[0]
    tm = TM_PROJ
    row = lambda i: (i, 0)
    fixed = lambda i: (0, 0)
    x_spec = pl.BlockSpec((tm * X_ROWS, LANES) if token_major else (tm, D_MODEL), row)
    outs = [jax.ShapeDtypeStruct((t, 2 * C_CONV), BF16),
            jax.ShapeDtypeStruct((t, C_POOL), BF16),
            jax.ShapeDtypeStruct((t, C_ATTN), BF16),
            jax.ShapeDtypeStruct((t, C_ATTN), BF16),
            jax.ShapeDtypeStruct((t, C_ATTN), BF16)]
    return pl.pallas_call(
        _in_proj_kernel,
        grid=(t // tm,),
        in_specs=[x_spec,
                  pl.BlockSpec((1, D_MODEL), fixed),
                  pl.BlockSpec((D_MODEL, N_IN), fixed),
                  pl.BlockSpec((1, LANES), fixed),
                  pl.BlockSpec((1, LANES), fixed)],
        out_specs=[pl.BlockSpec((tm, 2 * C_CONV), row),
                   pl.BlockSpec((tm, C_POOL), row),
                   pl.BlockSpec((tm, C_ATTN), row),
                   pl.BlockSpec((tm, C_ATTN), row),
                   pl.BlockSpec((tm, C_ATTN), row)],
        out_shape=outs,
        compiler_params=_params(("parallel",)),
        name="in_proj",
    )(x2, g, w_bf, qg, kg)


def _conv_kernel(z_ref, halo_ref, w_ref, b_ref, lng_ref, lnb_ref, pw_ref, pwb_ref,
                 o_ref, ext_ref):
    ts = z_ref.shape[1]

    def glu(z):
        z = z.astype(F32)
        return z[:, :C_CONV] * jax.nn.sigmoid(z[:, C_CONV:])

    halo = glu(halo_ref[0])
    ext_ref[0, 0:HALO, :] = jnp.where(pl.program_id(1) == 0, 0.0, halo)
    ext_ref[0, HALO:, :] = glu(z_ref[0])
    n_shifted = ts + HALO - SUBLANES
    for s in range(1, SUBLANES):
        ext_ref[s, 0:n_shifted, :] = ext_ref[0, s:s + n_shifted, :]

    base = HALO - (CONV_WIDTH - 1)
    rows = 64
    for c in range(ts // rows):
        acc = jnp.broadcast_to(b_ref[...], (rows, C_CONV))
        for j in range(CONV_WIDTH):
            phase = (base + j) % SUBLANES
            r0 = c * rows + base + j - phase
            acc = acc + ext_ref[phase, r0:r0 + rows, :] * w_ref[j:j + 1, :]
        mu = jnp.mean(acc, axis=-1, keepdims=True)
        xc = acc - mu
        y = xc * lax.rsqrt(jnp.mean(xc * xc, axis=-1, keepdims=True) + EPS)
        y = y * lng_ref[...] + lnb_ref[...]
        y = y * jax.nn.sigmoid(y)
        out = jnp.dot(y.astype(BF16), pw_ref[...], preferred_element_type=F32) + pwb_ref[...]
        o_ref[0, c * rows:(c + 1) * rows, :] = out.astype(BF16)


def _conv_mixer(z_conv, w, b, ln_g, ln_b, pw_bf, pw_b):
    bsz, s, _ = z_conv.shape
    ts = TS_MIX
    per = ts // HALO
    fixed = lambda b_, i: (0, 0)
    return pl.pallas_call(
        _conv_kernel,
        grid=(bsz, s // ts),
        in_specs=[pl.BlockSpec((1, ts, 2 * C_CONV), lambda b_, i: (b_, i, 0)),
                  pl.BlockSpec((1, HALO, 2 * C_CONV),
                               lambda b_, i: (b_, jnp.maximum(i * per - 1, 0), 0)),
                  pl.BlockSpec((CONV_WIDTH, C_CONV), fixed),
                  pl.BlockSpec((1, C_CONV), fixed),
                  pl.BlockSpec((1, C_CONV), fixed),
                  pl.BlockSpec((1, C_CONV), fixed),
                  pl.BlockSpec((C_CONV, C_CONV), fixed),
                  pl.BlockSpec((1, C_CONV), fixed)],
        out_specs=pl.BlockSpec((1, ts, C_CONV), lambda b_, i: (b_, i, 0)),
        out_shape=jax.ShapeDtypeStruct((bsz, s, C_CONV), BF16),
        scratch_shapes=[pltpu.VMEM((SUBLANES, ts + HALO, C_CONV), F32)],
        compiler_params=_params(("parallel", "parallel")),
        name="conv_mixer",
    )(z_conv, z_conv, w, b, ln_g, ln_b, pw_bf, pw_b)


def _pool_kernel(u_ref, halo_ref, w_ref, scale_ref, o_ref, ext_ref):
    ts = u_ref.shape[1]
    i = pl.program_id(1)
    n = ts + HALO
    ext_ref[0, 0:HALO, :] = jnp.where(i == 0, 0.0, halo_ref[0].astype(F32))
    u = u_ref[0].astype(F32)
    ext_ref[0, HALO:, :] = u

    assert POOL_WINDOWS == (2, 4, 8, 16) and HALO >= 4 * SUBLANES
    sums = []
    for k in range(1, 5):
        lo = SUBLANES * k
        shift = 1 << (k - 1)
        level = ext_ref[k - 1, lo:n, :] + ext_ref[k - 1, lo - shift:n - shift, :]
        sums.append(level[HALO - lo:, :])
        if k < 4:
            ext_ref[k, lo:n, :] = level

    lane = lax.broadcasted_iota(jnp.int32, (1, C_POOL), 1)
    group = lane // POOL_GROUP
    win = jnp.left_shift(2, group)
    acc = jnp.where(group == 0, sums[0],
                    jnp.where(group == 1, sums[1], jnp.where(group == 2, sums[2], sums[3])))
    pos = i * ts + lax.broadcasted_iota(jnp.int32, (ts, 1), 0)
    count = jnp.minimum(pos + 1, win).astype(F32)
    diff = acc / count - u
    out = jnp.dot(diff.astype(BF16), w_ref[...], preferred_element_type=F32) * scale_ref[...]
    o_ref[0] = out.astype(BF16)


def _pool_mixer(p_in, w_blockdiag_bf, scale):
    bsz, s, _ = p_in.shape
    ts = TS_MIX
    per = ts // HALO
    fixed = lambda b_, i: (0, 0)
    return pl.pallas_call(
        _pool_kernel,
        grid=(bsz, s // ts),
        in_specs=[pl.BlockSpec((1, ts, C_POOL), lambda b_, i: (b_, i, 0)),
                  pl.BlockSpec((1, HALO, C_POOL),
                               lambda b_, i: (b_, jnp.maximum(i * per - 1, 0), 0)),
                  pl.BlockSpec((C_POOL, C_POOL), fixed),
                  pl.BlockSpec((1, C_POOL), fixed)],
        out_specs=pl.BlockSpec((1, ts, C_POOL), lambda b_, i: (b_, i, 0)),
        out_shape=jax.ShapeDtypeStruct((bsz, s, C_POOL), BF16),
        scratch_shapes=[pltpu.VMEM((4, ts + HALO, C_POOL), F32)],
        compiler_params=_params(("parallel", "parallel")),
        name="pool_mixer",
    )(p_in, p_in, w_blockdiag_bf, scale)


def _attn_kernel(q_ref, k_ref, v_ref, qaug_ref, kaug_ref, lq1_ref, lk1_ref, lq2_ref, lk2_ref,
                 sg_ref, o_ref, kx_ref, q2_all, m_all, l_all, acc_all, *, lam_init):
    qi = pl.program_id(1)
    half = TQ // 2
    assert half == TK

    @pl.when(qi == 0)
    def _():
        for hd in range(N_HEADS):
            kx_ref[hd, :, 0:LANES] = k_ref[0, :, hd * LANES:(hd + 1) * LANES]
            kx_ref[hd, :, LANES:] = kaug_ref[hd]

    lam = (jnp.exp(jnp.sum(lq1_ref[...] * lk1_ref[...], axis=-1, keepdims=True))
           - jnp.exp(jnp.sum(lq2_ref[...] * lk2_ref[...], axis=-1, keepdims=True))
           + lam_init)
    first = lax.broadcasted_iota(jnp.int32, (1, LANES), 1) < HEAD_DIM
    row = lax.broadcasted_iota(jnp.int32, (2 * half, TK), 0)
    col = lax.broadcasted_iota(jnp.int32, (2 * half, TK), 1)
    tri = col <= (row & (half - 1))

    def step(hd, r0, nr, k0, width, masked_rows):
        q2_ref, m_ref, l_ref, acc_ref = q2_all.at[hd], m_all.at[hd], l_all.at[hd], acc_all.at[hd]
        kv_rows = pl.ds(pl.multiple_of(k0, TK), width)
        kb = kx_ref[hd, kv_rows, :]
        vb = v_ref[0, kv_rows, hd * LANES:(hd + 1) * LANES]
        s = lax.dot_general(q2_ref[r0:r0 + nr, :], kb, (((1,), (1,)), ((), ())),
                            preferred_element_type=F32)
        if masked_rows:
            s_m = jnp.where(tri, s[:masked_rows], -jnp.inf)
            s = s_m if masked_rows == nr else jnp.concatenate([s_m, s[masked_rows:]], axis=0)
        m_prev = m_ref[r0:r0 + nr]
        m_new = jnp.maximum(m_prev, jnp.max(s, axis=-1, keepdims=True))
        alpha = jnp.exp2(m_prev - m_new)
        p = jnp.exp2(s - jnp.concatenate([m_new] * (width // LANES), axis=1))
        l_ref[r0:r0 + nr] = alpha * l_ref[r0:r0 + nr] + jnp.sum(p, axis=-1, keepdims=True)
        acc_ref[r0:r0 + nr] = alpha * acc_ref[r0:r0 + nr] + jnp.dot(
            p.astype(BF16), vb, preferred_element_type=F32)
        m_ref[r0:r0 + nr] = m_new

    for hd in range(N_HEADS):
        q2_ref, m_ref, l_ref, acc_ref = q2_all.at[hd], m_all.at[hd], l_all.at[hd], acc_all.at[hd]
        qb = q_ref[0, :, hd * LANES:(hd + 1) * LANES]
        zero = jnp.zeros_like(qb)
        q_maps = (jnp.where(first, qb, zero), jnp.where(first, zero, qb))
        qa = qaug_ref[hd]
        for part in range(2):
            rows = slice(part * half, (part + 1) * half)
            for m in range(2):
                dst = slice((2 * part + m) * half, (2 * part + m + 1) * half)
                q2_ref[dst, 0:LANES] = q_maps[m][rows]
                q2_ref[dst, LANES:] = qa[rows]
        m_ref[...] = jnp.full(m_ref.shape, -jnp.inf, F32)
        l_ref[...] = jnp.zeros(l_ref.shape, F32)
        acc_ref[...] = jnp.zeros(acc_ref.shape, F32)

        def body(j, carry, hd=hd):
            step(hd, 0, 4 * half, j * TQ, TQ, 0)
            return carry

        lax.fori_loop(0, qi, body, 0)
        step(hd, 0, 4 * half, qi * TQ, TK, 2 * half)
        step(hd, 2 * half, 2 * half, qi * TQ + TK, TK, 2 * half)

        for part in range(2):
            r = 2 * part * half
            o = (acc_ref[r:r + half] / l_ref[r:r + half]
                 - lam * (acc_ref[r + half:r + 2 * half] / l_ref[r + half:r + 2 * half]))
            o = _rms(o, sg_ref[...]) * (1.0 - lam_init)
            o_ref[0, part * half:(part + 1) * half, hd * LANES:(hd + 1) * LANES] = o.astype(BF16)


def _alibi_aug(s):
    start = 2.0 ** (-8.0 / N_HEADS)
    slopes = np.array([start ** (i + 1) for i in range(N_HEADS)], dtype=np.float32)
    c = math.log2(math.e) * slopes.astype(np.float64)[:, None] * np.arange(s)[None, :]
    pieces = []
    rem = c
    for _ in range(3):
        piece = rem.astype(jnp.bfloat16)
        pieces.append(piece)
        rem = rem - piece.astype(np.float64)
    one = np.ones_like(pieces[0])
    zero = np.zeros((N_HEADS, s, LANES - 6), dtype=jnp.bfloat16)
    kaug = np.concatenate([np.stack(pieces + [one] * 3, axis=-1), zero], axis=-1)
    qaug = np.concatenate([np.stack([one] * 3 + [-p for p in pieces], axis=-1), zero], axis=-1)
    return jnp.asarray(qaug), jnp.asarray(kaug)


def _attention(q, k, v, qaug, kaug, lq1, lk1, lq2, lk2, sub_g, lam_init):
    bsz, s, _ = q.shape
    nq = s // TQ
    fixed2 = lambda b_, i: (0, 0)
    seq = lambda b_, i: (b_, 0, 0)
    vec = pl.BlockSpec((1, HEAD_DIM), fixed2)
    return pl.pallas_call(
        functools.partial(_attn_kernel, lam_init=lam_init),
        grid=(bsz, nq),
        in_specs=[pl.BlockSpec((1, TQ, C_ATTN), lambda b_, i: (b_, i, 0)),
                  pl.BlockSpec((1, s, C_ATTN), seq),
                  pl.BlockSpec((1, s, C_ATTN), seq),
                  pl.BlockSpec((N_HEADS, TQ, LANES), lambda b_, i: (0, i, 0)),
                  pl.BlockSpec((N_HEADS, s, LANES), lambda b_, i: (0, 0, 0)),
                  vec, vec, vec, vec,
                  pl.BlockSpec((1, V_DIM), fixed2)],
        out_specs=pl.BlockSpec((1, TQ, C_ATTN), lambda b_, i: (b_, i, 0)),
        out_shape=jax.ShapeDtypeStruct((bsz, s, C_ATTN), BF16),
        scratch_shapes=[pltpu.VMEM((N_HEADS, s, 2 * LANES), BF16),
                        pltpu.VMEM((N_HEADS, 2 * TQ, 2 * LANES), BF16),
                        pltpu.VMEM((N_HEADS, 2 * TQ, LANES), F32),
                        pltpu.VMEM((N_HEADS, 2 * TQ, LANES), F32),
                        pltpu.VMEM((N_HEADS, 2 * TQ, V_DIM), F32)],
        compiler_params=_params(("parallel", "arbitrary")),
        name="diff_attention",
    )(q, k, v, qaug, kaug, lq1, lk1, lq2, lk2, sub_g)


def _split_bf16(a):
    hi = a.astype(BF16)
    lo = (a - hi.astype(F32)).astype(BF16)
    return hi, lo


def _out_proj_kernel(x_ref, yc_ref, yp_ref, ya_ref, w_ref, g_ref, rw_hi_ref, rw_lo_ref, rb_ref,
                     xm_ref, meta_t_ref, counts_ref, cnt_ref):
    x = _load_tokens(x_ref)
    x = x + jnp.dot(yc_ref[...], w_ref[0:C_CONV, :], preferred_element_type=F32)
    x = x + jnp.dot(yp_ref[...], w_ref[C_CONV:C_CONV + C_POOL, :], preferred_element_type=F32)
    x = x + jnp.dot(ya_ref[...], w_ref[C_CONV + C_POOL:, :], preferred_element_type=F32)
    tm = x.shape[0]
    _store_tokens(xm_ref, x, XM_ROWS)

    hf_hi, hf_lo = _split_bf16(_rms(x, g_ref[...]))

    logits = (jnp.dot(hf_hi, rw_hi_ref[...], preferred_element_type=F32)
              + jnp.dot(hf_lo, rw_hi_ref[...], preferred_element_type=F32)
              + jnp.dot(hf_hi, rw_lo_ref[...], preferred_element_type=F32)) + rb_ref[...]
    lane = lax.broadcasted_iota(jnp.int32, (tm, LANES), 1)
    neg = -jnp.inf

    def first_argmax(val, vmax):
        return jnp.min(jnp.where(val == vmax, lane, LANES), axis=-1, keepdims=True)

    lg = jnp.where(lane < N_GROUPS, logits, neg)
    g_max = jnp.max(lg, axis=-1, keepdims=True)
    g_idx = first_argmax(lg, g_max)
    g_p = 1.0 / jnp.sum(jnp.exp(lg - g_max), axis=-1, keepdims=True)

    e_lane = lane - N_GROUPS
    in_group = (e_lane >= g_idx * PER_GROUP) & (e_lane < (g_idx + 1) * PER_GROUP)
    le = jnp.where(in_group, logits, neg)
    e_max = jnp.max(le, axis=-1, keepdims=True)
    pe = jnp.exp(le - e_max)
    pe = pe / jnp.sum(pe, axis=-1, keepdims=True)
    p1 = jnp.max(pe, axis=-1, keepdims=True)
    i1 = first_argmax(jnp.where(in_group, pe, neg), p1)
    pe2 = jnp.where(in_group & (lane != i1), pe, neg)
    p2 = jnp.max(pe2, axis=-1, keepdims=True)
    i2 = first_argmax(pe2, p2)
    denom = p1 + p2
    gate1 = g_p * (p1 / denom)
    gate2 = g_p * (p2 / denom)

    loc1 = i1 - N_GROUPS - g_idx * PER_GROUP
    loc2 = i2 - N_GROUPS - g_idx * PER_GROUP
    a = jnp.minimum(loc1, loc2)
    b = jnp.maximum(loc1, loc2)
    pair = jnp.where(a == 0, 0, jnp.where(a == 1, 3, 5)) + (b - a - 1)
    bucket = g_idx * N_PAIRS + pair
    gate_a = jnp.where(loc1 < loc2, gate1, gate2)
    gate_b = jnp.where(loc1 < loc2, gate2, gate1)

    @pl.when(pl.program_id(0) == 0)
    def _():
        cnt_ref[...] = jnp.zeros(cnt_ref.shape, F32)

    in_bucket = lane == bucket
    earlier = (lax.broadcasted_iota(jnp.int32, (tm, tm), 1)
               < lax.broadcasted_iota(jnp.int32, (tm, tm), 0)).astype(BF16)
    before = jnp.dot(earlier, in_bucket.astype(BF16), preferred_element_type=F32) + cnt_ref[...]
    rank = jnp.sum(jnp.where(in_bucket, before, 0.0), axis=-1, keepdims=True)
    cnt_ref[...] += jnp.sum(in_bucket.astype(F32), axis=0, keepdims=True)
    counts_ref[...] = cnt_ref[...]

    meta = jnp.where(lane == META_GATE_A, gate_a,
                     jnp.where(lane == META_GATE_B, gate_b,
                               jnp.where(lane == META_BUCKET, bucket.astype(F32),
                                         jnp.where(lane == META_RANK, rank, 0.0))))
    xm_ref[pl.ds(X_ROWS, tm, stride=XM_ROWS), :] = meta
    meta_t_ref[...] = meta.T[0:META_ROWS, :]


def _out_proj(x2, yc, yp, ya, w_bf, g, rw_hi, rw_lo, rb):
    token_major = x2.shape[1] == LANES
    t = x2.shape[0] // X_ROWS if token_major else x2.shape[0]
    tm = TM_PROJ
    row = lambda i: (i, 0)
    fixed = lambda i: (0, 0)
    x_spec = pl.BlockSpec((tm * X_ROWS, LANES) if token_major else (tm, D_MODEL), row)
    return pl.pallas_call(
        _out_proj_kernel,
        grid=(t // tm,),
        in_specs=[x_spec,
                  pl.BlockSpec((tm, C_CONV), row),
                  pl.BlockSpec((tm, C_POOL), row),
                  pl.BlockSpec((tm, C_ATTN), row),
                  pl.BlockSpec((D_MODEL, D_MODEL), fixed),
                  pl.BlockSpec((1, D_MODEL), fixed),
                  pl.BlockSpec((D_MODEL, LANES), fixed),
                  pl.BlockSpec((D_MODEL, LANES), fixed),
                  pl.BlockSpec((1, LANES), fixed)],
        out_specs=[pl.BlockSpec((tm * XM_ROWS, LANES), row),
                   pl.BlockSpec((META_ROWS, tm), row),
                   pl.BlockSpec((1, LANES), fixed)],
        out_shape=[jax.ShapeDtypeStruct((t * XM_ROWS, LANES), F32),
                   jax.ShapeDtypeStruct((t // tm * META_ROWS, tm), F32),
                   jax.ShapeDtypeStruct((1, LANES), F32)],
        scratch_shapes=[pltpu.VMEM((1, LANES), F32)],
        compiler_params=_params(("arbitrary",)),
        name="out_proj_router",
    )(x2, yc, yp, ya, w_bf, g, rw_hi, rw_lo, rb)


def _for_chunk_rows(dest_ref, start_copy, wait_copy):
    base = pl.program_id(0) * COPY_CHUNK

    def body(blk, carry):
        rows = [blk * DMA_UNROLL + u for u in range(DMA_UNROLL)]
        slots = [dest_ref[base + r] for r in rows]
        for u, (r, slot) in enumerate(zip(rows, slots)):
            start_copy(r, slot, u % 2)
        return carry

    lax.fori_loop(0, COPY_CHUNK // DMA_UNROLL, body, 0)
    for piece in range(COPY_CHUNK // COPY_WAIT):
        wait_copy(piece)


def _dispatch_kernel(dest_ref, x_ref, out_hbm, sem):
    def start_copy(r, slot, priority):
        pltpu.make_async_copy(x_ref.at[pl.ds(r * XM_ROWS, XM_ROWS)],
                              out_hbm.at[pl.ds(slot * XM_ROWS, XM_ROWS)],
                              sem.at[0]).start(priority=priority)

    def wait_copy(piece):
        n = COPY_WAIT * XM_ROWS
        pltpu.make_async_copy(x_ref.at[pl.ds(piece * n, n)], out_hbm.at[pl.ds(0, n)],
                              sem.at[0]).wait()

    _for_chunk_rows(dest_ref, start_copy, wait_copy)


def _dispatch(xm, dest):
    n_tokens = xm.shape[0] // XM_ROWS
    n_rows = dest.shape[0]
    assert n_tokens % COPY_CHUNK == 0 and n_rows % COPY_CHUNK == 0
    last = n_tokens // COPY_CHUNK - 1
    grid_spec = pltpu.PrefetchScalarGridSpec(
        num_scalar_prefetch=1,
        grid=(n_rows // COPY_CHUNK,),
        in_specs=[pl.BlockSpec((COPY_CHUNK * XM_ROWS, LANES),
                               lambda i, dest: (jnp.minimum(i, last), 0))],
        out_specs=pl.BlockSpec(memory_space=pl.ANY),
        scratch_shapes=[pltpu.SemaphoreType.DMA((1,))])
    return pl.pallas_call(
        _dispatch_kernel,
        grid_spec=grid_spec,
        out_shape=jax.ShapeDtypeStruct((n_rows * XM_ROWS, LANES), F32),
        compiler_params=_params(("arbitrary",)),
        name="moe_dispatch",
    )(dest, xm)


def _combine_kernel(dest_ref, ys_hbm, o_ref, *scratch, to_rows):
    buf = scratch[0] if to_rows else o_ref
    sem = scratch[-1]

    def start_copy(r, slot, priority):
        pltpu.make_async_copy(ys_hbm.at[pl.ds(slot * X_ROWS, X_ROWS)],
                              buf.at[pl.ds(r * X_ROWS, X_ROWS)],
                              sem.at[0]).start(priority=priority)

    def wait_copy(piece):
        n = COPY_WAIT * X_ROWS
        pltpu.make_async_copy(ys_hbm.at[pl.ds(0, n)], buf.at[pl.ds(piece * n, n)],
                              sem.at[0]).wait()

    _for_chunk_rows(dest_ref, start_copy, wait_copy)
    if to_rows:
        o_ref[...] = _load_tokens(buf)


def _combine(ys, dest, n_tokens, to_rows):
    assert n_tokens % COPY_CHUNK == 0
    chunk_rows = COPY_CHUNK * X_ROWS
    if to_rows:
        out_spec = pl.BlockSpec((COPY_CHUNK, D_MODEL), lambda i, dest: (i, 0))
        out_shape = jax.ShapeDtypeStruct((n_tokens, D_MODEL), F32)
        scratch = [pltpu.VMEM((chunk_rows, LANES), F32)]
    else:
        out_spec = pl.BlockSpec((chunk_rows, LANES), lambda i, dest: (i, 0))
        out_shape = jax.ShapeDtypeStruct((n_tokens * X_ROWS, LANES), F32)
        scratch = []
    grid_spec = pltpu.PrefetchScalarGridSpec(
        num_scalar_prefetch=1,
        grid=(n_tokens // COPY_CHUNK,),
        in_specs=[pl.BlockSpec(memory_space=pl.ANY)],
        out_specs=out_spec,
        scratch_shapes=scratch + [pltpu.SemaphoreType.DMA((1,))])
    return pl.pallas_call(
        functools.partial(_combine_kernel, to_rows=to_rows),
        grid_spec=grid_spec,
        out_shape=out_shape,
        compiler_params=_params(("arbitrary",)),
        name="moe_combine",
    )(dest, ys)


def _moe_kernel(te1_ref, te2_ref, nv_ref, used_ref,
                xs_ref, g_ref, wg1_ref, wu1_ref, wd1_ref, wg2_ref, wu2_ref, wd2_ref, ys_ref):
    i = pl.program_id(0)

    @pl.when(nv_ref[i] > 0)
    def _():
        x = _load_tokens(xs_ref, XM_ROWS)
        meta = xs_ref[pl.ds(X_ROWS, TM_MOE, stride=XM_ROWS), :]
        hf = _rms(x, g_ref[...]).astype(BF16)

        def expert(wg_ref, wu_ref, wd_ref, gate):
            a = jnp.dot(hf, wg_ref[0], preferred_element_type=F32)
            u = jnp.dot(hf, wu_ref[0], preferred_element_type=F32)
            act = a * jax.nn.sigmoid(a) * u * gate
            return jnp.dot(act.astype(BF16), wd_ref[0], preferred_element_type=F32)

        y = expert(wg1_ref, wu1_ref, wd1_ref, meta[:, META_GATE_A:META_GATE_A + 1])
        y = y + expert(wg2_ref, wu2_ref, wd2_ref, meta[:, META_GATE_B:META_GATE_B + 1])
        _store_tokens(ys_ref, x + y)

    @pl.when(nv_ref[i] == 0)
    def _():
        ys_ref[...] = jnp.zeros(ys_ref.shape, F32)


def _moe(xs, g, te1, te2, nv, used, wg_bf, wu_bf, wd_bf):
    tm = TM_MOE
    n_tiles = nv.shape[0]
    fixed = lambda i, te1, te2, nv, used: (0, 0)
    tile_in = lambda i, te1, te2, nv, used: (jnp.minimum(i, used[0] - 1), 0)
    tile_out = lambda i, te1, te2, nv, used: (i, 0)
    expert1 = lambda i, te1, te2, nv, used: (te1[i], 0, 0)
    expert2 = lambda i, te1, te2, nv, used: (te2[i], 0, 0)
    w_in = lambda index_map: pl.BlockSpec((1, D_MODEL, D_EXPERT), index_map)
    w_out = lambda index_map: pl.BlockSpec((1, D_EXPERT, D_MODEL), index_map)
    grid_spec = pltpu.PrefetchScalarGridSpec(
        num_scalar_prefetch=4,
        grid=(n_tiles,),
        in_specs=[pl.BlockSpec((tm * XM_ROWS, LANES), tile_in),
                  pl.BlockSpec((1, D_MODEL), fixed),
                  w_in(expert1), w_in(expert1), w_out(expert1),
                  w_in(expert2), w_in(expert2), w_out(expert2)],
        out_specs=pl.BlockSpec((tm * X_ROWS, LANES), tile_out))
    return pl.pallas_call(
        _moe_kernel,
        grid_spec=grid_spec,
        out_shape=jax.ShapeDtypeStruct((n_tiles * tm * X_ROWS, LANES), F32),
        compiler_params=_params(("arbitrary",)),
        name="moe",
    )(te1, te2, nv, used, xs, g, wg_bf, wu_bf, wd_bf, wg_bf, wu_bf, wd_bf)


def _route(meta_t, counts, t):
    tm = TM_MOE
    nb = N_GROUPS * N_PAIRS
    n_tiles = t // tm + nb
    meta = meta_t.reshape(t // TM_PROJ, META_ROWS, TM_PROJ)
    bucket = meta[:, META_BUCKET, :].reshape(t).astype(jnp.int32)
    rank = meta[:, META_RANK, :].reshape(t).astype(jnp.int32)
    counts = counts[0, :nb].astype(jnp.int32)
    ids = jnp.arange(nb, dtype=jnp.int32)

    tiles = (counts + tm - 1) // tm
    tile_end = jnp.cumsum(tiles)
    tile_start = tile_end - tiles
    used = tile_end[-1]
    row0 = tile_start * tm
    dest = rank + jnp.sum(jnp.where(bucket[:, None] == ids[None, :], row0[None, :], 0), axis=1)

    ti = jnp.arange(n_tiles, dtype=jnp.int32)
    tic = jnp.minimum(ti, used - 1)
    tb = jnp.sum((tile_end[None, :] <= tic[:, None]).astype(jnp.int32), axis=1)
    onehot_tb = tb[:, None] == ids[None, :]
    pick = lambda table: jnp.sum(jnp.where(onehot_tb, table[None, :], 0), axis=1)
    nv = jnp.where(ti < used, jnp.clip(pick(counts) - (tic - pick(tile_start)) * tm, 0, tm), 0)
    pair_a = jnp.asarray([0, 0, 0, 1, 1, 2], jnp.int32)
    pair_b = jnp.asarray([1, 2, 3, 2, 3, 3], jnp.int32)
    te1 = pick((ids // N_PAIRS) * PER_GROUP + pair_a[ids % N_PAIRS])
    te2 = pick((ids // N_PAIRS) * PER_GROUP + pair_b[ids % N_PAIRS])

    pads = tiles * tm - counts
    pad_end = jnp.cumsum(pads)
    k = jnp.arange(nb * tm, dtype=jnp.int32)
    kb = jnp.minimum(jnp.sum((pad_end[None, :] <= k[:, None]).astype(jnp.int32), axis=1), nb - 1)
    onehot_kb = kb[:, None] == ids[None, :]
    pick_k = lambda table: jnp.sum(jnp.where(onehot_kb, table[None, :], 0), axis=1)
    in_pad = pick_k(row0 + counts) + (k - pick_k(pad_end - pads))
    free_rows = jnp.where(k < pad_end[-1], in_pad, used * tm + (k - pad_end[-1]))
    i32 = lambda a: a.astype(jnp.int32)
    return (i32(jnp.concatenate([dest, free_rows])), i32(te1), i32(te2), i32(nv),
            i32(used).reshape(1))


def _block_diag(w):
    n = w.shape[0]
    eye = jnp.eye(n, dtype=w.dtype)
    return jnp.einsum("gcd,gh->gchd", w, eye).reshape(n * POOL_GROUP, n * POOL_GROUP)


def kernel(x, attn_norm_g, w_in, conv_w, conv_b, conv_ln_g, conv_ln_b, conv_pw_w, conv_pw_b,
           pool_w, pool_scale, q_norm_g, k_norm_g, lambda_q1, lambda_k1, lambda_q2, lambda_k2,
           attn_sub_norm_g, w_out, ffn_norm_g, router_g_w, router_g_b, router_e_w, router_e_b,
           w_gate, w_up, w_down):
    bsz, s, d = x.shape
    depth = w_in.shape[0]
    t = bsz * s
    qaug, kaug = _alibi_aug(s)
    row = lambda a: a.reshape(1, -1)

    x2 = x.reshape(t, d)
    for l in range(depth):
        lam_init = 0.8 - 0.6 * math.exp(-0.3 * l)
        z_conv, p_in, q, k, v = _in_proj(
            x2, row(attn_norm_g[l]), w_in[l].astype(BF16),
            row(jnp.tile(q_norm_g[l], 2)), row(jnp.tile(k_norm_g[l], 2)))

        y_conv = _conv_mixer(z_conv.reshape(bsz, s, -1), conv_w[l], row(conv_b[l]),
                             row(conv_ln_g[l]), row(conv_ln_b[l]),
                             conv_pw_w[l].astype(BF16), row(conv_pw_b[l]))
        y_pool = _pool_mixer(p_in.reshape(bsz, s, -1), _block_diag(pool_w[l]).astype(BF16),
                             row(pool_scale[l]))
        y_attn = _attention(q.reshape(bsz, s, -1), k.reshape(bsz, s, -1), v.reshape(bsz, s, -1),
                            qaug, kaug, row(lambda_q1[l]), row(lambda_k1[l]), row(lambda_q2[l]),
                            row(lambda_k2[l]), row(attn_sub_norm_g[l]), lam_init)

        pad = LANES - N_GROUPS - N_EXPERTS
        rw = jnp.pad(jnp.concatenate([router_g_w[l], router_e_w[l]], axis=1), ((0, 0), (0, pad)))
        rb = jnp.pad(jnp.concatenate([router_g_b[l], router_e_b[l]]), (0, pad))
        rw_hi = rw.astype(BF16)
        rw_lo = (rw - rw_hi.astype(F32)).astype(BF16)
        xm, meta_t, counts = _out_proj(
            x2, y_conv.reshape(t, -1), y_pool.reshape(t, -1), y_attn.reshape(t, -1),
            w_out[l].astype(BF16), row(ffn_norm_g[l]), rw_hi, rw_lo, row(rb))

        dest, te1, te2, nv, used = _route(meta_t, counts, t)
        xs = _dispatch(xm, dest)
        ys = _moe(xs, row(ffn_norm_g[l]), te1, te2, nv, used, w_gate[l].astype(BF16),
                  w_up[l].astype(BF16), w_down[l].astype(BF16))
        x2 = _combine(ys, dest, t, to_rows=l == depth - 1)
    return x2.reshape(bsz, s, d)
```

```python
import functools
import math

import jax
import jax.numpy as jnp
import numpy as np
from jax import lax
from jax.experimental import pallas as pl
from jax.experimental.pallas import tpu as pltpu

D_MODEL = 1024
C_CONV = 256
C_POOL = 256
C_ATTN = 512
N_HEADS = 4
HEAD_DIM = 64
V_DIM = 2 * HEAD_DIM
CONV_WIDTH = 31
POOL_WINDOWS = (2, 4, 8, 16)
POOL_GROUP = C_POOL // len(POOL_WINDOWS)
N_IN = 2 * C_CONV + C_POOL + 3 * C_ATTN
N_GROUPS = 4
PER_GROUP = 4
N_EXPERTS = N_GROUPS * PER_GROUP
D_EXPERT = 256
EPS = 1e-6

LANES = 128
SUBLANES = 8
VMEM_LIMIT = 48 * 1024 * 1024

TM_PROJ = 512
TS_MIX = 512
HALO = 32
TQ = 512
TK = 256
TM_MOE = 256
N_PAIRS = 6
X_ROWS = D_MODEL // LANES
XM_ROWS = X_ROWS + 1
META_GATE_A, META_GATE_B, META_BUCKET, META_RANK = 0, 1, 2, 3
META_ROWS = 8
DMA_UNROLL = 8
COPY_CHUNK = 2048
COPY_WAIT = 512

F32 = jnp.float32
BF16 = jnp.bfloat16


def _params(sem):
    return pltpu.CompilerParams(dimension_semantics=sem, vmem_limit_bytes=VMEM_LIMIT)


def _rms(x, g):
    return x * lax.rsqrt(jnp.mean(x * x, axis=-1, keepdims=True) + EPS) * g


def _load_tokens(x_ref, rows_per_token=X_ROWS):
    if x_ref.shape[-1] != LANES:
        return x_ref[...]
    tm = x_ref.shape[0] // rows_per_token
    return jnp.concatenate(
        [x_ref[pl.ds(c, tm, stride=rows_per_token), :] for c in range(X_ROWS)], axis=1)


def _store_tokens(o_ref, x, rows_per_token=X_ROWS):
    tm = x.shape[0]
    for c in range(X_ROWS):
        o_ref[pl.ds(c, tm, stride=rows_per_token), :] = x[:, c * LANES:(c + 1) * LANES]


def _in_proj_kernel(x_ref, g_ref, w_ref, qg_ref, kg_ref,
                    conv_ref, pool_ref, q_ref, k_ref, v_ref):
    h = _rms(_load_tokens(x_ref), g_ref[...]).astype(BF16)

    def proj(c0, width):
        return jnp.dot(h, w_ref[:, c0:c0 + width], preferred_element_type=F32)

    conv_ref[...] = proj(0, 2 * C_CONV).astype(BF16)
    pool_ref[...] = proj(2 * C_CONV, C_POOL).astype(BF16)

    lane = lax.broadcasted_iota(jnp.int32, (1, LANES), 1)
    first = lane < HEAD_DIM

    def qk_norm(z, gain_ref, out_ref, post_scale):
        gain = gain_ref[...]
        for hd in range(N_HEADS):
            blk = z[:, hd * LANES:(hd + 1) * LANES]
            sq = blk * blk
            s_all = jnp.sum(sq, axis=-1, keepdims=True)
            s_lo = jnp.sum(jnp.where(first, sq, 0.0), axis=-1, keepdims=True)
            r_lo = lax.rsqrt(s_lo * (1.0 / HEAD_DIM) + EPS)
            r_hi = lax.rsqrt((s_all - s_lo) * (1.0 / HEAD_DIM) + EPS)
            r = jnp.where(first, r_lo, r_hi)
            out_ref[:, hd * LANES:(hd + 1) * LANES] = (blk * r * gain * post_scale).astype(BF16)

    c_q = 2 * C_CONV + C_POOL
    qk_norm(proj(c_q, C_ATTN), qg_ref, q_ref, HEAD_DIM ** -0.5 * math.log2(math.e))
    qk_norm(proj(c_q + C_ATTN, C_ATTN), kg_ref, k_ref, 1.0)
    v_ref[...] = proj(c_q + 2 * C_ATTN, C_ATTN).astype(BF16)


def _in_proj(x2, g, w_bf, qg, kg):
    token_major = x2.shape[1] == LANES
    t = x2.shape[0] // X_ROWS if token_major else x2.shape[0]
    tm = TM_PROJ
    row = lambda i: (i, 0)
    fixed = lambda i: (0, 0)
    x_spec = pl.BlockSpec((tm * X_ROWS, LANES) if token_major else (tm, D_MODEL), row)
    outs = [jax.ShapeDtypeStruct((t, 2 * C_CONV), BF16),
            jax.ShapeDtypeStruct((t, C_POOL), BF16),
            jax.ShapeDtypeStruct((t, C_ATTN), BF16),
            jax.ShapeDtypeStruct((t, C_ATTN), BF16),
            jax.ShapeDtypeStruct((t, C_ATTN), BF16)]
    return pl.pallas_call(
        _in_proj_kernel,
        grid=(t // tm,),
        in_specs=[x_spec,
                  pl.BlockSpec((1, D_MODEL), fixed),
                  pl.BlockSpec((D_MODEL, N_IN), fixed),
                  pl.BlockSpec((1, LANES), fixed),
                  pl.BlockSpec((1, LANES), fixed)],
        out_specs=[pl.BlockSpec((tm, 2 * C_CONV), row),
                   pl.BlockSpec((tm, C_POOL), row),
                   pl.BlockSpec((tm, C_ATTN), row),
                   pl.BlockSpec((tm, C_ATTN), row),
                   pl.BlockSpec((tm, C_ATTN), row)],
        out_shape=outs,
        compiler_params=_params(("parallel",)),
        name="in_proj",
    )(x2, g, w_bf, qg, kg)


def _conv_kernel(z_ref, halo_ref, w_ref, b_ref, lng_ref, lnb_ref, pw_ref, pwb_ref,
                 o_ref, ext_ref):
    ts = z_ref.shape[1]

    def glu(z):
        z = z.astype(F32)
        return z[:, :C_CONV] * jax.nn.sigmoid(z[:, C_CONV:])

    halo = glu(halo_ref[0])
    ext_ref[0, 0:HALO, :] = jnp.where(pl.program_id(1) == 0, 0.0, halo)
    ext_ref[0, HALO:, :] = glu(z_ref[0])
    n_shifted = ts + HALO - SUBLANES
    for s in range(1, SUBLANES):
        ext_ref[s, 0:n_shifted, :] = ext_ref[0, s:s + n_shifted, :]

    base = HALO - (CONV_WIDTH - 1)
    rows = 64
    for c in range(ts // rows):
        acc = jnp.broadcast_to(b_ref[...], (rows, C_CONV))
        for j in range(CONV_WIDTH):
            phase = (base + j) % SUBLANES
            r0 = c * rows + base + j - phase
            acc = acc + ext_ref[phase, r0:r0 + rows, :] * w_ref[j:j + 1, :]
        mu = jnp.mean(acc, axis=-1, keepdims=True)
        xc = acc - mu
        y = xc * lax.rsqrt(jnp.mean(xc * xc, axis=-1, keepdims=True) + EPS)
        y = y * lng_ref[...] + lnb_ref[...]
        y = y * jax.nn.sigmoid(y)
        out = jnp.dot(y.astype(BF16), pw_ref[...], preferred_element_type=F32) + pwb_ref[...]
        o_ref[0, c * rows:(c + 1) * rows, :] = out.astype(BF16)


def _conv_mixer(z_conv, w, b, ln_g, ln_b, pw_bf, pw_b):
    bsz, s, _ = z_conv.shape
    ts = TS_MIX
    per = ts // HALO
    fixed = lambda b_, i: (0, 0)
    return pl.pallas_call(
        _conv_kernel,
        grid=(bsz, s // ts),
        in_specs=[pl.BlockSpec((1, ts, 2 * C_CONV), lambda b_, i: (b_, i, 0)),
                  pl.BlockSpec((1, HALO, 2 * C_CONV),
                               lambda b_, i: (b_, jnp.maximum(i * per - 1, 0), 0)),
                  pl.BlockSpec((CONV_WIDTH, C_CONV), fixed),
                  pl.BlockSpec((1, C_CONV), fixed),
                  pl.BlockSpec((1, C_CONV), fixed),
                  pl.BlockSpec((1, C_CONV), fixed),
                  pl.BlockSpec((C_CONV, C_CONV), fixed),
                  pl.BlockSpec((1, C_CONV), fixed)],
        out_specs=pl.BlockSpec((1, ts, C_CONV), lambda b_, i: (b_, i, 0)),
        out_shape=jax.ShapeDtypeStruct((bsz, s, C_CONV), BF16),
        scratch_shapes=[pltpu.VMEM((SUBLANES, ts + HALO, C_CONV), F32)],
        compiler_params=_params(("parallel", "parallel")),
        name="conv_mixer",
    )(z_conv, z_conv, w, b, ln_g, ln_b, pw_bf, pw_b)


def _pool_kernel(u_ref, halo_ref, w_ref, scale_ref, o_ref, ext_ref):
    ts = u_ref.shape[1]
    i = pl.program_id(1)
    n = ts + HALO
    ext_ref[0, 0:HALO, :] = jnp.where(i == 0, 0.0, halo_ref[0].astype(F32))
    u = u_ref[0].astype(F32)
    ext_ref[0, HALO:, :] = u

    assert POOL_WINDOWS == (2, 4, 8, 16) and HALO >= 4 * SUBLANES
    sums = []
    for k in range(1, 5):
        lo = SUBLANES * k
        shift = 1 << (k - 1)
        level = ext_ref[k - 1, lo:n, :] + ext_ref[k - 1, lo - shift:n - shift, :]
        sums.append(level[HALO - lo:, :])
        if k < 4:
            ext_ref[k, lo:n, :] = level

    lane = lax.broadcasted_iota(jnp.int32, (1, C_POOL), 1)
    group = lane // POOL_GROUP
    win = jnp.left_shift(2, group)
    acc = jnp.where(group == 0, sums[0],
                    jnp.where(group == 1, sums[1], jnp.where(group == 2, sums[2], sums[3])))
    pos = i * ts + lax.broadcasted_iota(jnp.int32, (ts, 1), 0)
    count = jnp.minimum(pos + 1, win).astype(F32)
    diff = acc / count - u
    out = jnp.dot(diff.astype(BF16), w_ref[...], preferred_element_type=F32) * scale_ref[...]
    o_ref[0] = out.astype(BF16)


def _pool_mixer(p_in, w_blockdiag_bf, scale):
    bsz, s, _ = p_in.shape
    ts = TS_MIX
    per = ts // HALO
    fixed = lambda b_, i: (0, 0)
    return pl.pallas_call(
        _pool_kernel,
        grid=(bsz, s // ts),
        in_specs=[pl.BlockSpec((1, ts, C_POOL), lambda b_, i: (b_, i, 0)),
                  pl.BlockSpec((1, HALO, C_POOL),
                               lambda b_, i: (b_, jnp.maximum(i * per - 1, 0), 0)),
                  pl.BlockSpec((C_POOL, C_POOL), fixed),
                  pl.BlockSpec((1, C_POOL), fixed)],
        out_specs=pl.BlockSpec((1, ts, C_POOL), lambda b_, i: (b_, i, 0)),
        out_shape=jax.ShapeDtypeStruct((bsz, s, C_POOL), BF16),
        scratch_shapes=[pltpu.VMEM((4, ts + HALO, C_POOL), F32)],
        compiler_params=_params(("parallel", "parallel")),
        name="pool_mixer",
    )(p_in, p_in, w_blockdiag_bf, scale)


def _attn_kernel(q_ref, k_ref, v_ref, qaug_ref, kaug_ref, lq1_ref, lk1_ref, lq2_ref, lk2_ref,
                 sg_ref, o_ref, kx_ref, q2_all, m_all, l_all, acc_all, *, lam_init):
    qi = pl.program_id(1)
    half = TQ // 2
    assert half == TK

    @pl.when(qi == 0)
    def _():
        for hd in range(N_HEADS):
            kx_ref[hd, :, 0:LANES] = k_ref[0, :, hd * LANES:(hd + 1) * LANES]
            kx_ref[hd, :, LANES:] = kaug_ref[hd]

    lam = (jnp.exp(jnp.sum(lq1_ref[...] * lk1_ref[...], axis=-1, keepdims=True))
           - jnp.exp(jnp.sum(lq2_ref[...] * lk2_ref[...], axis=-1, keepdims=True))
           + lam_init)
    first = lax.broadcasted_iota(jnp.int32, (1, LANES), 1) < HEAD_DIM
    row = lax.broadcasted_iota(jnp.int32, (2 * half, TK), 0)
    col = lax.broadcasted_iota(jnp.int32, (2 * half, TK), 1)
    tri = col <= (row & (half - 1))

    def step(hd, r0, nr, k0, width, masked, first_block):
        q2_ref, m_ref, l_ref, acc_ref = q2_all.at[hd], m_all.at[hd], l_all.at[hd], acc_all.at[hd]
        kv_rows = pl.ds(pl.multiple_of(k0, TK), width)
        kb = kx_ref[hd, kv_rows, :]
        vb = v_ref[0, kv_rows, hd * LANES:(hd + 1) * LANES]
        s = lax.dot_general(q2_ref[r0:r0 + nr, :], kb, (((1,), (1,)), ((), ())),
                            preferred_element_type=F32)
        if masked:
            s = jnp.where(tri, s, -jnp.inf)
        s_max = jnp.max(s, axis=-1, keepdims=True)
        if first_block:
            m_new = jnp.broadcast_to(s_max, (nr, LANES))
        else:
            m_prev = m_ref[r0:r0 + nr]
            m_new = jnp.maximum(m_prev, s_max)
            alpha = jnp.exp2(m_prev - m_new)
        p = jnp.exp2(s - jnp.concatenate([m_new] * (width // LANES), axis=1))
        l_new = jnp.sum(p, axis=-1, keepdims=True)
        acc_new = jnp.dot(p.astype(BF16), vb, preferred_element_type=F32)
        if first_block:
            l_ref[r0:r0 + nr] = jnp.broadcast_to(l_new, (nr, LANES))
            acc_ref[r0:r0 + nr] = acc_new
        else:
            l_ref[r0:r0 + nr] = alpha * l_ref[r0:r0 + nr] + l_new
            acc_ref[r0:r0 + nr] = alpha * acc_ref[r0:r0 + nr] + acc_new
        m_ref[r0:r0 + nr] = m_new

    for hd in range(N_HEADS):
        q2_ref, m_ref, l_ref, acc_ref = q2_all.at[hd], m_all.at[hd], l_all.at[hd], acc_all.at[hd]
        qb = q_ref[0, :, hd * LANES:(hd + 1) * LANES]
        zero = jnp.zeros_like(qb)
        q_maps = (jnp.where(first, qb, zero), jnp.where(first, zero, qb))
        qa = qaug_ref[hd]
        for part in range(2):
            rows = slice(part * half, (part + 1) * half)
            for m in range(2):
                dst = slice((2 * part + m) * half, (2 * part + m + 1) * half)
                q2_ref[dst, 0:LANES] = q_maps[m][rows]
                q2_ref[dst, LANES:] = qa[rows]

        step(hd, 0, 2 * half, qi * TQ, TK, True, True)
        step(hd, 2 * half, 2 * half, qi * TQ, TK, False, True)
        step(hd, 2 * half, 2 * half, qi * TQ + TK, TK, True, False)

        def body(j, carry, hd=hd):
            step(hd, 0, 4 * half, 2 * j * TQ, 2 * TQ, False, False)
            return carry

        lax.fori_loop(0, qi // 2, body, 0)

        @pl.when(qi % 2 == 1)
        def _(hd=hd):
            step(hd, 0, 4 * half, (qi - 1) * TQ, TQ, False, False)

        for part in range(2):
            r = 2 * part * half
            o = (acc_ref[r:r + half] / l_ref[r:r + half]
                 - lam * (acc_ref[r + half:r + 2 * half] / l_ref[r + half:r + 2 * half]))
            o = _rms(o, sg_ref[...]) * (1.0 - lam_init)
            o_ref[0, part * half:(part + 1) * half, hd * LANES:(hd + 1) * LANES] = o.astype(BF16)


def _alibi_aug(s):
    start = 2.0 ** (-8.0 / N_HEADS)
    slopes = np.array([start ** (i + 1) for i in range(N_HEADS)], dtype=np.float32)
    c = math.log2(math.e) * slopes.astype(np.float64)[:, None] * np.arange(s)[None, :]
    pieces = []
    rem = c
    for _ in range(3):
        piece = rem.astype(jnp.bfloat16)
        pieces.append(piece)
        rem = rem - piece.astype(np.float64)
    one = np.ones_like(pieces[0])
    zero = np.zeros((N_HEADS, s, LANES - 6), dtype=jnp.bfloat16)
    kaug = np.concatenate([np.stack(pieces + [one] * 3, axis=-1), zero], axis=-1)
    qaug = np.concatenate([np.stack([one] * 3 + [-p for p in pieces], axis=-1), zero], axis=-1)
    return jnp.asarray(qaug), jnp.asarray(kaug)


def _attention(q, k, v, qaug, kaug, lq1, lk1, lq2, lk2, sub_g, lam_init):
    bsz, s, _ = q.shape
    nq = s // TQ
    fixed2 = lambda b_, i: (0, 0)
    seq = lambda b_, i: (b_, 0, 0)
    vec = pl.BlockSpec((1, HEAD_DIM), fixed2)
    return pl.pallas_call(
        functools.partial(_attn_kernel, lam_init=lam_init),
        grid=(bsz, nq),
        in_specs=[pl.BlockSpec((1, TQ, C_ATTN), lambda b_, i: (b_, i, 0)),
                  pl.BlockSpec((1, s, C_ATTN), seq),
                  pl.BlockSpec((1, s, C_ATTN), seq),
                  pl.BlockSpec((N_HEADS, TQ, LANES), lambda b_, i: (0, i, 0)),
                  pl.BlockSpec((N_HEADS, s, LANES), lambda b_, i: (0, 0, 0)),
                  vec, vec, vec, vec,
                  pl.BlockSpec((1, V_DIM), fixed2)],
        out_specs=pl.BlockSpec((1, TQ, C_ATTN), lambda b_, i: (b_, i, 0)),
        out_shape=jax.ShapeDtypeStruct((bsz, s, C_ATTN), BF16),
        scratch_shapes=[pltpu.VMEM((N_HEADS, s, 2 * LANES), BF16),
                        pltpu.VMEM((N_HEADS, 2 * TQ, 2 * LANES), BF16),
                        pltpu.VMEM((N_HEADS, 2 * TQ, LANES), F32),
                        pltpu.VMEM((N_HEADS, 2 * TQ, LANES), F32),
                        pltpu.VMEM((N_HEADS, 2 * TQ, V_DIM), F32)],
        compiler_params=_params(("parallel", "arbitrary")),
        name="diff_attention",
    )(q, k, v, qaug, kaug, lq1, lk1, lq2, lk2, sub_g)


def _split_bf16(a):
    hi = a.astype(BF16)
    lo = (a - hi.astype(F32)).astype(BF16)
    return hi, lo


def _out_proj_kernel(x_ref, yc_ref, yp_ref, ya_ref, w_ref, g_ref, rw_ref, rb_ref,
                     xm_ref, meta_t_ref, counts_ref, cnt_ref):
    x = _load_tokens(x_ref)
    x = x + jnp.dot(yc_ref[...], w_ref[0:C_CONV, :], preferred_element_type=F32)
    x = x + jnp.dot(yp_ref[...], w_ref[C_CONV:C_CONV + C_POOL, :], preferred_element_type=F32)
    x = x + jnp.dot(ya_ref[...], w_ref[C_CONV + C_POOL:, :], preferred_element_type=F32)
    tm = x.shape[0]
    _store_tokens(xm_ref, x, XM_ROWS)

    hf_hi, hf_lo = _split_bf16(_rms(x, g_ref[...]))

    both = jnp.dot(hf_hi, rw_ref[...], preferred_element_type=F32)
    logits = (both[:, 0:LANES] + both[:, LANES:]
              + jnp.dot(hf_lo, rw_ref[:, 0:LANES], preferred_element_type=F32)) + rb_ref[...]
    lane = lax.broadcasted_iota(jnp.int32, (tm, LANES), 1)
    lane_f = lane.astype(F32)
    neg = -jnp.inf

    def first_argmax(val, vmax):
        first = jnp.min(jnp.where(val == vmax, lane_f, float(LANES)), axis=-1, keepdims=True)
        return first.astype(jnp.int32)

    lg = jnp.where(lane < N_GROUPS, logits, neg)
    g_max = jnp.max(lg, axis=-1, keepdims=True)
    g_idx = first_argmax(lg, g_max)
    g_p = 1.0 / jnp.sum(jnp.exp(lg - g_max), axis=-1, keepdims=True)

    e_lane = lane - N_GROUPS
    in_group = (e_lane >= g_idx * PER_GROUP) & (e_lane < (g_idx + 1) * PER_GROUP)
    le = jnp.where(in_group, logits, neg)
    e_max = jnp.max(le, axis=-1, keepdims=True)
    pe = jnp.exp(le - e_max)
    pe = pe / jnp.sum(pe, axis=-1, keepdims=True)
    p1 = jnp.max(pe, axis=-1, keepdims=True)
    i1 = first_argmax(jnp.where(in_group, pe, neg), p1)
    pe2 = jnp.where(in_group & (lane != i1), pe, neg)
    p2 = jnp.max(pe2, axis=-1, keepdims=True)
    i2 = first_argmax(pe2, p2)
    denom = p1 + p2
    gate1 = g_p * (p1 / denom)
    gate2 = g_p * (p2 / denom)

    loc1 = i1 - N_GROUPS - g_idx * PER_GROUP
    loc2 = i2 - N_GROUPS - g_idx * PER_GROUP
    a = jnp.minimum(loc1, loc2)
    b = jnp.maximum(loc1, loc2)
    pair = jnp.where(a == 0, 0, jnp.where(a == 1, 3, 5)) + (b - a - 1)
    bucket = g_idx * N_PAIRS + pair
    gate_a = jnp.where(loc1 < loc2, gate1, gate2)
    gate_b = jnp.where(loc1 < loc2, gate2, gate1)

    @pl.when(pl.program_id(0) == 0)
    def _():
        cnt_ref[...] = jnp.zeros(cnt_ref.shape, F32)

    in_bucket = lane == bucket
    earlier = (lax.broadcasted_iota(jnp.int32, (tm, tm), 1)
               < lax.broadcasted_iota(jnp.int32, (tm, tm), 0)).astype(BF16)
    before = jnp.dot(earlier, in_bucket.astype(BF16), preferred_element_type=F32) + cnt_ref[...]
    rank = jnp.sum(jnp.where(in_bucket, before, 0.0), axis=-1, keepdims=True)
    cnt_ref[...] += jnp.sum(in_bucket.astype(F32), axis=0, keepdims=True)
    counts_ref[...] = cnt_ref[...]

    meta = jnp.where(lane == META_GATE_A, gate_a,
                     jnp.where(lane == META_GATE_B, gate_b,
                               jnp.where(lane == META_BUCKET, bucket.astype(F32),
                                         jnp.where(lane == META_RANK, rank, 0.0))))
    xm_ref[pl.ds(X_ROWS, tm, stride=XM_ROWS), :] = meta
    meta_t_ref[...] = meta.T[0:META_ROWS, :]


def _out_proj(x2, yc, yp, ya, w_bf, g, rw_hi_lo, rb):
    token_major = x2.shape[1] == LANES
    t = x2.shape[0] // X_ROWS if token_major else x2.shape[0]
    tm = TM_PROJ
    row = lambda i: (i, 0)
    fixed = lambda i: (0, 0)
    x_spec = pl.BlockSpec((tm * X_ROWS, LANES) if token_major else (tm, D_MODEL), row)
    return pl.pallas_call(
        _out_proj_kernel,
        grid=(t // tm,),
        in_specs=[x_spec,
                  pl.BlockSpec((tm, C_CONV), row),
                  pl.BlockSpec((tm, C_POOL), row),
                  pl.BlockSpec((tm, C_ATTN), row),
                  pl.BlockSpec((D_MODEL, D_MODEL), fixed),
                  pl.BlockSpec((1, D_MODEL), fixed),
                  pl.BlockSpec((D_MODEL, 2 * LANES), fixed),
                  pl.BlockSpec((1, LANES), fixed)],
        out_specs=[pl.BlockSpec((tm * XM_ROWS, LANES), row),
                   pl.BlockSpec((META_ROWS, tm), row),
                   pl.BlockSpec((1, LANES), fixed)],
        out_shape=[jax.ShapeDtypeStruct((t * XM_ROWS, LANES), F32),
                   jax.ShapeDtypeStruct((t // tm * META_ROWS, tm), F32),
                   jax.ShapeDtypeStruct((1, LANES), F32)],
        scratch_shapes=[pltpu.VMEM((1, LANES), F32)],
        compiler_params=_params(("arbitrary",)),
        name="out_proj_router",
    )(x2, yc, yp, ya, w_bf, g, rw_hi_lo, rb)


def _for_chunk_rows(dest_ref, start_copy, wait_copy):
    base = pl.program_id(0) * COPY_CHUNK

    def body(blk, carry):
        rows = [blk * DMA_UNROLL + u for u in range(DMA_UNROLL)]
        slots = [dest_ref[base + r] for r in rows]
        for u, (r, slot) in enumerate(zip(rows, slots)):
            start_copy(r, slot, u % 2)
        return carry

    lax.fori_loop(0, COPY_CHUNK // DMA_UNROLL, body, 0)
    for piece in range(COPY_CHUNK // COPY_WAIT):
        wait_copy(piece)


def _dispatch_kernel(dest_ref, x_ref, out_hbm, sem):
    def start_copy(r, slot, priority):
        pltpu.make_async_copy(x_ref.at[pl.ds(r * XM_ROWS, XM_ROWS)],
                              out_hbm.at[pl.ds(slot * XM_ROWS, XM_ROWS)],
                              sem.at[0]).start(priority=priority)

    def wait_copy(piece):
        n = COPY_WAIT * XM_ROWS
        pltpu.make_async_copy(x_ref.at[pl.ds(piece * n, n)], out_hbm.at[pl.ds(0, n)],
                              sem.at[0]).wait()

    _for_chunk_rows(dest_ref, start_copy, wait_copy)


def _dispatch(xm, dest):
    n_tokens = xm.shape[0] // XM_ROWS
    n_rows = dest.shape[0]
    assert n_tokens % COPY_CHUNK == 0 and n_rows % COPY_CHUNK == 0
    last = n_tokens // COPY_CHUNK - 1
    grid_spec = pltpu.PrefetchScalarGridSpec(
        num_scalar_prefetch=1,
        grid=(n_rows // COPY_CHUNK,),
        in_specs=[pl.BlockSpec((COPY_CHUNK * XM_ROWS, LANES),
                               lambda i, dest: (jnp.minimum(i, last), 0))],
        out_specs=pl.BlockSpec(memory_space=pl.ANY),
        scratch_shapes=[pltpu.SemaphoreType.DMA((1,))])
    return pl.pallas_call(
        _dispatch_kernel,
        grid_spec=grid_spec,
        out_shape=jax.ShapeDtypeStruct((n_rows * XM_ROWS, LANES), F32),
        compiler_params=_params(("arbitrary",)),
        name="moe_dispatch",
    )(dest, xm)


def _combine_kernel(dest_ref, ys_hbm, o_ref, *scratch, to_rows):
    buf = scratch[0] if to_rows else o_ref
    sem = scratch[-1]

    def start_copy(r, slot, priority):
        pltpu.make_async_copy(ys_hbm.at[pl.ds(slot * X_ROWS, X_ROWS)],
                              buf.at[pl.ds(r * X_ROWS, X_ROWS)],
                              sem.at[0]).start(priority=priority)

    def wait_copy(piece):
        n = COPY_WAIT * X_ROWS
        pltpu.make_async_copy(ys_hbm.at[pl.ds(0, n)], buf.at[pl.ds(piece * n, n)],
                              sem.at[0]).wait()

    _for_chunk_rows(dest_ref, start_copy, wait_copy)
    if to_rows:
        o_ref[...] = _load_tokens(buf)


def _combine(ys, dest, n_tokens, to_rows):
    assert n_tokens % COPY_CHUNK == 0
    chunk_rows = COPY_CHUNK * X_ROWS
    if to_rows:
        out_spec = pl.BlockSpec((COPY_CHUNK, D_MODEL), lambda i, dest: (i, 0))
        out_shape = jax.ShapeDtypeStruct((n_tokens, D_MODEL), F32)
        scratch = [pltpu.VMEM((chunk_rows, LANES), F32)]
    else:
        out_spec = pl.BlockSpec((chunk_rows, LANES), lambda i, dest: (i, 0))
        out_shape = jax.ShapeDtypeStruct((n_tokens * X_ROWS, LANES), F32)
        scratch = []
    grid_spec = pltpu.PrefetchScalarGridSpec(
        num_scalar_prefetch=1,
        grid=(n_tokens // COPY_CHUNK,),
        in_specs=[pl.BlockSpec(memory_space=pl.ANY)],
        out_specs=out_spec,
        scratch_shapes=scratch + [pltpu.SemaphoreType.DMA((1,))])
    return pl.pallas_call(
        functools.partial(_combine_kernel, to_rows=to_rows),
        grid_spec=grid_spec,
        out_shape=out_shape,
        compiler_params=_params(("arbitrary",)),
        name="moe_combine",
    )(dest, ys)


def _moe_kernel(te1_ref, te2_ref, nv_ref, used_ref,
                xs_ref, g_ref, wg1_ref, wu1_ref, wd1_ref, wg2_ref, wu2_ref, wd2_ref, ys_ref):
    i = pl.program_id(0)

    @pl.when(nv_ref[i] > 0)
    def _():
        x = _load_tokens(xs_ref, XM_ROWS)
        meta = xs_ref[pl.ds(X_ROWS, TM_MOE, stride=XM_ROWS), :]
        hf = _rms(x, g_ref[...]).astype(BF16)

        def expert(wg_ref, wu_ref, wd_ref, gate):
            a = jnp.dot(hf, wg_ref[0], preferred_element_type=F32)
            u = jnp.dot(hf, wu_ref[0], preferred_element_type=F32)
            act = a * jax.nn.sigmoid(a) * u * gate
            return jnp.dot(act.astype(BF16), wd_ref[0], preferred_element_type=F32)

        y = expert(wg1_ref, wu1_ref, wd1_ref, meta[:, META_GATE_A:META_GATE_A + 1])
        y = y + expert(wg2_ref, wu2_ref, wd2_ref, meta[:, META_GATE_B:META_GATE_B + 1])
        _store_tokens(ys_ref, x + y)

    @pl.when(nv_ref[i] == 0)
    def _():
        ys_ref[...] = jnp.zeros(ys_ref.shape, F32)


def _moe(xs, g, te1, te2, nv, used, wg_bf, wu_bf, wd_bf):
    tm = TM_MOE
    n_tiles = nv.shape[0]
    fixed = lambda i, te1, te2, nv, used: (0, 0)
    tile_in = lambda i, te1, te2, nv, used: (jnp.minimum(i, used[0] - 1), 0)
    tile_out = lambda i, te1, te2, nv, used: (i, 0)
    expert1 = lambda i, te1, te2, nv, used: (te1[i], 0, 0)
    expert2 = lambda i, te1, te2, nv, used: (te2[i], 0, 0)
    w_in = lambda index_map: pl.BlockSpec((1, D_MODEL, D_EXPERT), index_map)
    w_out = lambda index_map: pl.BlockSpec((1, D_EXPERT, D_MODEL), index_map)
    grid_spec = pltpu.PrefetchScalarGridSpec(
        num_scalar_prefetch=4,
        grid=(n_tiles,),
        in_specs=[pl.BlockSpec((tm * XM_ROWS, LANES), tile_in),
                  pl.BlockSpec((1, D_MODEL), fixed),
                  w_in(expert1), w_in(expert1), w_out(expert1),
                  w_in(expert2), w_in(expert2), w_out(expert2)],
        out_specs=pl.BlockSpec((tm * X_ROWS, LANES), tile_out))
    return pl.pallas_call(
        _moe_kernel,
        grid_spec=grid_spec,
        out_shape=jax.ShapeDtypeStruct((n_tiles * tm * X_ROWS, LANES), F32),
        compiler_params=_params(("arbitrary",)),
        name="moe",
    )(te1, te2, nv, used, xs, g, wg_bf, wu_bf, wd_bf, wg_bf, wu_bf, wd_bf)


def _route(meta_t, counts, t):
    tm = TM_MOE
    nb = N_GROUPS * N_PAIRS
    n_tiles = t // tm + nb
    meta = meta_t.reshape(t // TM_PROJ, META_ROWS, TM_PROJ)
    bucket = meta[:, META_BUCKET, :].reshape(t).astype(jnp.int32)
    rank = meta[:, META_RANK, :].reshape(t).astype(jnp.int32)
    counts = counts[0, :nb].astype(jnp.int32)
    ids = jnp.arange(nb, dtype=jnp.int32)

    tiles = (counts + tm - 1) // tm
    tile_end = jnp.cumsum(tiles)
    tile_start = tile_end - tiles
    used = tile_end[-1]
    row0 = tile_start * tm
    dest = rank + jnp.sum(jnp.where(bucket[:, None] == ids[None, :], row0[None, :], 0), axis=1)

    ti = jnp.arange(n_tiles, dtype=jnp.int32)
    tic = jnp.minimum(ti, used - 1)
    tb = jnp.sum((tile_end[None, :] <= tic[:, None]).astype(jnp.int32), axis=1)
    onehot_tb = tb[:, None] == ids[None, :]
    pick = lambda table: jnp.sum(jnp.where(onehot_tb, table[None, :], 0), axis=1)
    nv = jnp.where(ti < used, jnp.clip(pick(counts) - (tic - pick(tile_start)) * tm, 0, tm), 0)
    pair_a = jnp.asarray([0, 0, 0, 1, 1, 2], jnp.int32)
    pair_b = jnp.asarray([1, 2, 3, 2, 3, 3], jnp.int32)
    te1 = pick((ids // N_PAIRS) * PER_GROUP + pair_a[ids % N_PAIRS])
    te2 = pick((ids // N_PAIRS) * PER_GROUP + pair_b[ids % N_PAIRS])

    pads = tiles * tm - counts
    pad_end = jnp.cumsum(pads)
    k = jnp.arange(nb * tm, dtype=jnp.int32)
    kb = jnp.minimum(jnp.sum((pad_end[None, :] <= k[:, None]).astype(jnp.int32), axis=1), nb - 1)
    onehot_kb = kb[:, None] == ids[None, :]
    pick_k = lambda table: jnp.sum(jnp.where(onehot_kb, table[None, :], 0), axis=1)
    in_pad = pick_k(row0 + counts) + (k - pick_k(pad_end - pads))
    free_rows = jnp.where(k < pad_end[-1], in_pad, used * tm + (k - pad_end[-1]))
    i32 = lambda a: a.astype(jnp.int32)
    return (i32(jnp.concatenate([dest, free_rows])), i32(te1), i32(te2), i32(nv),
            i32(used).reshape(1))


def _block_diag(w):
    n = w.shape[0]
    eye = jnp.eye(n, dtype=w.dtype)
    return jnp.einsum("gcd,gh->gchd", w, eye).reshape(n * POOL_GROUP, n * POOL_GROUP)


def kernel(x, attn_norm_g, w_in, conv_w, conv_b, conv_ln_g, conv_ln_b, conv_pw_w, conv_pw_b,
           pool_w, pool_scale, q_norm_g, k_norm_g, lambda_q1, lambda_k1, lambda_q2, lambda_k2,
           attn_sub_norm_g, w_out, ffn_norm_g, router_g_w, router_g_b, router_e_w, router_e_b,
           w_gate, w_up, w_down):
    bsz, s, d = x.shape
    depth = w_in.shape[0]
    t = bsz * s
    qaug, kaug = _alibi_aug(s)
    row = lambda a: a.reshape(1, -1)

    x2 = x.reshape(t, d)
    for l in range(depth):
        lam_init = 0.8 - 0.6 * math.exp(-0.3 * l)
        z_conv, p_in, q, k, v = _in_proj(
            x2, row(attn_norm_g[l]), w_in[l].astype(BF16),
            row(jnp.tile(q_norm_g[l], 2)), row(jnp.tile(k_norm_g[l], 2)))

        y_conv = _conv_mixer(z_conv.reshape(bsz, s, -1), conv_w[l], row(conv_b[l]),
                             row(conv_ln_g[l]), row(conv_ln_b[l]),
                             conv_pw_w[l].astype(BF16), row(conv_pw_b[l]))
        y_pool = _pool_mixer(p_in.reshape(bsz, s, -1), _block_diag(pool_w[l]).astype(BF16),
                             row(pool_scale[l]))
        y_attn = _attention(q.reshape(bsz, s, -1), k.reshape(bsz, s, -1), v.reshape(bsz, s, -1),
                            qaug, kaug, row(lambda_q1[l]), row(lambda_k1[l]), row(lambda_q2[l]),
                            row(lambda_k2[l]), row(attn_sub_norm_g[l]), lam_init)

        pad = LANES - N_GROUPS - N_EXPERTS
        rw = jnp.pad(jnp.concatenate([router_g_w[l], router_e_w[l]], axis=1), ((0, 0), (0, pad)))
        rb = jnp.pad(jnp.concatenate([router_g_b[l], router_e_b[l]]), (0, pad))
        rw_hi = rw.astype(BF16)
        rw_lo = (rw - rw_hi.astype(F32)).astype(BF16)
        rw_hi_lo = jnp.concatenate([rw_hi, rw_lo], axis=1)
        xm, meta_t, counts = _out_proj(
            x2, y_conv.reshape(t, -1), y_pool.reshape(t, -1), y_attn.reshape(t, -1),
            w_out[l].astype(BF16), row(ffn_norm_g[l]), rw_hi_lo, row(rb))

        dest, te1, te2, nv, used = _route(meta_t, counts, t)
        xs = _dispatch(xm, dest)
        ys = _moe(xs, row(ffn_norm_g[l]), te1, te2, nv, used, w_gate[l].astype(BF16),
                  w_up[l].astype(BF16), w_down[l].astype(BF16))
        x2 = _combine(ys, dest, t, to_rows=l == depth - 1)
    return x2.reshape(bsz, s, d)
```

```python
import functools
import math

import jax
import jax.numpy as jnp
import numpy as np
from jax import lax
from jax.experimental import pallas as pl
from jax.experimental.pallas import tpu as pltpu

D_MODEL = 1024
C_CONV = 256
C_POOL = 256
C_ATTN = 512
N_HEADS = 4
HEAD_DIM = 64
V_DIM = 2 * HEAD_DIM
CONV_WIDTH = 31
POOL_WINDOWS = (2, 4, 8, 16)
POOL_GROUP = C_POOL // len(POOL_WINDOWS)
N_IN = 2 * C_CONV + C_POOL + 3 * C_ATTN
N_GROUPS = 4
PER_GROUP = 4
N_EXPERTS = N_GROUPS * PER_GROUP
D_EXPERT = 256
EPS = 1e-6

LANES = 128
SUBLANES = 8
VMEM_LIMIT = 48 * 1024 * 1024

TM_PROJ = 512
HALO = 32
TQ = 512
TK = 256
TM_MOE = 256
N_PAIRS = 6
X_ROWS = D_MODEL // LANES
XM_ROWS = X_ROWS + 1
META_GATE_A, META_GATE_B, META_BUCKET, META_RANK = 0, 1, 2, 3
META_ROWS = 8
DMA_UNROLL = 8
COPY_CHUNK = 2048
COPY_WAIT = 512

F32 = jnp.float32
BF16 = jnp.bfloat16


def _params(sem):
    return pltpu.CompilerParams(dimension_semantics=sem, vmem_limit_bytes=VMEM_LIMIT)


def _rms(x, g):
    return x * lax.rsqrt(jnp.mean(x * x, axis=-1, keepdims=True) + EPS) * g


def _load_tokens(x_ref, rows_per_token=X_ROWS):
    if x_ref.shape[-1] != LANES:
        return x_ref[...]
    tm = x_ref.shape[0] // rows_per_token
    return jnp.concatenate(
        [x_ref[pl.ds(c, tm, stride=rows_per_token), :] for c in range(X_ROWS)], axis=1)


def _store_tokens(o_ref, x, rows_per_token=X_ROWS):
    tm = x.shape[0]
    for c in range(X_ROWS):
        o_ref[pl.ds(c, tm, stride=rows_per_token), :] = x[:, c * LANES:(c + 1) * LANES]


def _in_proj_kernel(x_ref, g_ref, w_ref, qg_ref, kg_ref,
                    cw_ref, cb_ref, lng_ref, lnb_ref, pw_ref, pwb_ref, plw_ref, pls_ref,
                    yc_ref, yp_ref, q_ref, k_ref, v_ref, cext_ref, pext_ref, *, tiles_per_seq):
    ts = yc_ref.shape[0]
    seq_tile = pl.program_id(0) % tiles_per_seq
    h = _rms(_load_tokens(x_ref), g_ref[...]).astype(BF16)

    def proj(c0, width):
        return jnp.dot(h, w_ref[:, c0:c0 + width], preferred_element_type=F32)

    @pl.when(seq_tile == 0)
    def _():
        cext_ref[0, 0:HALO, :] = jnp.zeros((HALO, C_CONV), F32)
        pext_ref[0, 0:HALO, :] = jnp.zeros((HALO, C_POOL), F32)

    z = proj(0, 2 * C_CONV)
    cext_ref[0, HALO:, :] = z[:, :C_CONV] * jax.nn.sigmoid(z[:, C_CONV:])
    pext_ref[0, HALO:, :] = proj(2 * C_CONV, C_POOL)

    lane = lax.broadcasted_iota(jnp.int32, (1, LANES), 1)
    first = lane < HEAD_DIM

    def qk_norm(z, gain_ref, out_ref, post_scale):
        gain = gain_ref[...]
        for hd in range(N_HEADS):
            blk = z[:, hd * LANES:(hd + 1) * LANES]
            sq = blk * blk
            s_all = jnp.sum(sq, axis=-1, keepdims=True)
            s_lo = jnp.sum(jnp.where(first, sq, 0.0), axis=-1, keepdims=True)
            r_lo = lax.rsqrt(s_lo * (1.0 / HEAD_DIM) + EPS)
            r_hi = lax.rsqrt((s_all - s_lo) * (1.0 / HEAD_DIM) + EPS)
            r = jnp.where(first, r_lo, r_hi)
            out_ref[:, hd * LANES:(hd + 1) * LANES] = (blk * r * gain * post_scale).astype(BF16)

    c_q = 2 * C_CONV + C_POOL
    qk_norm(proj(c_q, C_ATTN), qg_ref, q_ref, HEAD_DIM ** -0.5 * math.log2(math.e))
    qk_norm(proj(c_q + C_ATTN, C_ATTN), kg_ref, k_ref, 1.0)
    v_ref[...] = proj(c_q + 2 * C_ATTN, C_ATTN).astype(BF16)

    _conv_group(cext_ref, ts, cw_ref, cb_ref, lng_ref, lnb_ref, pw_ref, pwb_ref, yc_ref)
    _pool_group(pext_ref, ts, seq_tile * ts, plw_ref, pls_ref, yp_ref)

    cext_ref[0, 0:HALO, :] = cext_ref[0, ts:ts + HALO, :]
    pext_ref[0, 0:HALO, :] = pext_ref[0, ts:ts + HALO, :]


def _in_proj(x2, seq_len, g, w_bf, qg, kg, conv_w, conv_b, ln_g, ln_b, pw_bf, pw_b,
             pool_w_bf, pool_scale):
    token_major = x2.shape[1] == LANES
    t = x2.shape[0] // X_ROWS if token_major else x2.shape[0]
    tm = TM_PROJ
    assert seq_len % tm == 0
    row = lambda i: (i, 0)
    fixed = lambda i: (0, 0)
    full = lambda a: pl.BlockSpec(a.shape, fixed)
    x_spec = pl.BlockSpec((tm * X_ROWS, LANES) if token_major else (tm, D_MODEL), row)
    small = [g, w_bf, qg, kg, conv_w, conv_b, ln_g, ln_b, pw_bf, pw_b, pool_w_bf, pool_scale]
    widths = [C_CONV, C_POOL, C_ATTN, C_ATTN, C_ATTN]
    return pl.pallas_call(
        functools.partial(_in_proj_kernel, tiles_per_seq=seq_len // tm),
        grid=(t // tm,),
        in_specs=[x_spec] + [full(a) for a in small],
        out_specs=[pl.BlockSpec((tm, w), row) for w in widths],
        out_shape=[jax.ShapeDtypeStruct((t, w), BF16) for w in widths],
        scratch_shapes=[pltpu.VMEM((SUBLANES, tm + HALO, C_CONV), F32),
                        pltpu.VMEM((4, tm + HALO, C_POOL), F32)],
        compiler_params=_params(("arbitrary",)),
        name="in_proj_mix",
    )(x2, *small)


def _conv_group(ext_ref, ts, w_ref, b_ref, lng_ref, lnb_ref, pw_ref, pwb_ref, o_ref):
    n_shifted = ts + HALO - SUBLANES
    for s in range(1, SUBLANES):
        ext_ref[s, 0:n_shifted, :] = ext_ref[0, s:s + n_shifted, :]

    base = HALO - (CONV_WIDTH - 1)
    rows = 64
    for c in range(ts // rows):
        acc = jnp.broadcast_to(b_ref[...], (rows, C_CONV))
        for j in range(CONV_WIDTH):
            phase = (base + j) % SUBLANES
            r0 = c * rows + base + j - phase
            acc = acc + ext_ref[phase, r0:r0 + rows, :] * w_ref[j:j + 1, :]
        mu = jnp.mean(acc, axis=-1, keepdims=True)
        xc = acc - mu
        y = xc * lax.rsqrt(jnp.mean(xc * xc, axis=-1, keepdims=True) + EPS)
        y = y * lng_ref[...] + lnb_ref[...]
        y = y * jax.nn.sigmoid(y)
        out = jnp.dot(y.astype(BF16), pw_ref[...], preferred_element_type=F32) + pwb_ref[...]
        o_ref[c * rows:(c + 1) * rows, :] = out.astype(BF16)


def _pool_group(ext_ref, ts, pos0, w_ref, scale_ref, o_ref):
    n = ts + HALO
    assert POOL_WINDOWS == (2, 4, 8, 16) and HALO >= 4 * SUBLANES
    sums = []
    for k in range(1, 5):
        lo = SUBLANES * k
        shift = 1 << (k - 1)
        level = ext_ref[k - 1, lo:n, :] + ext_ref[k - 1, lo - shift:n - shift, :]
        sums.append(level[HALO - lo:, :])
        if k < 4:
            ext_ref[k, lo:n, :] = level

    lane = lax.broadcasted_iota(jnp.int32, (1, C_POOL), 1)
    group = lane // POOL_GROUP
    win = jnp.left_shift(2, group)
    acc = jnp.where(group == 0, sums[0],
                    jnp.where(group == 1, sums[1], jnp.where(group == 2, sums[2], sums[3])))
    pos = pos0 + lax.broadcasted_iota(jnp.int32, (ts, 1), 0)
    count = jnp.minimum(pos + 1, win).astype(F32)
    diff = acc / count - ext_ref[0, HALO:, :]
    out = jnp.dot(diff.astype(BF16), w_ref[...], preferred_element_type=F32) * scale_ref[...]
    o_ref[...] = out.astype(BF16)


def _attn_kernel(q_ref, k_ref, v_ref, qaug_ref, kaug_ref, lq1_ref, lk1_ref, lq2_ref, lk2_ref,
                 sg_ref, o_ref, kx_ref, q2_all, m_all, l_all, acc_all, *, lam_init):
    qi = pl.program_id(1)
    half = TQ // 2
    assert half == TK

    @pl.when(qi == 0)
    def _():
        for hd in range(N_HEADS):
            kx_ref[hd, :, 0:LANES] = k_ref[0, :, hd * LANES:(hd + 1) * LANES]
            kx_ref[hd, :, LANES:] = kaug_ref[hd]

    lam = (jnp.exp(jnp.sum(lq1_ref[...] * lk1_ref[...], axis=-1, keepdims=True))
           - jnp.exp(jnp.sum(lq2_ref[...] * lk2_ref[...], axis=-1, keepdims=True))
           + lam_init)
    first = lax.broadcasted_iota(jnp.int32, (1, LANES), 1) < HEAD_DIM
    row = lax.broadcasted_iota(jnp.int32, (2 * half, TK), 0)
    col = lax.broadcasted_iota(jnp.int32, (2 * half, TK), 1)
    tri = col <= (row & (half - 1))

    def step(hd, r0, nr, k0, width, masked, first_block):
        q2_ref, m_ref, l_ref, acc_ref = q2_all.at[hd], m_all.at[hd], l_all.at[hd], acc_all.at[hd]
        kv_rows = pl.ds(pl.multiple_of(k0, TK), width)
        kb = kx_ref[hd, kv_rows, :]
        vb = v_ref[0, kv_rows, hd * LANES:(hd + 1) * LANES]
        s = lax.dot_general(q2_ref[r0:r0 + nr, :], kb, (((1,), (1,)), ((), ())),
                            preferred_element_type=F32)
        if masked:
            s = jnp.where(tri, s, -jnp.inf)
        s_max = jnp.max(s, axis=-1, keepdims=True)
        if first_block:
            m_new = jnp.broadcast_to(s_max, (nr, LANES))
        else:
            m_prev = m_ref[r0:r0 + nr]
            m_new = jnp.maximum(m_prev, s_max)
            alpha = jnp.exp2(m_prev - m_new)
        p = jnp.exp2(s - jnp.concatenate([m_new] * (width // LANES), axis=1))
        l_new = jnp.sum(p, axis=-1, keepdims=True)
        acc_new = jnp.dot(p.astype(BF16), vb, preferred_element_type=F32)
        if first_block:
            l_ref[r0:r0 + nr] = jnp.broadcast_to(l_new, (nr, LANES))
            acc_ref[r0:r0 + nr] = acc_new
        else:
            l_ref[r0:r0 + nr] = alpha * l_ref[r0:r0 + nr] + l_new
            acc_ref[r0:r0 + nr] = alpha * acc_ref[r0:r0 + nr] + acc_new
        m_ref[r0:r0 + nr] = m_new

    for hd in range(N_HEADS):
        q2_ref, m_ref, l_ref, acc_ref = q2_all.at[hd], m_all.at[hd], l_all.at[hd], acc_all.at[hd]
        qb = q_ref[0, :, hd * LANES:(hd + 1) * LANES]
        zero = jnp.zeros_like(qb)
        q_maps = (jnp.where(first, qb, zero), jnp.where(first, zero, qb))
        qa = qaug_ref[hd]
        for part in range(2):
            rows = slice(part * half, (part + 1) * half)
            for m in range(2):
                dst = slice((2 * part + m) * half, (2 * part + m + 1) * half)
                q2_ref[dst, 0:LANES] = q_maps[m][rows]
                q2_ref[dst, LANES:] = qa[rows]

        step(hd, 0, 2 * half, qi * TQ, TK, True, True)
        step(hd, 2 * half, 2 * half, qi * TQ, TK, False, True)
        step(hd, 2 * half, 2 * half, qi * TQ + TK, TK, True, False)

        def body(j, carry, hd=hd):
            step(hd, 0, 4 * half, 2 * j * TQ, 2 * TQ, False, False)
            return carry

        lax.fori_loop(0, qi // 2, body, 0)

        @pl.when(qi % 2 == 1)
        def _(hd=hd):
            step(hd, 0, 4 * half, (qi - 1) * TQ, TQ, False, False)

        for part in range(2):
            r = 2 * part * half
            o = (acc_ref[r:r + half] / l_ref[r:r + half]
                 - lam * (acc_ref[r + half:r + 2 * half] / l_ref[r + half:r + 2 * half]))
            o = _rms(o, sg_ref[...]) * (1.0 - lam_init)
            o_ref[0, part * half:(part + 1) * half, hd * LANES:(hd + 1) * LANES] = o.astype(BF16)


def _alibi_aug(s):
    start = 2.0 ** (-8.0 / N_HEADS)
    slopes = np.array([start ** (i + 1) for i in range(N_HEADS)], dtype=np.float32)
    c = math.log2(math.e) * slopes.astype(np.float64)[:, None] * np.arange(s)[None, :]
    pieces = []
    rem = c
    for _ in range(3):
        piece = rem.astype(jnp.bfloat16)
        pieces.append(piece)
        rem = rem - piece.astype(np.float64)
    one = np.ones_like(pieces[0])
    zero = np.zeros((N_HEADS, s, LANES - 6), dtype=jnp.bfloat16)
    kaug = np.concatenate([np.stack(pieces + [one] * 3, axis=-1), zero], axis=-1)
    qaug = np.concatenate([np.stack([one] * 3 + [-p for p in pieces], axis=-1), zero], axis=-1)
    return jnp.asarray(qaug), jnp.asarray(kaug)


def _attention(q, k, v, qaug, kaug, lq1, lk1, lq2, lk2, sub_g, lam_init):
    bsz, s, _ = q.shape
    nq = s // TQ
    fixed2 = lambda b_, i: (0, 0)
    seq = lambda b_, i: (b_, 0, 0)
    vec = pl.BlockSpec((1, HEAD_DIM), fixed2)
    return pl.pallas_call(
        functools.partial(_attn_kernel, lam_init=lam_init),
        grid=(bsz, nq),
        in_specs=[pl.BlockSpec((1, TQ, C_ATTN), lambda b_, i: (b_, i, 0)),
                  pl.BlockSpec((1, s, C_ATTN), seq),
                  pl.BlockSpec((1, s, C_ATTN), seq),
                  pl.BlockSpec((N_HEADS, TQ, LANES), lambda b_, i: (0, i, 0)),
                  pl.BlockSpec((N_HEADS, s, LANES), lambda b_, i: (0, 0, 0)),
                  vec, vec, vec, vec,
                  pl.BlockSpec((1, V_DIM), fixed2)],
        out_specs=pl.BlockSpec((1, TQ, C_ATTN), lambda b_, i: (b_, i, 0)),
        out_shape=jax.ShapeDtypeStruct((bsz, s, C_ATTN), BF16),
        scratch_shapes=[pltpu.VMEM((N_HEADS, s, 2 * LANES), BF16),
                        pltpu.VMEM((N_HEADS, 2 * TQ, 2 * LANES), BF16),
                        pltpu.VMEM((N_HEADS, 2 * TQ, LANES), F32),
                        pltpu.VMEM((N_HEADS, 2 * TQ, LANES), F32),
                        pltpu.VMEM((N_HEADS, 2 * TQ, V_DIM), F32)],
        compiler_params=_params(("parallel", "arbitrary")),
        name="diff_attention",
    )(q, k, v, qaug, kaug, lq1, lk1, lq2, lk2, sub_g)


def _split_bf16(a):
    hi = a.astype(BF16)
    lo = (a - hi.astype(F32)).astype(BF16)
    return hi, lo


def _out_proj_kernel(x_ref, yc_ref, yp_ref, ya_ref, w_ref, g_ref, rw_ref, rb_ref,
                     xm_ref, meta_t_ref, counts_ref, cnt_ref):
    x = _load_tokens(x_ref)
    x = x + jnp.dot(yc_ref[...], w_ref[0:C_CONV, :], preferred_element_type=F32)
    x = x + jnp.dot(yp_ref[...], w_ref[C_CONV:C_CONV + C_POOL, :], preferred_element_type=F32)
    x = x + jnp.dot(ya_ref[...], w_ref[C_CONV + C_POOL:, :], preferred_element_type=F32)
    tm = x.shape[0]
    _store_tokens(xm_ref, x, XM_ROWS)

    hf_hi, hf_lo = _split_bf16(_rms(x, g_ref[...]))

    both = jnp.dot(hf_hi, rw_ref[...], preferred_element_type=F32)
    logits = (both[:, 0:LANES] + both[:, LANES:]
              + jnp.dot(hf_lo, rw_ref[:, 0:LANES], preferred_element_type=F32)) + rb_ref[...]
    lane = lax.broadcasted_iota(jnp.int32, (tm, LANES), 1)
    lane_f = lane.astype(F32)
    neg = -jnp.inf

    def first_argmax(val, vmax):
        first = jnp.min(jnp.where(val == vmax, lane_f, float(LANES)), axis=-1, keepdims=True)
        return first.astype(jnp.int32)

    lg = jnp.where(lane < N_GROUPS, logits, neg)
    g_max = jnp.max(lg, axis=-1, keepdims=True)
    g_idx = first_argmax(lg, g_max)
    g_p = 1.0 / jnp.sum(jnp.exp(lg - g_max), axis=-1, keepdims=True)

    e_lane = lane - N_GROUPS
    in_group = (e_lane >= g_idx * PER_GROUP) & (e_lane < (g_idx + 1) * PER_GROUP)
    le = jnp.where(in_group, logits, neg)
    e_max = jnp.max(le, axis=-1, keepdims=True)
    pe = jnp.exp(le - e_max)
    pe = pe / jnp.sum(pe, axis=-1, keepdims=True)
    p1 = jnp.max(pe, axis=-1, keepdims=True)
    i1 = first_argmax(jnp.where(in_group, pe, neg), p1)
    pe2 = jnp.where(in_group & (lane != i1), pe, neg)
    p2 = jnp.max(pe2, axis=-1, keepdims=True)
    i2 = first_argmax(pe2, p2)
    denom = p1 + p2
    gate1 = g_p * (p1 / denom)
    gate2 = g_p * (p2 / denom)

    loc1 = i1 - N_GROUPS - g_idx * PER_GROUP
    loc2 = i2 - N_GROUPS - g_idx * PER_GROUP
    a = jnp.minimum(loc1, loc2)
    b = jnp.maximum(loc1, loc2)
    pair = jnp.where(a == 0, 0, jnp.where(a == 1, 3, 5)) + (b - a - 1)
    bucket = g_idx * N_PAIRS + pair
    gate_a = jnp.where(loc1 < loc2, gate1, gate2)
    gate_b = jnp.where(loc1 < loc2, gate2, gate1)

    @pl.when(pl.program_id(0) == 0)
    def _():
        cnt_ref[...] = jnp.zeros(cnt_ref.shape, F32)

    in_bucket = lane == bucket
    earlier = (lax.broadcasted_iota(jnp.int32, (tm, tm), 1)
               < lax.broadcasted_iota(jnp.int32, (tm, tm), 0)).astype(BF16)
    before = jnp.dot(earlier, in_bucket.astype(BF16), preferred_element_type=F32) + cnt_ref[...]
    rank = jnp.sum(jnp.where(in_bucket, before, 0.0), axis=-1, keepdims=True)
    cnt_ref[...] += jnp.sum(in_bucket.astype(F32), axis=0, keepdims=True)
    counts_ref[...] = cnt_ref[...]

    meta = jnp.where(lane == META_GATE_A, gate_a,
                     jnp.where(lane == META_GATE_B, gate_b,
                               jnp.where(lane == META_BUCKET, bucket.astype(F32),
                                         jnp.where(lane == META_RANK, rank, 0.0))))
    xm_ref[pl.ds(X_ROWS, tm, stride=XM_ROWS), :] = meta
    meta_t_ref[...] = meta.T[0:META_ROWS, :]


def _out_proj(x2, yc, yp, ya, w_bf, g, rw_hi_lo, rb):
    token_major = x2.shape[1] == LANES
    t = x2.shape[0] // X_ROWS if token_major else x2.shape[0]
    tm = TM_PROJ
    row = lambda i: (i, 0)
    fixed = lambda i: (0, 0)
    x_spec = pl.BlockSpec((tm * X_ROWS, LANES) if token_major else (tm, D_MODEL), row)
    return pl.pallas_call(
        _out_proj_kernel,
        grid=(t // tm,),
        in_specs=[x_spec,
                  pl.BlockSpec((tm, C_CONV), row),
                  pl.BlockSpec((tm, C_POOL), row),
                  pl.BlockSpec((tm, C_ATTN), row),
                  pl.BlockSpec((D_MODEL, D_MODEL), fixed),
                  pl.BlockSpec((1, D_MODEL), fixed),
                  pl.BlockSpec((D_MODEL, 2 * LANES), fixed),
                  pl.BlockSpec((1, LANES), fixed)],
        out_specs=[pl.BlockSpec((tm * XM_ROWS, LANES), row),
                   pl.BlockSpec((META_ROWS, tm), row),
                   pl.BlockSpec((1, LANES), fixed)],
        out_shape=[jax.ShapeDtypeStruct((t * XM_ROWS, LANES), F32),
                   jax.ShapeDtypeStruct((t // tm * META_ROWS, tm), F32),
                   jax.ShapeDtypeStruct((1, LANES), F32)],
        scratch_shapes=[pltpu.VMEM((1, LANES), F32)],
        compiler_params=_params(("arbitrary",)),
        name="out_proj_router",
    )(x2, yc, yp, ya, w_bf, g, rw_hi_lo, rb)


def _for_chunk_rows(dest_ref, start_copy, wait_copy):
    base = pl.program_id(0) * COPY_CHUNK

    def body(blk, carry):
        rows = [blk * DMA_UNROLL + u for u in range(DMA_UNROLL)]
        slots = [dest_ref[base + r] for r in rows]
        for u, (r, slot) in enumerate(zip(rows, slots)):
            start_copy(r, slot, u % 2)
        return carry

    lax.fori_loop(0, COPY_CHUNK // DMA_UNROLL, body, 0)
    for piece in range(COPY_CHUNK // COPY_WAIT):
        wait_copy(piece)


def _dispatch_kernel(dest_ref, x_ref, out_hbm, sem):
    def start_copy(r, slot, priority):
        pltpu.make_async_copy(x_ref.at[pl.ds(r * XM_ROWS, XM_ROWS)],
                              out_hbm.at[pl.ds(slot * XM_ROWS, XM_ROWS)],
                              sem.at[0]).start(priority=priority)

    def wait_copy(piece):
        n = COPY_WAIT * XM_ROWS
        pltpu.make_async_copy(x_ref.at[pl.ds(piece * n, n)], out_hbm.at[pl.ds(0, n)],
                              sem.at[0]).wait()

    _for_chunk_rows(dest_ref, start_copy, wait_copy)


def _dispatch(xm, dest):
    n_tokens = xm.shape[0] // XM_ROWS
    n_rows = dest.shape[0]
    assert n_tokens % COPY_CHUNK == 0 and n_rows % COPY_CHUNK == 0
    last = n_tokens // COPY_CHUNK - 1
    grid_spec = pltpu.PrefetchScalarGridSpec(
        num_scalar_prefetch=1,
        grid=(n_rows // COPY_CHUNK,),
        in_specs=[pl.BlockSpec((COPY_CHUNK * XM_ROWS, LANES),
                               lambda i, dest: (jnp.minimum(i, last), 0))],
        out_specs=pl.BlockSpec(memory_space=pl.ANY),
        scratch_shapes=[pltpu.SemaphoreType.DMA((1,))])
    return pl.pallas_call(
        _dispatch_kernel,
        grid_spec=grid_spec,
        out_shape=jax.ShapeDtypeStruct((n_rows * XM_ROWS, LANES), F32),
        compiler_params=_params(("arbitrary",)),
        name="moe_dispatch",
    )(dest, xm)


def _combine_kernel(dest_ref, ys_hbm, o_ref, *scratch, to_rows):
    buf = scratch[0] if to_rows else o_ref
    sem = scratch[-1]

    def start_copy(r, slot, priority):
        pltpu.make_async_copy(ys_hbm.at[pl.ds(slot * X_ROWS, X_ROWS)],
                              buf.at[pl.ds(r * X_ROWS, X_ROWS)],
                              sem.at[0]).start(priority=priority)

    def wait_copy(piece):
        n = COPY_WAIT * X_ROWS
        pltpu.make_async_copy(ys_hbm.at[pl.ds(0, n)], buf.at[pl.ds(piece * n, n)],
                              sem.at[0]).wait()

    _for_chunk_rows(dest_ref, start_copy, wait_copy)
    if to_rows:
        o_ref[...] = _load_tokens(buf)


def _combine(ys, dest, n_tokens, to_rows):
    assert n_tokens % COPY_CHUNK == 0
    chunk_rows = COPY_CHUNK * X_ROWS
    if to_rows:
        out_spec = pl.BlockSpec((COPY_CHUNK, D_MODEL), lambda i, dest: (i, 0))
        out_shape = jax.ShapeDtypeStruct((n_tokens, D_MODEL), F32)
        scratch = [pltpu.VMEM((chunk_rows, LANES), F32)]
    else:
        out_spec = pl.BlockSpec((chunk_rows, LANES), lambda i, dest: (i, 0))
        out_shape = jax.ShapeDtypeStruct((n_tokens * X_ROWS, LANES), F32)
        scratch = []
    grid_spec = pltpu.PrefetchScalarGridSpec(
        num_scalar_prefetch=1,
        grid=(n_tokens // COPY_CHUNK,),
        in_specs=[pl.BlockSpec(memory_space=pl.ANY)],
        out_specs=out_spec,
        scratch_shapes=scratch + [pltpu.SemaphoreType.DMA((1,))])
    return pl.pallas_call(
        functools.partial(_combine_kernel, to_rows=to_rows),
        grid_spec=grid_spec,
        out_shape=out_shape,
        compiler_params=_params(("arbitrary",)),
        name="moe_combine",
    )(dest, ys)


def _moe_kernel(te1_ref, te2_ref, nv_ref, used_ref,
                xs_ref, g_ref, wg1_ref, wu1_ref, wd1_ref, wg2_ref, wu2_ref, wd2_ref, ys_ref):
    i = pl.program_id(0)

    @pl.when(nv_ref[i] > 0)
    def _():
        x = _load_tokens(xs_ref, XM_ROWS)
        meta = xs_ref[pl.ds(X_ROWS, TM_MOE, stride=XM_ROWS), :]
        hf = _rms(x, g_ref[...]).astype(BF16)

        def expert(wg_ref, wu_ref, wd_ref, gate):
            a = jnp.dot(hf, wg_ref[0], preferred_element_type=F32)
            u = jnp.dot(hf, wu_ref[0], preferred_element_type=F32)
            act = a * jax.nn.sigmoid(a) * u * gate
            return jnp.dot(act.astype(BF16), wd_ref[0], preferred_element_type=F32)

        y = expert(wg1_ref, wu1_ref, wd1_ref, meta[:, META_GATE_A:META_GATE_A + 1])
        y = y + expert(wg2_ref, wu2_ref, wd2_ref, meta[:, META_GATE_B:META_GATE_B + 1])
        _store_tokens(ys_ref, x + y)

    @pl.when(nv_ref[i] == 0)
    def _():
        ys_ref[...] = jnp.zeros(ys_ref.shape, F32)


def _moe(xs, g, te1, te2, nv, used, wg_bf, wu_bf, wd_bf):
    tm = TM_MOE
    n_tiles = nv.shape[0]
    fixed = lambda i, te1, te2, nv, used: (0, 0)
    tile_in = lambda i, te1, te2, nv, used: (jnp.minimum(i, used[0] - 1), 0)
    tile_out = lambda i, te1, te2, nv, used: (i, 0)
    expert1 = lambda i, te1, te2, nv, used: (te1[i], 0, 0)
    expert2 = lambda i, te1, te2, nv, used: (te2[i], 0, 0)
    w_in = lambda index_map: pl.BlockSpec((1, D_MODEL, D_EXPERT), index_map)
    w_out = lambda index_map: pl.BlockSpec((1, D_EXPERT, D_MODEL), index_map)
    grid_spec = pltpu.PrefetchScalarGridSpec(
        num_scalar_prefetch=4,
        grid=(n_tiles,),
        in_specs=[pl.BlockSpec((tm * XM_ROWS, LANES), tile_in),
                  pl.BlockSpec((1, D_MODEL), fixed),
                  w_in(expert1), w_in(expert1), w_out(expert1),
                  w_in(expert2), w_in(expert2), w_out(expert2)],
        out_specs=pl.BlockSpec((tm * X_ROWS, LANES), tile_out))
    return pl.pallas_call(
        _moe_kernel,
        grid_spec=grid_spec,
        out_shape=jax.ShapeDtypeStruct((n_tiles * tm * X_ROWS, LANES), F32),
        compiler_params=_params(("arbitrary",)),
        name="moe",
    )(te1, te2, nv, used, xs, g, wg_bf, wu_bf, wd_bf, wg_bf, wu_bf, wd_bf)


def _route(meta_t, counts, t):
    tm = TM_MOE
    nb = N_GROUPS * N_PAIRS
    n_tiles = t // tm + nb
    meta = meta_t.reshape(t // TM_PROJ, META_ROWS, TM_PROJ)
    bucket = meta[:, META_BUCKET, :].reshape(t).astype(jnp.int32)
    rank = meta[:, META_RANK, :].reshape(t).astype(jnp.int32)
    counts = counts[0, :nb].astype(jnp.int32)
    ids = jnp.arange(nb, dtype=jnp.int32)

    tiles = (counts + tm - 1) // tm
    tile_end = jnp.cumsum(tiles)
    tile_start = tile_end - tiles
    used = tile_end[-1]
    row0 = tile_start * tm
    dest = rank + jnp.sum(jnp.where(bucket[:, None] == ids[None, :], row0[None, :], 0), axis=1)

    ti = jnp.arange(n_tiles, dtype=jnp.int32)
    tic = jnp.minimum(ti, used - 1)
    tb = jnp.sum((tile_end[None, :] <= tic[:, None]).astype(jnp.int32), axis=1)
    onehot_tb = tb[:, None] == ids[None, :]
    pick = lambda table: jnp.sum(jnp.where(onehot_tb, table[None, :], 0), axis=1)
    nv = jnp.where(ti < used, jnp.clip(pick(counts) - (tic - pick(tile_start)) * tm, 0, tm), 0)
    pair_a = jnp.asarray([0, 0, 0, 1, 1, 2], jnp.int32)
    pair_b = jnp.asarray([1, 2, 3, 2, 3, 3], jnp.int32)
    te1 = pick((ids // N_PAIRS) * PER_GROUP + pair_a[ids % N_PAIRS])
    te2 = pick((ids // N_PAIRS) * PER_GROUP + pair_b[ids % N_PAIRS])

    pads = tiles * tm - counts
    pad_end = jnp.cumsum(pads)
    k = jnp.arange(nb * tm, dtype=jnp.int32)
    kb = jnp.minimum(jnp.sum((pad_end[None, :] <= k[:, None]).astype(jnp.int32), axis=1), nb - 1)
    onehot_kb = kb[:, None] == ids[None, :]
    pick_k = lambda table: jnp.sum(jnp.where(onehot_kb, table[None, :], 0), axis=1)
    in_pad = pick_k(row0 + counts) + (k - pick_k(pad_end - pads))
    free_rows = jnp.where(k < pad_end[-1], in_pad, used * tm + (k - pad_end[-1]))
    i32 = lambda a: a.astype(jnp.int32)
    return (i32(jnp.concatenate([dest, free_rows])), i32(te1), i32(te2), i32(nv),
            i32(used).reshape(1))


def _block_diag(w):
    n = w.shape[0]
    eye = jnp.eye(n, dtype=w.dtype)
    return jnp.einsum("gcd,gh->gchd", w, eye).reshape(n * POOL_GROUP, n * POOL_GROUP)


def kernel(x, attn_norm_g, w_in, conv_w, conv_b, conv_ln_g, conv_ln_b, conv_pw_w, conv_pw_b,
           pool_w, pool_scale, q_norm_g, k_norm_g, lambda_q1, lambda_k1, lambda_q2, lambda_k2,
           attn_sub_norm_g, w_out, ffn_norm_g, router_g_w, router_g_b, router_e_w, router_e_b,
           w_gate, w_up, w_down):
    bsz, s, d = x.shape
    depth = w_in.shape[0]
    t = bsz * s
    qaug, kaug = _alibi_aug(s)
    row = lambda a: a.reshape(1, -1)

    x2 = x.reshape(t, d)
    for l in range(depth):
        lam_init = 0.8 - 0.6 * math.exp(-0.3 * l)
        y_conv, y_pool, q, k, v = _in_proj(
            x2, s, row(attn_norm_g[l]), w_in[l].astype(BF16),
            row(jnp.tile(q_norm_g[l], 2)), row(jnp.tile(k_norm_g[l], 2)),
            conv_w[l], row(conv_b[l]), row(conv_ln_g[l]), row(conv_ln_b[l]),
            conv_pw_w[l].astype(BF16), row(conv_pw_b[l]),
            _block_diag(pool_w[l]).astype(BF16), row(pool_scale[l]))
        y_attn = _attention(q.reshape(bsz, s, -1), k.reshape(bsz, s, -1), v.reshape(bsz, s, -1),
                            qaug, kaug, row(lambda_q1[l]), row(lambda_k1[l]), row(lambda_q2[l]),
                            row(lambda_k2[l]), row(attn_sub_norm_g[l]), lam_init)

        pad = LANES - N_GROUPS - N_EXPERTS
        rw = jnp.pad(jnp.concatenate([router_g_w[l], router_e_w[l]], axis=1), ((0, 0), (0, pad)))
        rb = jnp.pad(jnp.concatenate([router_g_b[l], router_e_b[l]]), (0, pad))
        rw_hi = rw.astype(BF16)
        rw_lo = (rw - rw_hi.astype(F32)).astype(BF16)
        rw_hi_lo = jnp.concatenate([rw_hi, rw_lo], axis=1)
        xm, meta_t, counts = _out_proj(
            x2, y_conv, y_pool, y_attn.reshape(t, -1),
            w_out[l].astype(BF16), row(ffn_norm_g[l]), rw_hi_lo, row(rb))

        dest, te1, te2, nv, used = _route(meta_t, counts, t)
        xs = _dispatch(xm, dest)
        ys = _moe(xs, row(ffn_norm_g[l]), te1, te2, nv, used, w_gate[l].astype(BF16),
                  w_up[l].astype(BF16), w_down[l].astype(BF16))
        x2 = _combine(ys, dest, t, to_rows=l == depth - 1)
    return x2.reshape(bsz, s, d)
```

```python
import functools
import math

import jax
import jax.numpy as jnp
import numpy as np
from jax import lax
from jax.experimental import pallas as pl
from jax.experimental.pallas import tpu as pltpu

D_MODEL = 1024
C_CONV = 256
C_POOL = 256
C_ATTN = 512
N_HEADS = 4
HEAD_DIM = 64
V_DIM = 2 * HEAD_DIM
CONV_WIDTH = 31
POOL_WINDOWS = (2, 4, 8, 16)
POOL_GROUP = C_POOL // len(POOL_WINDOWS)
N_IN = 2 * C_CONV + C_POOL + 3 * C_ATTN
N_GROUPS = 4
PER_GROUP = 4
N_EXPERTS = N_GROUPS * PER_GROUP
D_EXPERT = 256
EPS = 1e-6

LANES = 128
SUBLANES = 8
VMEM_LIMIT = 48 * 1024 * 1024

TM_PROJ = 512
HALO = 32
TQ = 512
TK = 256
TM_MOE = 256
N_PAIRS = 6
X_ROWS = D_MODEL // LANES
XM_ROWS = X_ROWS + 1
META_GATE_A, META_GATE_B, META_BUCKET, META_RANK = 0, 1, 2, 3
META_ROWS = 8
DMA_UNROLL = 8
COPY_CHUNK = 2048
COPY_WAIT = 512

F32 = jnp.float32
BF16 = jnp.bfloat16


def _params(sem):
    return pltpu.CompilerParams(dimension_semantics=sem, vmem_limit_bytes=VMEM_LIMIT)


def _rms(x, g):
    return x * lax.rsqrt(jnp.mean(x * x, axis=-1, keepdims=True) + EPS) * g


def _load_tokens(x_ref, rows_per_token=X_ROWS):
    if x_ref.shape[-1] != LANES:
        return x_ref[...]
    tm = x_ref.shape[0] // rows_per_token
    return jnp.concatenate(
        [x_ref[pl.ds(c, tm, stride=rows_per_token), :] for c in range(X_ROWS)], axis=1)


def _store_tokens(o_ref, x, rows_per_token=X_ROWS):
    tm = x.shape[0]
    for c in range(X_ROWS):
        o_ref[pl.ds(c, tm, stride=rows_per_token), :] = x[:, c * LANES:(c + 1) * LANES]


def _in_proj_kernel(x_ref, g_ref, w_ref, qg_ref, kg_ref,
                    cw_ref, cb_ref, lng_ref, lnb_ref, pw_ref, pwb_ref, plw_ref, pls_ref,
                    yc_ref, yp_ref, q_ref, k_ref, v_ref, cext_ref, pext_ref, *, tiles_per_seq):
    ts = yc_ref.shape[0]
    seq_tile = pl.program_id(0) % tiles_per_seq
    h = _rms(_load_tokens(x_ref), g_ref[...]).astype(BF16)

    def proj(c0, width):
        return jnp.dot(h, w_ref[:, c0:c0 + width], preferred_element_type=F32)

    @pl.when(seq_tile == 0)
    def _():
        cext_ref[0, 0:HALO, :] = jnp.zeros((HALO, C_CONV), F32)
        pext_ref[0, 0:HALO, :] = jnp.zeros((HALO, C_POOL), F32)

    z = proj(0, 2 * C_CONV)
    cext_ref[0, HALO:, :] = z[:, :C_CONV] * jax.nn.sigmoid(z[:, C_CONV:])
    pext_ref[0, HALO:, :] = proj(2 * C_CONV, C_POOL)

    lane = lax.broadcasted_iota(jnp.int32, (1, LANES), 1)
    first = lane < HEAD_DIM

    def qk_norm(z, gain_ref, out_ref, post_scale):
        gain = gain_ref[...]
        for hd in range(N_HEADS):
            blk = z[:, hd * LANES:(hd + 1) * LANES]
            sq = blk * blk
            s_all = jnp.sum(sq, axis=-1, keepdims=True)
            s_lo = jnp.sum(jnp.where(first, sq, 0.0), axis=-1, keepdims=True)
            r_lo = lax.rsqrt(s_lo * (1.0 / HEAD_DIM) + EPS)
            r_hi = lax.rsqrt((s_all - s_lo) * (1.0 / HEAD_DIM) + EPS)
            r = jnp.where(first, r_lo, r_hi)
            out_ref[:, hd * LANES:(hd + 1) * LANES] = (blk * r * gain * post_scale).astype(BF16)

    c_q = 2 * C_CONV + C_POOL
    qk_norm(proj(c_q, C_ATTN), qg_ref, q_ref, HEAD_DIM ** -0.5 * math.log2(math.e))
    qk_norm(proj(c_q + C_ATTN, C_ATTN), kg_ref, k_ref, 1.0)
    v_ref[...] = proj(c_q + 2 * C_ATTN, C_ATTN).astype(BF16)

    _conv_group(cext_ref, ts, cw_ref, cb_ref, lng_ref, lnb_ref, pw_ref, pwb_ref, yc_ref)
    _pool_group(pext_ref, ts, seq_tile * ts, plw_ref, pls_ref, yp_ref)

    cext_ref[0, 0:HALO, :] = cext_ref[0, ts:ts + HALO, :]
    pext_ref[0, 0:HALO, :] = pext_ref[0, ts:ts + HALO, :]


def _in_proj(x2, seq_len, g, w_bf, qg, kg, conv_w, conv_b, ln_g, ln_b, pw_bf, pw_b,
             pool_w_bf, pool_scale):
    token_major = x2.shape[1] == LANES
    t = x2.shape[0] // X_ROWS if token_major else x2.shape[0]
    tm = TM_PROJ
    assert seq_len % tm == 0
    row = lambda i: (i, 0)
    fixed = lambda i: (0, 0)
    full = lambda a: pl.BlockSpec(a.shape, fixed)
    x_spec = pl.BlockSpec((tm * X_ROWS, LANES) if token_major else (tm, D_MODEL), row)
    small = [g, w_bf, qg, kg, conv_w, conv_b, ln_g, ln_b, pw_bf, pw_b, pool_w_bf, pool_scale]
    widths = [C_CONV, C_POOL, C_ATTN, C_ATTN, C_ATTN]
    return pl.pallas_call(
        functools.partial(_in_proj_kernel, tiles_per_seq=seq_len // tm),
        grid=(t // tm,),
        in_specs=[x_spec] + [full(a) for a in small],
        out_specs=[pl.BlockSpec((tm, w), row) for w in widths],
        out_shape=[jax.ShapeDtypeStruct((t, w), BF16) for w in widths],
        scratch_shapes=[pltpu.VMEM((SUBLANES, tm + HALO, C_CONV), F32),
                        pltpu.VMEM((4, tm + HALO, C_POOL), F32)],
        compiler_params=_params(("arbitrary",)),
        name="in_proj_mix",
    )(x2, *small)


def _conv_group(ext_ref, ts, w_ref, b_ref, lng_ref, lnb_ref, pw_ref, pwb_ref, o_ref):
    n_shifted = ts + HALO - SUBLANES
    for s in range(1, SUBLANES):
        ext_ref[s, 0:n_shifted, :] = ext_ref[0, s:s + n_shifted, :]

    base = HALO - (CONV_WIDTH - 1)
    rows = 64
    for c in range(ts // rows):
        acc = jnp.broadcast_to(b_ref[...], (rows, C_CONV))
        for j in range(CONV_WIDTH):
            phase = (base + j) % SUBLANES
            r0 = c * rows + base + j - phase
            acc = acc + ext_ref[phase, r0:r0 + rows, :] * w_ref[j:j + 1, :]
        mu = jnp.mean(acc, axis=-1, keepdims=True)
        xc = acc - mu
        y = xc * lax.rsqrt(jnp.mean(xc * xc, axis=-1, keepdims=True) + EPS)
        y = y * lng_ref[...] + lnb_ref[...]
        y = y * jax.nn.sigmoid(y)
        out = jnp.dot(y.astype(BF16), pw_ref[...], preferred_element_type=F32) + pwb_ref[...]
        o_ref[c * rows:(c + 1) * rows, :] = out.astype(BF16)


def _pool_group(ext_ref, ts, pos0, w_ref, scale_ref, o_ref):
    n = ts + HALO
    assert POOL_WINDOWS == (2, 4, 8, 16) and HALO >= 4 * SUBLANES
    sums = []
    for k in range(1, 5):
        lo = SUBLANES * k
        shift = 1 << (k - 1)
        level = ext_ref[k - 1, lo:n, :] + ext_ref[k - 1, lo - shift:n - shift, :]
        sums.append(level[HALO - lo:, :])
        if k < 4:
            ext_ref[k, lo:n, :] = level

    lane = lax.broadcasted_iota(jnp.int32, (1, C_POOL), 1)
    group = lane // POOL_GROUP
    win = jnp.left_shift(2, group)
    acc = jnp.where(group == 0, sums[0],
                    jnp.where(group == 1, sums[1], jnp.where(group == 2, sums[2], sums[3])))
    pos = pos0 + lax.broadcasted_iota(jnp.int32, (ts, 1), 0)
    count = jnp.minimum(pos + 1, win).astype(F32)
    diff = acc / count - ext_ref[0, HALO:, :]
    out = jnp.dot(diff.astype(BF16), w_ref[...], preferred_element_type=F32) * scale_ref[...]
    o_ref[...] = out.astype(BF16)


def _attn_kernel(q_ref, k_ref, v_ref, qaug_ref, kaug_ref, lq1_ref, lk1_ref, lq2_ref, lk2_ref,
                 sg_ref, o_ref, kx_ref, q2_all, m_all, l_all, acc_all, *, lam_init):
    qi = pl.program_id(1)
    half = TQ // 2
    assert half == TK

    @pl.when(qi == 0)
    def _():
        for hd in range(N_HEADS):
            kx_ref[hd, :, 0:LANES] = k_ref[0, :, hd * LANES:(hd + 1) * LANES]
            kx_ref[hd, :, LANES:] = kaug_ref[hd]

    lam = (jnp.exp(jnp.sum(lq1_ref[...] * lk1_ref[...], axis=-1, keepdims=True))
           - jnp.exp(jnp.sum(lq2_ref[...] * lk2_ref[...], axis=-1, keepdims=True))
           + lam_init)
    first = lax.broadcasted_iota(jnp.int32, (1, LANES), 1) < HEAD_DIM
    row = lax.broadcasted_iota(jnp.int32, (2 * half, TK), 0)
    col = lax.broadcasted_iota(jnp.int32, (2 * half, TK), 1)
    tri = col <= (row & (half - 1))

    def step(hd, r0, nr, k0, width, masked, first_block):
        q2_ref, m_ref, l_ref, acc_ref = q2_all.at[hd], m_all.at[hd], l_all.at[hd], acc_all.at[hd]
        kv_rows = pl.ds(pl.multiple_of(k0, TK), width)
        kb = kx_ref[hd, kv_rows, :]
        vb = v_ref[0, kv_rows, hd * LANES:(hd + 1) * LANES]
        s = lax.dot_general(q2_ref[r0:r0 + nr, :], kb, (((1,), (1,)), ((), ())),
                            preferred_element_type=F32)
        if masked:
            s = jnp.where(tri, s, -jnp.inf)
        s_max = jnp.max(s, axis=-1, keepdims=True)
        if first_block:
            m_new = jnp.broadcast_to(s_max, (nr, LANES))
        else:
            m_prev = m_ref[r0:r0 + nr]
            m_new = jnp.maximum(m_prev, s_max)
            alpha = jnp.exp2(m_prev - m_new)
        p = jnp.exp2(s - jnp.concatenate([m_new] * (width // LANES), axis=1))
        l_new = jnp.sum(p, axis=-1, keepdims=True)
        acc_new = jnp.dot(p.astype(BF16), vb, preferred_element_type=F32)
        if first_block:
            l_ref[r0:r0 + nr] = jnp.broadcast_to(l_new, (nr, LANES))
            acc_ref[r0:r0 + nr] = acc_new
        else:
            l_ref[r0:r0 + nr] = alpha * l_ref[r0:r0 + nr] + l_new
            acc_ref[r0:r0 + nr] = alpha * acc_ref[r0:r0 + nr] + acc_new
        m_ref[r0:r0 + nr] = m_new

    for hd in range(N_HEADS):
        q2_ref = q2_all.at[hd]
        qb = q_ref[0, :, hd * LANES:(hd + 1) * LANES]
        zero = jnp.zeros_like(qb)
        q_maps = (jnp.where(first, qb, zero), jnp.where(first, zero, qb))
        qa = qaug_ref[hd]
        for part in range(2):
            rows = slice(part * half, (part + 1) * half)
            for m in range(2):
                dst = slice((2 * part + m) * half, (2 * part + m + 1) * half)
                q2_ref[dst, 0:LANES] = q_maps[m][rows]
                q2_ref[dst, LANES:] = qa[rows]

    for hd in range(N_HEADS):
        step(hd, 0, 2 * half, qi * TQ, TK, True, True)
        step(hd, 2 * half, 2 * half, qi * TQ, TK, False, True)
    for hd in range(N_HEADS):
        step(hd, 2 * half, 2 * half, qi * TQ + TK, TK, True, False)

    def body(j, carry):
        for hd in range(N_HEADS):
            step(hd, 0, 4 * half, 2 * j * TQ, 2 * TQ, False, False)
        return carry

    lax.fori_loop(0, qi // 2, body, 0)

    @pl.when(qi % 2 == 1)
    def _():
        for hd in range(N_HEADS):
            step(hd, 0, 4 * half, (qi - 1) * TQ, TQ, False, False)

    for hd in range(N_HEADS):
        l_ref, acc_ref = l_all.at[hd], acc_all.at[hd]
        for part in range(2):
            r = 2 * part * half
            o = (acc_ref[r:r + half] / l_ref[r:r + half]
                 - lam * (acc_ref[r + half:r + 2 * half] / l_ref[r + half:r + 2 * half]))
            o = _rms(o, sg_ref[...]) * (1.0 - lam_init)
            o_ref[0, part * half:(part + 1) * half, hd * LANES:(hd + 1) * LANES] = o.astype(BF16)


def _alibi_aug(s):
    start = 2.0 ** (-8.0 / N_HEADS)
    slopes = np.array([start ** (i + 1) for i in range(N_HEADS)], dtype=np.float32)
    c = math.log2(math.e) * slopes.astype(np.float64)[:, None] * np.arange(s)[None, :]
    pieces = []
    rem = c
    for _ in range(3):
        piece = rem.astype(jnp.bfloat16)
        pieces.append(piece)
        rem = rem - piece.astype(np.float64)
    one = np.ones_like(pieces[0])
    zero = np.zeros((N_HEADS, s, LANES - 6), dtype=jnp.bfloat16)
    kaug = np.concatenate([np.stack(pieces + [one] * 3, axis=-1), zero], axis=-1)
    qaug = np.concatenate([np.stack([one] * 3 + [-p for p in pieces], axis=-1), zero], axis=-1)
    return jnp.asarray(qaug), jnp.asarray(kaug)


def _attention(q, k, v, qaug, kaug, lq1, lk1, lq2, lk2, sub_g, lam_init):
    bsz, s, _ = q.shape
    nq = s // TQ
    fixed2 = lambda b_, i: (0, 0)
    seq = lambda b_, i: (b_, 0, 0)
    vec = pl.BlockSpec((1, HEAD_DIM), fixed2)
    return pl.pallas_call(
        functools.partial(_attn_kernel, lam_init=lam_init),
        grid=(bsz, nq),
        in_specs=[pl.BlockSpec((1, TQ, C_ATTN), lambda b_, i: (b_, i, 0)),
                  pl.BlockSpec((1, s, C_ATTN), seq),
                  pl.BlockSpec((1, s, C_ATTN), seq),
                  pl.BlockSpec((N_HEADS, TQ, LANES), lambda b_, i: (0, i, 0)),
                  pl.BlockSpec((N_HEADS, s, LANES), lambda b_, i: (0, 0, 0)),
                  vec, vec, vec, vec,
                  pl.BlockSpec((1, V_DIM), fixed2)],
        out_specs=pl.BlockSpec((1, TQ, C_ATTN), lambda b_, i: (b_, i, 0)),
        out_shape=jax.ShapeDtypeStruct((bsz, s, C_ATTN), BF16),
        scratch_shapes=[pltpu.VMEM((N_HEADS, s, 2 * LANES), BF16),
                        pltpu.VMEM((N_HEADS, 2 * TQ, 2 * LANES), BF16),
                        pltpu.VMEM((N_HEADS, 2 * TQ, LANES), F32),
                        pltpu.VMEM((N_HEADS, 2 * TQ, LANES), F32),
                        pltpu.VMEM((N_HEADS, 2 * TQ, V_DIM), F32)],
        compiler_params=_params(("parallel", "arbitrary")),
        name="diff_attention",
    )(q, k, v, qaug, kaug, lq1, lk1, lq2, lk2, sub_g)


def _split_bf16(a):
    hi = a.astype(BF16)
    lo = (a - hi.astype(F32)).astype(BF16)
    return hi, lo


def _out_proj_kernel(x_ref, yc_ref, yp_ref, ya_ref, w_ref, g_ref, rw_ref, rb_ref,
                     xm_ref, meta_t_ref, counts_ref, cnt_ref):
    x = _load_tokens(x_ref)
    x = x + jnp.dot(yc_ref[...], w_ref[0:C_CONV, :], preferred_element_type=F32)
    x = x + jnp.dot(yp_ref[...], w_ref[C_CONV:C_CONV + C_POOL, :], preferred_element_type=F32)
    x = x + jnp.dot(ya_ref[...], w_ref[C_CONV + C_POOL:, :], preferred_element_type=F32)
    tm = x.shape[0]
    _store_tokens(xm_ref, x, XM_ROWS)

    hf_hi, hf_lo = _split_bf16(_rms(x, g_ref[...]))

    both = jnp.dot(hf_hi, rw_ref[...], preferred_element_type=F32)
    logits = (both[:, 0:LANES] + both[:, LANES:]
              + jnp.dot(hf_lo, rw_ref[:, 0:LANES], preferred_element_type=F32)) + rb_ref[...]
    lane = lax.broadcasted_iota(jnp.int32, (tm, LANES), 1)
    lane_f = lane.astype(F32)
    neg = -jnp.inf

    def first_argmax(val, vmax):
        first = jnp.min(jnp.where(val == vmax, lane_f, float(LANES)), axis=-1, keepdims=True)
        return first.astype(jnp.int32)

    lg = jnp.where(lane < N_GROUPS, logits, neg)
    g_max = jnp.max(lg, axis=-1, keepdims=True)
    g_idx = first_argmax(lg, g_max)
    g_p = 1.0 / jnp.sum(jnp.exp(lg - g_max), axis=-1, keepdims=True)

    e_lane = lane - N_GROUPS
    in_group = (e_lane >= g_idx * PER_GROUP) & (e_lane < (g_idx + 1) * PER_GROUP)
    le = jnp.where(in_group, logits, neg)
    e_max = jnp.max(le, axis=-1, keepdims=True)
    pe = jnp.exp(le - e_max)
    pe = pe / jnp.sum(pe, axis=-1, keepdims=True)
    p1 = jnp.max(pe, axis=-1, keepdims=True)
    i1 = first_argmax(jnp.where(in_group, pe, neg), p1)
    pe2 = jnp.where(in_group & (lane != i1), pe, neg)
    p2 = jnp.max(pe2, axis=-1, keepdims=True)
    i2 = first_argmax(pe2, p2)
    denom = p1 + p2
    gate1 = g_p * (p1 / denom)
    gate2 = g_p * (p2 / denom)

    loc1 = i1 - N_GROUPS - g_idx * PER_GROUP
    loc2 = i2 - N_GROUPS - g_idx * PER_GROUP
    a = jnp.minimum(loc1, loc2)
    b = jnp.maximum(loc1, loc2)
    pair = jnp.where(a == 0, 0, jnp.where(a == 1, 3, 5)) + (b - a - 1)
    bucket = g_idx * N_PAIRS + pair
    gate_a = jnp.where(loc1 < loc2, gate1, gate2)
    gate_b = jnp.where(loc1 < loc2, gate2, gate1)

    @pl.when(pl.program_id(0) == 0)
    def _():
        cnt_ref[...] = jnp.zeros(cnt_ref.shape, F32)

    in_bucket = lane == bucket
    earlier = (lax.broadcasted_iota(jnp.int32, (tm, tm), 1)
               < lax.broadcasted_iota(jnp.int32, (tm, tm), 0)).astype(BF16)
    before = jnp.dot(earlier, in_bucket.astype(BF16), preferred_element_type=F32) + cnt_ref[...]
    rank = jnp.sum(jnp.where(in_bucket, before, 0.0), axis=-1, keepdims=True)
    cnt_ref[...] += jnp.sum(in_bucket.astype(F32), axis=0, keepdims=True)
    counts_ref[...] = cnt_ref[...]

    meta = jnp.where(lane == META_GATE_A, gate_a,
                     jnp.where(lane == META_GATE_B, gate_b,
                               jnp.where(lane == META_BUCKET, bucket.astype(F32),
                                         jnp.where(lane == META_RANK, rank, 0.0))))
    xm_ref[pl.ds(X_ROWS, tm, stride=XM_ROWS), :] = meta
    meta_t_ref[...] = meta.T[0:META_ROWS, :]


def _out_proj(x2, yc, yp, ya, w_bf, g, rw_hi_lo, rb):
    token_major = x2.shape[1] == LANES
    t = x2.shape[0] // X_ROWS if token_major else x2.shape[0]
    tm = TM_PROJ
    row = lambda i: (i, 0)
    fixed = lambda i: (0, 0)
    x_spec = pl.BlockSpec((tm * X_ROWS, LANES) if token_major else (tm, D_MODEL), row)
    return pl.pallas_call(
        _out_proj_kernel,
        grid=(t // tm,),
        in_specs=[x_spec,
                  pl.BlockSpec((tm, C_CONV), row),
                  pl.BlockSpec((tm, C_POOL), row),
                  pl.BlockSpec((tm, C_ATTN), row),
                  pl.BlockSpec((D_MODEL, D_MODEL), fixed),
                  pl.BlockSpec((1, D_MODEL), fixed),
                  pl.BlockSpec((D_MODEL, 2 * LANES), fixed),
                  pl.BlockSpec((1, LANES), fixed)],
        out_specs=[pl.BlockSpec((tm * XM_ROWS, LANES), row),
                   pl.BlockSpec((META_ROWS, tm), row),
                   pl.BlockSpec((1, LANES), fixed)],
        out_shape=[jax.ShapeDtypeStruct((t * XM_ROWS, LANES), F32),
                   jax.ShapeDtypeStruct((t // tm * META_ROWS, tm), F32),
                   jax.ShapeDtypeStruct((1, LANES), F32)],
        scratch_shapes=[pltpu.VMEM((1, LANES), F32)],
        compiler_params=_params(("arbitrary",)),
        name="out_proj_router",
    )(x2, yc, yp, ya, w_bf, g, rw_hi_lo, rb)


def _for_chunk_rows(dest_ref, start_copy, wait_copy):
    base = pl.program_id(0) * COPY_CHUNK

    def body(blk, carry):
        rows = [blk * DMA_UNROLL + u for u in range(DMA_UNROLL)]
        slots = [dest_ref[base + r] for r in rows]
        for u, (r, slot) in enumerate(zip(rows, slots)):
            start_copy(r, slot, u % 2)
        return carry

    lax.fori_loop(0, COPY_CHUNK // DMA_UNROLL, body, 0)
    for piece in range(COPY_CHUNK // COPY_WAIT):
        wait_copy(piece)


def _dispatch_kernel(dest_ref, x_ref, out_hbm, sem):
    def start_copy(r, slot, priority):
        pltpu.make_async_copy(x_ref.at[pl.ds(r * XM_ROWS, XM_ROWS)],
                              out_hbm.at[pl.ds(slot * XM_ROWS, XM_ROWS)],
                              sem.at[0]).start(priority=priority)

    def wait_copy(piece):
        n = COPY_WAIT * XM_ROWS
        pltpu.make_async_copy(x_ref.at[pl.ds(piece * n, n)], out_hbm.at[pl.ds(0, n)],
                              sem.at[0]).wait()

    _for_chunk_rows(dest_ref, start_copy, wait_copy)


def _dispatch(xm, dest):
    n_tokens = xm.shape[0] // XM_ROWS
    n_rows = dest.shape[0]
    assert n_tokens % COPY_CHUNK == 0 and n_rows % COPY_CHUNK == 0
    last = n_tokens // COPY_CHUNK - 1
    grid_spec = pltpu.PrefetchScalarGridSpec(
        num_scalar_prefetch=1,
        grid=(n_rows // COPY_CHUNK,),
        in_specs=[pl.BlockSpec((COPY_CHUNK * XM_ROWS, LANES),
                               lambda i, dest: (jnp.minimum(i, last), 0))],
        out_specs=pl.BlockSpec(memory_space=pl.ANY),
        scratch_shapes=[pltpu.SemaphoreType.DMA((1,))])
    return pl.pallas_call(
        _dispatch_kernel,
        grid_spec=grid_spec,
        out_shape=jax.ShapeDtypeStruct((n_rows * XM_ROWS, LANES), F32),
        compiler_params=_params(("arbitrary",)),
        name="moe_dispatch",
    )(dest, xm)


def _combine_kernel(dest_ref, ys_hbm, o_ref, *scratch, to_rows):
    buf = scratch[0] if to_rows else o_ref
    sem = scratch[-1]

    def start_copy(r, slot, priority):
        pltpu.make_async_copy(ys_hbm.at[pl.ds(slot * X_ROWS, X_ROWS)],
                              buf.at[pl.ds(r * X_ROWS, X_ROWS)],
                              sem.at[0]).start(priority=priority)

    def wait_copy(piece):
        n = COPY_WAIT * X_ROWS
        pltpu.make_async_copy(ys_hbm.at[pl.ds(0, n)], buf.at[pl.ds(piece * n, n)],
                              sem.at[0]).wait()

    _for_chunk_rows(dest_ref, start_copy, wait_copy)
    if to_rows:
        o_ref[...] = _load_tokens(buf)


def _combine(ys, dest, n_tokens, to_rows):
    assert n_tokens % COPY_CHUNK == 0
    chunk_rows = COPY_CHUNK * X_ROWS
    if to_rows:
        out_spec = pl.BlockSpec((COPY_CHUNK, D_MODEL), lambda i, dest: (i, 0))
        out_shape = jax.ShapeDtypeStruct((n_tokens, D_MODEL), F32)
        scratch = [pltpu.VMEM((chunk_rows, LANES), F32)]
    else:
        out_spec = pl.BlockSpec((chunk_rows, LANES), lambda i, dest: (i, 0))
        out_shape = jax.ShapeDtypeStruct((n_tokens * X_ROWS, LANES), F32)
        scratch = []
    grid_spec = pltpu.PrefetchScalarGridSpec(
        num_scalar_prefetch=1,
        grid=(n_tokens // COPY_CHUNK,),
        in_specs=[pl.BlockSpec(memory_space=pl.ANY)],
        out_specs=out_spec,
        scratch_shapes=scratch + [pltpu.SemaphoreType.DMA((1,))])
    return pl.pallas_call(
        functools.partial(_combine_kernel, to_rows=to_rows),
        grid_spec=grid_spec,
        out_shape=out_shape,
        compiler_params=_params(("arbitrary",)),
        name="moe_combine",
    )(dest, ys)


def _moe_kernel(te1_ref, te2_ref, nv_ref, used_ref,
                xs_ref, g_ref, wg1_ref, wu1_ref, wd1_ref, wg2_ref, wu2_ref, wd2_ref, ys_ref):
    i = pl.program_id(0)

    @pl.when(nv_ref[i] > 0)
    def _():
        x = _load_tokens(xs_ref, XM_ROWS)
        meta = xs_ref[pl.ds(X_ROWS, TM_MOE, stride=XM_ROWS), :]
        hf = _rms(x, g_ref[...]).astype(BF16)

        def expert(wg_ref, wu_ref, wd_ref, gate):
            a = jnp.dot(hf, wg_ref[0], preferred_element_type=F32)
            u = jnp.dot(hf, wu_ref[0], preferred_element_type=F32)
            act = a * jax.nn.sigmoid(a) * u * gate
            return jnp.dot(act.astype(BF16), wd_ref[0], preferred_element_type=F32)

        y = expert(wg1_ref, wu1_ref, wd1_ref, meta[:, META_GATE_A:META_GATE_A + 1])
        y = y + expert(wg2_ref, wu2_ref, wd2_ref, meta[:, META_GATE_B:META_GATE_B + 1])
        _store_tokens(ys_ref, x + y)

    @pl.when(nv_ref[i] == 0)
    def _():
        ys_ref[...] = jnp.zeros(ys_ref.shape, F32)


def _moe(xs, g, te1, te2, nv, used, wg_bf, wu_bf, wd_bf):
    tm = TM_MOE
    n_tiles = nv.shape[0]
    fixed = lambda i, te1, te2, nv, used: (0, 0)
    tile_in = lambda i, te1, te2, nv, used: (jnp.minimum(i, used[0] - 1), 0)
    tile_out = lambda i, te1, te2, nv, used: (i, 0)
    expert1 = lambda i, te1, te2, nv, used: (te1[i], 0, 0)
    expert2 = lambda i, te1, te2, nv, used: (te2[i], 0, 0)
    w_in = lambda index_map: pl.BlockSpec((1, D_MODEL, D_EXPERT), index_map)
    w_out = lambda index_map: pl.BlockSpec((1, D_EXPERT, D_MODEL), index_map)
    grid_spec = pltpu.PrefetchScalarGridSpec(
        num_scalar_prefetch=4,
        grid=(n_tiles,),
        in_specs=[pl.BlockSpec((tm * XM_ROWS, LANES), tile_in),
                  pl.BlockSpec((1, D_MODEL), fixed),
                  w_in(expert1), w_in(expert1), w_out(expert1),
                  w_in(expert2), w_in(expert2), w_out(expert2)],
        out_specs=pl.BlockSpec((tm * X_ROWS, LANES), tile_out))
    return pl.pallas_call(
        _moe_kernel,
        grid_spec=grid_spec,
        out_shape=jax.ShapeDtypeStruct((n_tiles * tm * X_ROWS, LANES), F32),
        compiler_params=_params(("arbitrary",)),
        name="moe",
    )(te1, te2, nv, used, xs, g, wg_bf, wu_bf, wd_bf, wg_bf, wu_bf, wd_bf)


def _route(meta_t, counts, t):
    tm = TM_MOE
    nb = N_GROUPS * N_PAIRS
    n_tiles = t // tm + nb
    meta = meta_t.reshape(t // TM_PROJ, META_ROWS, TM_PROJ)
    bucket = meta[:, META_BUCKET, :].reshape(t).astype(jnp.int32)
    rank = meta[:, META_RANK, :].reshape(t).astype(jnp.int32)
    counts = counts[0, :nb].astype(jnp.int32)
    ids = jnp.arange(nb, dtype=jnp.int32)

    tiles = (counts + tm - 1) // tm
    tile_end = jnp.cumsum(tiles)
    tile_start = tile_end - tiles
    used = tile_end[-1]
    row0 = tile_start * tm
    dest = rank + jnp.sum(jnp.where(bucket[:, None] == ids[None, :], row0[None, :], 0), axis=1)

    ti = jnp.arange(n_tiles, dtype=jnp.int32)
    tic = jnp.minimum(ti, used - 1)
    tb = jnp.sum((tile_end[None, :] <= tic[:, None]).astype(jnp.int32), axis=1)
    onehot_tb = tb[:, None] == ids[None, :]
    pick = lambda table: jnp.sum(jnp.where(onehot_tb, table[None, :], 0), axis=1)
    nv = jnp.where(ti < used, jnp.clip(pick(counts) - (tic - pick(tile_start)) * tm, 0, tm), 0)
    pair_a = jnp.asarray([0, 0, 0, 1, 1, 2], jnp.int32)
    pair_b = jnp.asarray([1, 2, 3, 2, 3, 3], jnp.int32)
    te1 = pick((ids // N_PAIRS) * PER_GROUP + pair_a[ids % N_PAIRS])
    te2 = pick((ids // N_PAIRS) * PER_GROUP + pair_b[ids % N_PAIRS])

    pads = tiles * tm - counts
    pad_end = jnp.cumsum(pads)
    k = jnp.arange(nb * tm, dtype=jnp.int32)
    kb = jnp.minimum(jnp.sum((pad_end[None, :] <= k[:, None]).astype(jnp.int32), axis=1), nb - 1)
    onehot_kb = kb[:, None] == ids[None, :]
    pick_k = lambda table: jnp.sum(jnp.where(onehot_kb, table[None, :], 0), axis=1)
    in_pad = pick_k(row0 + counts) + (k - pick_k(pad_end - pads))
    free_rows = jnp.where(k < pad_end[-1], in_pad, used * tm + (k - pad_end[-1]))
    i32 = lambda a: a.astype(jnp.int32)
    return (i32(jnp.concatenate([dest, free_rows])), i32(te1), i32(te2), i32(nv),
            i32(used).reshape(1))


def _block_diag(w):
    n = w.shape[0]
    eye = jnp.eye(n, dtype=w.dtype)
    return jnp.einsum("gcd,gh->gchd", w, eye).reshape(n * POOL_GROUP, n * POOL_GROUP)


def kernel(x, attn_norm_g, w_in, conv_w, conv_b, conv_ln_g, conv_ln_b, conv_pw_w, conv_pw_b,
           pool_w, pool_scale, q_norm_g, k_norm_g, lambda_q1, lambda_k1, lambda_q2, lambda_k2,
           attn_sub_norm_g, w_out, ffn_norm_g, router_g_w, router_g_b, router_e_w, router_e_b,
           w_gate, w_up, w_down):
    bsz, s, d = x.shape
    depth = w_in.shape[0]
    t = bsz * s
    qaug, kaug = _alibi_aug(s)
    row = lambda a: a.reshape(1, -1)

    x2 = x.reshape(t, d)
    for l in range(depth):
        lam_init = 0.8 - 0.6 * math.exp(-0.3 * l)
        y_conv, y_pool, q, k, v = _in_proj(
            x2, s, row(attn_norm_g[l]), w_in[l].astype(BF16),
            row(jnp.tile(q_norm_g[l], 2)), row(jnp.tile(k_norm_g[l], 2)),
            conv_w[l], row(conv_b[l]), row(conv_ln_g[l]), row(conv_ln_b[l]),
            conv_pw_w[l].astype(BF16), row(conv_pw_b[l]),
            _block_diag(pool_w[l]).astype(BF16), row(pool_scale[l]))
        y_attn = _attention(q.reshape(bsz, s, -1), k.reshape(bsz, s, -1), v.reshape(bsz, s, -1),
                            qaug, kaug, row(lambda_q1[l]), row(lambda_k1[l]), row(lambda_q2[l]),
                            row(lambda_k2[l]), row(attn_sub_norm_g[l]), lam_init)

        pad = LANES - N_GROUPS - N_EXPERTS
        rw = jnp.pad(jnp.concatenate([router_g_w[l], router_e_w[l]], axis=1), ((0, 0), (0, pad)))
        rb = jnp.pad(jnp.concatenate([router_g_b[l], router_e_b[l]]), (0, pad))
        rw_hi = rw.astype(BF16)
        rw_lo = (rw - rw_hi.astype(F32)).astype(BF16)
        rw_hi_lo = jnp.concatenate([rw_hi, rw_lo], axis=1)
        xm, meta_t, counts = _out_proj(
            x2, y_conv, y_pool, y_attn.reshape(t, -1),
            w_out[l].astype(BF16), row(ffn_norm_g[l]), rw_hi_lo, row(rb))

        dest, te1, te2, nv, used = _route(meta_t, counts, t)
        xs = _dispatch(xm, dest)
        ys = _moe(xs, row(ffn_norm_g[l]), te1, te2, nv, used, w_gate[l].astype(BF16),
                  w_up[l].astype(BF16), w_down[l].astype(BF16))
        x2 = _combine(ys, dest, t, to_rows=l == depth - 1)
    return x2.reshape(bsz, s, d)
```

```python
import functools
import math

import jax
import jax.numpy as jnp
import numpy as np
from jax import lax
from jax.experimental import pallas as pl
from jax.experimental.pallas import tpu as pltpu

D_MODEL = 1024
C_CONV = 256
C_POOL = 256
C_ATTN = 512
N_HEADS = 4
HEAD_DIM = 64
V_DIM = 2 * HEAD_DIM
CONV_WIDTH = 31
POOL_WINDOWS = (2, 4, 8, 16)
POOL_GROUP = C_POOL // len(POOL_WINDOWS)
N_IN = 2 * C_CONV + C_POOL + 3 * C_ATTN
N_GROUPS = 4
PER_GROUP = 4
N_EXPERTS = N_GROUPS * PER_GROUP
D_EXPERT = 256
EPS = 1e-6

LANES = 128
SUBLANES = 8
VMEM_LIMIT = 48 * 1024 * 1024

TM_PROJ = 512
HALO = 32
TQ = 512
TK = 256
TM_MOE = 256
N_PAIRS = 6
X_ROWS = D_MODEL // LANES
XM_ROWS = X_ROWS + 1
META_GATE_A, META_GATE_B, META_BUCKET, META_RANK = 0, 1, 2, 3
META_ROWS = 8
DMA_UNROLL = 8
COPY_CHUNK = 2048
COPY_WAIT = 512

F32 = jnp.float32
BF16 = jnp.bfloat16


def _params(sem):
    return pltpu.CompilerParams(dimension_semantics=sem, vmem_limit_bytes=VMEM_LIMIT)


def _rms(x, g):
    return x * lax.rsqrt(jnp.mean(x * x, axis=-1, keepdims=True) + EPS) * g


def _load_tokens(x_ref, rows_per_token=X_ROWS):
    if x_ref.shape[-1] != LANES:
        return x_ref[...]
    tm = x_ref.shape[0] // rows_per_token
    return jnp.concatenate(
        [x_ref[pl.ds(c, tm, stride=rows_per_token), :] for c in range(X_ROWS)], axis=1)


def _store_tokens(o_ref, x, rows_per_token=X_ROWS):
    tm = x.shape[0]
    for c in range(X_ROWS):
        o_ref[pl.ds(c, tm, stride=rows_per_token), :] = x[:, c * LANES:(c + 1) * LANES]


def _in_proj_kernel(x_ref, g_ref, w_ref, qg_ref, kg_ref,
                    cw_ref, cb_ref, lng_ref, lnb_ref, pw_ref, pwb_ref, plw_ref, pls_ref,
                    yc_ref, yp_ref, q_ref, k_ref, v_ref, cext_ref, pext_ref, *, tiles_per_seq):
    ts = yc_ref.shape[0]
    seq_tile = pl.program_id(0) % tiles_per_seq
    h = _rms(_load_tokens(x_ref), g_ref[...]).astype(BF16)

    def proj(c0, width):
        return jnp.dot(h, w_ref[:, c0:c0 + width], preferred_element_type=F32)

    @pl.when(seq_tile == 0)
    def _():
        cext_ref[0, 0:HALO, :] = jnp.zeros((HALO, C_CONV), F32)
        pext_ref[0, 0:HALO, :] = jnp.zeros((HALO, C_POOL), F32)

    z = proj(0, 2 * C_CONV)
    cext_ref[0, HALO:, :] = z[:, :C_CONV] * jax.nn.sigmoid(z[:, C_CONV:])
    pext_ref[0, HALO:, :] = proj(2 * C_CONV, C_POOL)

    lane = lax.broadcasted_iota(jnp.int32, (1, LANES), 1)
    first = lane < HEAD_DIM

    def qk_norm(z, gain_ref, out_ref, post_scale):
        gain = gain_ref[...]
        for hd in range(N_HEADS):
            blk = z[:, hd * LANES:(hd + 1) * LANES]
            sq = blk * blk
            s_all = jnp.sum(sq, axis=-1, keepdims=True)
            s_lo = jnp.sum(jnp.where(first, sq, 0.0), axis=-1, keepdims=True)
            r_lo = lax.rsqrt(s_lo * (1.0 / HEAD_DIM) + EPS)
            r_hi = lax.rsqrt((s_all - s_lo) * (1.0 / HEAD_DIM) + EPS)
            r = jnp.where(first, r_lo, r_hi)
            out_ref[:, hd * LANES:(hd + 1) * LANES] = (blk * r * gain * post_scale).astype(BF16)

    c_q = 2 * C_CONV + C_POOL
    qk_norm(proj(c_q, C_ATTN), qg_ref, q_ref, HEAD_DIM ** -0.5 * math.log2(math.e))
    qk_norm(proj(c_q + C_ATTN, C_ATTN), kg_ref, k_ref, 1.0)
    v_ref[...] = proj(c_q + 2 * C_ATTN, C_ATTN).astype(BF16)

    _conv_group(cext_ref, ts, cw_ref, cb_ref, lng_ref, lnb_ref, pw_ref, pwb_ref, yc_ref)
    _pool_group(pext_ref, ts, seq_tile * ts, plw_ref, pls_ref, yp_ref)

    cext_ref[0, 0:HALO, :] = cext_ref[0, ts:ts + HALO, :]
    pext_ref[0, 0:HALO, :] = pext_ref[0, ts:ts + HALO, :]


def _in_proj(x2, seq_len, g, w_bf, qg, kg, conv_w, conv_b, ln_g, ln_b, pw_bf, pw_b,
             pool_w_bf, pool_scale):
    token_major = x2.shape[1] == LANES
    t = x2.shape[0] // X_ROWS if token_major else x2.shape[0]
    tm = TM_PROJ
    assert seq_len % tm == 0
    row = lambda i: (i, 0)
    fixed = lambda i: (0, 0)
    full = lambda a: pl.BlockSpec(a.shape, fixed)
    x_spec = pl.BlockSpec((tm * X_ROWS, LANES) if token_major else (tm, D_MODEL), row)
    small = [g, w_bf, qg, kg, conv_w, conv_b, ln_g, ln_b, pw_bf, pw_b, pool_w_bf, pool_scale]
    widths = [C_CONV, C_POOL, C_ATTN, C_ATTN, C_ATTN]
    return pl.pallas_call(
        functools.partial(_in_proj_kernel, tiles_per_seq=seq_len // tm),
        grid=(t // tm,),
        in_specs=[x_spec] + [full(a) for a in small],
        out_specs=[pl.BlockSpec((tm, w), row) for w in widths],
        out_shape=[jax.ShapeDtypeStruct((t, w), BF16) for w in widths],
        scratch_shapes=[pltpu.VMEM((SUBLANES, tm + HALO, C_CONV), F32),
                        pltpu.VMEM((4, tm + HALO, C_POOL), F32)],
        compiler_params=_params(("arbitrary",)),
        name="in_proj_mix",
    )(x2, *small)


def _conv_group(ext_ref, ts, w_ref, b_ref, lng_ref, lnb_ref, pw_ref, pwb_ref, o_ref):
    n_shifted = ts + HALO - SUBLANES
    for s in range(1, SUBLANES):
        ext_ref[s, 0:n_shifted, :] = ext_ref[0, s:s + n_shifted, :]

    base = HALO - (CONV_WIDTH - 1)
    rows = 64
    for c in range(ts // rows):
        acc = jnp.broadcast_to(b_ref[...], (rows, C_CONV))
        for j in range(CONV_WIDTH):
            phase = (base + j) % SUBLANES
            r0 = c * rows + base + j - phase
            acc = acc + ext_ref[phase, r0:r0 + rows, :] * w_ref[j:j + 1, :]
        mu = jnp.mean(acc, axis=-1, keepdims=True)
        xc = acc - mu
        y = xc * lax.rsqrt(jnp.mean(xc * xc, axis=-1, keepdims=True) + EPS)
        y = y * lng_ref[...] + lnb_ref[...]
        y = y * jax.nn.sigmoid(y)
        out = jnp.dot(y.astype(BF16), pw_ref[...], preferred_element_type=F32) + pwb_ref[...]
        o_ref[c * rows:(c + 1) * rows, :] = out.astype(BF16)


def _pool_group(ext_ref, ts, pos0, w_ref, scale_ref, o_ref):
    n = ts + HALO
    assert POOL_WINDOWS == (2, 4, 8, 16) and HALO >= 4 * SUBLANES
    sums = []
    for k in range(1, 5):
        lo = SUBLANES * k
        shift = 1 << (k - 1)
        level = ext_ref[k - 1, lo:n, :] + ext_ref[k - 1, lo - shift:n - shift, :]
        sums.append(level[HALO - lo:, :])
        if k < 4:
            ext_ref[k, lo:n, :] = level

    lane = lax.broadcasted_iota(jnp.int32, (1, C_POOL), 1)
    group = lane // POOL_GROUP
    win = jnp.left_shift(2, group)
    acc = jnp.where(group == 0, sums[0],
                    jnp.where(group == 1, sums[1], jnp.where(group == 2, sums[2], sums[3])))
    pos = pos0 + lax.broadcasted_iota(jnp.int32, (ts, 1), 0)
    count = jnp.minimum(pos + 1, win).astype(F32)
    diff = acc / count - ext_ref[0, HALO:, :]
    out = jnp.dot(diff.astype(BF16), w_ref[...], preferred_element_type=F32) * scale_ref[...]
    o_ref[...] = out.astype(BF16)


def _attn_kernel(q_ref, k_ref, v_ref, qaug_ref, kaug_ref, lq1_ref, lk1_ref, lq2_ref, lk2_ref,
                 sg_ref, o_ref, kx_ref, q2_all, m_all, l_all, acc_all, *, lam_init):
    qi = pl.program_id(1)
    half = TQ // 2
    assert half == TK

    @pl.when(qi == 0)
    def _():
        for hd in range(N_HEADS):
            kx_ref[hd, :, 0:LANES] = k_ref[0, :, hd * LANES:(hd + 1) * LANES]
            kx_ref[hd, :, LANES:] = kaug_ref[hd]

    lam = (jnp.exp(jnp.sum(lq1_ref[...] * lk1_ref[...], axis=-1, keepdims=True))
           - jnp.exp(jnp.sum(lq2_ref[...] * lk2_ref[...], axis=-1, keepdims=True))
           + lam_init)
    first = lax.broadcasted_iota(jnp.int32, (1, LANES), 1) < HEAD_DIM
    row = lax.broadcasted_iota(jnp.int32, (2 * half, TK), 0)
    col = lax.broadcasted_iota(jnp.int32, (2 * half, TK), 1)
    tri = col <= (row & (half - 1))

    def step(hd, r0, nr, k0, width, masked, first_block):
        q2_ref, m_ref, l_ref, acc_ref = q2_all.at[hd], m_all.at[hd], l_all.at[hd], acc_all.at[hd]
        kv_rows = pl.ds(pl.multiple_of(k0, TK), width)
        kb = kx_ref[hd, kv_rows, :]
        vb = v_ref[0, kv_rows, hd * LANES:(hd + 1) * LANES]
        s = lax.dot_general(q2_ref[r0:r0 + nr, :], kb, (((1,), (1,)), ((), ())),
                            preferred_element_type=F32)
        if masked:
            s = jnp.where(tri, s, -jnp.inf)
        s_max = jnp.max(s, axis=-1, keepdims=True)
        if first_block:
            m_new = jnp.broadcast_to(s_max, (nr, LANES))
        else:
            m_prev = m_ref[r0:r0 + nr]
            m_new = jnp.maximum(m_prev, s_max)
            alpha = jnp.exp2(m_prev - m_new)
        p = jnp.exp2(s - jnp.concatenate([m_new] * (width // LANES), axis=1))
        l_new = jnp.sum(p, axis=-1, keepdims=True)
        acc_new = jnp.dot(p.astype(BF16), vb, preferred_element_type=F32)
        if first_block:
            l_ref[r0:r0 + nr] = jnp.broadcast_to(l_new, (nr, LANES))
            acc_ref[r0:r0 + nr] = acc_new
        else:
            l_ref[r0:r0 + nr] = alpha * l_ref[r0:r0 + nr] + l_new
            acc_ref[r0:r0 + nr] = alpha * acc_ref[r0:r0 + nr] + acc_new
        m_ref[r0:r0 + nr] = m_new

    for hd in range(N_HEADS):
        q2_ref = q2_all.at[hd]
        qb = q_ref[0, :, hd * LANES:(hd + 1) * LANES]
        zero = jnp.zeros_like(qb)
        q_maps = (jnp.where(first, qb, zero), jnp.where(first, zero, qb))
        qa = qaug_ref[hd]
        for part in range(2):
            rows = slice(part * half, (part + 1) * half)
            for m in range(2):
                dst = slice((2 * part + m) * half, (2 * part + m + 1) * half)
                q2_ref[dst, 0:LANES] = q_maps[m][rows]
                q2_ref[dst, LANES:] = qa[rows]

    for hd in range(N_HEADS):
        step(hd, 0, 2 * half, qi * TQ, TK, True, True)
        step(hd, 2 * half, 2 * half, qi * TQ, TK, False, True)
    for hd in range(N_HEADS):
        step(hd, 2 * half, 2 * half, qi * TQ + TK, TK, True, False)

    def body(j, carry):
        for hd in range(N_HEADS):
            step(hd, 0, 4 * half, 2 * j * TQ, 2 * TQ, False, False)
        return carry

    lax.fori_loop(0, qi // 2, body, 0)

    @pl.when(qi % 2 == 1)
    def _():
        for hd in range(N_HEADS):
            step(hd, 0, 4 * half, (qi - 1) * TQ, TQ, False, False)

    for hd in range(N_HEADS):
        l_ref, acc_ref = l_all.at[hd], acc_all.at[hd]
        for part in range(2):
            r = 2 * part * half
            o = (acc_ref[r:r + half] / l_ref[r:r + half]
                 - lam * (acc_ref[r + half:r + 2 * half] / l_ref[r + half:r + 2 * half]))
            o = _rms(o, sg_ref[...]) * (1.0 - lam_init)
            o_ref[0, part * half:(part + 1) * half, hd * LANES:(hd + 1) * LANES] = o.astype(BF16)


def _alibi_aug(s):
    start = 2.0 ** (-8.0 / N_HEADS)
    slopes = np.array([start ** (i + 1) for i in range(N_HEADS)], dtype=np.float32)
    c = math.log2(math.e) * slopes.astype(np.float64)[:, None] * np.arange(s)[None, :]
    pieces = []
    rem = c
    for _ in range(3):
        piece = rem.astype(jnp.bfloat16)
        pieces.append(piece)
        rem = rem - piece.astype(np.float64)
    one = np.ones_like(pieces[0])
    zero = np.zeros((N_HEADS, s, LANES - 6), dtype=jnp.bfloat16)
    kaug = np.concatenate([np.stack(pieces + [one] * 3, axis=-1), zero], axis=-1)
    qaug = np.concatenate([np.stack([one] * 3 + [-p for p in pieces], axis=-1), zero], axis=-1)
    return jnp.asarray(qaug), jnp.asarray(kaug)


def _attention(q, k, v, qaug, kaug, lq1, lk1, lq2, lk2, sub_g, lam_init):
    bsz, s, _ = q.shape
    nq = s // TQ
    fixed2 = lambda b_, i: (0, 0)
    seq = lambda b_, i: (b_, 0, 0)
    vec = pl.BlockSpec((1, HEAD_DIM), fixed2)
    return pl.pallas_call(
        functools.partial(_attn_kernel, lam_init=lam_init),
        grid=(bsz, nq),
        in_specs=[pl.BlockSpec((1, TQ, C_ATTN), lambda b_, i: (b_, i, 0)),
                  pl.BlockSpec((1, s, C_ATTN), seq),
                  pl.BlockSpec((1, s, C_ATTN), seq),
                  pl.BlockSpec((N_HEADS, TQ, LANES), lambda b_, i: (0, i, 0)),
                  pl.BlockSpec((N_HEADS, s, LANES), lambda b_, i: (0, 0, 0)),
                  vec, vec, vec, vec,
                  pl.BlockSpec((1, V_DIM), fixed2)],
        out_specs=pl.BlockSpec((1, TQ, C_ATTN), lambda b_, i: (b_, i, 0)),
        out_shape=jax.ShapeDtypeStruct((bsz, s, C_ATTN), BF16),
        scratch_shapes=[pltpu.VMEM((N_HEADS, s, 2 * LANES), BF16),
                        pltpu.VMEM((N_HEADS, 2 * TQ, 2 * LANES), BF16),
                        pltpu.VMEM((N_HEADS, 2 * TQ, LANES), F32),
                        pltpu.VMEM((N_HEADS, 2 * TQ, LANES), F32),
                        pltpu.VMEM((N_HEADS, 2 * TQ, V_DIM), F32)],
        compiler_params=_params(("parallel", "arbitrary")),
        name="diff_attention",
    )(q, k, v, qaug, kaug, lq1, lk1, lq2, lk2, sub_g)


def _split_bf16(a):
    hi = a.astype(BF16)
    lo = (a - hi.astype(F32)).astype(BF16)
    return hi, lo


def _out_proj_kernel(x_ref, yc_ref, yp_ref, ya_ref, w_ref, g_ref, rw_ref, rb_ref,
                     xm_ref, meta_t_ref, counts_ref, cnt_ref):
    x = _load_tokens(x_ref)
    x = x + jnp.dot(yc_ref[...], w_ref[0:C_CONV, :], preferred_element_type=F32)
    x = x + jnp.dot(yp_ref[...], w_ref[C_CONV:C_CONV + C_POOL, :], preferred_element_type=F32)
    x = x + jnp.dot(ya_ref[...], w_ref[C_CONV + C_POOL:, :], preferred_element_type=F32)
    tm = x.shape[0]
    _store_tokens(xm_ref, x, XM_ROWS)

    hf_hi, hf_lo = _split_bf16(_rms(x, g_ref[...]))

    both = jnp.dot(hf_hi, rw_ref[...], preferred_element_type=F32)
    logits = (both[:, 0:LANES] + both[:, LANES:]
              + jnp.dot(hf_lo, rw_ref[:, 0:LANES], preferred_element_type=F32)) + rb_ref[...]
    lane = lax.broadcasted_iota(jnp.int32, (tm, LANES), 1)
    lane_f = lane.astype(F32)
    neg = -jnp.inf

    def first_argmax(val, vmax):
        first = jnp.min(jnp.where(val == vmax, lane_f, float(LANES)), axis=-1, keepdims=True)
        return first.astype(jnp.int32)

    lg = jnp.where(lane < N_GROUPS, logits, neg)
    g_max = jnp.max(lg, axis=-1, keepdims=True)
    g_idx = first_argmax(lg, g_max)
    g_p = 1.0 / jnp.sum(jnp.exp(lg - g_max), axis=-1, keepdims=True)

    e_lane = lane - N_GROUPS
    in_group = (e_lane >= g_idx * PER_GROUP) & (e_lane < (g_idx + 1) * PER_GROUP)
    le = jnp.where(in_group, logits, neg)
    e_max = jnp.max(le, axis=-1, keepdims=True)
    pe = jnp.exp(le - e_max)
    pe = pe / jnp.sum(pe, axis=-1, keepdims=True)
    p1 = jnp.max(pe, axis=-1, keepdims=True)
    i1 = first_argmax(jnp.where(in_group, pe, neg), p1)
    pe2 = jnp.where(in_group & (lane != i1), pe, neg)
    p2 = jnp.max(pe2, axis=-1, keepdims=True)
    i2 = first_argmax(pe2, p2)
    denom = p1 + p2
    gate1 = g_p * (p1 / denom)
    gate2 = g_p * (p2 / denom)

    loc1 = i1 - N_GROUPS - g_idx * PER_GROUP
    loc2 = i2 - N_GROUPS - g_idx * PER_GROUP
    a = jnp.minimum(loc1, loc2)
    b = jnp.maximum(loc1, loc2)
    pair = jnp.where(a == 0, 0, jnp.where(a == 1, 3, 5)) + (b - a - 1)
    bucket = g_idx * N_PAIRS + pair
    gate_a = jnp.where(loc1 < loc2, gate1, gate2)
    gate_b = jnp.where(loc1 < loc2, gate2, gate1)

    @pl.when(pl.program_id(0) == 0)
    def _():
        cnt_ref[...] = jnp.zeros(cnt_ref.shape, F32)

    in_bucket = lane == bucket
    earlier = (lax.broadcasted_iota(jnp.int32, (tm, tm), 1)
               < lax.broadcasted_iota(jnp.int32, (tm, tm), 0)).astype(BF16)
    before = jnp.dot(earlier, in_bucket.astype(BF16), preferred_element_type=F32) + cnt_ref[...]
    rank = jnp.sum(jnp.where(in_bucket, before, 0.0), axis=-1, keepdims=True)
    cnt_ref[...] += jnp.sum(in_bucket.astype(F32), axis=0, keepdims=True)
    counts_ref[...] = cnt_ref[...]

    meta = jnp.where(lane == META_GATE_A, gate_a,
                     jnp.where(lane == META_GATE_B, gate_b,
                               jnp.where(lane == META_BUCKET, bucket.astype(F32),
                                         jnp.where(lane == META_RANK, rank, 0.0))))
    xm_ref[pl.ds(X_ROWS, tm, stride=XM_ROWS), :] = meta
    meta_t_ref[...] = meta.T[0:META_ROWS, :]


def _out_proj(x2, yc, yp, ya, w_bf, g, rw_hi_lo, rb):
    token_major = x2.shape[1] == LANES
    t = x2.shape[0] // X_ROWS if token_major else x2.shape[0]
    tm = TM_PROJ
    row = lambda i: (i, 0)
    fixed = lambda i: (0, 0)
    x_spec = pl.BlockSpec((tm * X_ROWS, LANES) if token_major else (tm, D_MODEL), row)
    return pl.pallas_call(
        _out_proj_kernel,
        grid=(t // tm,),
        in_specs=[x_spec,
                  pl.BlockSpec((tm, C_CONV), row),
                  pl.BlockSpec((tm, C_POOL), row),
                  pl.BlockSpec((tm, C_ATTN), row),
                  pl.BlockSpec((D_MODEL, D_MODEL), fixed),
                  pl.BlockSpec((1, D_MODEL), fixed),
                  pl.BlockSpec((D_MODEL, 2 * LANES), fixed),
                  pl.BlockSpec((1, LANES), fixed)],
        out_specs=[pl.BlockSpec((tm * XM_ROWS, LANES), row),
                   pl.BlockSpec((META_ROWS, tm), row),
                   pl.BlockSpec((1, LANES), fixed)],
        out_shape=[jax.ShapeDtypeStruct((t * XM_ROWS, LANES), F32),
                   jax.ShapeDtypeStruct((t // tm * META_ROWS, tm), F32),
                   jax.ShapeDtypeStruct((1, LANES), F32)],
        scratch_shapes=[pltpu.VMEM((1, LANES), F32)],
        compiler_params=_params(("arbitrary",)),
        name="out_proj_router",
    )(x2, yc, yp, ya, w_bf, g, rw_hi_lo, rb)


def _for_chunk_rows(dest_ref, start_copy, wait_copy):
    base = pl.program_id(0) * COPY_CHUNK

    def body(blk, carry):
        rows = [blk * DMA_UNROLL + u for u in range(DMA_UNROLL)]
        slots = [dest_ref[base + r] for r in rows]
        for u, (r, slot) in enumerate(zip(rows, slots)):
            start_copy(r, slot, u % 2)
        return carry

    lax.fori_loop(0, COPY_CHUNK // DMA_UNROLL, body, 0)
    for piece in range(COPY_CHUNK // COPY_WAIT):
        wait_copy(piece)


def _dispatch_kernel(dest_ref, x_ref, out_hbm, sem):
    def start_copy(r, slot, priority):
        pltpu.make_async_copy(x_ref.at[pl.ds(r * XM_ROWS, XM_ROWS)],
                              out_hbm.at[pl.ds(slot * XM_ROWS, XM_ROWS)],
                              sem.at[0]).start(priority=priority)

    def wait_copy(piece):
        n = COPY_WAIT * XM_ROWS
        pltpu.make_async_copy(x_ref.at[pl.ds(piece * n, n)], out_hbm.at[pl.ds(0, n)],
                              sem.at[0]).wait()

    _for_chunk_rows(dest_ref, start_copy, wait_copy)


def _dispatch(xm, dest):
    n_tokens = xm.shape[0] // XM_ROWS
    n_rows = dest.shape[0]
    assert n_tokens % COPY_CHUNK == 0 and n_rows % COPY_CHUNK == 0
    last = n_tokens // COPY_CHUNK - 1
    grid_spec = pltpu.PrefetchScalarGridSpec(
        num_scalar_prefetch=1,
        grid=(n_rows // COPY_CHUNK,),
        in_specs=[pl.BlockSpec((COPY_CHUNK * XM_ROWS, LANES),
                               lambda i, dest: (jnp.minimum(i, last), 0))],
        out_specs=pl.BlockSpec(memory_space=pl.ANY),
        scratch_shapes=[pltpu.SemaphoreType.DMA((1,))])
    return pl.pallas_call(
        _dispatch_kernel,
        grid_spec=grid_spec,
        out_shape=jax.ShapeDtypeStruct((n_rows * XM_ROWS, LANES), F32),
        compiler_params=_params(("arbitrary",)),
        name="moe_dispatch",
    )(dest, xm)


def _combine_kernel(dest_ref, ys_hbm, o_ref, *scratch, to_rows):
    buf = scratch[0] if to_rows else o_ref
    sem = scratch[-1]

    def start_copy(r, slot, priority):
        pltpu.make_async_copy(ys_hbm.at[pl.ds(slot * X_ROWS, X_ROWS)],
                              buf.at[pl.ds(r * X_ROWS, X_ROWS)],
                              sem.at[0]).start(priority=priority)

    def wait_copy(piece):
        n = COPY_WAIT * X_ROWS
        pltpu.make_async_copy(ys_hbm.at[pl.ds(0, n)], buf.at[pl.ds(piece * n, n)],
                              sem.at[0]).wait()

    _for_chunk_rows(dest_ref, start_copy, wait_copy)
    if to_rows:
        o_ref[...] = _load_tokens(buf)


def _combine(ys, dest, n_tokens, to_rows):
    assert n_tokens % COPY_CHUNK == 0
    chunk_rows = COPY_CHUNK * X_ROWS
    if to_rows:
        out_spec = pl.BlockSpec((COPY_CHUNK, D_MODEL), lambda i, dest: (i, 0))
        out_shape = jax.ShapeDtypeStruct((n_tokens, D_MODEL), F32)
        scratch = [pltpu.VMEM((chunk_rows, LANES), F32)]
    else:
        out_spec = pl.BlockSpec((chunk_rows, LANES), lambda i, dest: (i, 0))
        out_shape = jax.ShapeDtypeStruct((n_tokens * X_ROWS, LANES), F32)
        scratch = []
    grid_spec = pltpu.PrefetchScalarGridSpec(
        num_scalar_prefetch=1,
        grid=(n_tokens // COPY_CHUNK,),
        in_specs=[pl.BlockSpec(memory_space=pl.ANY)],
        out_specs=out_spec,
        scratch_shapes=scratch + [pltpu.SemaphoreType.DMA((1,))])
    return pl.pallas_call(
        functools.partial(_combine_kernel, to_rows=to_rows),
        grid_spec=grid_spec,
        out_shape=out_shape,
        compiler_params=_params(("arbitrary",)),
        name="moe_combine",
    )(dest, ys)


def _moe_kernel(te1_ref, te2_ref, nv_ref, used_ref,
                xs_ref, g_ref, wg1_ref, wu1_ref, wd1_ref, wg2_ref, wu2_ref, wd2_ref, ys_ref,
                w_in_bf, w_out_bf):
    i = pl.program_id(0)

    prev = jnp.maximum(i - 1, 0)
    new_pair = (i == 0) | (te1_ref[i] != te1_ref[prev]) | (te2_ref[i] != te2_ref[prev])

    @pl.when(new_pair)
    def _():
        for slot, w_ref in enumerate((wg1_ref, wu1_ref, wg2_ref, wu2_ref)):
            w_in_bf[slot] = w_ref[0].astype(BF16)
        for slot, w_ref in enumerate((wd1_ref, wd2_ref)):
            w_out_bf[slot] = w_ref[0].astype(BF16)

    @pl.when(nv_ref[i] > 0)
    def _():
        x = _load_tokens(xs_ref, XM_ROWS)
        meta = xs_ref[pl.ds(X_ROWS, TM_MOE, stride=XM_ROWS), :]
        hf = _rms(x, g_ref[...]).astype(BF16)

        def expert(e, gate):
            a = jnp.dot(hf, w_in_bf[2 * e], preferred_element_type=F32)
            u = jnp.dot(hf, w_in_bf[2 * e + 1], preferred_element_type=F32)
            act = a * jax.nn.sigmoid(a) * u * gate
            return jnp.dot(act.astype(BF16), w_out_bf[e], preferred_element_type=F32)

        y = expert(0, meta[:, META_GATE_A:META_GATE_A + 1])
        y = y + expert(1, meta[:, META_GATE_B:META_GATE_B + 1])
        _store_tokens(ys_ref, x + y)

    @pl.when(nv_ref[i] == 0)
    def _():
        ys_ref[...] = jnp.zeros(ys_ref.shape, F32)


def _moe(xs, g, te1, te2, nv, used, w_gate, w_up, w_down):
    tm = TM_MOE
    n_tiles = nv.shape[0]
    fixed = lambda i, te1, te2, nv, used: (0, 0)
    tile_in = lambda i, te1, te2, nv, used: (jnp.minimum(i, used[0] - 1), 0)
    tile_out = lambda i, te1, te2, nv, used: (i, 0)
    expert1 = lambda i, te1, te2, nv, used: (te1[i], 0, 0)
    expert2 = lambda i, te1, te2, nv, used: (te2[i], 0, 0)
    w_in = lambda index_map: pl.BlockSpec((1, D_MODEL, D_EXPERT), index_map)
    w_out = lambda index_map: pl.BlockSpec((1, D_EXPERT, D_MODEL), index_map)
    grid_spec = pltpu.PrefetchScalarGridSpec(
        num_scalar_prefetch=4,
        grid=(n_tiles,),
        in_specs=[pl.BlockSpec((tm * XM_ROWS, LANES), tile_in),
                  pl.BlockSpec((1, D_MODEL), fixed),
                  w_in(expert1), w_in(expert1), w_out(expert1),
                  w_in(expert2), w_in(expert2), w_out(expert2)],
        out_specs=pl.BlockSpec((tm * X_ROWS, LANES), tile_out),
        scratch_shapes=[pltpu.VMEM((4, D_MODEL, D_EXPERT), BF16),
                        pltpu.VMEM((2, D_EXPERT, D_MODEL), BF16)])
    return pl.pallas_call(
        _moe_kernel,
        grid_spec=grid_spec,
        out_shape=jax.ShapeDtypeStruct((n_tiles * tm * X_ROWS, LANES), F32),
        compiler_params=_params(("arbitrary",)),
        name="moe",
    )(te1, te2, nv, used, xs, g, w_gate, w_up, w_down, w_gate, w_up, w_down)


def _route(meta_t, counts, t):
    tm = TM_MOE
    nb = N_GROUPS * N_PAIRS
    n_tiles = t // tm + nb
    meta = meta_t.reshape(t // TM_PROJ, META_ROWS, TM_PROJ)
    bucket = meta[:, META_BUCKET, :].reshape(t).astype(jnp.int32)
    rank = meta[:, META_RANK, :].reshape(t).astype(jnp.int32)
    counts = counts[0, :nb].astype(jnp.int32)
    ids = jnp.arange(nb, dtype=jnp.int32)

    tiles = (counts + tm - 1) // tm
    tile_end = jnp.cumsum(tiles)
    tile_start = tile_end - tiles
    used = tile_end[-1]
    row0 = tile_start * tm
    dest = rank + jnp.sum(jnp.where(bucket[:, None] == ids[None, :], row0[None, :], 0), axis=1)

    ti = jnp.arange(n_tiles, dtype=jnp.int32)
    tic = jnp.minimum(ti, used - 1)
    tb = jnp.sum((tile_end[None, :] <= tic[:, None]).astype(jnp.int32), axis=1)
    onehot_tb = tb[:, None] == ids[None, :]
    pick = lambda table: jnp.sum(jnp.where(onehot_tb, table[None, :], 0), axis=1)
    nv = jnp.where(ti < used, jnp.clip(pick(counts) - (tic - pick(tile_start)) * tm, 0, tm), 0)
    pair_a = jnp.asarray([0, 0, 0, 1, 1, 2], jnp.int32)
    pair_b = jnp.asarray([1, 2, 3, 2, 3, 3], jnp.int32)
    te1 = pick((ids // N_PAIRS) * PER_GROUP + pair_a[ids % N_PAIRS])
    te2 = pick((ids // N_PAIRS) * PER_GROUP + pair_b[ids % N_PAIRS])

    pads = tiles * tm - counts
    pad_end = jnp.cumsum(pads)
    k = jnp.arange(nb * tm, dtype=jnp.int32)
    kb = jnp.minimum(jnp.sum((pad_end[None, :] <= k[:, None]).astype(jnp.int32), axis=1), nb - 1)
    onehot_kb = kb[:, None] == ids[None, :]
    pick_k = lambda table: jnp.sum(jnp.where(onehot_kb, table[None, :], 0), axis=1)
    in_pad = pick_k(row0 + counts) + (k - pick_k(pad_end - pads))
    free_rows = jnp.where(k < pad_end[-1], in_pad, used * tm + (k - pad_end[-1]))
    i32 = lambda a: a.astype(jnp.int32)
    return (i32(jnp.concatenate([dest, free_rows])), i32(te1), i32(te2), i32(nv),
            i32(used).reshape(1))


def _block_diag(w):
    n = w.shape[0]
    eye = jnp.eye(n, dtype=w.dtype)
    return jnp.einsum("gcd,gh->gchd", w, eye).reshape(n * POOL_GROUP, n * POOL_GROUP)


def kernel(x, attn_norm_g, w_in, conv_w, conv_b, conv_ln_g, conv_ln_b, conv_pw_w, conv_pw_b,
           pool_w, pool_scale, q_norm_g, k_norm_g, lambda_q1, lambda_k1, lambda_q2, lambda_k2,
           attn_sub_norm_g, w_out, ffn_norm_g, router_g_w, router_g_b, router_e_w, router_e_b,
           w_gate, w_up, w_down):
    bsz, s, d = x.shape
    depth = w_in.shape[0]
    t = bsz * s
    qaug, kaug = _alibi_aug(s)
    row = lambda a: a.reshape(1, -1)

    x2 = x.reshape(t, d)
    for l in range(depth):
        lam_init = 0.8 - 0.6 * math.exp(-0.3 * l)
        y_conv, y_pool, q, k, v = _in_proj(
            x2, s, row(attn_norm_g[l]), w_in[l].astype(BF16),
            row(jnp.tile(q_norm_g[l], 2)), row(jnp.tile(k_norm_g[l], 2)),
            conv_w[l], row(conv_b[l]), row(conv_ln_g[l]), row(conv_ln_b[l]),
            conv_pw_w[l].astype(BF16), row(conv_pw_b[l]),
            _block_diag(pool_w[l]).astype(BF16), row(pool_scale[l]))
        y_attn = _attention(q.reshape(bsz, s, -1), k.reshape(bsz, s, -1), v.reshape(bsz, s, -1),
                            qaug, kaug, row(lambda_q1[l]), row(lambda_k1[l]), row(lambda_q2[l]),
                            row(lambda_k2[l]), row(attn_sub_norm_g[l]), lam_init)

        pad = LANES - N_GROUPS - N_EXPERTS
        rw = jnp.pad(jnp.concatenate([router_g_w[l], router_e_w[l]], axis=1), ((0, 0), (0, pad)))
        rb = jnp.pad(jnp.concatenate([router_g_b[l], router_e_b[l]]), (0, pad))
        rw_hi = rw.astype(BF16)
        rw_lo = (rw - rw_hi.astype(F32)).astype(BF16)
        rw_hi_lo = jnp.concatenate([rw_hi, rw_lo], axis=1)
        xm, meta_t, counts = _out_proj(
            x2, y_conv, y_pool, y_attn.reshape(t, -1),
            w_out[l].astype(BF16), row(ffn_norm_g[l]), rw_hi_lo, row(rb))

        dest, te1, te2, nv, used = _route(meta_t, counts, t)
        xs = _dispatch(xm, dest)
        ys = _moe(xs, row(ffn_norm_g[l]), te1, te2, nv, used, w_gate[l], w_up[l], w_down[l])
        x2 = _combine(ys, dest, t, to_rows=l == depth - 1)
    return x2.reshape(bsz, s, d)
```

```python
import functools
import math

import jax
import jax.numpy as jnp
import numpy as np
from jax import lax
from jax.experimental import pallas as pl
from jax.experimental.pallas import tpu as pltpu

D_MODEL = 1024
C_CONV = 256
C_POOL = 256
C_ATTN = 512
N_HEADS = 4
HEAD_DIM = 64
V_DIM = 2 * HEAD_DIM
CONV_WIDTH = 31
POOL_WINDOWS = (2, 4, 8, 16)
POOL_GROUP = C_POOL // len(POOL_WINDOWS)
N_IN = 2 * C_CONV + C_POOL + 3 * C_ATTN
N_GROUPS = 4
PER_GROUP = 4
N_EXPERTS = N_GROUPS * PER_GROUP
D_EXPERT = 256
EPS = 1e-6

LANES = 128
SUBLANES = 8
VMEM_LIMIT = 48 * 1024 * 1024

TM_PROJ = 512
TM_OUT = 512
HALO = 32
TQ = 512
TK = 256
TM_MOE = 256
N_PAIRS = 6
X_ROWS = D_MODEL // LANES
XM_ROWS = X_ROWS + 1
META_GATE_A, META_GATE_B, META_BUCKET, META_RANK = 0, 1, 2, 3
META_ROWS = 8
DMA_UNROLL = 8
COPY_CHUNK = 2048
COPY_WAIT = 512

F32 = jnp.float32
BF16 = jnp.bfloat16


def _params(sem):
    return pltpu.CompilerParams(dimension_semantics=sem, vmem_limit_bytes=VMEM_LIMIT)


def _rms(x, g):
    return x * lax.rsqrt(jnp.mean(x * x, axis=-1, keepdims=True) + EPS) * g


def _load_tokens(x_ref, rows_per_token=X_ROWS):
    if x_ref.shape[-1] != LANES:
        return x_ref[...]
    tm = x_ref.shape[0] // rows_per_token
    return jnp.concatenate(
        [x_ref[pl.ds(c, tm, stride=rows_per_token), :] for c in range(X_ROWS)], axis=1)


def _store_tokens(o_ref, x, rows_per_token=X_ROWS):
    tm = x.shape[0]
    for c in range(X_ROWS):
        o_ref[pl.ds(c, tm, stride=rows_per_token), :] = x[:, c * LANES:(c + 1) * LANES]


def _in_proj_kernel(x_ref, g_ref, w_ref, qg_ref, kg_ref,
                    cw_ref, cb_ref, lng_ref, lnb_ref, pw_ref, pwb_ref, plw_ref, pls_ref,
                    yc_ref, yp_ref, q_ref, k_ref, v_ref, cext_ref, pext_ref, *, tiles_per_seq):
    ts = yc_ref.shape[0]
    seq_tile = pl.program_id(0) % tiles_per_seq
    h = _rms(_load_tokens(x_ref), g_ref[...]).astype(BF16)

    def proj(c0, width):
        return jnp.dot(h, w_ref[:, c0:c0 + width], preferred_element_type=F32)

    @pl.when(seq_tile == 0)
    def _():
        cext_ref[0, 0:HALO, :] = jnp.zeros((HALO, C_CONV), F32)
        pext_ref[0, 0:HALO, :] = jnp.zeros((HALO, C_POOL), F32)

    z = proj(0, 2 * C_CONV)
    cext_ref[0, HALO:, :] = z[:, :C_CONV] * jax.nn.sigmoid(z[:, C_CONV:])
    pext_ref[0, HALO:, :] = proj(2 * C_CONV, C_POOL)

    lane = lax.broadcasted_iota(jnp.int32, (1, LANES), 1)
    first = lane < HEAD_DIM

    def qk_norm(z, gain_ref, out_ref):
        gain = gain_ref[...]
        for hd in range(N_HEADS):
            blk = z[:, hd * LANES:(hd + 1) * LANES]
            sq = blk * blk
            s_all = jnp.sum(sq, axis=-1, keepdims=True)
            s_lo = jnp.sum(jnp.where(first, sq, 0.0), axis=-1, keepdims=True)
            r_lo = lax.rsqrt(s_lo + HEAD_DIM * EPS)
            r_hi = lax.rsqrt((s_all - s_lo) + HEAD_DIM * EPS)
            r = jnp.where(first, r_lo, r_hi)
            out_ref[:, hd * LANES:(hd + 1) * LANES] = (blk * r * gain).astype(BF16)

    c_q = 2 * C_CONV + C_POOL
    qk_norm(proj(c_q, C_ATTN), qg_ref, q_ref)
    qk_norm(proj(c_q + C_ATTN, C_ATTN), kg_ref, k_ref)
    v_ref[...] = proj(c_q + 2 * C_ATTN, C_ATTN).astype(BF16)

    _conv_group(cext_ref, ts, cw_ref, cb_ref, lng_ref, lnb_ref, pw_ref, pwb_ref, yc_ref)
    _pool_group(pext_ref, ts, seq_tile * ts, plw_ref, pls_ref, yp_ref)

    cext_ref[0, 0:HALO, :] = cext_ref[0, ts:ts + HALO, :]
    pext_ref[0, 0:HALO, :] = pext_ref[0, ts:ts + HALO, :]


def _in_proj(x2, seq_len, g, w_bf, qg, kg, conv_w, conv_b, ln_g, ln_b, pw_bf, pw_b,
             pool_w_bf, pool_scale):
    token_major = x2.shape[1] == LANES
    t = x2.shape[0] // X_ROWS if token_major else x2.shape[0]
    tm = TM_PROJ
    assert seq_len % tm == 0
    row = lambda i: (i, 0)
    fixed = lambda i: (0, 0)
    full = lambda a: pl.BlockSpec(a.shape, fixed)
    x_spec = pl.BlockSpec((tm * X_ROWS, LANES) if token_major else (tm, D_MODEL), row)
    small = [g, w_bf, qg, kg, conv_w, conv_b, ln_g, ln_b, pw_bf, pw_b, pool_w_bf, pool_scale]
    widths = [C_CONV, C_POOL, C_ATTN, C_ATTN, C_ATTN]
    return pl.pallas_call(
        functools.partial(_in_proj_kernel, tiles_per_seq=seq_len // tm),
        grid=(t // tm,),
        in_specs=[x_spec] + [full(a) for a in small],
        out_specs=[pl.BlockSpec((tm, w), row) for w in widths],
        out_shape=[jax.ShapeDtypeStruct((t, w), BF16) for w in widths],
        scratch_shapes=[pltpu.VMEM((SUBLANES, tm + HALO, C_CONV), F32),
                        pltpu.VMEM((4, tm + HALO, C_POOL), F32)],
        compiler_params=_params(("arbitrary",)),
        name="in_proj_mix",
    )(x2, *small)


def _conv_group(ext_ref, ts, w_ref, b_ref, lng_ref, lnb_ref, pw_ref, pwb_ref, o_ref):
    n_shifted = ts + HALO - SUBLANES
    for s in range(1, SUBLANES):
        ext_ref[s, 0:n_shifted, :] = ext_ref[0, s:s + n_shifted, :]

    base = HALO - (CONV_WIDTH - 1)
    rows = 128
    for c in range(ts // rows):
        acc = jnp.broadcast_to(b_ref[...], (rows, C_CONV))
        for j in range(CONV_WIDTH):
            phase = (base + j) % SUBLANES
            r0 = c * rows + base + j - phase
            acc = acc + ext_ref[phase, r0:r0 + rows, :] * w_ref[j:j + 1, :]
        mu = jnp.mean(acc, axis=-1, keepdims=True)
        xc = acc - mu
        y = xc * lax.rsqrt(jnp.mean(xc * xc, axis=-1, keepdims=True) + EPS)
        y = y * lng_ref[...] + lnb_ref[...]
        y = y * jax.nn.sigmoid(y)
        out = jnp.dot(y.astype(BF16), pw_ref[...], preferred_element_type=F32) + pwb_ref[...]
        o_ref[c * rows:(c + 1) * rows, :] = out.astype(BF16)


def _pool_group(ext_ref, ts, pos0, w_ref, scale_ref, o_ref):
    n = ts + HALO
    assert POOL_WINDOWS == (2, 4, 8, 16) and HALO >= 4 * SUBLANES
    sums = []
    for k in range(1, 5):
        lo = SUBLANES * k
        shift = 1 << (k - 1)
        level = ext_ref[k - 1, lo:n, :] + ext_ref[k - 1, lo - shift:n - shift, :]
        sums.append(level[HALO - lo:, :])
        if k < 4:
            ext_ref[k, lo:n, :] = level

    lane = lax.broadcasted_iota(jnp.int32, (1, C_POOL), 1)
    group = lane // POOL_GROUP
    win = jnp.left_shift(2, group)
    acc = jnp.where(group == 0, sums[0],
                    jnp.where(group == 1, sums[1], jnp.where(group == 2, sums[2], sums[3])))
    pos = pos0 + lax.broadcasted_iota(jnp.int32, (ts, 1), 0)
    count = jnp.minimum(pos + 1, win).astype(F32)
    diff = acc / count - ext_ref[0, HALO:, :]
    out = jnp.dot(diff.astype(BF16), w_ref[...], preferred_element_type=F32) * scale_ref[...]
    o_ref[...] = out.astype(BF16)


def _attn_kernel(q_ref, k_ref, v_ref, qaug_ref, kaug_ref, lq1_ref, lk1_ref, lq2_ref, lk2_ref,
                 sg_ref, o_ref, kx_ref, q2_all, m_all, l_all, acc_all, *, lam_init):
    qi = pl.program_id(1)
    half = TQ // 2
    assert half == TK

    @pl.when(qi == 0)
    def _():
        for hd in range(N_HEADS):
            kx_ref[hd, :, 0:LANES] = k_ref[0, :, hd * LANES:(hd + 1) * LANES]
            kx_ref[hd, :, LANES:] = kaug_ref[hd]

    lam = (jnp.exp(jnp.sum(lq1_ref[...] * lk1_ref[...], axis=-1, keepdims=True))
           - jnp.exp(jnp.sum(lq2_ref[...] * lk2_ref[...], axis=-1, keepdims=True))
           + lam_init)
    first = lax.broadcasted_iota(jnp.int32, (1, LANES), 1) < HEAD_DIM
    row = lax.broadcasted_iota(jnp.int32, (2 * half, TK), 0)
    col = lax.broadcasted_iota(jnp.int32, (2 * half, TK), 1)
    tri = col <= (row & (half - 1))

    def step(hd, r0, nr, k0, width, masked, first_block):
        q2_ref, m_ref, l_ref, acc_ref = q2_all.at[hd], m_all.at[hd], l_all.at[hd], acc_all.at[hd]
        kv_rows = pl.ds(pl.multiple_of(k0, TK), width)
        kb = kx_ref[hd, kv_rows, :]
        vb = v_ref[0, kv_rows, hd * LANES:(hd + 1) * LANES]
        s = lax.dot_general(q2_ref[r0:r0 + nr, :], kb, (((1,), (1,)), ((), ())),
                            preferred_element_type=F32)
        if masked:
            s = jnp.where(tri, s, -jnp.inf)
        s_max = jnp.max(s, axis=-1, keepdims=True)
        if first_block:
            m_new = jnp.broadcast_to(s_max, (nr, LANES))
        else:
            m_prev = m_ref[r0:r0 + nr]
            m_new = jnp.maximum(m_prev, s_max)
            alpha = jnp.exp2(m_prev - m_new)
        p = jnp.exp2(s - jnp.concatenate([m_new] * (width // LANES), axis=1))
        l_new = jnp.sum(p, axis=-1, keepdims=True)
        acc_new = jnp.dot(p.astype(BF16), vb, preferred_element_type=F32)
        if first_block:
            l_ref[r0:r0 + nr] = jnp.broadcast_to(l_new, (nr, LANES))
            acc_ref[r0:r0 + nr] = acc_new
        else:
            l_ref[r0:r0 + nr] = alpha * l_ref[r0:r0 + nr] + l_new
            acc_ref[r0:r0 + nr] = alpha * acc_ref[r0:r0 + nr] + acc_new
        m_ref[r0:r0 + nr] = m_new

    for hd in range(N_HEADS):
        q2_ref = q2_all.at[hd]
        qb = q_ref[0, :, hd * LANES:(hd + 1) * LANES]
        zero = jnp.zeros_like(qb)
        q_maps = (jnp.where(first, qb, zero), jnp.where(first, zero, qb))
        qa = qaug_ref[hd]
        for part in range(2):
            rows = slice(part * half, (part + 1) * half)
            for m in range(2):
                dst = slice((2 * part + m) * half, (2 * part + m + 1) * half)
                q2_ref[dst, 0:LANES] = q_maps[m][rows]
                q2_ref[dst, LANES:] = qa[rows]

    for hd in range(N_HEADS):
        step(hd, 0, 2 * half, qi * TQ, TK, True, True)
        step(hd, 2 * half, 2 * half, qi * TQ, TK, False, True)
    for hd in range(N_HEADS):
        step(hd, 2 * half, 2 * half, qi * TQ + TK, TK, True, False)

    def body(j, carry):
        for hd in range(N_HEADS):
            step(hd, 0, 4 * half, 2 * j * TQ, 2 * TQ, False, False)
        return carry

    lax.fori_loop(0, qi // 2, body, 0)

    @pl.when(qi % 2 == 1)
    def _():
        for hd in range(N_HEADS):
            step(hd, 0, 4 * half, (qi - 1) * TQ, TQ, False, False)

    for hd in range(N_HEADS):
        l_ref, acc_ref = l_all.at[hd], acc_all.at[hd]
        for part in range(2):
            r = 2 * part * half
            o = (acc_ref[r:r + half] / l_ref[r:r + half]
                 - lam * (acc_ref[r + half:r + 2 * half] / l_ref[r + half:r + 2 * half]))
            o = _rms(o, sg_ref[...]) * (1.0 - lam_init)
            o_ref[0, part * half:(part + 1) * half, hd * LANES:(hd + 1) * LANES] = o.astype(BF16)


def _alibi_aug(s):
    start = 2.0 ** (-8.0 / N_HEADS)
    slopes = np.array([start ** (i + 1) for i in range(N_HEADS)], dtype=np.float32)
    c = math.log2(math.e) * slopes.astype(np.float64)[:, None] * np.arange(s)[None, :]
    pieces = []
    rem = c
    for _ in range(3):
        piece = rem.astype(jnp.bfloat16)
        pieces.append(piece)
        rem = rem - piece.astype(np.float64)
    one = np.ones_like(pieces[0])
    zero = np.zeros((N_HEADS, s, LANES - 6), dtype=jnp.bfloat16)
    kaug = np.concatenate([np.stack(pieces + [one] * 3, axis=-1), zero], axis=-1)
    qaug = np.concatenate([np.stack([one] * 3 + [-p for p in pieces], axis=-1), zero], axis=-1)
    return jnp.asarray(qaug), jnp.asarray(kaug)


def _attention(q, k, v, qaug, kaug, lq1, lk1, lq2, lk2, sub_g, lam_init):
    bsz, s, _ = q.shape
    nq = s // TQ
    fixed2 = lambda b_, i: (0, 0)
    seq = lambda b_, i: (b_, 0, 0)
    vec = pl.BlockSpec((1, HEAD_DIM), fixed2)
    return pl.pallas_call(
        functools.partial(_attn_kernel, lam_init=lam_init),
        grid=(bsz, nq),
        in_specs=[pl.BlockSpec((1, TQ, C_ATTN), lambda b_, i: (b_, i, 0)),
                  pl.BlockSpec((1, s, C_ATTN), seq),
                  pl.BlockSpec((1, s, C_ATTN), seq),
                  pl.BlockSpec((N_HEADS, TQ, LANES), lambda b_, i: (0, i, 0)),
                  pl.BlockSpec((N_HEADS, s, LANES), lambda b_, i: (0, 0, 0)),
                  vec, vec, vec, vec,
                  pl.BlockSpec((1, V_DIM), fixed2)],
        out_specs=pl.BlockSpec((1, TQ, C_ATTN), lambda b_, i: (b_, i, 0)),
        out_shape=jax.ShapeDtypeStruct((bsz, s, C_ATTN), BF16),
        scratch_shapes=[pltpu.VMEM((N_HEADS, s, 2 * LANES), BF16),
                        pltpu.VMEM((N_HEADS, 2 * TQ, 2 * LANES), BF16),
                        pltpu.VMEM((N_HEADS, 2 * TQ, LANES), F32),
                        pltpu.VMEM((N_HEADS, 2 * TQ, LANES), F32),
                        pltpu.VMEM((N_HEADS, 2 * TQ, V_DIM), F32)],
        compiler_params=_params(("parallel", "arbitrary")),
        name="diff_attention",
    )(q, k, v, qaug, kaug, lq1, lk1, lq2, lk2, sub_g)


def _split_bf16(a):
    hi = a.astype(BF16)
    lo = (a - hi.astype(F32)).astype(BF16)
    return hi, lo


def _out_proj_kernel(x_ref, yc_ref, yp_ref, ya_ref, w_ref, g_ref, rw_ref, rb_ref,
                     xm_ref, meta_t_ref, counts_ref, cnt_ref):
    x = _load_tokens(x_ref)
    x = x + jnp.dot(yc_ref[...], w_ref[0:C_CONV, :], preferred_element_type=F32)
    x = x + jnp.dot(yp_ref[...], w_ref[C_CONV:C_CONV + C_POOL, :], preferred_element_type=F32)
    x = x + jnp.dot(ya_ref[...], w_ref[C_CONV + C_POOL:, :], preferred_element_type=F32)
    tm = x.shape[0]
    _store_tokens(xm_ref, x, XM_ROWS)

    hf_hi, hf_lo = _split_bf16(_rms(x, g_ref[...]))

    both = jnp.dot(hf_hi, rw_ref[...], preferred_element_type=F32)
    logits = (both[:, 0:LANES] + both[:, LANES:]
              + jnp.dot(hf_lo, rw_ref[:, 0:LANES], preferred_element_type=F32)) + rb_ref[...]
    lane = lax.broadcasted_iota(jnp.int32, (tm, LANES), 1)
    lane_f = lane.astype(F32)
    neg = -jnp.inf

    def first_argmax(val, vmax):
        first = jnp.min(jnp.where(val == vmax, lane_f, float(LANES)), axis=-1, keepdims=True)
        return first.astype(jnp.int32)

    lg = jnp.where(lane < N_GROUPS, logits, neg)
    g_max = jnp.max(lg, axis=-1, keepdims=True)
    g_idx = first_argmax(lg, g_max)
    g_p = 1.0 / jnp.sum(jnp.exp(lg - g_max), axis=-1, keepdims=True)

    e_lane = lane - N_GROUPS
    in_group = (e_lane >= g_idx * PER_GROUP) & (e_lane < (g_idx + 1) * PER_GROUP)
    le = jnp.where(in_group, logits, neg)
    e_max = jnp.max(le, axis=-1, keepdims=True)
    pe = jnp.exp(le - e_max)
    pe = pe / jnp.sum(pe, axis=-1, keepdims=True)
    p1 = jnp.max(pe, axis=-1, keepdims=True)
    i1 = first_argmax(jnp.where(in_group, pe, neg), p1)
    pe2 = jnp.where(in_group & (lane != i1), pe, neg)
    p2 = jnp.max(pe2, axis=-1, keepdims=True)
    i2 = first_argmax(pe2, p2)
    denom = p1 + p2
    gate1 = g_p * (p1 / denom)
    gate2 = g_p * (p2 / denom)

    loc1 = i1 - N_GROUPS - g_idx * PER_GROUP
    loc2 = i2 - N_GROUPS - g_idx * PER_GROUP
    a = jnp.minimum(loc1, loc2)
    b = jnp.maximum(loc1, loc2)
    pair = jnp.where(a == 0, 0, jnp.where(a == 1, 3, 5)) + (b - a - 1)
    bucket = g_idx * N_PAIRS + pair
    gate_a = jnp.where(loc1 < loc2, gate1, gate2)
    gate_b = jnp.where(loc1 < loc2, gate2, gate1)

    @pl.when(pl.program_id(0) == 0)
    def _():
        cnt_ref[...] = jnp.zeros(cnt_ref.shape, F32)

    in_bucket = lane == bucket
    earlier = (lax.broadcasted_iota(jnp.int32, (tm, tm), 1)
               < lax.broadcasted_iota(jnp.int32, (tm, tm), 0)).astype(BF16)
    before = jnp.dot(earlier, in_bucket.astype(BF16), preferred_element_type=F32) + cnt_ref[...]
    rank = jnp.sum(jnp.where(in_bucket, before, 0.0), axis=-1, keepdims=True)
    cnt_ref[...] += jnp.sum(in_bucket.astype(F32), axis=0, keepdims=True)
    counts_ref[...] = cnt_ref[...]

    meta = jnp.where(lane == META_GATE_A, gate_a,
                     jnp.where(lane == META_GATE_B, gate_b,
                               jnp.where(lane == META_BUCKET, bucket.astype(F32),
                                         jnp.where(lane == META_RANK, rank, 0.0))))
    xm_ref[pl.ds(X_ROWS, tm, stride=XM_ROWS), :] = meta
    meta_t_ref[...] = meta.T[0:META_ROWS, :]


def _out_proj(x2, yc, yp, ya, w_bf, g, rw_hi_lo, rb):
    token_major = x2.shape[1] == LANES
    t = x2.shape[0] // X_ROWS if token_major else x2.shape[0]
    tm = TM_OUT
    row = lambda i: (i, 0)
    fixed = lambda i: (0, 0)
    x_spec = pl.BlockSpec((tm * X_ROWS, LANES) if token_major else (tm, D_MODEL), row)
    return pl.pallas_call(
        _out_proj_kernel,
        grid=(t // tm,),
        in_specs=[x_spec,
                  pl.BlockSpec((tm, C_CONV), row),
                  pl.BlockSpec((tm, C_POOL), row),
                  pl.BlockSpec((tm, C_ATTN), row),
                  pl.BlockSpec((D_MODEL, D_MODEL), fixed),
                  pl.BlockSpec((1, D_MODEL), fixed),
                  pl.BlockSpec((D_MODEL, 2 * LANES), fixed),
                  pl.BlockSpec((1, LANES), fixed)],
        out_specs=[pl.BlockSpec((tm * XM_ROWS, LANES), row),
                   pl.BlockSpec((META_ROWS, tm), row),
                   pl.BlockSpec((1, LANES), fixed)],
        out_shape=[jax.ShapeDtypeStruct((t * XM_ROWS, LANES), F32),
                   jax.ShapeDtypeStruct((t // tm * META_ROWS, tm), F32),
                   jax.ShapeDtypeStruct((1, LANES), F32)],
        scratch_shapes=[pltpu.VMEM((1, LANES), F32)],
        compiler_params=_params(("arbitrary",)),
        name="out_proj_router",
    )(x2, yc, yp, ya, w_bf, g, rw_hi_lo, rb)


def _for_chunk_rows(dest_ref, start_copy, wait_copy):
    base = pl.program_id(0) * COPY_CHUNK

    def body(blk, carry):
        rows = [blk * DMA_UNROLL + u for u in range(DMA_UNROLL)]
        slots = [dest_ref[base + r] for r in rows]
        for u, (r, slot) in enumerate(zip(rows, slots)):
            start_copy(r, slot, u % 2)
        return carry

    lax.fori_loop(0, COPY_CHUNK // DMA_UNROLL, body, 0)
    for piece in range(COPY_CHUNK // COPY_WAIT):
        wait_copy(piece)


def _dispatch_kernel(dest_ref, x_ref, out_hbm, sem):
    def start_copy(r, slot, priority):
        pltpu.make_async_copy(x_ref.at[pl.ds(r * XM_ROWS, XM_ROWS)],
                              out_hbm.at[pl.ds(slot * XM_ROWS, XM_ROWS)],
                              sem.at[0]).start(priority=priority)

    def wait_copy(piece):
        n = COPY_WAIT * XM_ROWS
        pltpu.make_async_copy(x_ref.at[pl.ds(piece * n, n)], out_hbm.at[pl.ds(0, n)],
                              sem.at[0]).wait()

    _for_chunk_rows(dest_ref, start_copy, wait_copy)


def _dispatch(xm, dest):
    n_tokens = xm.shape[0] // XM_ROWS
    n_rows = dest.shape[0]
    assert n_tokens % COPY_CHUNK == 0 and n_rows % COPY_CHUNK == 0
    last = n_tokens // COPY_CHUNK - 1
    grid_spec = pltpu.PrefetchScalarGridSpec(
        num_scalar_prefetch=1,
        grid=(n_rows // COPY_CHUNK,),
        in_specs=[pl.BlockSpec((COPY_CHUNK * XM_ROWS, LANES),
                               lambda i, dest: (jnp.minimum(i, last), 0))],
        out_specs=pl.BlockSpec(memory_space=pl.ANY),
        scratch_shapes=[pltpu.SemaphoreType.DMA((1,))])
    return pl.pallas_call(
        _dispatch_kernel,
        grid_spec=grid_spec,
        out_shape=jax.ShapeDtypeStruct((n_rows * XM_ROWS, LANES), F32),
        compiler_params=_params(("arbitrary",)),
        name="moe_dispatch",
    )(dest, xm)


def _combine_kernel(dest_ref, ys_hbm, o_ref, *scratch, to_rows):
    buf = scratch[0] if to_rows else o_ref
    sem = scratch[-1]

    def start_copy(r, slot, priority):
        pltpu.make_async_copy(ys_hbm.at[pl.ds(slot * X_ROWS, X_ROWS)],
                              buf.at[pl.ds(r * X_ROWS, X_ROWS)],
                              sem.at[0]).start(priority=priority)

    def wait_copy(piece):
        n = COPY_WAIT * X_ROWS
        pltpu.make_async_copy(ys_hbm.at[pl.ds(0, n)], buf.at[pl.ds(piece * n, n)],
                              sem.at[0]).wait()

    _for_chunk_rows(dest_ref, start_copy, wait_copy)
    if to_rows:
        o_ref[...] = _load_tokens(buf)


def _combine(ys, dest, n_tokens, to_rows):
    assert n_tokens % COPY_CHUNK == 0
    chunk_rows = COPY_CHUNK * X_ROWS
    if to_rows:
        out_spec = pl.BlockSpec((COPY_CHUNK, D_MODEL), lambda i, dest: (i, 0))
        out_shape = jax.ShapeDtypeStruct((n_tokens, D_MODEL), F32)
        scratch = [pltpu.VMEM((chunk_rows, LANES), F32)]
    else:
        out_spec = pl.BlockSpec((chunk_rows, LANES), lambda i, dest: (i, 0))
        out_shape = jax.ShapeDtypeStruct((n_tokens * X_ROWS, LANES), F32)
        scratch = []
    grid_spec = pltpu.PrefetchScalarGridSpec(
        num_scalar_prefetch=1,
        grid=(n_tokens // COPY_CHUNK,),
        in_specs=[pl.BlockSpec(memory_space=pl.ANY)],
        out_specs=out_spec,
        scratch_shapes=scratch + [pltpu.SemaphoreType.DMA((1,))])
    return pl.pallas_call(
        functools.partial(_combine_kernel, to_rows=to_rows),
        grid_spec=grid_spec,
        out_shape=out_shape,
        compiler_params=_params(("arbitrary",)),
        name="moe_combine",
    )(dest, ys)


def _moe_kernel(te1_ref, te2_ref, nv_ref, used_ref,
                xs_ref, g_ref, wg1_ref, wu1_ref, wd1_ref, wg2_ref, wu2_ref, wd2_ref, ys_ref):
    i = pl.program_id(0)

    @pl.when(nv_ref[i] > 0)
    def _():
        x = _load_tokens(xs_ref, XM_ROWS)
        meta = xs_ref[pl.ds(X_ROWS, TM_MOE, stride=XM_ROWS), :]
        hf = _rms(x, g_ref[...]).astype(BF16)

        def expert(wg_ref, wu_ref, wd_ref, gate):
            a = jnp.dot(hf, wg_ref[0], preferred_element_type=F32)
            u = jnp.dot(hf, wu_ref[0], preferred_element_type=F32)
            act = a * jax.nn.sigmoid(a) * u * gate
            return jnp.dot(act.astype(BF16), wd_ref[0], preferred_element_type=F32)

        y = expert(wg1_ref, wu1_ref, wd1_ref, meta[:, META_GATE_A:META_GATE_A + 1])
        y = y + expert(wg2_ref, wu2_ref, wd2_ref, meta[:, META_GATE_B:META_GATE_B + 1])
        _store_tokens(ys_ref, x + y)

    @pl.when(nv_ref[i] == 0)
    def _():
        ys_ref[...] = jnp.zeros(ys_ref.shape, F32)


def _moe(xs, g, te1, te2, nv, used, wg_bf, wu_bf, wd_bf):
    tm = TM_MOE
    n_tiles = nv.shape[0]
    fixed = lambda i, te1, te2, nv, used: (0, 0)
    tile_in = lambda i, te1, te2, nv, used: (jnp.minimum(i, used[0] - 1), 0)
    tile_out = lambda i, te1, te2, nv, used: (i, 0)
    expert1 = lambda i, te1, te2, nv, used: (te1[i], 0, 0)
    expert2 = lambda i, te1, te2, nv, used: (te2[i], 0, 0)
    w_in = lambda index_map: pl.BlockSpec((1, D_MODEL, D_EXPERT), index_map)
    w_out = lambda index_map: pl.BlockSpec((1, D_EXPERT, D_MODEL), index_map)
    grid_spec = pltpu.PrefetchScalarGridSpec(
        num_scalar_prefetch=4,
        grid=(n_tiles,),
        in_specs=[pl.BlockSpec((tm * XM_ROWS, LANES), tile_in),
                  pl.BlockSpec((1, D_MODEL), fixed),
                  w_in(expert1), w_in(expert1), w_out(expert1),
                  w_in(expert2), w_in(expert2), w_out(expert2)],
        out_specs=pl.BlockSpec((tm * X_ROWS, LANES), tile_out))
    return pl.pallas_call(
        _moe_kernel,
        grid_spec=grid_spec,
        out_shape=jax.ShapeDtypeStruct((n_tiles * tm * X_ROWS, LANES), F32),
        compiler_params=_params(("arbitrary",)),
        name="moe",
    )(te1, te2, nv, used, xs, g, wg_bf, wu_bf, wd_bf, wg_bf, wu_bf, wd_bf)


def _route(meta_t, counts, t):
    tm = TM_MOE
    nb = N_GROUPS * N_PAIRS
    n_tiles = t // tm + nb
    meta = meta_t.reshape(t // TM_OUT, META_ROWS, TM_OUT)
    bucket = meta[:, META_BUCKET, :].reshape(t).astype(jnp.int32)
    rank = meta[:, META_RANK, :].reshape(t).astype(jnp.int32)
    counts = counts[0, :nb].astype(jnp.int32)
    ids = jnp.arange(nb, dtype=jnp.int32)

    tiles = (counts + tm - 1) // tm
    tile_end = jnp.cumsum(tiles)
    tile_start = tile_end - tiles
    used = tile_end[-1]
    row0 = tile_start * tm
    dest = rank + jnp.sum(jnp.where(bucket[:, None] == ids[None, :], row0[None, :], 0), axis=1)

    ti = jnp.arange(n_tiles, dtype=jnp.int32)
    tic = jnp.minimum(ti, used - 1)
    tb = jnp.sum((tile_end[None, :] <= tic[:, None]).astype(jnp.int32), axis=1)
    onehot_tb = tb[:, None] == ids[None, :]
    pick = lambda table: jnp.sum(jnp.where(onehot_tb, table[None, :], 0), axis=1)
    nv = jnp.where(ti < used, jnp.clip(pick(counts) - (tic - pick(tile_start)) * tm, 0, tm), 0)
    pair_a = jnp.asarray([0, 0, 0, 1, 1, 2], jnp.int32)
    pair_b = jnp.asarray([1, 2, 3, 2, 3, 3], jnp.int32)
    te1 = pick((ids // N_PAIRS) * PER_GROUP + pair_a[ids % N_PAIRS])
    te2 = pick((ids // N_PAIRS) * PER_GROUP + pair_b[ids % N_PAIRS])

    pads = tiles * tm - counts
    pad_end = jnp.cumsum(pads)
    k = jnp.arange(nb * tm, dtype=jnp.int32)
    kb = jnp.minimum(jnp.sum((pad_end[None, :] <= k[:, None]).astype(jnp.int32), axis=1), nb - 1)
    onehot_kb = kb[:, None] == ids[None, :]
    pick_k = lambda table: jnp.sum(jnp.where(onehot_kb, table[None, :], 0), axis=1)
    in_pad = pick_k(row0 + counts) + (k - pick_k(pad_end - pads))
    free_rows = jnp.where(k < pad_end[-1], in_pad, used * tm + (k - pad_end[-1]))
    i32 = lambda a: a.astype(jnp.int32)
    return (i32(jnp.concatenate([dest, free_rows])), i32(te1), i32(te2), i32(nv),
            i32(used).reshape(1))


def _block_diag(w):
    n = w.shape[0]
    eye = jnp.eye(n, dtype=w.dtype)
    return jnp.einsum("gcd,gh->gchd", w, eye).reshape(n * POOL_GROUP, n * POOL_GROUP)


def kernel(x, attn_norm_g, w_in, conv_w, conv_b, conv_ln_g, conv_ln_b, conv_pw_w, conv_pw_b,
           pool_w, pool_scale, q_norm_g, k_norm_g, lambda_q1, lambda_k1, lambda_q2, lambda_k2,
           attn_sub_norm_g, w_out, ffn_norm_g, router_g_w, router_g_b, router_e_w, router_e_b,
           w_gate, w_up, w_down):
    bsz, s, d = x.shape
    depth = w_in.shape[0]
    t = bsz * s
    qaug, kaug = _alibi_aug(s)
    row = lambda a: a.reshape(1, -1)

    x2 = x.reshape(t, d)
    for l in range(depth):
        lam_init = 0.8 - 0.6 * math.exp(-0.3 * l)
        y_conv, y_pool, q, k, v = _in_proj(
            x2, s, row(attn_norm_g[l]), w_in[l].astype(BF16),
            row(jnp.tile(q_norm_g[l], 2) * (HEAD_DIM ** 0.5 * HEAD_DIM ** -0.5 * math.log2(math.e))),
            row(jnp.tile(k_norm_g[l], 2) * HEAD_DIM ** 0.5),
            conv_w[l], row(conv_b[l]), row(conv_ln_g[l]), row(conv_ln_b[l]),
            conv_pw_w[l].astype(BF16), row(conv_pw_b[l]),
            _block_diag(pool_w[l]).astype(BF16), row(pool_scale[l]))
        y_attn = _attention(q.reshape(bsz, s, -1), k.reshape(bsz, s, -1), v.reshape(bsz, s, -1),
                            qaug, kaug, row(lambda_q1[l]), row(lambda_k1[l]), row(lambda_q2[l]),
                            row(lambda_k2[l]), row(attn_sub_norm_g[l]), lam_init)

        pad = LANES - N_GROUPS - N_EXPERTS
        rw = jnp.pad(jnp.concatenate([router_g_w[l], router_e_w[l]], axis=1), ((0, 0), (0, pad)))
        rb = jnp.pad(jnp.concatenate([router_g_b[l], router_e_b[l]]), (0, pad))
        rw_hi = rw.astype(BF16)
        rw_lo = (rw - rw_hi.astype(F32)).astype(BF16)
        rw_hi_lo = jnp.concatenate([rw_hi, rw_lo], axis=1)
        xm, meta_t, counts = _out_proj(
            x2, y_conv, y_pool, y_attn.reshape(t, -1),
            w_out[l].astype(BF16), row(ffn_norm_g[l]), rw_hi_lo, row(rb))

        dest, te1, te2, nv, used = _route(meta_t, counts, t)
        xs = _dispatch(xm, dest)
        ys = _moe(xs, row(ffn_norm_g[l]), te1, te2, nv, used, w_gate[l].astype(BF16),
                  w_up[l].astype(BF16), w_down[l].astype(BF16))
        x2 = _combine(ys, dest, t, to_rows=l == depth - 1)
    return x2.reshape(bsz, s, d)
```

```python
import functools
import math

import jax
import jax.numpy as jnp
import numpy as np
from jax import lax
from jax.experimental import pallas as pl
from jax.experimental.pallas import tpu as pltpu

D_MODEL = 1024
C_CONV = 256
C_POOL = 256
C_ATTN = 512
N_HEADS = 4
HEAD_DIM = 64
V_DIM = 2 * HEAD_DIM
CONV_WIDTH = 31
POOL_WINDOWS = (2, 4, 8, 16)
POOL_GROUP = C_POOL // len(POOL_WINDOWS)
N_IN = 2 * C_CONV + C_POOL + 3 * C_ATTN
N_GROUPS = 4
PER_GROUP = 4
N_EXPERTS = N_GROUPS * PER_GROUP
D_EXPERT = 256
EPS = 1e-6

LANES = 128
SUBLANES = 8
VMEM_LIMIT = 48 * 1024 * 1024

TM_PROJ = 512
TM_OUT = 512
HALO = 32
TQ = 512
TK = 256
TM_MOE = 512
N_PAIRS = 6
X_ROWS = D_MODEL // LANES
XM_ROWS = X_ROWS + 1
META_GATE_A, META_GATE_B, META_BUCKET, META_RANK = 0, 1, 2, 3
META_ROWS = 8
DMA_UNROLL = 8
COPY_CHUNK = 2048
COPY_WAIT = 512

F32 = jnp.float32
BF16 = jnp.bfloat16


def _params(sem):
    return pltpu.CompilerParams(dimension_semantics=sem, vmem_limit_bytes=VMEM_LIMIT)


def _rms(x, g):
    return x * lax.rsqrt(jnp.mean(x * x, axis=-1, keepdims=True) + EPS) * g


def _load_tokens(x_ref, rows_per_token=X_ROWS):
    if x_ref.shape[-1] != LANES:
        return x_ref[...]
    tm = x_ref.shape[0] // rows_per_token
    return jnp.concatenate(
        [x_ref[pl.ds(c, tm, stride=rows_per_token), :] for c in range(X_ROWS)], axis=1)


def _store_tokens(o_ref, x, rows_per_token=X_ROWS):
    tm = x.shape[0]
    for c in range(X_ROWS):
        o_ref[pl.ds(c, tm, stride=rows_per_token), :] = x[:, c * LANES:(c + 1) * LANES]


def _in_proj_kernel(x_ref, g_ref, w_ref, qg_ref, kg_ref,
                    cw_ref, cb_ref, lng_ref, lnb_ref, pw_ref, pwb_ref, plw_ref, pls_ref,
                    yc_ref, yp_ref, q_ref, k_ref, v_ref, cext_ref, pext_ref, *, tiles_per_seq):
    ts = yc_ref.shape[0]
    seq_tile = pl.program_id(0) % tiles_per_seq
    h = _rms(_load_tokens(x_ref), g_ref[...]).astype(BF16)

    def proj(c0, width):
        return jnp.dot(h, w_ref[:, c0:c0 + width], preferred_element_type=F32)

    @pl.when(seq_tile == 0)
    def _():
        cext_ref[0, 0:HALO, :] = jnp.zeros((HALO, C_CONV), F32)
        pext_ref[0, 0:HALO, :] = jnp.zeros((HALO, C_POOL), F32)

    z = proj(0, 2 * C_CONV)
    cext_ref[0, HALO:, :] = z[:, :C_CONV] * jax.nn.sigmoid(z[:, C_CONV:])
    pext_ref[0, HALO:, :] = proj(2 * C_CONV, C_POOL)

    lane = lax.broadcasted_iota(jnp.int32, (1, LANES), 1)
    first = lane < HEAD_DIM

    def qk_norm(z, gain_ref, out_ref):
        gain = gain_ref[...]
        for hd in range(N_HEADS):
            blk = z[:, hd * LANES:(hd + 1) * LANES]
            sq = blk * blk
            s_all = jnp.sum(sq, axis=-1, keepdims=True)
            s_lo = jnp.sum(jnp.where(first, sq, 0.0), axis=-1, keepdims=True)
            r_lo = lax.rsqrt(s_lo + HEAD_DIM * EPS)
            r_hi = lax.rsqrt((s_all - s_lo) + HEAD_DIM * EPS)
            r = jnp.where(first, r_lo, r_hi)
            out_ref[:, hd * LANES:(hd + 1) * LANES] = (blk * r * gain).astype(BF16)

    c_q = 2 * C_CONV + C_POOL
    qk_norm(proj(c_q, C_ATTN), qg_ref, q_ref)
    qk_norm(proj(c_q + C_ATTN, C_ATTN), kg_ref, k_ref)
    v_ref[...] = proj(c_q + 2 * C_ATTN, C_ATTN).astype(BF16)

    _conv_group(cext_ref, ts, cw_ref, cb_ref, lng_ref, lnb_ref, pw_ref, pwb_ref, yc_ref)
    _pool_group(pext_ref, ts, seq_tile * ts, plw_ref, pls_ref, yp_ref)

    cext_ref[0, 0:HALO, :] = cext_ref[0, ts:ts + HALO, :]
    pext_ref[0, 0:HALO, :] = pext_ref[0, ts:ts + HALO, :]


def _in_proj(x2, seq_len, g, w_bf, qg, kg, conv_w, conv_b, ln_g, ln_b, pw_bf, pw_b,
             pool_w_bf, pool_scale):
    token_major = x2.shape[1] == LANES
    t = x2.shape[0] // X_ROWS if token_major else x2.shape[0]
    tm = TM_PROJ
    assert seq_len % tm == 0
    row = lambda i: (i, 0)
    fixed = lambda i: (0, 0)
    full = lambda a: pl.BlockSpec(a.shape, fixed)
    x_spec = pl.BlockSpec((tm * X_ROWS, LANES) if token_major else (tm, D_MODEL), row)
    small = [g, w_bf, qg, kg, conv_w, conv_b, ln_g, ln_b, pw_bf, pw_b, pool_w_bf, pool_scale]
    widths = [C_CONV, C_POOL, C_ATTN, C_ATTN, C_ATTN]
    return pl.pallas_call(
        functools.partial(_in_proj_kernel, tiles_per_seq=seq_len // tm),
        grid=(t // tm,),
        in_specs=[x_spec] + [full(a) for a in small],
        out_specs=[pl.BlockSpec((tm, w), row) for w in widths],
        out_shape=[jax.ShapeDtypeStruct((t, w), BF16) for w in widths],
        scratch_shapes=[pltpu.VMEM((SUBLANES, tm + HALO, C_CONV), F32),
                        pltpu.VMEM((4, tm + HALO, C_POOL), F32)],
        compiler_params=_params(("arbitrary",)),
        name="in_proj_mix",
    )(x2, *small)


def _conv_group(ext_ref, ts, w_ref, b_ref, lng_ref, lnb_ref, pw_ref, pwb_ref, o_ref):
    n_shifted = ts + HALO - SUBLANES
    for s in range(1, SUBLANES):
        ext_ref[s, 0:n_shifted, :] = ext_ref[0, s:s + n_shifted, :]

    base = HALO - (CONV_WIDTH - 1)
    rows = 128
    for c in range(ts // rows):
        acc = jnp.broadcast_to(b_ref[...], (rows, C_CONV))
        for j in range(CONV_WIDTH):
            phase = (base + j) % SUBLANES
            r0 = c * rows + base + j - phase
            acc = acc + ext_ref[phase, r0:r0 + rows, :] * w_ref[j:j + 1, :]
        mu = jnp.mean(acc, axis=-1, keepdims=True)
        xc = acc - mu
        y = xc * lax.rsqrt(jnp.mean(xc * xc, axis=-1, keepdims=True) + EPS)
        y = y * lng_ref[...] + lnb_ref[...]
        y = y * jax.nn.sigmoid(y)
        out = jnp.dot(y.astype(BF16), pw_ref[...], preferred_element_type=F32) + pwb_ref[...]
        o_ref[c * rows:(c + 1) * rows, :] = out.astype(BF16)


def _pool_group(ext_ref, ts, pos0, w_ref, scale_ref, o_ref):
    n = ts + HALO
    assert POOL_WINDOWS == (2, 4, 8, 16) and HALO >= 4 * SUBLANES
    sums = []
    for k in range(1, 5):
        lo = SUBLANES * k
        shift = 1 << (k - 1)
        level = ext_ref[k - 1, lo:n, :] + ext_ref[k - 1, lo - shift:n - shift, :]
        sums.append(level[HALO - lo:, :])
        if k < 4:
            ext_ref[k, lo:n, :] = level

    lane = lax.broadcasted_iota(jnp.int32, (1, C_POOL), 1)
    group = lane // POOL_GROUP
    win = jnp.left_shift(2, group)
    acc = jnp.where(group == 0, sums[0],
                    jnp.where(group == 1, sums[1], jnp.where(group == 2, sums[2], sums[3])))
    pos = pos0 + lax.broadcasted_iota(jnp.int32, (ts, 1), 0)
    count = jnp.minimum(pos + 1, win).astype(F32)
    diff = acc / count - ext_ref[0, HALO:, :]
    out = jnp.dot(diff.astype(BF16), w_ref[...], preferred_element_type=F32) * scale_ref[...]
    o_ref[...] = out.astype(BF16)


def _attn_kernel(q_ref, k_ref, v_ref, qaug_ref, kaug_ref, lq1_ref, lk1_ref, lq2_ref, lk2_ref,
                 sg_ref, o_ref, kx_ref, q2_all, m_all, l_all, acc_all, *, lam_init):
    qi = pl.program_id(1)
    half = TQ // 2
    assert half == TK

    @pl.when(qi == 0)
    def _():
        for hd in range(N_HEADS):
            kx_ref[hd, :, 0:LANES] = k_ref[0, :, hd * LANES:(hd + 1) * LANES]
            kx_ref[hd, :, LANES:] = kaug_ref[hd]

    lam = (jnp.exp(jnp.sum(lq1_ref[...] * lk1_ref[...], axis=-1, keepdims=True))
           - jnp.exp(jnp.sum(lq2_ref[...] * lk2_ref[...], axis=-1, keepdims=True))
           + lam_init)
    first = lax.broadcasted_iota(jnp.int32, (1, LANES), 1) < HEAD_DIM
    row = lax.broadcasted_iota(jnp.int32, (2 * half, TK), 0)
    col = lax.broadcasted_iota(jnp.int32, (2 * half, TK), 1)
    tri = col <= (row & (half - 1))

    def step(hd, r0, nr, k0, width, masked, first_block):
        q2_ref, m_ref, l_ref, acc_ref = q2_all.at[hd], m_all.at[hd], l_all.at[hd], acc_all.at[hd]
        kv_rows = pl.ds(pl.multiple_of(k0, TK), width)
        kb = kx_ref[hd, kv_rows, :]
        vb = v_ref[0, kv_rows, hd * LANES:(hd + 1) * LANES]
        s = lax.dot_general(q2_ref[r0:r0 + nr, :], kb, (((1,), (1,)), ((), ())),
                            preferred_element_type=F32)
        if masked:
            s = jnp.where(tri, s, -jnp.inf)
        s_max = jnp.max(s, axis=-1, keepdims=True)
        if first_block:
            m_new = jnp.broadcast_to(s_max, (nr, LANES))
        else:
            m_prev = m_ref[r0:r0 + nr]
            m_new = jnp.maximum(m_prev, s_max)
            alpha = jnp.exp2(m_prev - m_new)
        p = jnp.exp2(s - jnp.concatenate([m_new] * (width // LANES), axis=1))
        l_new = jnp.sum(p, axis=-1, keepdims=True)
        acc_new = jnp.dot(p.astype(BF16), vb, preferred_element_type=F32)
        if first_block:
            l_ref[r0:r0 + nr] = jnp.broadcast_to(l_new, (nr, LANES))
            acc_ref[r0:r0 + nr] = acc_new
        else:
            l_ref[r0:r0 + nr] = alpha * l_ref[r0:r0 + nr] + l_new
            acc_ref[r0:r0 + nr] = alpha * acc_ref[r0:r0 + nr] + acc_new
        m_ref[r0:r0 + nr] = m_new

    for hd in range(N_HEADS):
        q2_ref = q2_all.at[hd]
        qb = q_ref[0, :, hd * LANES:(hd + 1) * LANES]
        zero = jnp.zeros_like(qb)
        q_maps = (jnp.where(first, qb, zero), jnp.where(first, zero, qb))
        qa = qaug_ref[hd]
        for part in range(2):
            rows = slice(part * half, (part + 1) * half)
            for m in range(2):
                dst = slice((2 * part + m) * half, (2 * part + m + 1) * half)
                q2_ref[dst, 0:LANES] = q_maps[m][rows]
                q2_ref[dst, LANES:] = qa[rows]

    for hd in range(N_HEADS):
        step(hd, 0, 2 * half, qi * TQ, TK, True, True)
        step(hd, 2 * half, 2 * half, qi * TQ, TK, False, True)
    for hd in range(N_HEADS):
        step(hd, 2 * half, 2 * half, qi * TQ + TK, TK, True, False)

    def body(j, carry):
        for hd in range(N_HEADS):
            step(hd, 0, 4 * half, 2 * j * TQ, 2 * TQ, False, False)
        return carry

    lax.fori_loop(0, qi // 2, body, 0)

    @pl.when(qi % 2 == 1)
    def _():
        for hd in range(N_HEADS):
            step(hd, 0, 4 * half, (qi - 1) * TQ, TQ, False, False)

    for hd in range(N_HEADS):
        l_ref, acc_ref = l_all.at[hd], acc_all.at[hd]
        for part in range(2):
            r = 2 * part * half
            o = (acc_ref[r:r + half] / l_ref[r:r + half]
                 - lam * (acc_ref[r + half:r + 2 * half] / l_ref[r + half:r + 2 * half]))
            o = _rms(o, sg_ref[...]) * (1.0 - lam_init)
            o_ref[0, part * half:(part + 1) * half, hd * LANES:(hd + 1) * LANES] = o.astype(BF16)


def _alibi_aug(s):
    start = 2.0 ** (-8.0 / N_HEADS)
    slopes = np.array([start ** (i + 1) for i in range(N_HEADS)], dtype=np.float32)
    c = math.log2(math.e) * slopes.astype(np.float64)[:, None] * np.arange(s)[None, :]
    pieces = []
    rem = c
    for _ in range(3):
        piece = rem.astype(jnp.bfloat16)
        pieces.append(piece)
        rem = rem - piece.astype(np.float64)
    one = np.ones_like(pieces[0])
    zero = np.zeros((N_HEADS, s, LANES - 6), dtype=jnp.bfloat16)
    kaug = np.concatenate([np.stack(pieces + [one] * 3, axis=-1), zero], axis=-1)
    qaug = np.concatenate([np.stack([one] * 3 + [-p for p in pieces], axis=-1), zero], axis=-1)
    return jnp.asarray(qaug), jnp.asarray(kaug)


def _attention(q, k, v, qaug, kaug, lq1, lk1, lq2, lk2, sub_g, lam_init):
    bsz, s, _ = q.shape
    nq = s // TQ
    fixed2 = lambda b_, i: (0, 0)
    seq = lambda b_, i: (b_, 0, 0)
    vec = pl.BlockSpec((1, HEAD_DIM), fixed2)
    return pl.pallas_call(
        functools.partial(_attn_kernel, lam_init=lam_init),
        grid=(bsz, nq),
        in_specs=[pl.BlockSpec((1, TQ, C_ATTN), lambda b_, i: (b_, i, 0)),
                  pl.BlockSpec((1, s, C_ATTN), seq),
                  pl.BlockSpec((1, s, C_ATTN), seq),
                  pl.BlockSpec((N_HEADS, TQ, LANES), lambda b_, i: (0, i, 0)),
                  pl.BlockSpec((N_HEADS, s, LANES), lambda b_, i: (0, 0, 0)),
                  vec, vec, vec, vec,
                  pl.BlockSpec((1, V_DIM), fixed2)],
        out_specs=pl.BlockSpec((1, TQ, C_ATTN), lambda b_, i: (b_, i, 0)),
        out_shape=jax.ShapeDtypeStruct((bsz, s, C_ATTN), BF16),
        scratch_shapes=[pltpu.VMEM((N_HEADS, s, 2 * LANES), BF16),
                        pltpu.VMEM((N_HEADS, 2 * TQ, 2 * LANES), BF16),
                        pltpu.VMEM((N_HEADS, 2 * TQ, LANES), F32),
                        pltpu.VMEM((N_HEADS, 2 * TQ, LANES), F32),
                        pltpu.VMEM((N_HEADS, 2 * TQ, V_DIM), F32)],
        compiler_params=_params(("parallel", "arbitrary")),
        name="diff_attention",
    )(q, k, v, qaug, kaug, lq1, lk1, lq2, lk2, sub_g)


def _split_bf16(a):
    hi = a.astype(BF16)
    lo = (a - hi.astype(F32)).astype(BF16)
    return hi, lo


def _out_proj_kernel(x_ref, yc_ref, yp_ref, ya_ref, w_ref, g_ref, rw_ref, rb_ref,
                     xm_ref, meta_t_ref, counts_ref, cnt_ref):
    x = _load_tokens(x_ref)
    x = x + jnp.dot(yc_ref[...], w_ref[0:C_CONV, :], preferred_element_type=F32)
    x = x + jnp.dot(yp_ref[...], w_ref[C_CONV:C_CONV + C_POOL, :], preferred_element_type=F32)
    x = x + jnp.dot(ya_ref[...], w_ref[C_CONV + C_POOL:, :], preferred_element_type=F32)
    tm = x.shape[0]
    _store_tokens(xm_ref, x, XM_ROWS)

    hf_hi, hf_lo = _split_bf16(_rms(x, g_ref[...]))

    both = jnp.dot(hf_hi, rw_ref[...], preferred_element_type=F32)
    logits = (both[:, 0:LANES] + both[:, LANES:]
              + jnp.dot(hf_lo, rw_ref[:, 0:LANES], preferred_element_type=F32)) + rb_ref[...]
    lane = lax.broadcasted_iota(jnp.int32, (tm, LANES), 1)
    lane_f = lane.astype(F32)
    neg = -jnp.inf

    def first_argmax(val, vmax):
        first = jnp.min(jnp.where(val == vmax, lane_f, float(LANES)), axis=-1, keepdims=True)
        return first.astype(jnp.int32)

    lg = jnp.where(lane < N_GROUPS, logits, neg)
    g_max = jnp.max(lg, axis=-1, keepdims=True)
    g_idx = first_argmax(lg, g_max)
    g_p = 1.0 / jnp.sum(jnp.exp(lg - g_max), axis=-1, keepdims=True)

    e_lane = lane - N_GROUPS
    in_group = (e_lane >= g_idx * PER_GROUP) & (e_lane < (g_idx + 1) * PER_GROUP)
    le = jnp.where(in_group, logits, neg)
    e_max = jnp.max(le, axis=-1, keepdims=True)
    pe = jnp.exp(le - e_max)
    pe = pe / jnp.sum(pe, axis=-1, keepdims=True)
    p1 = jnp.max(pe, axis=-1, keepdims=True)
    i1 = first_argmax(jnp.where(in_group, pe, neg), p1)
    pe2 = jnp.where(in_group & (lane != i1), pe, neg)
    p2 = jnp.max(pe2, axis=-1, keepdims=True)
    i2 = first_argmax(pe2, p2)
    denom = p1 + p2
    gate1 = g_p * (p1 / denom)
    gate2 = g_p * (p2 / denom)

    loc1 = i1 - N_GROUPS - g_idx * PER_GROUP
    loc2 = i2 - N_GROUPS - g_idx * PER_GROUP
    a = jnp.minimum(loc1, loc2)
    b = jnp.maximum(loc1, loc2)
    pair = jnp.where(a == 0, 0, jnp.where(a == 1, 3, 5)) + (b - a - 1)
    bucket = g_idx * N_PAIRS + pair
    gate_a = jnp.where(loc1 < loc2, gate1, gate2)
    gate_b = jnp.where(loc1 < loc2, gate2, gate1)

    @pl.when(pl.program_id(0) == 0)
    def _():
        cnt_ref[...] = jnp.zeros(cnt_ref.shape, F32)

    in_bucket = lane == bucket
    earlier = (lax.broadcasted_iota(jnp.int32, (tm, tm), 1)
               < lax.broadcasted_iota(jnp.int32, (tm, tm), 0)).astype(BF16)
    before = jnp.dot(earlier, in_bucket.astype(BF16), preferred_element_type=F32) + cnt_ref[...]
    rank = jnp.sum(jnp.where(in_bucket, before, 0.0), axis=-1, keepdims=True)
    cnt_ref[...] += jnp.sum(in_bucket.astype(F32), axis=0, keepdims=True)
    counts_ref[...] = cnt_ref[...]

    meta = jnp.where(lane == META_GATE_A, gate_a,
                     jnp.where(lane == META_GATE_B, gate_b,
                               jnp.where(lane == META_BUCKET, bucket.astype(F32),
                                         jnp.where(lane == META_RANK, rank, 0.0))))
    xm_ref[pl.ds(X_ROWS, tm, stride=XM_ROWS), :] = meta
    meta_t_ref[...] = meta.T[0:META_ROWS, :]


def _out_proj(x2, yc, yp, ya, w_bf, g, rw_hi_lo, rb):
    token_major = x2.shape[1] == LANES
    t = x2.shape[0] // X_ROWS if token_major else x2.shape[0]
    tm = TM_OUT
    row = lambda i: (i, 0)
    fixed = lambda i: (0, 0)
    x_spec = pl.BlockSpec((tm * X_ROWS, LANES) if token_major else (tm, D_MODEL), row)
    return pl.pallas_call(
        _out_proj_kernel,
        grid=(t // tm,),
        in_specs=[x_spec,
                  pl.BlockSpec((tm, C_CONV), row),
                  pl.BlockSpec((tm, C_POOL), row),
                  pl.BlockSpec((tm, C_ATTN), row),
                  pl.BlockSpec((D_MODEL, D_MODEL), fixed),
                  pl.BlockSpec((1, D_MODEL), fixed),
                  pl.BlockSpec((D_MODEL, 2 * LANES), fixed),
                  pl.BlockSpec((1, LANES), fixed)],
        out_specs=[pl.BlockSpec((tm * XM_ROWS, LANES), row),
                   pl.BlockSpec((META_ROWS, tm), row),
                   pl.BlockSpec((1, LANES), fixed)],
        out_shape=[jax.ShapeDtypeStruct((t * XM_ROWS, LANES), F32),
                   jax.ShapeDtypeStruct((t // tm * META_ROWS, tm), F32),
                   jax.ShapeDtypeStruct((1, LANES), F32)],
        scratch_shapes=[pltpu.VMEM((1, LANES), F32)],
        compiler_params=_params(("arbitrary",)),
        name="out_proj_router",
    )(x2, yc, yp, ya, w_bf, g, rw_hi_lo, rb)


def _for_chunk_rows(dest_ref, start_copy, wait_copy):
    base = pl.program_id(0) * COPY_CHUNK

    def body(blk, carry):
        rows = [blk * DMA_UNROLL + u for u in range(DMA_UNROLL)]
        slots = [dest_ref[base + r] for r in rows]
        for u, (r, slot) in enumerate(zip(rows, slots)):
            start_copy(r, slot, u % 2)
        return carry

    lax.fori_loop(0, COPY_CHUNK // DMA_UNROLL, body, 0)
    for piece in range(COPY_CHUNK // COPY_WAIT):
        wait_copy(piece)


def _dispatch_kernel(dest_ref, x_ref, out_hbm, sem):
    def start_copy(r, slot, priority):
        pltpu.make_async_copy(x_ref.at[pl.ds(r * XM_ROWS, XM_ROWS)],
                              out_hbm.at[pl.ds(slot * XM_ROWS, XM_ROWS)],
                              sem.at[0]).start(priority=priority)

    def wait_copy(piece):
        n = COPY_WAIT * XM_ROWS
        pltpu.make_async_copy(x_ref.at[pl.ds(piece * n, n)], out_hbm.at[pl.ds(0, n)],
                              sem.at[0]).wait()

    _for_chunk_rows(dest_ref, start_copy, wait_copy)


def _dispatch(xm, dest):
    n_tokens = xm.shape[0] // XM_ROWS
    n_rows = dest.shape[0]
    assert n_tokens % COPY_CHUNK == 0 and n_rows % COPY_CHUNK == 0
    last = n_tokens // COPY_CHUNK - 1
    grid_spec = pltpu.PrefetchScalarGridSpec(
        num_scalar_prefetch=1,
        grid=(n_rows // COPY_CHUNK,),
        in_specs=[pl.BlockSpec((COPY_CHUNK * XM_ROWS, LANES),
                               lambda i, dest: (jnp.minimum(i, last), 0))],
        out_specs=pl.BlockSpec(memory_space=pl.ANY),
        scratch_shapes=[pltpu.SemaphoreType.DMA((1,))])
    return pl.pallas_call(
        _dispatch_kernel,
        grid_spec=grid_spec,
        out_shape=jax.ShapeDtypeStruct((n_rows * XM_ROWS, LANES), F32),
        compiler_params=_params(("arbitrary",)),
        name="moe_dispatch",
    )(dest, xm)


def _combine_kernel(dest_ref, ys_hbm, o_ref, *scratch, to_rows):
    buf = scratch[0] if to_rows else o_ref
    sem = scratch[-1]

    def start_copy(r, slot, priority):
        pltpu.make_async_copy(ys_hbm.at[pl.ds(slot * X_ROWS, X_ROWS)],
                              buf.at[pl.ds(r * X_ROWS, X_ROWS)],
                              sem.at[0]).start(priority=priority)

    def wait_copy(piece):
        n = COPY_WAIT * X_ROWS
        pltpu.make_async_copy(ys_hbm.at[pl.ds(0, n)], buf.at[pl.ds(piece * n, n)],
                              sem.at[0]).wait()

    _for_chunk_rows(dest_ref, start_copy, wait_copy)
    if to_rows:
        o_ref[...] = _load_tokens(buf)


def _combine(ys, dest, n_tokens, to_rows):
    assert n_tokens % COPY_CHUNK == 0
    chunk_rows = COPY_CHUNK * X_ROWS
    if to_rows:
        out_spec = pl.BlockSpec((COPY_CHUNK, D_MODEL), lambda i, dest: (i, 0))
        out_shape = jax.ShapeDtypeStruct((n_tokens, D_MODEL), F32)
        scratch = [pltpu.VMEM((chunk_rows, LANES), F32)]
    else:
        out_spec = pl.BlockSpec((chunk_rows, LANES), lambda i, dest: (i, 0))
        out_shape = jax.ShapeDtypeStruct((n_tokens * X_ROWS, LANES), F32)
        scratch = []
    grid_spec = pltpu.PrefetchScalarGridSpec(
        num_scalar_prefetch=1,
        grid=(n_tokens // COPY_CHUNK,),
        in_specs=[pl.BlockSpec(memory_space=pl.ANY)],
        out_specs=out_spec,
        scratch_shapes=scratch + [pltpu.SemaphoreType.DMA((1,))])
    return pl.pallas_call(
        functools.partial(_combine_kernel, to_rows=to_rows),
        grid_spec=grid_spec,
        out_shape=out_shape,
        compiler_params=_params(("arbitrary",)),
        name="moe_combine",
    )(dest, ys)


def _moe_kernel(te1_ref, te2_ref, nv_ref, used_ref,
                xs_ref, g_ref, wg1_ref, wu1_ref, wd1_ref, wg2_ref, wu2_ref, wd2_ref, ys_ref):
    i = pl.program_id(0)

    @pl.when(nv_ref[i] > 0)
    def _():
        x = _load_tokens(xs_ref, XM_ROWS)
        meta = xs_ref[pl.ds(X_ROWS, TM_MOE, stride=XM_ROWS), :]
        hf = _rms(x, g_ref[...]).astype(BF16)

        def expert(wg_ref, wu_ref, wd_ref, gate):
            a = jnp.dot(hf, wg_ref[0], preferred_element_type=F32)
            u = jnp.dot(hf, wu_ref[0], preferred_element_type=F32)
            act = a * jax.nn.sigmoid(a) * u * gate
            return jnp.dot(act.astype(BF16), wd_ref[0], preferred_element_type=F32)

        y = expert(wg1_ref, wu1_ref, wd1_ref, meta[:, META_GATE_A:META_GATE_A + 1])
        y = y + expert(wg2_ref, wu2_ref, wd2_ref, meta[:, META_GATE_B:META_GATE_B + 1])
        _store_tokens(ys_ref, x + y)

    @pl.when(nv_ref[i] == 0)
    def _():
        ys_ref[...] = jnp.zeros(ys_ref.shape, F32)


def _moe(xs, g, te1, te2, nv, used, wg_bf, wu_bf, wd_bf):
    tm = TM_MOE
    n_tiles = nv.shape[0]
    fixed = lambda i, te1, te2, nv, used: (0, 0)
    tile_in = lambda i, te1, te2, nv, used: (jnp.minimum(i, used[0] - 1), 0)
    tile_out = lambda i, te1, te2, nv, used: (i, 0)
    expert1 = lambda i, te1, te2, nv, used: (te1[i], 0, 0)
    expert2 = lambda i, te1, te2, nv, used: (te2[i], 0, 0)
    w_in = lambda index_map: pl.BlockSpec((1, D_MODEL, D_EXPERT), index_map)
    w_out = lambda index_map: pl.BlockSpec((1, D_EXPERT, D_MODEL), index_map)
    grid_spec = pltpu.PrefetchScalarGridSpec(
        num_scalar_prefetch=4,
        grid=(n_tiles,),
        in_specs=[pl.BlockSpec((tm * XM_ROWS, LANES), tile_in),
                  pl.BlockSpec((1, D_MODEL), fixed),
                  w_in(expert1), w_in(expert1), w_out(expert1),
                  w_in(expert2), w_in(expert2), w_out(expert2)],
        out_specs=pl.BlockSpec((tm * X_ROWS, LANES), tile_out))
    return pl.pallas_call(
        _moe_kernel,
        grid_spec=grid_spec,
        out_shape=jax.ShapeDtypeStruct((n_tiles * tm * X_ROWS, LANES), F32),
        compiler_params=_params(("arbitrary",)),
        name="moe",
    )(te1, te2, nv, used, xs, g, wg_bf, wu_bf, wd_bf, wg_bf, wu_bf, wd_bf)


def _route(meta_t, counts, t):
    tm = TM_MOE
    nb = N_GROUPS * N_PAIRS
    n_tiles = t // tm + nb
    meta = meta_t.reshape(t // TM_OUT, META_ROWS, TM_OUT)
    bucket = meta[:, META_BUCKET, :].reshape(t).astype(jnp.int32)
    rank = meta[:, META_RANK, :].reshape(t).astype(jnp.int32)
    counts = counts[0, :nb].astype(jnp.int32)
    ids = jnp.arange(nb, dtype=jnp.int32)

    tiles = (counts + tm - 1) // tm
    tile_end = jnp.cumsum(tiles)
    tile_start = tile_end - tiles
    used = tile_end[-1]
    row0 = tile_start * tm
    dest = rank + jnp.sum(jnp.where(bucket[:, None] == ids[None, :], row0[None, :], 0), axis=1)

    ti = jnp.arange(n_tiles, dtype=jnp.int32)
    tic = jnp.minimum(ti, used - 1)
    tb = jnp.sum((tile_end[None, :] <= tic[:, None]).astype(jnp.int32), axis=1)
    onehot_tb = tb[:, None] == ids[None, :]
    pick = lambda table: jnp.sum(jnp.where(onehot_tb, table[None, :], 0), axis=1)
    nv = jnp.where(ti < used, jnp.clip(pick(counts) - (tic - pick(tile_start)) * tm, 0, tm), 0)
    pair_a = jnp.asarray([0, 0, 0, 1, 1, 2], jnp.int32)
    pair_b = jnp.asarray([1, 2, 3, 2, 3, 3], jnp.int32)
    te1 = pick((ids // N_PAIRS) * PER_GROUP + pair_a[ids % N_PAIRS])
    te2 = pick((ids // N_PAIRS) * PER_GROUP + pair_b[ids % N_PAIRS])

    pads = tiles * tm - counts
    pad_end = jnp.cumsum(pads)
    k = jnp.arange(nb * tm, dtype=jnp.int32)
    kb = jnp.minimum(jnp.sum((pad_end[None, :] <= k[:, None]).astype(jnp.int32), axis=1), nb - 1)
    onehot_kb = kb[:, None] == ids[None, :]
    pick_k = lambda table: jnp.sum(jnp.where(onehot_kb, table[None, :], 0), axis=1)
    in_pad = pick_k(row0 + counts) + (k - pick_k(pad_end - pads))
    free_rows = jnp.where(k < pad_end[-1], in_pad, used * tm + (k - pad_end[-1]))
    i32 = lambda a: a.astype(jnp.int32)
    return (i32(jnp.concatenate([dest, free_rows])), i32(te1), i32(te2), i32(nv),
            i32(used).reshape(1))


def _block_diag(w):
    n = w.shape[0]
    eye = jnp.eye(n, dtype=w.dtype)
    return jnp.einsum("gcd,gh->gchd", w, eye).reshape(n * POOL_GROUP, n * POOL_GROUP)


def kernel(x, attn_norm_g, w_in, conv_w, conv_b, conv_ln_g, conv_ln_b, conv_pw_w, conv_pw_b,
           pool_w, pool_scale, q_norm_g, k_norm_g, lambda_q1, lambda_k1, lambda_q2, lambda_k2,
           attn_sub_norm_g, w_out, ffn_norm_g, router_g_w, router_g_b, router_e_w, router_e_b,
           w_gate, w_up, w_down):
    bsz, s, d = x.shape
    depth = w_in.shape[0]
    t = bsz * s
    qaug, kaug = _alibi_aug(s)
    row = lambda a: a.reshape(1, -1)

    x2 = x.reshape(t, d)
    for l in range(depth):
        lam_init = 0.8 - 0.6 * math.exp(-0.3 * l)
        y_conv, y_pool, q, k, v = _in_proj(
            x2, s, row(attn_norm_g[l]), w_in[l].astype(BF16),
            row(jnp.tile(q_norm_g[l], 2) * (HEAD_DIM ** 0.5 * HEAD_DIM ** -0.5 * math.log2(math.e))),
            row(jnp.tile(k_norm_g[l], 2) * HEAD_DIM ** 0.5),
            conv_w[l], row(conv_b[l]), row(conv_ln_g[l]), row(conv_ln_b[l]),
            conv_pw_w[l].astype(BF16), row(conv_pw_b[l]),
            _block_diag(pool_w[l]).astype(BF16), row(pool_scale[l]))
        y_attn = _attention(q.reshape(bsz, s, -1), k.reshape(bsz, s, -1), v.reshape(bsz, s, -1),
                            qaug, kaug, row(lambda_q1[l]), row(lambda_k1[l]), row(lambda_q2[l]),
                            row(lambda_k2[l]), row(attn_sub_norm_g[l]), lam_init)

        pad = LANES - N_GROUPS - N_EXPERTS
        rw = jnp.pad(jnp.concatenate([router_g_w[l], router_e_w[l]], axis=1), ((0, 0), (0, pad)))
        rb = jnp.pad(jnp.concatenate([router_g_b[l], router_e_b[l]]), (0, pad))
        rw_hi = rw.astype(BF16)
        rw_lo = (rw - rw_hi.astype(F32)).astype(BF16)
        rw_hi_lo = jnp.concatenate([rw_hi, rw_lo], axis=1)
        xm, meta_t, counts = _out_proj(
            x2, y_conv, y_pool, y_attn.reshape(t, -1),
            w_out[l].astype(BF16), row(ffn_norm_g[l]), rw_hi_lo, row(rb))

        dest, te1, te2, nv, used = _route(meta_t, counts, t)
        xs = _dispatch(xm, dest)
        ys = _moe(xs, row(ffn_norm_g[l]), te1, te2, nv, used, w_gate[l].astype(BF16),
                  w_up[l].astype(BF16), w_down[l].astype(BF16))
        x2 = _combine(ys, dest, t, to_rows=l == depth - 1)
    return x2.reshape(bsz, s, d)
```

```python
import functools
import math

import jax
import jax.numpy as jnp
import numpy as np
from jax import lax
from jax.experimental import pallas as pl
from jax.experimental.pallas import tpu as pltpu

D_MODEL = 1024
C_CONV = 256
C_POOL = 256
C_ATTN = 512
N_HEADS = 4
HEAD_DIM = 64
V_DIM = 2 * HEAD_DIM
CONV_WIDTH = 31
POOL_WINDOWS = (2, 4, 8, 16)
POOL_GROUP = C_POOL // len(POOL_WINDOWS)
N_IN = 2 * C_CONV + C_POOL + 3 * C_ATTN
N_GROUPS = 4
PER_GROUP = 4
N_EXPERTS = N_GROUPS * PER_GROUP
D_EXPERT = 256
EPS = 1e-6

LANES = 128
SUBLANES = 8
VMEM_LIMIT = 48 * 1024 * 1024

TM_PROJ = 512
TM_OUT = 512
HALO = 32
TQ = 512
TK = 256
TM_MOE = 512
N_PAIRS = 6
X_ROWS = D_MODEL // LANES
XM_ROWS = X_ROWS + 1
META_GATE_A, META_GATE_B, META_BUCKET, META_RANK = 0, 1, 2, 3
META_ROWS = 8
DMA_UNROLL = 8
COPY_CHUNK = 2048
COPY_WAIT = 512

F32 = jnp.float32
BF16 = jnp.bfloat16


def _params(sem):
    return pltpu.CompilerParams(dimension_semantics=sem, vmem_limit_bytes=VMEM_LIMIT)


def _rms(x, g):
    return x * lax.rsqrt(jnp.mean(x * x, axis=-1, keepdims=True) + EPS) * g


def _load_tokens(x_ref, rows_per_token=X_ROWS):
    if x_ref.shape[-1] != LANES:
        return x_ref[...]
    tm = x_ref.shape[0] // rows_per_token
    return jnp.concatenate(
        [x_ref[pl.ds(c, tm, stride=rows_per_token), :] for c in range(X_ROWS)], axis=1)


def _store_tokens(o_ref, x, rows_per_token=X_ROWS):
    tm = x.shape[0]
    for c in range(X_ROWS):
        o_ref[pl.ds(c, tm, stride=rows_per_token), :] = x[:, c * LANES:(c + 1) * LANES]


def _in_proj_kernel(x_ref, g_ref, w_ref, qg_ref, kg_ref,
                    cw_ref, cb_ref, lng_ref, lnb_ref, pw_ref, pwb_ref, plw_ref, pls_ref,
                    yc_ref, yp_ref, q_ref, k_ref, v_ref, cext_ref, pext_ref, *, tiles_per_seq):
    ts = yc_ref.shape[0]
    seq_tile = pl.program_id(0) % tiles_per_seq
    h = _rms(_load_tokens(x_ref), g_ref[...]).astype(BF16)

    def proj(c0, width):
        return jnp.dot(h, w_ref[:, c0:c0 + width], preferred_element_type=F32)

    @pl.when(seq_tile == 0)
    def _():
        cext_ref[0, 0:HALO, :] = jnp.zeros((HALO, C_CONV), F32)
        pext_ref[0, 0:HALO, :] = jnp.zeros((HALO, C_POOL), F32)

    z = proj(0, 2 * C_CONV)
    cext_ref[0, HALO:, :] = z[:, :C_CONV] * jax.nn.sigmoid(z[:, C_CONV:])
    pext_ref[0, HALO:, :] = proj(2 * C_CONV, C_POOL)

    lane = lax.broadcasted_iota(jnp.int32, (1, LANES), 1)
    first = lane < HEAD_DIM

    def qk_norm(z, gain_ref, out_ref):
        gain = gain_ref[...]
        for hd in range(N_HEADS):
            blk = z[:, hd * LANES:(hd + 1) * LANES]
            sq = blk * blk
            s_all = jnp.sum(sq, axis=-1, keepdims=True)
            s_lo = jnp.sum(jnp.where(first, sq, 0.0), axis=-1, keepdims=True)
            r_lo = lax.rsqrt(s_lo + HEAD_DIM * EPS)
            r_hi = lax.rsqrt((s_all - s_lo) + HEAD_DIM * EPS)
            r = jnp.where(first, r_lo, r_hi)
            out_ref[:, hd * LANES:(hd + 1) * LANES] = (blk * r * gain).astype(BF16)

    c_q = 2 * C_CONV + C_POOL
    qk_norm(proj(c_q, C_ATTN), qg_ref, q_ref)
    qk_norm(proj(c_q + C_ATTN, C_ATTN), kg_ref, k_ref)
    v_ref[...] = proj(c_q + 2 * C_ATTN, C_ATTN).astype(BF16)

    _conv_group(cext_ref, ts, cw_ref, cb_ref, lng_ref, lnb_ref, pw_ref, pwb_ref, yc_ref)
    _pool_group(pext_ref, ts, seq_tile * ts, plw_ref, pls_ref, yp_ref)

    cext_ref[0, 0:HALO, :] = cext_ref[0, ts:ts + HALO, :]
    pext_ref[0, 0:HALO, :] = pext_ref[0, ts:ts + HALO, :]


def _in_proj(x2, seq_len, g, w_bf, qg, kg, conv_w, conv_b, ln_g, ln_b, pw_bf, pw_b,
             pool_w_bf, pool_scale):
    token_major = x2.shape[1] == LANES
    t = x2.shape[0] // X_ROWS if token_major else x2.shape[0]
    tm = TM_PROJ
    assert seq_len % tm == 0
    row = lambda i: (i, 0)
    fixed = lambda i: (0, 0)
    full = lambda a: pl.BlockSpec(a.shape, fixed)
    x_spec = pl.BlockSpec((tm * X_ROWS, LANES) if token_major else (tm, D_MODEL), row)
    small = [g, w_bf, qg, kg, conv_w, conv_b, ln_g, ln_b, pw_bf, pw_b, pool_w_bf, pool_scale]
    widths = [C_CONV, C_POOL, C_ATTN, C_ATTN, C_ATTN]
    return pl.pallas_call(
        functools.partial(_in_proj_kernel, tiles_per_seq=seq_len // tm),
        grid=(t // tm,),
        in_specs=[x_spec] + [full(a) for a in small],
        out_specs=[pl.BlockSpec((tm, w), row) for w in widths],
        out_shape=[jax.ShapeDtypeStruct((t, w), BF16) for w in widths],
        scratch_shapes=[pltpu.VMEM((SUBLANES, tm + HALO, C_CONV), F32),
                        pltpu.VMEM((4, tm + HALO, C_POOL), F32)],
        compiler_params=_params(("arbitrary",)),
        name="in_proj_mix",
    )(x2, *small)


def _conv_group(ext_ref, ts, w_ref, b_ref, lng_ref, lnb_ref, pw_ref, pwb_ref, o_ref):
    n_shifted = ts + HALO - SUBLANES
    for s in range(1, SUBLANES):
        ext_ref[s, 0:n_shifted, :] = ext_ref[0, s:s + n_shifted, :]

    base = HALO - (CONV_WIDTH - 1)
    rows = 128
    for c in range(ts // rows):
        acc = jnp.broadcast_to(b_ref[...], (rows, C_CONV))
        for j in range(CONV_WIDTH):
            phase = (base + j) % SUBLANES
            r0 = c * rows + base + j - phase
            acc = acc + ext_ref[phase, r0:r0 + rows, :] * w_ref[j:j + 1, :]
        mu = jnp.mean(acc, axis=-1, keepdims=True)
        xc = acc - mu
        y = xc * lax.rsqrt(jnp.mean(xc * xc, axis=-1, keepdims=True) + EPS)
        y = y * lng_ref[...] + lnb_ref[...]
        y = y * jax.nn.sigmoid(y)
        out = jnp.dot(y.astype(BF16), pw_ref[...], preferred_element_type=F32) + pwb_ref[...]
        o_ref[c * rows:(c + 1) * rows, :] = out.astype(BF16)


def _pool_group(ext_ref, ts, pos0, w_ref, scale_ref, o_ref):
    n = ts + HALO
    assert POOL_WINDOWS == (2, 4, 8, 16) and HALO >= 4 * SUBLANES
    sums = []
    for k in range(1, 5):
        lo = SUBLANES * k
        shift = 1 << (k - 1)
        level = ext_ref[k - 1, lo:n, :] + ext_ref[k - 1, lo - shift:n - shift, :]
        sums.append(level[HALO - lo:, :])
        if k < 4:
            ext_ref[k, lo:n, :] = level

    lane = lax.broadcasted_iota(jnp.int32, (1, C_POOL), 1)
    group = lane // POOL_GROUP
    win = jnp.left_shift(2, group)
    acc = jnp.where(group == 0, sums[0],
                    jnp.where(group == 1, sums[1], jnp.where(group == 2, sums[2], sums[3])))
    pos = pos0 + lax.broadcasted_iota(jnp.int32, (ts, 1), 0)
    count = jnp.minimum(pos + 1, win).astype(F32)
    diff = acc / count - ext_ref[0, HALO:, :]
    out = jnp.dot(diff.astype(BF16), w_ref[...], preferred_element_type=F32) * scale_ref[...]
    o_ref[...] = out.astype(BF16)


def _attn_kernel(q_ref, k_ref, v_ref, qaug_ref, kaug_ref, lq1_ref, lk1_ref, lq2_ref, lk2_ref,
                 sg_ref, o_ref, kx_ref, q2_all, m_all, l_all, acc_all, *, lam_init):
    qi = pl.program_id(1)
    half = TQ // 2
    assert half == TK

    @pl.when(qi == 0)
    def _():
        for hd in range(N_HEADS):
            kx_ref[hd, :, 0:LANES] = k_ref[0, :, hd * LANES:(hd + 1) * LANES]
            kx_ref[hd, :, LANES:] = kaug_ref[hd]

    lam = (jnp.exp(jnp.sum(lq1_ref[...] * lk1_ref[...], axis=-1, keepdims=True))
           - jnp.exp(jnp.sum(lq2_ref[...] * lk2_ref[...], axis=-1, keepdims=True))
           + lam_init)
    first = lax.broadcasted_iota(jnp.int32, (1, LANES), 1) < HEAD_DIM
    row = lax.broadcasted_iota(jnp.int32, (2 * half, TK), 0)
    col = lax.broadcasted_iota(jnp.int32, (2 * half, TK), 1)
    tri = col <= (row & (half - 1))

    def step(hd, r0, nr, k0, width, masked, first_block):
        q2_ref, m_ref, l_ref, acc_ref = q2_all.at[hd], m_all.at[hd], l_all.at[hd], acc_all.at[hd]
        kv_rows = pl.ds(pl.multiple_of(k0, TK), width)
        kb = kx_ref[hd, kv_rows, :]
        vb = v_ref[0, kv_rows, hd * LANES:(hd + 1) * LANES]
        s = lax.dot_general(q2_ref[r0:r0 + nr, :], kb, (((1,), (1,)), ((), ())),
                            preferred_element_type=F32)
        if masked:
            s = jnp.where(tri, s, -jnp.inf)
        s_max = jnp.max(s, axis=-1, keepdims=True)
        if first_block:
            m_new = jnp.broadcast_to(s_max, (nr, LANES))
        else:
            m_prev = m_ref[r0:r0 + nr]
            m_new = jnp.maximum(m_prev, s_max)
            alpha = jnp.exp2(m_prev - m_new)
        p = jnp.exp2(s - jnp.concatenate([m_new] * (width // LANES), axis=1))
        l_new = jnp.sum(p, axis=-1, keepdims=True)
        acc_new = jnp.dot(p.astype(BF16), vb, preferred_element_type=F32)
        if first_block:
            l_ref[r0:r0 + nr] = jnp.broadcast_to(l_new, (nr, LANES))
            acc_ref[r0:r0 + nr] = acc_new
        else:
            l_ref[r0:r0 + nr] = alpha * l_ref[r0:r0 + nr] + l_new
            acc_ref[r0:r0 + nr] = alpha * acc_ref[r0:r0 + nr] + acc_new
        m_ref[r0:r0 + nr] = m_new

    for hd in range(N_HEADS):
        q2_ref = q2_all.at[hd]
        qb = q_ref[0, :, hd * LANES:(hd + 1) * LANES]
        zero = jnp.zeros_like(qb)
        q_maps = (jnp.where(first, qb, zero), jnp.where(first, zero, qb))
        qa = qaug_ref[hd]
        for part in range(2):
            rows = slice(part * half, (part + 1) * half)
            for m in range(2):
                dst = slice((2 * part + m) * half, (2 * part + m + 1) * half)
                q2_ref[dst, 0:LANES] = q_maps[m][rows]
                q2_ref[dst, LANES:] = qa[rows]

    for hd in range(N_HEADS):
        step(hd, 0, 2 * half, qi * TQ, TK, True, True)
        step(hd, 2 * half, 2 * half, qi * TQ, TK, False, True)
    for hd in range(N_HEADS):
        step(hd, 2 * half, 2 * half, qi * TQ + TK, TK, True, False)

    def body(j, carry):
        for hd in range(N_HEADS):
            step(hd, 0, 4 * half, 2 * j * TQ, 2 * TQ, False, False)
        return carry

    lax.fori_loop(0, qi // 2, body, 0)

    @pl.when(qi % 2 == 1)
    def _():
        for hd in range(N_HEADS):
            step(hd, 0, 4 * half, (qi - 1) * TQ, TQ, False, False)

    for hd in range(N_HEADS):
        l_ref, acc_ref = l_all.at[hd], acc_all.at[hd]
        for part in range(2):
            r = 2 * part * half
            o = (acc_ref[r:r + half] / l_ref[r:r + half]
                 - lam * (acc_ref[r + half:r + 2 * half] / l_ref[r + half:r + 2 * half]))
            o = _rms(o, sg_ref[...]) * (1.0 - lam_init)
            o_ref[0, part * half:(part + 1) * half, hd * LANES:(hd + 1) * LANES] = o.astype(BF16)


def _alibi_aug(s):
    start = 2.0 ** (-8.0 / N_HEADS)
    slopes = np.array([start ** (i + 1) for i in range(N_HEADS)], dtype=np.float32)
    c = math.log2(math.e) * slopes.astype(np.float64)[:, None] * np.arange(s)[None, :]
    pieces = []
    rem = c
    for _ in range(3):
        piece = rem.astype(jnp.bfloat16)
        pieces.append(piece)
        rem = rem - piece.astype(np.float64)
    one = np.ones_like(pieces[0])
    zero = np.zeros((N_HEADS, s, LANES - 6), dtype=jnp.bfloat16)
    kaug = np.concatenate([np.stack(pieces + [one] * 3, axis=-1), zero], axis=-1)
    qaug = np.concatenate([np.stack([one] * 3 + [-p for p in pieces], axis=-1), zero], axis=-1)
    return jnp.asarray(qaug), jnp.asarray(kaug)


def _attention(q, k, v, qaug, kaug, lq1, lk1, lq2, lk2, sub_g, lam_init):
    bsz, s, _ = q.shape
    nq = s // TQ
    fixed2 = lambda b_, i: (0, 0)
    seq = lambda b_, i: (b_, 0, 0)
    vec = pl.BlockSpec((1, HEAD_DIM), fixed2)
    return pl.pallas_call(
        functools.partial(_attn_kernel, lam_init=lam_init),
        grid=(bsz, nq),
        in_specs=[pl.BlockSpec((1, TQ, C_ATTN), lambda b_, i: (b_, i, 0)),
                  pl.BlockSpec((1, s, C_ATTN), seq),
                  pl.BlockSpec((1, s, C_ATTN), seq),
                  pl.BlockSpec((N_HEADS, TQ, LANES), lambda b_, i: (0, i, 0)),
                  pl.BlockSpec((N_HEADS, s, LANES), lambda b_, i: (0, 0, 0)),
                  vec, vec, vec, vec,
                  pl.BlockSpec((1, V_DIM), fixed2)],
        out_specs=pl.BlockSpec((1, TQ, C_ATTN), lambda b_, i: (b_, i, 0)),
        out_shape=jax.ShapeDtypeStruct((bsz, s, C_ATTN), BF16),
        scratch_shapes=[pltpu.VMEM((N_HEADS, s, 2 * LANES), BF16),
                        pltpu.VMEM((N_HEADS, 2 * TQ, 2 * LANES), BF16),
                        pltpu.VMEM((N_HEADS, 2 * TQ, LANES), F32),
                        pltpu.VMEM((N_HEADS, 2 * TQ, LANES), F32),
                        pltpu.VMEM((N_HEADS, 2 * TQ, V_DIM), F32)],
        compiler_params=_params(("parallel", "arbitrary")),
        name="diff_attention",
    )(q, k, v, qaug, kaug, lq1, lk1, lq2, lk2, sub_g)


def _split_bf16(a):
    hi = a.astype(BF16)
    lo = (a - hi.astype(F32)).astype(BF16)
    return hi, lo


def _out_proj_kernel(x_ref, yc_ref, yp_ref, ya_ref, w_ref, g_ref, rw_ref, rb_ref,
                     xm_ref, meta_t_ref, counts_ref, cnt_ref):
    mix = jnp.concatenate([yc_ref[...], yp_ref[...], ya_ref[...]], axis=1)
    x = _load_tokens(x_ref) + jnp.dot(mix, w_ref[...], preferred_element_type=F32)
    tm = x.shape[0]
    _store_tokens(xm_ref, x, XM_ROWS)

    hf_hi, hf_lo = _split_bf16(_rms(x, g_ref[...]))

    both = jnp.dot(hf_hi, rw_ref[...], preferred_element_type=F32)
    logits = (both[:, 0:LANES] + both[:, LANES:]
              + jnp.dot(hf_lo, rw_ref[:, 0:LANES], preferred_element_type=F32)) + rb_ref[...]
    lane = lax.broadcasted_iota(jnp.int32, (tm, LANES), 1)
    lane_f = lane.astype(F32)
    neg = -jnp.inf

    def first_argmax(val, vmax):
        first = jnp.min(jnp.where(val == vmax, lane_f, float(LANES)), axis=-1, keepdims=True)
        return first.astype(jnp.int32)

    lg = jnp.where(lane < N_GROUPS, logits, neg)
    g_max = jnp.max(lg, axis=-1, keepdims=True)
    g_idx = first_argmax(lg, g_max)
    g_p = 1.0 / jnp.sum(jnp.exp(lg - g_max), axis=-1, keepdims=True)

    e_lane = lane - N_GROUPS
    in_group = (e_lane >= g_idx * PER_GROUP) & (e_lane < (g_idx + 1) * PER_GROUP)
    le = jnp.where(in_group, logits, neg)
    e_max = jnp.max(le, axis=-1, keepdims=True)
    pe = jnp.exp(le - e_max)
    pe = pe / jnp.sum(pe, axis=-1, keepdims=True)
    p1 = jnp.max(pe, axis=-1, keepdims=True)
    i1 = first_argmax(jnp.where(in_group, pe, neg), p1)
    pe2 = jnp.where(in_group & (lane != i1), pe, neg)
    p2 = jnp.max(pe2, axis=-1, keepdims=True)
    i2 = first_argmax(pe2, p2)
    denom = p1 + p2
    gate1 = g_p * (p1 / denom)
    gate2 = g_p * (p2 / denom)

    loc1 = i1 - N_GROUPS - g_idx * PER_GROUP
    loc2 = i2 - N_GROUPS - g_idx * PER_GROUP
    a = jnp.minimum(loc1, loc2)
    b = jnp.maximum(loc1, loc2)
    pair = jnp.where(a == 0, 0, jnp.where(a == 1, 3, 5)) + (b - a - 1)
    bucket = g_idx * N_PAIRS + pair
    gate_a = jnp.where(loc1 < loc2, gate1, gate2)
    gate_b = jnp.where(loc1 < loc2, gate2, gate1)

    @pl.when(pl.program_id(0) == 0)
    def _():
        cnt_ref[...] = jnp.zeros(cnt_ref.shape, F32)

    in_bucket = lane == bucket
    earlier = (lax.broadcasted_iota(jnp.int32, (tm, tm), 1)
               < lax.broadcasted_iota(jnp.int32, (tm, tm), 0)).astype(BF16)
    before = jnp.dot(earlier, in_bucket.astype(BF16), preferred_element_type=F32) + cnt_ref[...]
    rank = jnp.sum(jnp.where(in_bucket, before, 0.0), axis=-1, keepdims=True)
    cnt_ref[...] += jnp.sum(in_bucket.astype(F32), axis=0, keepdims=True)
    counts_ref[...] = cnt_ref[...]

    meta = jnp.where(lane == META_GATE_A, gate_a,
                     jnp.where(lane == META_GATE_B, gate_b,
                               jnp.where(lane == META_BUCKET, bucket.astype(F32),
                                         jnp.where(lane == META_RANK, rank, 0.0))))
    xm_ref[pl.ds(X_ROWS, tm, stride=XM_ROWS), :] = meta
    meta_t_ref[...] = meta.T[0:META_ROWS, :]


def _out_proj(x2, yc, yp, ya, w_bf, g, rw_hi_lo, rb):
    token_major = x2.shape[1] == LANES
    t = x2.shape[0] // X_ROWS if token_major else x2.shape[0]
    tm = TM_OUT
    row = lambda i: (i, 0)
    fixed = lambda i: (0, 0)
    x_spec = pl.BlockSpec((tm * X_ROWS, LANES) if token_major else (tm, D_MODEL), row)
    return pl.pallas_call(
        _out_proj_kernel,
        grid=(t // tm,),
        in_specs=[x_spec,
                  pl.BlockSpec((tm, C_CONV), row),
                  pl.BlockSpec((tm, C_POOL), row),
                  pl.BlockSpec((tm, C_ATTN), row),
                  pl.BlockSpec((D_MODEL, D_MODEL), fixed),
                  pl.BlockSpec((1, D_MODEL), fixed),
                  pl.BlockSpec((D_MODEL, 2 * LANES), fixed),
                  pl.BlockSpec((1, LANES), fixed)],
        out_specs=[pl.BlockSpec((tm * XM_ROWS, LANES), row),
                   pl.BlockSpec((META_ROWS, tm), row),
                   pl.BlockSpec((1, LANES), fixed)],
        out_shape=[jax.ShapeDtypeStruct((t * XM_ROWS, LANES), F32),
                   jax.ShapeDtypeStruct((t // tm * META_ROWS, tm), F32),
                   jax.ShapeDtypeStruct((1, LANES), F32)],
        scratch_shapes=[pltpu.VMEM((1, LANES), F32)],
        compiler_params=_params(("arbitrary",)),
        name="out_proj_router",
    )(x2, yc, yp, ya, w_bf, g, rw_hi_lo, rb)


def _for_chunk_rows(dest_ref, start_copy, wait_copy):
    base = pl.program_id(0) * COPY_CHUNK

    def body(blk, carry):
        rows = [blk * DMA_UNROLL + u for u in range(DMA_UNROLL)]
        slots = [dest_ref[base + r] for r in rows]
        for u, (r, slot) in enumerate(zip(rows, slots)):
            start_copy(r, slot, u % 2)
        return carry

    lax.fori_loop(0, COPY_CHUNK // DMA_UNROLL, body, 0)
    for piece in range(COPY_CHUNK // COPY_WAIT):
        wait_copy(piece)


def _dispatch_kernel(dest_ref, x_ref, out_hbm, sem):
    def start_copy(r, slot, priority):
        pltpu.make_async_copy(x_ref.at[pl.ds(r * XM_ROWS, XM_ROWS)],
                              out_hbm.at[pl.ds(slot * XM_ROWS, XM_ROWS)],
                              sem.at[0]).start(priority=priority)

    def wait_copy(piece):
        n = COPY_WAIT * XM_ROWS
        pltpu.make_async_copy(x_ref.at[pl.ds(piece * n, n)], out_hbm.at[pl.ds(0, n)],
                              sem.at[0]).wait()

    _for_chunk_rows(dest_ref, start_copy, wait_copy)


def _dispatch(xm, dest):
    n_tokens = xm.shape[0] // XM_ROWS
    n_rows = dest.shape[0]
    assert n_tokens % COPY_CHUNK == 0 and n_rows % COPY_CHUNK == 0
    last = n_tokens // COPY_CHUNK - 1
    grid_spec = pltpu.PrefetchScalarGridSpec(
        num_scalar_prefetch=1,
        grid=(n_rows // COPY_CHUNK,),
        in_specs=[pl.BlockSpec((COPY_CHUNK * XM_ROWS, LANES),
                               lambda i, dest: (jnp.minimum(i, last), 0))],
        out_specs=pl.BlockSpec(memory_space=pl.ANY),
        scratch_shapes=[pltpu.SemaphoreType.DMA((1,))])
    return pl.pallas_call(
        _dispatch_kernel,
        grid_spec=grid_spec,
        out_shape=jax.ShapeDtypeStruct((n_rows * XM_ROWS, LANES), F32),
        compiler_params=_params(("arbitrary",)),
        name="moe_dispatch",
    )(dest, xm)


def _combine_kernel(dest_ref, ys_hbm, o_ref, *scratch, to_rows):
    buf = scratch[0] if to_rows else o_ref
    sem = scratch[-1]

    def start_copy(r, slot, priority):
        pltpu.make_async_copy(ys_hbm.at[pl.ds(slot * X_ROWS, X_ROWS)],
                              buf.at[pl.ds(r * X_ROWS, X_ROWS)],
                              sem.at[0]).start(priority=priority)

    def wait_copy(piece):
        n = COPY_WAIT * X_ROWS
        pltpu.make_async_copy(ys_hbm.at[pl.ds(0, n)], buf.at[pl.ds(piece * n, n)],
                              sem.at[0]).wait()

    _for_chunk_rows(dest_ref, start_copy, wait_copy)
    if to_rows:
        o_ref[...] = _load_tokens(buf)


def _combine(ys, dest, n_tokens, to_rows):
    assert n_tokens % COPY_CHUNK == 0
    chunk_rows = COPY_CHUNK * X_ROWS
    if to_rows:
        out_spec = pl.BlockSpec((COPY_CHUNK, D_MODEL), lambda i, dest: (i, 0))
        out_shape = jax.ShapeDtypeStruct((n_tokens, D_MODEL), F32)
        scratch = [pltpu.VMEM((chunk_rows, LANES), F32)]
    else:
        out_spec = pl.BlockSpec((chunk_rows, LANES), lambda i, dest: (i, 0))
        out_shape = jax.ShapeDtypeStruct((n_tokens * X_ROWS, LANES), F32)
        scratch = []
    grid_spec = pltpu.PrefetchScalarGridSpec(
        num_scalar_prefetch=1,
        grid=(n_tokens // COPY_CHUNK,),
        in_specs=[pl.BlockSpec(memory_space=pl.ANY)],
        out_specs=out_spec,
        scratch_shapes=scratch + [pltpu.SemaphoreType.DMA((1,))])
    return pl.pallas_call(
        functools.partial(_combine_kernel, to_rows=to_rows),
        grid_spec=grid_spec,
        out_shape=out_shape,
        compiler_params=_params(("arbitrary",)),
        name="moe_combine",
    )(dest, ys)


def _moe_kernel(te1_ref, te2_ref, nv_ref, used_ref,
                xs_ref, g_ref, wg1_ref, wu1_ref, wd1_ref, wg2_ref, wu2_ref, wd2_ref, ys_ref):
    i = pl.program_id(0)

    @pl.when(nv_ref[i] > 0)
    def _():
        x = _load_tokens(xs_ref, XM_ROWS)
        meta = xs_ref[pl.ds(X_ROWS, TM_MOE, stride=XM_ROWS), :]
        hf = _rms(x, g_ref[...]).astype(BF16)

        def expert(wg_ref, wu_ref, wd_ref, gate):
            a = jnp.dot(hf, wg_ref[0], preferred_element_type=F32)
            u = jnp.dot(hf, wu_ref[0], preferred_element_type=F32)
            act = a * jax.nn.sigmoid(a) * u * gate
            return jnp.dot(act.astype(BF16), wd_ref[0], preferred_element_type=F32)

        y = expert(wg1_ref, wu1_ref, wd1_ref, meta[:, META_GATE_A:META_GATE_A + 1])
        y = y + expert(wg2_ref, wu2_ref, wd2_ref, meta[:, META_GATE_B:META_GATE_B + 1])
        _store_tokens(ys_ref, x + y)

    @pl.when(nv_ref[i] == 0)
    def _():
        ys_ref[...] = jnp.zeros(ys_ref.shape, F32)


def _moe(xs, g, te1, te2, nv, used, wg_bf, wu_bf, wd_bf):
    tm = TM_MOE
    n_tiles = nv.shape[0]
    fixed = lambda i, te1, te2, nv, used: (0, 0)
    tile_in = lambda i, te1, te2, nv, used: (jnp.minimum(i, used[0] - 1), 0)
    tile_out = lambda i, te1, te2, nv, used: (i, 0)
    expert1 = lambda i, te1, te2, nv, used: (te1[i], 0, 0)
    expert2 = lambda i, te1, te2, nv, used: (te2[i], 0, 0)
    w_in = lambda index_map: pl.BlockSpec((1, D_MODEL, D_EXPERT), index_map)
    w_out = lambda index_map: pl.BlockSpec((1, D_EXPERT, D_MODEL), index_map)
    grid_spec = pltpu.PrefetchScalarGridSpec(
        num_scalar_prefetch=4,
        grid=(n_tiles,),
        in_specs=[pl.BlockSpec((tm * XM_ROWS, LANES), tile_in),
                  pl.BlockSpec((1, D_MODEL), fixed),
                  w_in(expert1), w_in(expert1), w_out(expert1),
                  w_in(expert2), w_in(expert2), w_out(expert2)],
        out_specs=pl.BlockSpec((tm * X_ROWS, LANES), tile_out))
    return pl.pallas_call(
        _moe_kernel,
        grid_spec=grid_spec,
        out_shape=jax.ShapeDtypeStruct((n_tiles * tm * X_ROWS, LANES), F32),
        compiler_params=_params(("arbitrary",)),
        name="moe",
    )(te1, te2, nv, used, xs, g, wg_bf, wu_bf, wd_bf, wg_bf, wu_bf, wd_bf)


def _route(meta_t, counts, t):
    tm = TM_MOE
    nb = N_GROUPS * N_PAIRS
    n_tiles = t // tm + nb
    meta = meta_t.reshape(t // TM_OUT, META_ROWS, TM_OUT)
    bucket = meta[:, META_BUCKET, :].reshape(t).astype(jnp.int32)
    rank = meta[:, META_RANK, :].reshape(t).astype(jnp.int32)
    counts = counts[0, :nb].astype(jnp.int32)
    ids = jnp.arange(nb, dtype=jnp.int32)

    tiles = (counts + tm - 1) // tm
    tile_end = jnp.cumsum(tiles)
    tile_start = tile_end - tiles
    used = tile_end[-1]
    row0 = tile_start * tm
    dest = rank + jnp.sum(jnp.where(bucket[:, None] == ids[None, :], row0[None, :], 0), axis=1)

    ti = jnp.arange(n_tiles, dtype=jnp.int32)
    tic = jnp.minimum(ti, used - 1)
    tb = jnp.sum((tile_end[None, :] <= tic[:, None]).astype(jnp.int32), axis=1)
    onehot_tb = tb[:, None] == ids[None, :]
    pick = lambda table: jnp.sum(jnp.where(onehot_tb, table[None, :], 0), axis=1)
    nv = jnp.where(ti < used, jnp.clip(pick(counts) - (tic - pick(tile_start)) * tm, 0, tm), 0)
    pair_a = jnp.asarray([0, 0, 0, 1, 1, 2], jnp.int32)
    pair_b = jnp.asarray([1, 2, 3, 2, 3, 3], jnp.int32)
    te1 = pick((ids // N_PAIRS) * PER_GROUP + pair_a[ids % N_PAIRS])
    te2 = pick((ids // N_PAIRS) * PER_GROUP + pair_b[ids % N_PAIRS])

    pads = tiles * tm - counts
    pad_end = jnp.cumsum(pads)
    k = jnp.arange(nb * tm, dtype=jnp.int32)
    kb = jnp.minimum(jnp.sum((pad_end[None, :] <= k[:, None]).astype(jnp.int32), axis=1), nb - 1)
    onehot_kb = kb[:, None] == ids[None, :]
    pick_k = lambda table: jnp.sum(jnp.where(onehot_kb, table[None, :], 0), axis=1)
    in_pad = pick_k(row0 + counts) + (k - pick_k(pad_end - pads))
    free_rows = jnp.where(k < pad_end[-1], in_pad, used * tm + (k - pad_end[-1]))
    i32 = lambda a: a.astype(jnp.int32)
    return (i32(jnp.concatenate([dest, free_rows])), i32(te1), i32(te2), i32(nv),
            i32(used).reshape(1))


def _block_diag(w):
    n = w.shape[0]
    eye = jnp.eye(n, dtype=w.dtype)
    return jnp.einsum("gcd,gh->gchd", w, eye).reshape(n * POOL_GROUP, n * POOL_GROUP)


def kernel(x, attn_norm_g, w_in, conv_w, conv_b, conv_ln_g, conv_ln_b, conv_pw_w, conv_pw_b,
           pool_w, pool_scale, q_norm_g, k_norm_g, lambda_q1, lambda_k1, lambda_q2, lambda_k2,
           attn_sub_norm_g, w_out, ffn_norm_g, router_g_w, router_g_b, router_e_w, router_e_b,
           w_gate, w_up, w_down):
    bsz, s, d = x.shape
    depth = w_in.shape[0]
    t = bsz * s
    qaug, kaug = _alibi_aug(s)
    row = lambda a: a.reshape(1, -1)
    all_experts = lambda w: w.astype(BF16).reshape((depth * N_EXPERTS,) + w.shape[2:])
    wg_all, wu_all, wd_all = all_experts(w_gate), all_experts(w_up), all_experts(w_down)

    x2 = x.reshape(t, d)
    for l in range(depth):
        lam_init = 0.8 - 0.6 * math.exp(-0.3 * l)
        y_conv, y_pool, q, k, v = _in_proj(
            x2, s, row(attn_norm_g[l]), w_in[l].astype(BF16),
            row(jnp.tile(q_norm_g[l], 2) * (HEAD_DIM ** 0.5 * HEAD_DIM ** -0.5 * math.log2(math.e))),
            row(jnp.tile(k_norm_g[l], 2) * HEAD_DIM ** 0.5),
            conv_w[l], row(conv_b[l]), row(conv_ln_g[l]), row(conv_ln_b[l]),
            conv_pw_w[l].astype(BF16), row(conv_pw_b[l]),
            _block_diag(pool_w[l]).astype(BF16), row(pool_scale[l]))
        y_attn = _attention(q.reshape(bsz, s, -1), k.reshape(bsz, s, -1), v.reshape(bsz, s, -1),
                            qaug, kaug, row(lambda_q1[l]), row(lambda_k1[l]), row(lambda_q2[l]),
                            row(lambda_k2[l]), row(attn_sub_norm_g[l]), lam_init)

        pad = LANES - N_GROUPS - N_EXPERTS
        rw = jnp.pad(jnp.concatenate([router_g_w[l], router_e_w[l]], axis=1), ((0, 0), (0, pad)))
        rb = jnp.pad(jnp.concatenate([router_g_b[l], router_e_b[l]]), (0, pad))
        rw_hi = rw.astype(BF16)
        rw_lo = (rw - rw_hi.astype(F32)).astype(BF16)
        rw_hi_lo = jnp.concatenate([rw_hi, rw_lo], axis=1)
        xm, meta_t, counts = _out_proj(
            x2, y_conv, y_pool, y_attn.reshape(t, -1),
            w_out[l].astype(BF16), row(ffn_norm_g[l]), rw_hi_lo, row(rb))

        dest, te1, te2, nv, used = _route(meta_t, counts, t)
        xs = _dispatch(xm, dest)
        ys = _moe(xs, row(ffn_norm_g[l]), te1 + l * N_EXPERTS, te2 + l * N_EXPERTS, nv, used,
                  wg_all, wu_all, wd_all)
        x2 = _combine(ys, dest, t, to_rows=l == depth - 1)
    return x2.reshape(bsz, s, d)
```

```python
import functools
import math

import jax
import jax.numpy as jnp
import numpy as np
from jax import lax
from jax.experimental import pallas as pl
from jax.experimental.pallas import tpu as pltpu

D_MODEL = 1024
C_CONV = 256
C_POOL = 256
C_ATTN = 512
N_HEADS = 4
HEAD_DIM = 64
V_DIM = 2 * HEAD_DIM
CONV_WIDTH = 31
POOL_WINDOWS = (2, 4, 8, 16)
POOL_GROUP = C_POOL // len(POOL_WINDOWS)
N_IN = 2 * C_CONV + C_POOL + 3 * C_ATTN
N_GROUPS = 4
PER_GROUP = 4
N_EXPERTS = N_GROUPS * PER_GROUP
D_EXPERT = 256
EPS = 1e-6

LANES = 128
SUBLANES = 8
VMEM_LIMIT = 48 * 1024 * 1024

TM_PROJ = 512
TM_OUT = 512
HALO = 32
TQ = 512
TK = 256
TM_MOE = 512
N_PAIRS = 6
X_ROWS = D_MODEL // LANES
XM_ROWS = X_ROWS + 1
META_GATE_A, META_GATE_B, META_BUCKET, META_RANK = 0, 1, 2, 3
META_ROWS = 8
DMA_UNROLL = 8
DISPATCH_CHUNK = 4096
COPY_CHUNK = 2048
COPY_WAIT = 512

F32 = jnp.float32
BF16 = jnp.bfloat16


def _params(sem):
    return pltpu.CompilerParams(dimension_semantics=sem, vmem_limit_bytes=VMEM_LIMIT)


def _rms(x, g):
    return x * lax.rsqrt(jnp.mean(x * x, axis=-1, keepdims=True) + EPS) * g


def _load_tokens(x_ref, rows_per_token=X_ROWS):
    if x_ref.shape[-1] != LANES:
        return x_ref[...]
    tm = x_ref.shape[0] // rows_per_token
    return jnp.concatenate(
        [x_ref[pl.ds(c, tm, stride=rows_per_token), :] for c in range(X_ROWS)], axis=1)


def _store_tokens(o_ref, x, rows_per_token=X_ROWS):
    tm = x.shape[0]
    for c in range(X_ROWS):
        o_ref[pl.ds(c, tm, stride=rows_per_token), :] = x[:, c * LANES:(c + 1) * LANES]


def _in_proj_kernel(x_ref, g_ref, w_ref, qg_ref, kg_ref,
                    cw_ref, cb_ref, lng_ref, lnb_ref, pw_ref, pwb_ref, plw_ref, pls_ref,
                    yc_ref, yp_ref, q_ref, k_ref, v_ref, cext_ref, pext_ref, *, tiles_per_seq):
    ts = yc_ref.shape[0]
    seq_tile = pl.program_id(0) % tiles_per_seq
    h = _rms(_load_tokens(x_ref), g_ref[...]).astype(BF16)

    def proj(c0, width):
        return jnp.dot(h, w_ref[:, c0:c0 + width], preferred_element_type=F32)

    @pl.when(seq_tile == 0)
    def _():
        cext_ref[0, 0:HALO, :] = jnp.zeros((HALO, C_CONV), F32)
        pext_ref[0, 0:HALO, :] = jnp.zeros((HALO, C_POOL), F32)

    z = proj(0, 2 * C_CONV)
    cext_ref[0, HALO:, :] = z[:, :C_CONV] * jax.nn.sigmoid(z[:, C_CONV:])
    pext_ref[0, HALO:, :] = proj(2 * C_CONV, C_POOL)

    lane = lax.broadcasted_iota(jnp.int32, (1, LANES), 1)
    first = lane < HEAD_DIM

    def qk_norm(z, gain_ref, out_ref):
        gain = gain_ref[...]
        for hd in range(N_HEADS):
            blk = z[:, hd * LANES:(hd + 1) * LANES]
            sq = blk * blk
            s_all = jnp.sum(sq, axis=-1, keepdims=True)
            s_lo = jnp.sum(jnp.where(first, sq, 0.0), axis=-1, keepdims=True)
            r_lo = lax.rsqrt(s_lo + HEAD_DIM * EPS)
            r_hi = lax.rsqrt((s_all - s_lo) + HEAD_DIM * EPS)
            r = jnp.where(first, r_lo, r_hi)
            out_ref[:, hd * LANES:(hd + 1) * LANES] = (blk * r * gain).astype(BF16)

    c_q = 2 * C_CONV + C_POOL
    qk_norm(proj(c_q, C_ATTN), qg_ref, q_ref)
    qk_norm(proj(c_q + C_ATTN, C_ATTN), kg_ref, k_ref)
    v_ref[...] = proj(c_q + 2 * C_ATTN, C_ATTN).astype(BF16)

    _conv_group(cext_ref, ts, cw_ref, cb_ref, lng_ref, lnb_ref, pw_ref, pwb_ref, yc_ref)
    _pool_group(pext_ref, ts, seq_tile * ts, plw_ref, pls_ref, yp_ref)

    cext_ref[0, 0:HALO, :] = cext_ref[0, ts:ts + HALO, :]
    pext_ref[0, 0:HALO, :] = pext_ref[0, ts:ts + HALO, :]


def _in_proj(x2, seq_len, g, w_bf, qg, kg, conv_w, conv_b, ln_g, ln_b, pw_bf, pw_b,
             pool_w_bf, pool_scale):
    token_major = x2.shape[1] == LANES
    t = x2.shape[0] // X_ROWS if token_major else x2.shape[0]
    tm = TM_PROJ
    assert seq_len % tm == 0
    row = lambda i: (i, 0)
    fixed = lambda i: (0, 0)
    full = lambda a: pl.BlockSpec(a.shape, fixed)
    x_spec = pl.BlockSpec((tm * X_ROWS, LANES) if token_major else (tm, D_MODEL), row)
    small = [g, w_bf, qg, kg, conv_w, conv_b, ln_g, ln_b, pw_bf, pw_b, pool_w_bf, pool_scale]
    widths = [C_CONV, C_POOL, C_ATTN, C_ATTN, C_ATTN]
    return pl.pallas_call(
        functools.partial(_in_proj_kernel, tiles_per_seq=seq_len // tm),
        grid=(t // tm,),
        in_specs=[x_spec] + [full(a) for a in small],
        out_specs=[pl.BlockSpec((tm, w), row) for w in widths],
        out_shape=[jax.ShapeDtypeStruct((t, w), BF16) for w in widths],
        scratch_shapes=[pltpu.VMEM((SUBLANES, tm + HALO, C_CONV), F32),
                        pltpu.VMEM((4, tm + HALO, C_POOL), F32)],
        compiler_params=_params(("arbitrary",)),
        name="in_proj_mix",
    )(x2, *small)


def _conv_group(ext_ref, ts, w_ref, b_ref, lng_ref, lnb_ref, pw_ref, pwb_ref, o_ref):
    n_shifted = ts + HALO - SUBLANES
    for s in range(1, SUBLANES):
        ext_ref[s, 0:n_shifted, :] = ext_ref[0, s:s + n_shifted, :]

    base = HALO - (CONV_WIDTH - 1)
    rows = 128
    for c in range(ts // rows):
        acc = jnp.broadcast_to(b_ref[...], (rows, C_CONV))
        for j in range(CONV_WIDTH):
            phase = (base + j) % SUBLANES
            r0 = c * rows + base + j - phase
            acc = acc + ext_ref[phase, r0:r0 + rows, :] * w_ref[j:j + 1, :]
        mu = jnp.mean(acc, axis=-1, keepdims=True)
        xc = acc - mu
        y = xc * lax.rsqrt(jnp.mean(xc * xc, axis=-1, keepdims=True) + EPS)
        y = y * lng_ref[...] + lnb_ref[...]
        y = y * jax.nn.sigmoid(y)
        out = jnp.dot(y.astype(BF16), pw_ref[...], preferred_element_type=F32) + pwb_ref[...]
        o_ref[c * rows:(c + 1) * rows, :] = out.astype(BF16)


def _pool_group(ext_ref, ts, pos0, w_ref, scale_ref, o_ref):
    n = ts + HALO
    assert POOL_WINDOWS == (2, 4, 8, 16) and HALO >= 4 * SUBLANES
    sums = []
    for k in range(1, 5):
        lo = SUBLANES * k
        shift = 1 << (k - 1)
        level = ext_ref[k - 1, lo:n, :] + ext_ref[k - 1, lo - shift:n - shift, :]
        sums.append(level[HALO - lo:, :])
        if k < 4:
            ext_ref[k, lo:n, :] = level

    lane = lax.broadcasted_iota(jnp.int32, (1, C_POOL), 1)
    group = lane // POOL_GROUP
    win = jnp.left_shift(2, group)
    acc = jnp.where(group == 0, sums[0],
                    jnp.where(group == 1, sums[1], jnp.where(group == 2, sums[2], sums[3])))
    pos = pos0 + lax.broadcasted_iota(jnp.int32, (ts, 1), 0)
    count = jnp.minimum(pos + 1, win).astype(F32)
    diff = acc / count - ext_ref[0, HALO:, :]
    out = jnp.dot(diff.astype(BF16), w_ref[...], preferred_element_type=F32) * scale_ref[...]
    o_ref[...] = out.astype(BF16)


def _attn_kernel(q_ref, k_ref, v_ref, qaug_ref, kaug_ref, lq1_ref, lk1_ref, lq2_ref, lk2_ref,
                 sg_ref, o_ref, kx_ref, q2_all, m_all, l_all, acc_all, *, lam_init):
    qi = pl.program_id(1)
    half = TQ // 2
    assert half == TK

    @pl.when(qi == 0)
    def _():
        for hd in range(N_HEADS):
            kx_ref[hd, :, 0:LANES] = k_ref[0, :, hd * LANES:(hd + 1) * LANES]
            kx_ref[hd, :, LANES:] = kaug_ref[hd]

    lam = (jnp.exp(jnp.sum(lq1_ref[...] * lk1_ref[...], axis=-1, keepdims=True))
           - jnp.exp(jnp.sum(lq2_ref[...] * lk2_ref[...], axis=-1, keepdims=True))
           + lam_init)
    first = lax.broadcasted_iota(jnp.int32, (1, LANES), 1) < HEAD_DIM
    row = lax.broadcasted_iota(jnp.int32, (2 * half, TK), 0)
    col = lax.broadcasted_iota(jnp.int32, (2 * half, TK), 1)
    tri = col <= (row & (half - 1))

    def step(hd, r0, nr, k0, width, masked, first_block):
        q2_ref, m_ref, l_ref, acc_ref = q2_all.at[hd], m_all.at[hd], l_all.at[hd], acc_all.at[hd]
        kv_rows = pl.ds(pl.multiple_of(k0, TK), width)
        kb = kx_ref[hd, kv_rows, :]
        vb = v_ref[0, kv_rows, hd * LANES:(hd + 1) * LANES]
        s = lax.dot_general(q2_ref[r0:r0 + nr, :], kb, (((1,), (1,)), ((), ())),
                            preferred_element_type=F32)
        if masked:
            s = jnp.where(tri, s, -jnp.inf)
        s_max = jnp.max(s, axis=-1, keepdims=True)
        if first_block:
            m_new = jnp.broadcast_to(s_max, (nr, LANES))
        else:
            m_prev = m_ref[r0:r0 + nr]
            m_new = jnp.maximum(m_prev, s_max)
            alpha = jnp.exp2(m_prev - m_new)
        p = jnp.exp2(s - jnp.concatenate([m_new] * (width // LANES), axis=1))
        l_new = jnp.sum(p, axis=-1, keepdims=True)
        acc_new = jnp.dot(p.astype(BF16), vb, preferred_element_type=F32)
        if first_block:
            l_ref[r0:r0 + nr] = jnp.broadcast_to(l_new, (nr, LANES))
            acc_ref[r0:r0 + nr] = acc_new
        else:
            l_ref[r0:r0 + nr] = alpha * l_ref[r0:r0 + nr] + l_new
            acc_ref[r0:r0 + nr] = alpha * acc_ref[r0:r0 + nr] + acc_new
        m_ref[r0:r0 + nr] = m_new

    for hd in range(N_HEADS):
        q2_ref = q2_all.at[hd]
        qb = q_ref[0, :, hd * LANES:(hd + 1) * LANES]
        zero = jnp.zeros_like(qb)
        q_maps = (jnp.where(first, qb, zero), jnp.where(first, zero, qb))
        qa = qaug_ref[hd]
        for part in range(2):
            rows = slice(part * half, (part + 1) * half)
            for m in range(2):
                dst = slice((2 * part + m) * half, (2 * part + m + 1) * half)
                q2_ref[dst, 0:LANES] = q_maps[m][rows]
                q2_ref[dst, LANES:] = qa[rows]

    for hd in range(N_HEADS):
        step(hd, 0, 2 * half, qi * TQ, TK, True, True)
        step(hd, 2 * half, 2 * half, qi * TQ, TK, False, True)
    for hd in range(N_HEADS):
        step(hd, 2 * half, 2 * half, qi * TQ + TK, TK, True, False)

    def body(j, carry):
        for hd in range(N_HEADS):
            step(hd, 0, 4 * half, 2 * j * TQ, 2 * TQ, False, False)
        return carry

    lax.fori_loop(0, qi // 2, body, 0)

    @pl.when(qi % 2 == 1)
    def _():
        for hd in range(N_HEADS):
            step(hd, 0, 4 * half, (qi - 1) * TQ, TQ, False, False)

    for hd in range(N_HEADS):
        l_ref, acc_ref = l_all.at[hd], acc_all.at[hd]
        for part in range(2):
            r = 2 * part * half
            o = (acc_ref[r:r + half] / l_ref[r:r + half]
                 - lam * (acc_ref[r + half:r + 2 * half] / l_ref[r + half:r + 2 * half]))
            o = _rms(o, sg_ref[...]) * (1.0 - lam_init)
            o_ref[0, part * half:(part + 1) * half, hd * LANES:(hd + 1) * LANES] = o.astype(BF16)


def _alibi_aug(s):
    start = 2.0 ** (-8.0 / N_HEADS)
    slopes = np.array([start ** (i + 1) for i in range(N_HEADS)], dtype=np.float32)
    c = math.log2(math.e) * slopes.astype(np.float64)[:, None] * np.arange(s)[None, :]
    pieces = []
    rem = c
    for _ in range(3):
        piece = rem.astype(jnp.bfloat16)
        pieces.append(piece)
        rem = rem - piece.astype(np.float64)
    one = np.ones_like(pieces[0])
    zero = np.zeros((N_HEADS, s, LANES - 6), dtype=jnp.bfloat16)
    kaug = np.concatenate([np.stack(pieces + [one] * 3, axis=-1), zero], axis=-1)
    qaug = np.concatenate([np.stack([one] * 3 + [-p for p in pieces], axis=-1), zero], axis=-1)
    return jnp.asarray(qaug), jnp.asarray(kaug)


def _attention(q, k, v, qaug, kaug, lq1, lk1, lq2, lk2, sub_g, lam_init):
    bsz, s, _ = q.shape
    nq = s // TQ
    fixed2 = lambda b_, i: (0, 0)
    seq = lambda b_, i: (b_, 0, 0)
    vec = pl.BlockSpec((1, HEAD_DIM), fixed2)
    return pl.pallas_call(
        functools.partial(_attn_kernel, lam_init=lam_init),
        grid=(bsz, nq),
        in_specs=[pl.BlockSpec((1, TQ, C_ATTN), lambda b_, i: (b_, i, 0)),
                  pl.BlockSpec((1, s, C_ATTN), seq),
                  pl.BlockSpec((1, s, C_ATTN), seq),
                  pl.BlockSpec((N_HEADS, TQ, LANES), lambda b_, i: (0, i, 0)),
                  pl.BlockSpec((N_HEADS, s, LANES), lambda b_, i: (0, 0, 0)),
                  vec, vec, vec, vec,
                  pl.BlockSpec((1, V_DIM), fixed2)],
        out_specs=pl.BlockSpec((1, TQ, C_ATTN), lambda b_, i: (b_, i, 0)),
        out_shape=jax.ShapeDtypeStruct((bsz, s, C_ATTN), BF16),
        scratch_shapes=[pltpu.VMEM((N_HEADS, s, 2 * LANES), BF16),
                        pltpu.VMEM((N_HEADS, 2 * TQ, 2 * LANES), BF16),
                        pltpu.VMEM((N_HEADS, 2 * TQ, LANES), F32),
                        pltpu.VMEM((N_HEADS, 2 * TQ, LANES), F32),
                        pltpu.VMEM((N_HEADS, 2 * TQ, V_DIM), F32)],
        compiler_params=_params(("parallel", "arbitrary")),
        name="diff_attention",
    )(q, k, v, qaug, kaug, lq1, lk1, lq2, lk2, sub_g)


def _split_bf16(a):
    hi = a.astype(BF16)
    lo = (a - hi.astype(F32)).astype(BF16)
    return hi, lo


def _out_proj_kernel(x_ref, yc_ref, yp_ref, ya_ref, w_ref, g_ref, rw_ref, rb_ref,
                     xm_ref, meta_t_ref, counts_ref, cnt_ref):
    mix = jnp.concatenate([yc_ref[...], yp_ref[...], ya_ref[...]], axis=1)
    x = _load_tokens(x_ref) + jnp.dot(mix, w_ref[...], preferred_element_type=F32)
    tm = x.shape[0]
    _store_tokens(xm_ref, x, XM_ROWS)

    hf_hi, hf_lo = _split_bf16(_rms(x, g_ref[...]))

    both = jnp.dot(hf_hi, rw_ref[...], preferred_element_type=F32)
    logits = (both[:, 0:LANES] + both[:, LANES:]
              + jnp.dot(hf_lo, rw_ref[:, 0:LANES], preferred_element_type=F32)) + rb_ref[...]
    lane = lax.broadcasted_iota(jnp.int32, (tm, LANES), 1)
    lane_f = lane.astype(F32)
    neg = -jnp.inf

    def first_argmax(val, vmax):
        first = jnp.min(jnp.where(val == vmax, lane_f, float(LANES)), axis=-1, keepdims=True)
        return first.astype(jnp.int32)

    lg = jnp.where(lane < N_GROUPS, logits, neg)
    g_max = jnp.max(lg, axis=-1, keepdims=True)
    g_idx = first_argmax(lg, g_max)
    g_p = 1.0 / jnp.sum(jnp.exp(lg - g_max), axis=-1, keepdims=True)

    e_lane = lane - N_GROUPS
    in_group = (e_lane >= g_idx * PER_GROUP) & (e_lane < (g_idx + 1) * PER_GROUP)
    le = jnp.where(in_group, logits, neg)
    e_max = jnp.max(le, axis=-1, keepdims=True)
    pe = jnp.exp(le - e_max)
    pe = pe / jnp.sum(pe, axis=-1, keepdims=True)
    p1 = jnp.max(pe, axis=-1, keepdims=True)
    i1 = first_argmax(jnp.where(in_group, pe, neg), p1)
    pe2 = jnp.where(in_group & (lane != i1), pe, neg)
    p2 = jnp.max(pe2, axis=-1, keepdims=True)
    i2 = first_argmax(pe2, p2)
    denom = p1 + p2
    gate1 = g_p * (p1 / denom)
    gate2 = g_p * (p2 / denom)

    loc1 = i1 - N_GROUPS - g_idx * PER_GROUP
    loc2 = i2 - N_GROUPS - g_idx * PER_GROUP
    a = jnp.minimum(loc1, loc2)
    b = jnp.maximum(loc1, loc2)
    pair = jnp.where(a == 0, 0, jnp.where(a == 1, 3, 5)) + (b - a - 1)
    bucket = g_idx * N_PAIRS + pair
    gate_a = jnp.where(loc1 < loc2, gate1, gate2)
    gate_b = jnp.where(loc1 < loc2, gate2, gate1)

    @pl.when(pl.program_id(0) == 0)
    def _():
        cnt_ref[...] = jnp.zeros(cnt_ref.shape, F32)

    in_bucket = lane == bucket
    earlier = (lax.broadcasted_iota(jnp.int32, (tm, tm), 1)
               < lax.broadcasted_iota(jnp.int32, (tm, tm), 0)).astype(BF16)
    before = jnp.dot(earlier, in_bucket.astype(BF16), preferred_element_type=F32) + cnt_ref[...]
    rank = jnp.sum(jnp.where(in_bucket, before, 0.0), axis=-1, keepdims=True)
    cnt_ref[...] += jnp.sum(in_bucket.astype(F32), axis=0, keepdims=True)
    counts_ref[...] = cnt_ref[...]

    meta = jnp.where(lane == META_GATE_A, gate_a,
                     jnp.where(lane == META_GATE_B, gate_b,
                               jnp.where(lane == META_BUCKET, bucket.astype(F32),
                                         jnp.where(lane == META_RANK, rank, 0.0))))
    xm_ref[pl.ds(X_ROWS, tm, stride=XM_ROWS), :] = meta
    meta_t_ref[...] = meta.T[0:META_ROWS, :]


def _out_proj(x2, yc, yp, ya, w_bf, g, rw_hi_lo, rb):
    token_major = x2.shape[1] == LANES
    t = x2.shape[0] // X_ROWS if token_major else x2.shape[0]
    tm = TM_OUT
    row = lambda i: (i, 0)
    fixed = lambda i: (0, 0)
    x_spec = pl.BlockSpec((tm * X_ROWS, LANES) if token_major else (tm, D_MODEL), row)
    return pl.pallas_call(
        _out_proj_kernel,
        grid=(t // tm,),
        in_specs=[x_spec,
                  pl.BlockSpec((tm, C_CONV), row),
                  pl.BlockSpec((tm, C_POOL), row),
                  pl.BlockSpec((tm, C_ATTN), row),
                  pl.BlockSpec((D_MODEL, D_MODEL), fixed),
                  pl.BlockSpec((1, D_MODEL), fixed),
                  pl.BlockSpec((D_MODEL, 2 * LANES), fixed),
                  pl.BlockSpec((1, LANES), fixed)],
        out_specs=[pl.BlockSpec((tm * XM_ROWS, LANES), row),
                   pl.BlockSpec((META_ROWS, tm), row),
                   pl.BlockSpec((1, LANES), fixed)],
        out_shape=[jax.ShapeDtypeStruct((t * XM_ROWS, LANES), F32),
                   jax.ShapeDtypeStruct((t // tm * META_ROWS, tm), F32),
                   jax.ShapeDtypeStruct((1, LANES), F32)],
        scratch_shapes=[pltpu.VMEM((1, LANES), F32)],
        compiler_params=_params(("arbitrary",)),
        name="out_proj_router",
    )(x2, yc, yp, ya, w_bf, g, rw_hi_lo, rb)


def _for_chunk_rows(dest_ref, chunk, start_copy, wait_copy):
    base = pl.program_id(0) * chunk

    def body(blk, carry):
        rows = [blk * DMA_UNROLL + u for u in range(DMA_UNROLL)]
        slots = [dest_ref[base + r] for r in rows]
        for u, (r, slot) in enumerate(zip(rows, slots)):
            start_copy(r, slot, u % 2)
        return carry

    lax.fori_loop(0, chunk // DMA_UNROLL, body, 0)
    for piece in range(chunk // COPY_WAIT):
        wait_copy(piece)


def _dispatch_kernel(dest_ref, x_ref, out_hbm, sem):
    def start_copy(r, slot, priority):
        pltpu.make_async_copy(x_ref.at[pl.ds(r * XM_ROWS, XM_ROWS)],
                              out_hbm.at[pl.ds(slot * XM_ROWS, XM_ROWS)],
                              sem.at[0]).start(priority=priority)

    def wait_copy(piece):
        n = COPY_WAIT * XM_ROWS
        pltpu.make_async_copy(x_ref.at[pl.ds(piece * n, n)], out_hbm.at[pl.ds(0, n)],
                              sem.at[0]).wait()

    _for_chunk_rows(dest_ref, DISPATCH_CHUNK, start_copy, wait_copy)


def _dispatch(xm, dest):
    n_tokens = xm.shape[0] // XM_ROWS
    n_rows = dest.shape[0]
    chunk = DISPATCH_CHUNK
    assert n_tokens % chunk == 0 and n_rows % chunk == 0
    last = n_tokens // chunk - 1
    grid_spec = pltpu.PrefetchScalarGridSpec(
        num_scalar_prefetch=1,
        grid=(n_rows // chunk,),
        in_specs=[pl.BlockSpec((chunk * XM_ROWS, LANES),
                               lambda i, dest: (jnp.minimum(i, last), 0))],
        out_specs=pl.BlockSpec(memory_space=pl.ANY),
        scratch_shapes=[pltpu.SemaphoreType.DMA((1,))])
    return pl.pallas_call(
        _dispatch_kernel,
        grid_spec=grid_spec,
        out_shape=jax.ShapeDtypeStruct((n_rows * XM_ROWS, LANES), F32),
        compiler_params=_params(("arbitrary",)),
        name="moe_dispatch",
    )(dest, xm)


def _combine_kernel(dest_ref, ys_hbm, o_ref, *scratch, to_rows):
    buf = scratch[0] if to_rows else o_ref
    sem = scratch[-1]

    def start_copy(r, slot, priority):
        pltpu.make_async_copy(ys_hbm.at[pl.ds(slot * X_ROWS, X_ROWS)],
                              buf.at[pl.ds(r * X_ROWS, X_ROWS)],
                              sem.at[0]).start(priority=priority)

    def wait_copy(piece):
        n = COPY_WAIT * X_ROWS
        pltpu.make_async_copy(ys_hbm.at[pl.ds(0, n)], buf.at[pl.ds(piece * n, n)],
                              sem.at[0]).wait()

    _for_chunk_rows(dest_ref, COPY_CHUNK, start_copy, wait_copy)
    if to_rows:
        o_ref[...] = _load_tokens(buf)


def _combine(ys, dest, n_tokens, to_rows):
    assert n_tokens % COPY_CHUNK == 0
    chunk_rows = COPY_CHUNK * X_ROWS
    if to_rows:
        out_spec = pl.BlockSpec((COPY_CHUNK, D_MODEL), lambda i, dest: (i, 0))
        out_shape = jax.ShapeDtypeStruct((n_tokens, D_MODEL), F32)
        scratch = [pltpu.VMEM((chunk_rows, LANES), F32)]
    else:
        out_spec = pl.BlockSpec((chunk_rows, LANES), lambda i, dest: (i, 0))
        out_shape = jax.ShapeDtypeStruct((n_tokens * X_ROWS, LANES), F32)
        scratch = []
    grid_spec = pltpu.PrefetchScalarGridSpec(
        num_scalar_prefetch=1,
        grid=(n_tokens // COPY_CHUNK,),
        in_specs=[pl.BlockSpec(memory_space=pl.ANY)],
        out_specs=out_spec,
        scratch_shapes=scratch + [pltpu.SemaphoreType.DMA((1,))])
    return pl.pallas_call(
        functools.partial(_combine_kernel, to_rows=to_rows),
        grid_spec=grid_spec,
        out_shape=out_shape,
        compiler_params=_params(("arbitrary",)),
        name="moe_combine",
    )(dest, ys)


def _moe_kernel(te1_ref, te2_ref, nv_ref, used_ref,
                xs_ref, g_ref, wg1_ref, wu1_ref, wd1_ref, wg2_ref, wu2_ref, wd2_ref, ys_ref):
    i = pl.program_id(0)

    @pl.when(nv_ref[i] > 0)
    def _():
        x = _load_tokens(xs_ref, XM_ROWS)
        meta = xs_ref[pl.ds(X_ROWS, TM_MOE, stride=XM_ROWS), :]
        hf = _rms(x, g_ref[...]).astype(BF16)

        def expert(wg_ref, wu_ref, wd_ref, gate):
            a = jnp.dot(hf, wg_ref[0], preferred_element_type=F32)
            u = jnp.dot(hf, wu_ref[0], preferred_element_type=F32)
            act = a * jax.nn.sigmoid(a) * u * gate
            return jnp.dot(act.astype(BF16), wd_ref[0], preferred_element_type=F32)

        y = expert(wg1_ref, wu1_ref, wd1_ref, meta[:, META_GATE_A:META_GATE_A + 1])
        y = y + expert(wg2_ref, wu2_ref, wd2_ref, meta[:, META_GATE_B:META_GATE_B + 1])
        _store_tokens(ys_ref, x + y)

    @pl.when(nv_ref[i] == 0)
    def _():
        ys_ref[...] = jnp.zeros(ys_ref.shape, F32)


def _moe(xs, g, te1, te2, nv, used, wg_bf, wu_bf, wd_bf):
    tm = TM_MOE
    n_tiles = nv.shape[0]
    fixed = lambda i, te1, te2, nv, used: (0, 0)
    tile_in = lambda i, te1, te2, nv, used: (jnp.minimum(i, used[0] - 1), 0)
    tile_out = lambda i, te1, te2, nv, used: (i, 0)
    expert1 = lambda i, te1, te2, nv, used: (te1[i], 0, 0)
    expert2 = lambda i, te1, te2, nv, used: (te2[i], 0, 0)
    w_in = lambda index_map: pl.BlockSpec((1, D_MODEL, D_EXPERT), index_map)
    w_out = lambda index_map: pl.BlockSpec((1, D_EXPERT, D_MODEL), index_map)
    grid_spec = pltpu.PrefetchScalarGridSpec(
        num_scalar_prefetch=4,
        grid=(n_tiles,),
        in_specs=[pl.BlockSpec((tm * XM_ROWS, LANES), tile_in),
                  pl.BlockSpec((1, D_MODEL), fixed),
                  w_in(expert1), w_in(expert1), w_out(expert1),
                  w_in(expert2), w_in(expert2), w_out(expert2)],
        out_specs=pl.BlockSpec((tm * X_ROWS, LANES), tile_out))
    return pl.pallas_call(
        _moe_kernel,
        grid_spec=grid_spec,
        out_shape=jax.ShapeDtypeStruct((n_tiles * tm * X_ROWS, LANES), F32),
        compiler_params=_params(("arbitrary",)),
        name="moe",
    )(te1, te2, nv, used, xs, g, wg_bf, wu_bf, wd_bf, wg_bf, wu_bf, wd_bf)


def _route(meta_t, counts, t):
    tm = TM_MOE
    nb = N_GROUPS * N_PAIRS
    n_tiles = t // tm + nb
    meta = meta_t.reshape(t // TM_OUT, META_ROWS, TM_OUT)
    bucket = meta[:, META_BUCKET, :].reshape(t).astype(jnp.int32)
    rank = meta[:, META_RANK, :].reshape(t).astype(jnp.int32)
    counts = counts[0, :nb].astype(jnp.int32)
    ids = jnp.arange(nb, dtype=jnp.int32)

    tiles = (counts + tm - 1) // tm
    tile_end = jnp.cumsum(tiles)
    tile_start = tile_end - tiles
    used = tile_end[-1]
    row0 = tile_start * tm
    dest = rank + jnp.sum(jnp.where(bucket[:, None] == ids[None, :], row0[None, :], 0), axis=1)

    ti = jnp.arange(n_tiles, dtype=jnp.int32)
    tic = jnp.minimum(ti, used - 1)
    tb = jnp.sum((tile_end[None, :] <= tic[:, None]).astype(jnp.int32), axis=1)
    onehot_tb = tb[:, None] == ids[None, :]
    pick = lambda table: jnp.sum(jnp.where(onehot_tb, table[None, :], 0), axis=1)
    nv = jnp.where(ti < used, jnp.clip(pick(counts) - (tic - pick(tile_start)) * tm, 0, tm), 0)
    pair_a = jnp.asarray([0, 0, 0, 1, 1, 2], jnp.int32)
    pair_b = jnp.asarray([1, 2, 3, 2, 3, 3], jnp.int32)
    te1 = pick((ids // N_PAIRS) * PER_GROUP + pair_a[ids % N_PAIRS])
    te2 = pick((ids // N_PAIRS) * PER_GROUP + pair_b[ids % N_PAIRS])

    pads = tiles * tm - counts
    pad_end = jnp.cumsum(pads)
    k = jnp.arange(nb * tm, dtype=jnp.int32)
    kb = jnp.minimum(jnp.sum((pad_end[None, :] <= k[:, None]).astype(jnp.int32), axis=1), nb - 1)
    onehot_kb = kb[:, None] == ids[None, :]
    pick_k = lambda table: jnp.sum(jnp.where(onehot_kb, table[None, :], 0), axis=1)
    in_pad = pick_k(row0 + counts) + (k - pick_k(pad_end - pads))
    free_rows = jnp.where(k < pad_end[-1], in_pad, used * tm + (k - pad_end[-1]))
    i32 = lambda a: a.astype(jnp.int32)
    return (i32(jnp.concatenate([dest, free_rows])), i32(te1), i32(te2), i32(nv),
            i32(used).reshape(1))


def _block_diag(w):
    n = w.shape[0]
    eye = jnp.eye(n, dtype=w.dtype)
    return jnp.einsum("gcd,gh->gchd", w, eye).reshape(n * POOL_GROUP, n * POOL_GROUP)


def kernel(x, attn_norm_g, w_in, conv_w, conv_b, conv_ln_g, conv_ln_b, conv_pw_w, conv_pw_b,
           pool_w, pool_scale, q_norm_g, k_norm_g, lambda_q1, lambda_k1, lambda_q2, lambda_k2,
           attn_sub_norm_g, w_out, ffn_norm_g, router_g_w, router_g_b, router_e_w, router_e_b,
           w_gate, w_up, w_down):
    bsz, s, d = x.shape
    depth = w_in.shape[0]
    t = bsz * s
    qaug, kaug = _alibi_aug(s)
    row = lambda a: a.reshape(1, -1)
    all_experts = lambda w: w.astype(BF16).reshape((depth * N_EXPERTS,) + w.shape[2:])
    wg_all, wu_all, wd_all = all_experts(w_gate), all_experts(w_up), all_experts(w_down)

    x2 = x.reshape(t, d)
    for l in range(depth):
        lam_init = 0.8 - 0.6 * math.exp(-0.3 * l)
        y_conv, y_pool, q, k, v = _in_proj(
            x2, s, row(attn_norm_g[l]), w_in[l].astype(BF16),
            row(jnp.tile(q_norm_g[l], 2) * (HEAD_DIM ** 0.5 * HEAD_DIM ** -0.5 * math.log2(math.e))),
            row(jnp.tile(k_norm_g[l], 2) * HEAD_DIM ** 0.5),
            conv_w[l], row(conv_b[l]), row(conv_ln_g[l]), row(conv_ln_b[l]),
            conv_pw_w[l].astype(BF16), row(conv_pw_b[l]),
            _block_diag(pool_w[l]).astype(BF16), row(pool_scale[l]))
        y_attn = _attention(q.reshape(bsz, s, -1), k.reshape(bsz, s, -1), v.reshape(bsz, s, -1),
                            qaug, kaug, row(lambda_q1[l]), row(lambda_k1[l]), row(lambda_q2[l]),
                            row(lambda_k2[l]), row(attn_sub_norm_g[l]), lam_init)

        pad = LANES - N_GROUPS - N_EXPERTS
        rw = jnp.pad(jnp.concatenate([router_g_w[l], router_e_w[l]], axis=1), ((0, 0), (0, pad)))
        rb = jnp.pad(jnp.concatenate([router_g_b[l], router_e_b[l]]), (0, pad))
        rw_hi = rw.astype(BF16)
        rw_lo = (rw - rw_hi.astype(F32)).astype(BF16)
        rw_hi_lo = jnp.concatenate([rw_hi, rw_lo], axis=1)
        xm, meta_t, counts = _out_proj(
            x2, y_conv, y_pool, y_attn.reshape(t, -1),
            w_out[l].astype(BF16), row(ffn_norm_g[l]), rw_hi_lo, row(rb))

        dest, te1, te2, nv, used = _route(meta_t, counts, t)
        xs = _dispatch(xm, dest)
        ys = _moe(xs, row(ffn_norm_g[l]), te1 + l * N_EXPERTS, te2 + l * N_EXPERTS, nv, used,
                  wg_all, wu_all, wd_all)
        x2 = _combine(ys, dest, t, to_rows=l == depth - 1)
    return x2.reshape(bsz, s, d)
```

```python
import functools
import math

import jax
import jax.numpy as jnp
import numpy as np
from jax import lax
from jax.experimental import pallas as pl
from jax.experimental.pallas import tpu as pltpu

D_MODEL = 1024
C_CONV = 256
C_POOL = 256
C_ATTN = 512
N_HEADS = 4
HEAD_DIM = 64
V_DIM = 2 * HEAD_DIM
CONV_WIDTH = 31
POOL_WINDOWS = (2, 4, 8, 16)
POOL_GROUP = C_POOL // len(POOL_WINDOWS)
N_IN = 2 * C_CONV + C_POOL + 3 * C_ATTN
N_GROUPS = 4
PER_GROUP = 4
N_EXPERTS = N_GROUPS * PER_GROUP
D_EXPERT = 256
EPS = 1e-6

LANES = 128
SUBLANES = 8
VMEM_LIMIT = 48 * 1024 * 1024

TM_PROJ = 512
TM_OUT = 512
HALO = 32
TQ = 512
TK = 256
TM_MOE = 512
N_PAIRS = 6
X_ROWS = D_MODEL // LANES
XM_ROWS = X_ROWS + 1
META_GATE_A, META_GATE_B, META_BUCKET, META_RANK = 0, 1, 2, 3
META_ROWS = 8
DMA_UNROLL = 8
DISPATCH_CHUNK = 4096
COPY_CHUNK = 2048
COPY_WAIT = 512

F32 = jnp.float32
BF16 = jnp.bfloat16


def _params(sem):
    return pltpu.CompilerParams(dimension_semantics=sem, vmem_limit_bytes=VMEM_LIMIT)


def _rms(x, g):
    return x * lax.rsqrt(jnp.mean(x * x, axis=-1, keepdims=True) + EPS) * g


def _load_tokens(x_ref, rows_per_token=X_ROWS):
    if x_ref.shape[-1] != LANES:
        return x_ref[...]
    tm = x_ref.shape[0] // rows_per_token
    return jnp.concatenate(
        [x_ref[pl.ds(c, tm, stride=rows_per_token), :] for c in range(X_ROWS)], axis=1)


def _store_tokens(o_ref, x, rows_per_token=X_ROWS):
    tm = x.shape[0]
    for c in range(X_ROWS):
        o_ref[pl.ds(c, tm, stride=rows_per_token), :] = x[:, c * LANES:(c + 1) * LANES]


def _in_proj_kernel(x_ref, g_ref, w_ref, qg_ref, kg_ref,
                    cw_ref, cb_ref, lng_ref, lnb_ref, pw_ref, pwb_ref, plw_ref, pls_ref,
                    yc_ref, yp_ref, q_ref, k_ref, v_ref, cext_ref, pext_ref, *, tiles_per_seq):
    ts = yc_ref.shape[0]
    seq_tile = pl.program_id(0) % tiles_per_seq
    h = _rms(_load_tokens(x_ref), g_ref[...]).astype(BF16)

    def proj(c0, width):
        return jnp.dot(h, w_ref[:, c0:c0 + width], preferred_element_type=F32)

    @pl.when(seq_tile == 0)
    def _():
        cext_ref[0, 0:HALO, :] = jnp.zeros((HALO, C_CONV), F32)
        pext_ref[0, 0:HALO, :] = jnp.zeros((HALO, C_POOL), F32)

    z = proj(0, 2 * C_CONV)
    cext_ref[0, HALO:, :] = z[:, :C_CONV] * jax.nn.sigmoid(z[:, C_CONV:])
    pext_ref[0, HALO:, :] = proj(2 * C_CONV, C_POOL)

    lane = lax.broadcasted_iota(jnp.int32, (1, LANES), 1)
    first = lane < HEAD_DIM

    def qk_norm(z, gain_ref, out_ref):
        gain = gain_ref[...]
        for hd in range(N_HEADS):
            blk = z[:, hd * LANES:(hd + 1) * LANES]
            sq = blk * blk
            s_all = jnp.sum(sq, axis=-1, keepdims=True)
            s_lo = jnp.sum(jnp.where(first, sq, 0.0), axis=-1, keepdims=True)
            r_lo = lax.rsqrt(s_lo + HEAD_DIM * EPS)
            r_hi = lax.rsqrt((s_all - s_lo) + HEAD_DIM * EPS)
            r = jnp.where(first, r_lo, r_hi)
            out_ref[:, hd * LANES:(hd + 1) * LANES] = (blk * r * gain).astype(BF16)

    c_q = 2 * C_CONV + C_POOL
    qk_norm(proj(c_q, C_ATTN), qg_ref, q_ref)
    qk_norm(proj(c_q + C_ATTN, C_ATTN), kg_ref, k_ref)
    v_ref[...] = proj(c_q + 2 * C_ATTN, C_ATTN).astype(BF16)

    _conv_group(cext_ref, ts, cw_ref, cb_ref, lng_ref, lnb_ref, pw_ref, pwb_ref, yc_ref)
    _pool_group(pext_ref, ts, seq_tile * ts, plw_ref, pls_ref, yp_ref)

    cext_ref[0, 0:HALO, :] = cext_ref[0, ts:ts + HALO, :]
    pext_ref[0, 0:HALO, :] = pext_ref[0, ts:ts + HALO, :]


def _in_proj(x2, seq_len, g, w_bf, qg, kg, conv_w, conv_b, ln_g, ln_b, pw_bf, pw_b,
             pool_w_bf, pool_scale):
    token_major = x2.shape[1] == LANES
    t = x2.shape[0] // X_ROWS if token_major else x2.shape[0]
    tm = TM_PROJ
    assert seq_len % tm == 0
    row = lambda i: (i, 0)
    fixed = lambda i: (0, 0)
    full = lambda a: pl.BlockSpec(a.shape, fixed)
    x_spec = pl.BlockSpec((tm * X_ROWS, LANES) if token_major else (tm, D_MODEL), row)
    small = [g, w_bf, qg, kg, conv_w, conv_b, ln_g, ln_b, pw_bf, pw_b, pool_w_bf, pool_scale]
    widths = [C_CONV, C_POOL, C_ATTN, C_ATTN, C_ATTN]
    return pl.pallas_call(
        functools.partial(_in_proj_kernel, tiles_per_seq=seq_len // tm),
        grid=(t // tm,),
        in_specs=[x_spec] + [full(a) for a in small],
        out_specs=[pl.BlockSpec((tm, w), row) for w in widths],
        out_shape=[jax.ShapeDtypeStruct((t, w), BF16) for w in widths],
        scratch_shapes=[pltpu.VMEM((SUBLANES, tm + HALO, C_CONV), F32),
                        pltpu.VMEM((4, tm + HALO, C_POOL), F32)],
        compiler_params=_params(("arbitrary",)),
        name="in_proj_mix",
    )(x2, *small)


def _conv_group(ext_ref, ts, w_ref, b_ref, lng_ref, lnb_ref, pw_ref, pwb_ref, o_ref):
    n_shifted = ts + HALO - SUBLANES
    for s in range(1, SUBLANES):
        ext_ref[s, 0:n_shifted, :] = ext_ref[0, s:s + n_shifted, :]

    base = HALO - (CONV_WIDTH - 1)
    rows = 512
    for c in range(ts // rows):
        acc = jnp.broadcast_to(b_ref[...], (rows, C_CONV))
        for j in range(CONV_WIDTH):
            phase = (base + j) % SUBLANES
            r0 = c * rows + base + j - phase
            acc = acc + ext_ref[phase, r0:r0 + rows, :] * w_ref[j:j + 1, :]
        mu = jnp.mean(acc, axis=-1, keepdims=True)
        xc = acc - mu
        y = xc * lax.rsqrt(jnp.mean(xc * xc, axis=-1, keepdims=True) + EPS)
        y = y * lng_ref[...] + lnb_ref[...]
        y = y * jax.nn.sigmoid(y)
        out = jnp.dot(y.astype(BF16), pw_ref[...], preferred_element_type=F32) + pwb_ref[...]
        o_ref[c * rows:(c + 1) * rows, :] = out.astype(BF16)


def _pool_group(ext_ref, ts, pos0, w_ref, scale_ref, o_ref):
    n = ts + HALO
    assert POOL_WINDOWS == (2, 4, 8, 16) and HALO >= 4 * SUBLANES
    sums = []
    for k in range(1, 5):
        lo = SUBLANES * k
        shift = 1 << (k - 1)
        level = ext_ref[k - 1, lo:n, :] + ext_ref[k - 1, lo - shift:n - shift, :]
        sums.append(level[HALO - lo:, :])
        if k < 4:
            ext_ref[k, lo:n, :] = level

    lane = lax.broadcasted_iota(jnp.int32, (1, C_POOL), 1)
    group = lane // POOL_GROUP
    win = jnp.left_shift(2, group)
    acc = jnp.where(group == 0, sums[0],
                    jnp.where(group == 1, sums[1], jnp.where(group == 2, sums[2], sums[3])))
    pos = pos0 + lax.broadcasted_iota(jnp.int32, (ts, 1), 0)
    count = jnp.minimum(pos + 1, win).astype(F32)
    diff = acc / count - ext_ref[0, HALO:, :]
    out = jnp.dot(diff.astype(BF16), w_ref[...], preferred_element_type=F32) * scale_ref[...]
    o_ref[...] = out.astype(BF16)


def _attn_kernel(q_ref, k_ref, v_ref, qaug_ref, kaug_ref, lq1_ref, lk1_ref, lq2_ref, lk2_ref,
                 sg_ref, o_ref, kx_ref, q2_all, m_all, l_all, acc_all, *, lam_init):
    qi = pl.program_id(1)
    half = TQ // 2
    assert half == TK

    @pl.when(qi == 0)
    def _():
        for hd in range(N_HEADS):
            kx_ref[hd, :, 0:LANES] = k_ref[0, :, hd * LANES:(hd + 1) * LANES]
            kx_ref[hd, :, LANES:] = kaug_ref[hd]

    lam = (jnp.exp(jnp.sum(lq1_ref[...] * lk1_ref[...], axis=-1, keepdims=True))
           - jnp.exp(jnp.sum(lq2_ref[...] * lk2_ref[...], axis=-1, keepdims=True))
           + lam_init)
    first = lax.broadcasted_iota(jnp.int32, (1, LANES), 1) < HEAD_DIM
    row = lax.broadcasted_iota(jnp.int32, (2 * half, TK), 0)
    col = lax.broadcasted_iota(jnp.int32, (2 * half, TK), 1)
    tri = col <= (row & (half - 1))

    def step(hd, r0, nr, k0, width, masked, first_block):
        q2_ref, m_ref, l_ref, acc_ref = q2_all.at[hd], m_all.at[hd], l_all.at[hd], acc_all.at[hd]
        kv_rows = pl.ds(pl.multiple_of(k0, TK), width)
        kb = kx_ref[hd, kv_rows, :]
        vb = v_ref[0, kv_rows, hd * LANES:(hd + 1) * LANES]
        s = lax.dot_general(q2_ref[r0:r0 + nr, :], kb, (((1,), (1,)), ((), ())),
                            preferred_element_type=F32)
        if masked:
            s = jnp.where(tri, s, -jnp.inf)
        s_max = jnp.max(s, axis=-1, keepdims=True)
        if first_block:
            m_new = jnp.broadcast_to(s_max, (nr, LANES))
        else:
            m_prev = m_ref[r0:r0 + nr]
            m_new = jnp.maximum(m_prev, s_max)
            alpha = jnp.exp2(m_prev - m_new)
        p = jnp.exp2(s - jnp.concatenate([m_new] * (width // LANES), axis=1))
        l_new = jnp.sum(p, axis=-1, keepdims=True)
        acc_new = jnp.dot(p.astype(BF16), vb, preferred_element_type=F32)
        if first_block:
            l_ref[r0:r0 + nr] = jnp.broadcast_to(l_new, (nr, LANES))
            acc_ref[r0:r0 + nr] = acc_new
        else:
            l_ref[r0:r0 + nr] = alpha * l_ref[r0:r0 + nr] + l_new
            acc_ref[r0:r0 + nr] = alpha * acc_ref[r0:r0 + nr] + acc_new
        m_ref[r0:r0 + nr] = m_new

    for hd in range(N_HEADS):
        q2_ref = q2_all.at[hd]
        qb = q_ref[0, :, hd * LANES:(hd + 1) * LANES]
        zero = jnp.zeros_like(qb)
        q_maps = (jnp.where(first, qb, zero), jnp.where(first, zero, qb))
        qa = qaug_ref[hd]
        for part in range(2):
            rows = slice(part * half, (part + 1) * half)
            for m in range(2):
                dst = slice((2 * part + m) * half, (2 * part + m + 1) * half)
                q2_ref[dst, 0:LANES] = q_maps[m][rows]
                q2_ref[dst, LANES:] = qa[rows]

    for hd in range(N_HEADS):
        step(hd, 0, 2 * half, qi * TQ, TK, True, True)
        step(hd, 2 * half, 2 * half, qi * TQ, TK, False, True)
    for hd in range(N_HEADS):
        step(hd, 2 * half, 2 * half, qi * TQ + TK, TK, True, False)

    for n_before in range(1, k_ref.shape[1] // TQ):
        @pl.when(qi == n_before)
        def _(n_before=n_before):
            for hd in range(N_HEADS):
                step(hd, 0, 4 * half, 0, n_before * TQ, False, False)

    for hd in range(N_HEADS):
        l_ref, acc_ref = l_all.at[hd], acc_all.at[hd]
        for part in range(2):
            r = 2 * part * half
            o = (acc_ref[r:r + half] / l_ref[r:r + half]
                 - lam * (acc_ref[r + half:r + 2 * half] / l_ref[r + half:r + 2 * half]))
            o = _rms(o, sg_ref[...]) * (1.0 - lam_init)
            o_ref[0, part * half:(part + 1) * half, hd * LANES:(hd + 1) * LANES] = o.astype(BF16)


def _alibi_aug(s):
    start = 2.0 ** (-8.0 / N_HEADS)
    slopes = np.array([start ** (i + 1) for i in range(N_HEADS)], dtype=np.float32)
    c = math.log2(math.e) * slopes.astype(np.float64)[:, None] * np.arange(s)[None, :]
    pieces = []
    rem = c
    for _ in range(3):
        piece = rem.astype(jnp.bfloat16)
        pieces.append(piece)
        rem = rem - piece.astype(np.float64)
    one = np.ones_like(pieces[0])
    zero = np.zeros((N_HEADS, s, LANES - 6), dtype=jnp.bfloat16)
    kaug = np.concatenate([np.stack(pieces + [one] * 3, axis=-1), zero], axis=-1)
    qaug = np.concatenate([np.stack([one] * 3 + [-p for p in pieces], axis=-1), zero], axis=-1)
    return jnp.asarray(qaug), jnp.asarray(kaug)


def _attention(q, k, v, qaug, kaug, lq1, lk1, lq2, lk2, sub_g, lam_init):
    bsz, s, _ = q.shape
    nq = s // TQ
    fixed2 = lambda b_, i: (0, 0)
    seq = lambda b_, i: (b_, 0, 0)
    vec = pl.BlockSpec((1, HEAD_DIM), fixed2)
    return pl.pallas_call(
        functools.partial(_attn_kernel, lam_init=lam_init),
        grid=(bsz, nq),
        in_specs=[pl.BlockSpec((1, TQ, C_ATTN), lambda b_, i: (b_, i, 0)),
                  pl.BlockSpec((1, s, C_ATTN), seq),
                  pl.BlockSpec((1, s, C_ATTN), seq),
                  pl.BlockSpec((N_HEADS, TQ, LANES), lambda b_, i: (0, i, 0)),
                  pl.BlockSpec((N_HEADS, s, LANES), lambda b_, i: (0, 0, 0)),
                  vec, vec, vec, vec,
                  pl.BlockSpec((1, V_DIM), fixed2)],
        out_specs=pl.BlockSpec((1, TQ, C_ATTN), lambda b_, i: (b_, i, 0)),
        out_shape=jax.ShapeDtypeStruct((bsz, s, C_ATTN), BF16),
        scratch_shapes=[pltpu.VMEM((N_HEADS, s, 2 * LANES), BF16),
                        pltpu.VMEM((N_HEADS, 2 * TQ, 2 * LANES), BF16),
                        pltpu.VMEM((N_HEADS, 2 * TQ, LANES), F32),
                        pltpu.VMEM((N_HEADS, 2 * TQ, LANES), F32),
                        pltpu.VMEM((N_HEADS, 2 * TQ, V_DIM), F32)],
        compiler_params=_params(("parallel", "arbitrary")),
        name="diff_attention",
    )(q, k, v, qaug, kaug, lq1, lk1, lq2, lk2, sub_g)


def _split_bf16(a):
    hi = a.astype(BF16)
    lo = (a - hi.astype(F32)).astype(BF16)
    return hi, lo


def _out_proj_kernel(x_ref, yc_ref, yp_ref, ya_ref, w_ref, g_ref, rw_ref, rb_ref,
                     xm_ref, meta_t_ref, counts_ref, cnt_ref):
    @pl.when(pl.program_id(0) == 0)
    def _():
        cnt_ref[...] = jnp.zeros(cnt_ref.shape, F32)

    mix = jnp.concatenate([yc_ref[...], yp_ref[...], ya_ref[...]], axis=1)
    x = _load_tokens(x_ref) + jnp.dot(mix, w_ref[...], preferred_element_type=F32)
    tm = x.shape[0]
    _store_tokens(xm_ref, x, XM_ROWS)

    hf_hi, hf_lo = _split_bf16(_rms(x, g_ref[...]))

    both = jnp.dot(hf_hi, rw_ref[...], preferred_element_type=F32)
    logits = (both[:, 0:LANES] + both[:, LANES:]
              + jnp.dot(hf_lo, rw_ref[:, 0:LANES], preferred_element_type=F32)) + rb_ref[...]
    lane = lax.broadcasted_iota(jnp.int32, (tm, LANES), 1)
    lane_f = lane.astype(F32)
    neg = -jnp.inf

    def first_argmax(val, vmax):
        first = jnp.min(jnp.where(val == vmax, lane_f, float(LANES)), axis=-1, keepdims=True)
        return first.astype(jnp.int32)

    lg = jnp.where(lane < N_GROUPS, logits, neg)
    g_max = jnp.max(lg, axis=-1, keepdims=True)
    g_idx = first_argmax(lg, g_max)
    g_p = 1.0 / jnp.sum(jnp.exp(lg - g_max), axis=-1, keepdims=True)

    e_lane = lane - N_GROUPS
    in_group = (e_lane >= g_idx * PER_GROUP) & (e_lane < (g_idx + 1) * PER_GROUP)
    le = jnp.where(in_group, logits, neg)
    e_max = jnp.max(le, axis=-1, keepdims=True)
    pe = jnp.exp(le - e_max)
    pe = pe / jnp.sum(pe, axis=-1, keepdims=True)
    p1 = jnp.max(pe, axis=-1, keepdims=True)
    i1 = first_argmax(jnp.where(in_group, pe, neg), p1)
    pe2 = jnp.where(in_group & (lane != i1), pe, neg)
    p2 = jnp.max(pe2, axis=-1, keepdims=True)
    i2 = first_argmax(pe2, p2)
    denom = p1 + p2
    gate1 = g_p * (p1 / denom)
    gate2 = g_p * (p2 / denom)

    loc1 = i1 - N_GROUPS - g_idx * PER_GROUP
    loc2 = i2 - N_GROUPS - g_idx * PER_GROUP
    a = jnp.minimum(loc1, loc2)
    b = jnp.maximum(loc1, loc2)
    pair = jnp.where(a == 0, 0, jnp.where(a == 1, 3, 5)) + (b - a - 1)
    bucket = g_idx * N_PAIRS + pair
    gate_a = jnp.where(loc1 < loc2, gate1, gate2)
    gate_b = jnp.where(loc1 < loc2, gate2, gate1)

    in_bucket = lane == bucket
    earlier = (lax.broadcasted_iota(jnp.int32, (tm, tm), 1)
               < lax.broadcasted_iota(jnp.int32, (tm, tm), 0)).astype(BF16)
    before = jnp.dot(earlier, in_bucket.astype(BF16), preferred_element_type=F32) + cnt_ref[...]
    rank = jnp.sum(jnp.where(in_bucket, before, 0.0), axis=-1, keepdims=True)
    cnt_ref[...] += jnp.sum(in_bucket.astype(F32), axis=0, keepdims=True)
    counts_ref[...] = cnt_ref[...]

    meta = jnp.where(lane == META_GATE_A, gate_a,
                     jnp.where(lane == META_GATE_B, gate_b,
                               jnp.where(lane == META_BUCKET, bucket.astype(F32),
                                         jnp.where(lane == META_RANK, rank, 0.0))))
    xm_ref[pl.ds(X_ROWS, tm, stride=XM_ROWS), :] = meta
    meta_t_ref[...] = meta.T[0:META_ROWS, :]


def _out_proj(x2, yc, yp, ya, w_bf, g, rw_hi_lo, rb):
    token_major = x2.shape[1] == LANES
    t = x2.shape[0] // X_ROWS if token_major else x2.shape[0]
    tm = TM_OUT
    row = lambda i: (i, 0)
    fixed = lambda i: (0, 0)
    x_spec = pl.BlockSpec((tm * X_ROWS, LANES) if token_major else (tm, D_MODEL), row)
    return pl.pallas_call(
        _out_proj_kernel,
        grid=(t // tm,),
        in_specs=[x_spec,
                  pl.BlockSpec((tm, C_CONV), row),
                  pl.BlockSpec((tm, C_POOL), row),
                  pl.BlockSpec((tm, C_ATTN), row),
                  pl.BlockSpec((D_MODEL, D_MODEL), fixed),
                  pl.BlockSpec((1, D_MODEL), fixed),
                  pl.BlockSpec((D_MODEL, 2 * LANES), fixed),
                  pl.BlockSpec((1, LANES), fixed)],
        out_specs=[pl.BlockSpec((tm * XM_ROWS, LANES), row),
                   pl.BlockSpec((META_ROWS, tm), row),
                   pl.BlockSpec((1, LANES), fixed)],
        out_shape=[jax.ShapeDtypeStruct((t * XM_ROWS, LANES), F32),
                   jax.ShapeDtypeStruct((t // tm * META_ROWS, tm), F32),
                   jax.ShapeDtypeStruct((1, LANES), F32)],
        scratch_shapes=[pltpu.VMEM((1, LANES), F32)],
        compiler_params=_params(("arbitrary",)),
        name="out_proj_router",
    )(x2, yc, yp, ya, w_bf, g, rw_hi_lo, rb)


def _for_chunk_rows(dest_ref, chunk, start_copy, wait_copy):
    base = pl.program_id(0) * chunk

    def body(blk, carry):
        rows = [blk * DMA_UNROLL + u for u in range(DMA_UNROLL)]
        slots = [dest_ref[base + r] for r in rows]
        for u, (r, slot) in enumerate(zip(rows, slots)):
            start_copy(r, slot, u % 2)
        return carry

    lax.fori_loop(0, chunk // DMA_UNROLL, body, 0)
    for piece in range(chunk // COPY_WAIT):
        wait_copy(piece)


def _dispatch_kernel(dest_ref, x_ref, out_hbm, sem):
    def start_copy(r, slot, priority):
        pltpu.make_async_copy(x_ref.at[pl.ds(r * XM_ROWS, XM_ROWS)],
                              out_hbm.at[pl.ds(slot * XM_ROWS, XM_ROWS)],
                              sem.at[0]).start(priority=priority)

    def wait_copy(piece):
        n = COPY_WAIT * XM_ROWS
        pltpu.make_async_copy(x_ref.at[pl.ds(piece * n, n)], out_hbm.at[pl.ds(0, n)],
                              sem.at[0]).wait()

    _for_chunk_rows(dest_ref, DISPATCH_CHUNK, start_copy, wait_copy)


def _dispatch(xm, dest):
    n_tokens = xm.shape[0] // XM_ROWS
    n_rows = dest.shape[0]
    chunk = DISPATCH_CHUNK
    assert n_tokens % chunk == 0 and n_rows % chunk == 0
    last = n_tokens // chunk - 1
    grid_spec = pltpu.PrefetchScalarGridSpec(
        num_scalar_prefetch=1,
        grid=(n_rows // chunk,),
        in_specs=[pl.BlockSpec((chunk * XM_ROWS, LANES),
                               lambda i, dest: (jnp.minimum(i, last), 0))],
        out_specs=pl.BlockSpec(memory_space=pl.ANY),
        scratch_shapes=[pltpu.SemaphoreType.DMA((1,))])
    return pl.pallas_call(
        _dispatch_kernel,
        grid_spec=grid_spec,
        out_shape=jax.ShapeDtypeStruct((n_rows * XM_ROWS, LANES), F32),
        compiler_params=_params(("arbitrary",)),
        name="moe_dispatch",
    )(dest, xm)


def _combine_kernel(dest_ref, ys_hbm, o_ref, *scratch, to_rows):
    buf = scratch[0] if to_rows else o_ref
    sem = scratch[-1]

    def start_copy(r, slot, priority):
        pltpu.make_async_copy(ys_hbm.at[pl.ds(slot * X_ROWS, X_ROWS)],
                              buf.at[pl.ds(r * X_ROWS, X_ROWS)],
                              sem.at[0]).start(priority=priority)

    def wait_copy(piece):
        n = COPY_WAIT * X_ROWS
        pltpu.make_async_copy(ys_hbm.at[pl.ds(0, n)], buf.at[pl.ds(piece * n, n)],
                              sem.at[0]).wait()

    _for_chunk_rows(dest_ref, COPY_CHUNK, start_copy, wait_copy)
    if to_rows:
        o_ref[...] = _load_tokens(buf)


def _combine(ys, dest, n_tokens, to_rows):
    assert n_tokens % COPY_CHUNK == 0
    chunk_rows = COPY_CHUNK * X_ROWS
    if to_rows:
        out_spec = pl.BlockSpec((COPY_CHUNK, D_MODEL), lambda i, dest: (i, 0))
        out_shape = jax.ShapeDtypeStruct((n_tokens, D_MODEL), F32)
        scratch = [pltpu.VMEM((chunk_rows, LANES), F32)]
    else:
        out_spec = pl.BlockSpec((chunk_rows, LANES), lambda i, dest: (i, 0))
        out_shape = jax.ShapeDtypeStruct((n_tokens * X_ROWS, LANES), F32)
        scratch = []
    grid_spec = pltpu.PrefetchScalarGridSpec(
        num_scalar_prefetch=1,
        grid=(n_tokens // COPY_CHUNK,),
        in_specs=[pl.BlockSpec(memory_space=pl.ANY)],
        out_specs=out_spec,
        scratch_shapes=scratch + [pltpu.SemaphoreType.DMA((1,))])
    return pl.pallas_call(
        functools.partial(_combine_kernel, to_rows=to_rows),
        grid_spec=grid_spec,
        out_shape=out_shape,
        compiler_params=_params(("arbitrary",)),
        name="moe_combine",
    )(dest, ys)


def _moe_kernel(te1_ref, te2_ref, nv_ref, used_ref,
                xs_ref, g_ref, wg1_ref, wu1_ref, wd1_ref, wg2_ref, wu2_ref, wd2_ref, ys_ref):
    i = pl.program_id(0)

    @pl.when(nv_ref[i] > 0)
    def _():
        x = _load_tokens(xs_ref, XM_ROWS)
        meta = xs_ref[pl.ds(X_ROWS, TM_MOE, stride=XM_ROWS), :]
        hf = _rms(x, g_ref[...]).astype(BF16)

        def expert(wg_ref, wu_ref, wd_ref, gate):
            a = jnp.dot(hf, wg_ref[0], preferred_element_type=F32)
            u = jnp.dot(hf, wu_ref[0], preferred_element_type=F32)
            act = a * jax.nn.sigmoid(a) * u * gate
            return jnp.dot(act.astype(BF16), wd_ref[0], preferred_element_type=F32)

        y = expert(wg1_ref, wu1_ref, wd1_ref, meta[:, META_GATE_A:META_GATE_A + 1])
        y = y + expert(wg2_ref, wu2_ref, wd2_ref, meta[:, META_GATE_B:META_GATE_B + 1])
        _store_tokens(ys_ref, x + y)

    @pl.when(nv_ref[i] == 0)
    def _():
        ys_ref[...] = jnp.zeros(ys_ref.shape, F32)


def _moe(xs, g, te1, te2, nv, used, wg_bf, wu_bf, wd_bf):
    tm = TM_MOE
    n_tiles = nv.shape[0]
    fixed = lambda i, te1, te2, nv, used: (0, 0)
    tile_in = lambda i, te1, te2, nv, used: (jnp.minimum(i, used[0] - 1), 0)
    tile_out = lambda i, te1, te2, nv, used: (i, 0)
    expert1 = lambda i, te1, te2, nv, used: (te1[i], 0, 0)
    expert2 = lambda i, te1, te2, nv, used: (te2[i], 0, 0)
    w_in = lambda index_map: pl.BlockSpec((1, D_MODEL, D_EXPERT), index_map)
    w_out = lambda index_map: pl.BlockSpec((1, D_EXPERT, D_MODEL), index_map)
    grid_spec = pltpu.PrefetchScalarGridSpec(
        num_scalar_prefetch=4,
        grid=(n_tiles,),
        in_specs=[pl.BlockSpec((tm * XM_ROWS, LANES), tile_in),
                  pl.BlockSpec((1, D_MODEL), fixed),
                  w_in(expert1), w_in(expert1), w_out(expert1),
                  w_in(expert2), w_in(expert2), w_out(expert2)],
        out_specs=pl.BlockSpec((tm * X_ROWS, LANES), tile_out))
    return pl.pallas_call(
        _moe_kernel,
        grid_spec=grid_spec,
        out_shape=jax.ShapeDtypeStruct((n_tiles * tm * X_ROWS, LANES), F32),
        compiler_params=_params(("arbitrary",)),
        name="moe",
    )(te1, te2, nv, used, xs, g, wg_bf, wu_bf, wd_bf, wg_bf, wu_bf, wd_bf)


def _route(meta_t, counts, t):
    tm = TM_MOE
    nb = N_GROUPS * N_PAIRS
    n_tiles = t // tm + nb
    meta = meta_t.reshape(t // TM_OUT, META_ROWS, TM_OUT)
    bucket = meta[:, META_BUCKET, :].reshape(t).astype(jnp.int32)
    rank = meta[:, META_RANK, :].reshape(t).astype(jnp.int32)
    counts = counts[0, :nb].astype(jnp.int32)
    ids = jnp.arange(nb, dtype=jnp.int32)

    tiles = (counts + tm - 1) // tm
    tile_end = jnp.cumsum(tiles)
    tile_start = tile_end - tiles
    used = tile_end[-1]
    row0 = tile_start * tm
    dest = rank + jnp.sum(jnp.where(bucket[:, None] == ids[None, :], row0[None, :], 0), axis=1)

    ti = jnp.arange(n_tiles, dtype=jnp.int32)
    tic = jnp.minimum(ti, used - 1)
    tb = jnp.sum((tile_end[None, :] <= tic[:, None]).astype(jnp.int32), axis=1)
    onehot_tb = tb[:, None] == ids[None, :]
    pick = lambda table: jnp.sum(jnp.where(onehot_tb, table[None, :], 0), axis=1)
    nv = jnp.where(ti < used, jnp.clip(pick(counts) - (tic - pick(tile_start)) * tm, 0, tm), 0)
    pair_a = jnp.asarray([0, 0, 0, 1, 1, 2], jnp.int32)
    pair_b = jnp.asarray([1, 2, 3, 2, 3, 3], jnp.int32)
    te1 = pick((ids // N_PAIRS) * PER_GROUP + pair_a[ids % N_PAIRS])
    te2 = pick((ids // N_PAIRS) * PER_GROUP + pair_b[ids % N_PAIRS])

    pads = tiles * tm - counts
    pad_end = jnp.cumsum(pads)
    k = jnp.arange(nb * tm, dtype=jnp.int32)
    kb = jnp.minimum(jnp.sum((pad_end[None, :] <= k[:, None]).astype(jnp.int32), axis=1), nb - 1)
    onehot_kb = kb[:, None] == ids[None, :]
    pick_k = lambda table: jnp.sum(jnp.where(onehot_kb, table[None, :], 0), axis=1)
    in_pad = pick_k(row0 + counts) + (k - pick_k(pad_end - pads))
    free_rows = jnp.where(k < pad_end[-1], in_pad, used * tm + (k - pad_end[-1]))
    i32 = lambda a: a.astype(jnp.int32)
    return (i32(jnp.concatenate([dest, free_rows])), i32(te1), i32(te2), i32(nv),
            i32(used).reshape(1))


def _block_diag(w):
    n = w.shape[0]
    eye = jnp.eye(n, dtype=w.dtype)
    return jnp.einsum("gcd,gh->gchd", w, eye).reshape(n * POOL_GROUP, n * POOL_GROUP)


def kernel(x, attn_norm_g, w_in, conv_w, conv_b, conv_ln_g, conv_ln_b, conv_pw_w, conv_pw_b,
           pool_w, pool_scale, q_norm_g, k_norm_g, lambda_q1, lambda_k1, lambda_q2, lambda_k2,
           attn_sub_norm_g, w_out, ffn_norm_g, router_g_w, router_g_b, router_e_w, router_e_b,
           w_gate, w_up, w_down):
    bsz, s, d = x.shape
    depth = w_in.shape[0]
    t = bsz * s
    qaug, kaug = _alibi_aug(s)
    row = lambda a: a.reshape(1, -1)
    all_experts = lambda w: w.astype(BF16).reshape((depth * N_EXPERTS,) + w.shape[2:])
    wg_all, wu_all, wd_all = all_experts(w_gate), all_experts(w_up), all_experts(w_down)

    x2 = x.reshape(t, d)
    for l in range(depth):
        lam_init = 0.8 - 0.6 * math.exp(-0.3 * l)
        y_conv, y_pool, q, k, v = _in_proj(
            x2, s, row(attn_norm_g[l]), w_in[l].astype(BF16),
            row(jnp.tile(q_norm_g[l], 2) * (HEAD_DIM ** 0.5 * HEAD_DIM ** -0.5 * math.log2(math.e))),
            row(jnp.tile(k_norm_g[l], 2) * HEAD_DIM ** 0.5),
            conv_w[l], row(conv_b[l]), row(conv_ln_g[l]), row(conv_ln_b[l]),
            conv_pw_w[l].astype(BF16), row(conv_pw_b[l]),
            _block_diag(pool_w[l]).astype(BF16), row(pool_scale[l]))
        y_attn = _attention(q.reshape(bsz, s, -1), k.reshape(bsz, s, -1), v.reshape(bsz, s, -1),
                            qaug, kaug, row(lambda_q1[l]), row(lambda_k1[l]), row(lambda_q2[l]),
                            row(lambda_k2[l]), row(attn_sub_norm_g[l]), lam_init)

        pad = LANES - N_GROUPS - N_EXPERTS
        rw = jnp.pad(jnp.concatenate([router_g_w[l], router_e_w[l]], axis=1), ((0, 0), (0, pad)))
        rb = jnp.pad(jnp.concatenate([router_g_b[l], router_e_b[l]]), (0, pad))
        rw_hi = rw.astype(BF16)
        rw_lo = (rw - rw_hi.astype(F32)).astype(BF16)
        rw_hi_lo = jnp.concatenate([rw_hi, rw_lo], axis=1)
        xm, meta_t, counts = _out_proj(
            x2, y_conv, y_pool, y_attn.reshape(t, -1),
            w_out[l].astype(BF16), row(ffn_norm_g[l]), rw_hi_lo, row(rb))

        dest, te1, te2, nv, used = _route(meta_t, counts, t)
        xs = _dispatch(xm, dest)
        ys = _moe(xs, row(ffn_norm_g[l]), te1 + l * N_EXPERTS, te2 + l * N_EXPERTS, nv, used,
                  wg_all, wu_all, wd_all)
        x2 = _combine(ys, dest, t, to_rows=l == depth - 1)
    return x2.reshape(bsz, s, d)
```

```python
import functools
import math

import jax
import jax.numpy as jnp
import numpy as np
from jax import lax
from jax.experimental import pallas as pl
from jax.experimental.pallas import tpu as pltpu

D_MODEL = 1024
C_CONV = 256
C_POOL = 256
C_ATTN = 512
N_HEADS = 4
HEAD_DIM = 64
V_DIM = 2 * HEAD_DIM
CONV_WIDTH = 31
POOL_WINDOWS = (2, 4, 8, 16)
POOL_GROUP = C_POOL // len(POOL_WINDOWS)
N_IN = 2 * C_CONV + C_POOL + 3 * C_ATTN
N_GROUPS = 4
PER_GROUP = 4
N_EXPERTS = N_GROUPS * PER_GROUP
D_EXPERT = 256
EPS = 1e-6

LANES = 128
SUBLANES = 8
VMEM_LIMIT = 48 * 1024 * 1024

TM_PROJ = 512
TM_OUT = 512
HALO = 32
TQ = 512
TK = 256
TM_MOE = 512
MOE_RING = 3
N_PAIRS = 6
X_ROWS = D_MODEL // LANES
XM_ROWS = X_ROWS + 1
META_GATE_A, META_GATE_B, META_BUCKET, META_RANK = 0, 1, 2, 3
META_ROWS = 8
DMA_UNROLL = 8
DISPATCH_CHUNK = 4096
COPY_CHUNK = 2048
COPY_WAIT = 512

F32 = jnp.float32
BF16 = jnp.bfloat16


def _params(sem):
    return pltpu.CompilerParams(dimension_semantics=sem, vmem_limit_bytes=VMEM_LIMIT)


def _rms(x, g):
    return x * lax.rsqrt(jnp.mean(x * x, axis=-1, keepdims=True) + EPS) * g


def _load_tokens(x_ref, rows_per_token=X_ROWS):
    if x_ref.shape[-1] != LANES:
        return x_ref[...]
    tm = x_ref.shape[0] // rows_per_token
    return jnp.concatenate(
        [x_ref[pl.ds(c, tm, stride=rows_per_token), :] for c in range(X_ROWS)], axis=1)


def _store_tokens(o_ref, x, rows_per_token=X_ROWS):
    tm = x.shape[0]
    for c in range(X_ROWS):
        o_ref[pl.ds(c, tm, stride=rows_per_token), :] = x[:, c * LANES:(c + 1) * LANES]


def _in_proj_kernel(x_ref, g_ref, w_ref, qg_ref, kg_ref,
                    cw_ref, cb_ref, lng_ref, lnb_ref, pw_ref, pwb_ref, plw_ref, pls_ref,
                    yc_ref, yp_ref, q_ref, k_ref, v_ref, cext_ref, pext_ref, *, tiles_per_seq):
    ts = yc_ref.shape[0]
    seq_tile = pl.program_id(0) % tiles_per_seq
    h = _rms(_load_tokens(x_ref), g_ref[...]).astype(BF16)

    def proj(c0, width):
        return jnp.dot(h, w_ref[:, c0:c0 + width], preferred_element_type=F32)

    @pl.when(seq_tile == 0)
    def _():
        cext_ref[0, 0:HALO, :] = jnp.zeros((HALO, C_CONV), F32)
        pext_ref[0, 0:HALO, :] = jnp.zeros((HALO, C_POOL), F32)

    z = proj(0, 2 * C_CONV)
    cext_ref[0, HALO:, :] = z[:, :C_CONV] * jax.nn.sigmoid(z[:, C_CONV:])
    pext_ref[0, HALO:, :] = proj(2 * C_CONV, C_POOL)

    lane = lax.broadcasted_iota(jnp.int32, (1, LANES), 1)
    first = lane < HEAD_DIM

    def qk_norm(z, gain_ref, out_ref):
        gain = gain_ref[...]
        for hd in range(N_HEADS):
            blk = z[:, hd * LANES:(hd + 1) * LANES]
            sq = blk * blk
            s_all = jnp.sum(sq, axis=-1, keepdims=True)
            s_lo = jnp.sum(jnp.where(first, sq, 0.0), axis=-1, keepdims=True)
            r_lo = lax.rsqrt(s_lo + HEAD_DIM * EPS)
            r_hi = lax.rsqrt((s_all - s_lo) + HEAD_DIM * EPS)
            r = jnp.where(first, r_lo, r_hi)
            out_ref[:, hd * LANES:(hd + 1) * LANES] = (blk * r * gain).astype(BF16)

    c_q = 2 * C_CONV + C_POOL
    qk_norm(proj(c_q, C_ATTN), qg_ref, q_ref)
    qk_norm(proj(c_q + C_ATTN, C_ATTN), kg_ref, k_ref)
    v_ref[...] = proj(c_q + 2 * C_ATTN, C_ATTN).astype(BF16)

    _conv_group(cext_ref, ts, cw_ref, cb_ref, lng_ref, lnb_ref, pw_ref, pwb_ref, yc_ref)
    _pool_group(pext_ref, ts, seq_tile * ts, plw_ref, pls_ref, yp_ref)

    cext_ref[0, 0:HALO, :] = cext_ref[0, ts:ts + HALO, :]
    pext_ref[0, 0:HALO, :] = pext_ref[0, ts:ts + HALO, :]


def _in_proj(x2, seq_len, g, w_bf, qg, kg, conv_w, conv_b, ln_g, ln_b, pw_bf, pw_b,
             pool_w_bf, pool_scale):
    token_major = x2.shape[1] == LANES
    t = x2.shape[0] // X_ROWS if token_major else x2.shape[0]
    tm = TM_PROJ
    assert seq_len % tm == 0
    row = lambda i: (i, 0)
    fixed = lambda i: (0, 0)
    full = lambda a: pl.BlockSpec(a.shape, fixed)
    x_spec = pl.BlockSpec((tm * X_ROWS, LANES) if token_major else (tm, D_MODEL), row)
    small = [g, w_bf, qg, kg, conv_w, conv_b, ln_g, ln_b, pw_bf, pw_b, pool_w_bf, pool_scale]
    widths = [C_CONV, C_POOL, C_ATTN, C_ATTN, C_ATTN]
    return pl.pallas_call(
        functools.partial(_in_proj_kernel, tiles_per_seq=seq_len // tm),
        grid=(t // tm,),
        in_specs=[x_spec] + [full(a) for a in small],
        out_specs=[pl.BlockSpec((tm, w), row) for w in widths],
        out_shape=[jax.ShapeDtypeStruct((t, w), BF16) for w in widths],
        scratch_shapes=[pltpu.VMEM((SUBLANES, tm + HALO, C_CONV), F32),
                        pltpu.VMEM((4, tm + HALO, C_POOL), F32)],
        compiler_params=_params(("arbitrary",)),
        name="in_proj_mix",
    )(x2, *small)


def _conv_group(ext_ref, ts, w_ref, b_ref, lng_ref, lnb_ref, pw_ref, pwb_ref, o_ref):
    n_shifted = ts + HALO - SUBLANES
    for s in range(1, SUBLANES):
        ext_ref[s, 0:n_shifted, :] = ext_ref[0, s:s + n_shifted, :]

    base = HALO - (CONV_WIDTH - 1)
    rows = 128
    for c in range(ts // rows):
        acc = jnp.broadcast_to(b_ref[...], (rows, C_CONV))
        for j in range(CONV_WIDTH):
            phase = (base + j) % SUBLANES
            r0 = c * rows + base + j - phase
            acc = acc + ext_ref[phase, r0:r0 + rows, :] * w_ref[j:j + 1, :]
        mu = jnp.mean(acc, axis=-1, keepdims=True)
        xc = acc - mu
        y = xc * lax.rsqrt(jnp.mean(xc * xc, axis=-1, keepdims=True) + EPS)
        y = y * lng_ref[...] + lnb_ref[...]
        y = y * jax.nn.sigmoid(y)
        out = jnp.dot(y.astype(BF16), pw_ref[...], preferred_element_type=F32) + pwb_ref[...]
        o_ref[c * rows:(c + 1) * rows, :] = out.astype(BF16)


def _pool_group(ext_ref, ts, pos0, w_ref, scale_ref, o_ref):
    n = ts + HALO
    assert POOL_WINDOWS == (2, 4, 8, 16) and HALO >= 4 * SUBLANES
    sums = []
    for k in range(1, 5):
        lo = SUBLANES * k
        shift = 1 << (k - 1)
        level = ext_ref[k - 1, lo:n, :] + ext_ref[k - 1, lo - shift:n - shift, :]
        sums.append(level[HALO - lo:, :])
        if k < 4:
            ext_ref[k, lo:n, :] = level

    lane = lax.broadcasted_iota(jnp.int32, (1, C_POOL), 1)
    group = lane // POOL_GROUP
    win = jnp.left_shift(2, group)
    acc = jnp.where(group == 0, sums[0],
                    jnp.where(group == 1, sums[1], jnp.where(group == 2, sums[2], sums[3])))
    pos = pos0 + lax.broadcasted_iota(jnp.int32, (ts, 1), 0)
    count = jnp.minimum(pos + 1, win).astype(F32)
    diff = acc / count - ext_ref[0, HALO:, :]
    out = jnp.dot(diff.astype(BF16), w_ref[...], preferred_element_type=F32) * scale_ref[...]
    o_ref[...] = out.astype(BF16)


def _attn_kernel(q_ref, k_ref, v_ref, qaug_ref, kaug_ref, lq1_ref, lk1_ref, lq2_ref, lk2_ref,
                 sg_ref, o_ref, kx_ref, q2_all, m_all, l_all, acc_all, *, lam_init):
    qi = pl.program_id(1)
    half = TQ // 2
    assert half == TK

    @pl.when(qi == 0)
    def _():
        for hd in range(N_HEADS):
            kx_ref[hd, :, 0:LANES] = k_ref[0, :, hd * LANES:(hd + 1) * LANES]
            kx_ref[hd, :, LANES:] = kaug_ref[hd]

    lam = (jnp.exp(jnp.sum(lq1_ref[...] * lk1_ref[...], axis=-1, keepdims=True))
           - jnp.exp(jnp.sum(lq2_ref[...] * lk2_ref[...], axis=-1, keepdims=True))
           + lam_init)
    first = lax.broadcasted_iota(jnp.int32, (1, LANES), 1) < HEAD_DIM
    row = lax.broadcasted_iota(jnp.int32, (2 * half, TK), 0)
    col = lax.broadcasted_iota(jnp.int32, (2 * half, TK), 1)
    tri = col <= (row & (half - 1))

    def step(hd, r0, nr, k0, width, masked, first_block):
        q2_ref, m_ref, l_ref, acc_ref = q2_all.at[hd], m_all.at[hd], l_all.at[hd], acc_all.at[hd]
        kv_rows = pl.ds(pl.multiple_of(k0, TK), width)
        kb = kx_ref[hd, kv_rows, :]
        vb = v_ref[0, kv_rows, hd * LANES:(hd + 1) * LANES]
        s = lax.dot_general(q2_ref[r0:r0 + nr, :], kb, (((1,), (1,)), ((), ())),
                            preferred_element_type=F32)
        if masked:
            s = jnp.where(tri, s, -jnp.inf)
        s_max = jnp.max(s, axis=-1, keepdims=True)
        if first_block:
            m_new = jnp.broadcast_to(s_max, (nr, LANES))
        else:
            m_prev = m_ref[r0:r0 + nr]
            m_new = jnp.maximum(m_prev, s_max)
            alpha = jnp.exp2(m_prev - m_new)
        p = jnp.exp2(s - jnp.concatenate([m_new] * (width // LANES), axis=1))
        l_new = jnp.sum(p, axis=-1, keepdims=True)
        acc_new = jnp.dot(p.astype(BF16), vb, preferred_element_type=F32)
        if first_block:
            l_ref[r0:r0 + nr] = jnp.broadcast_to(l_new, (nr, LANES))
            acc_ref[r0:r0 + nr] = acc_new
        else:
            l_ref[r0:r0 + nr] = alpha * l_ref[r0:r0 + nr] + l_new
            acc_ref[r0:r0 + nr] = alpha * acc_ref[r0:r0 + nr] + acc_new
        m_ref[r0:r0 + nr] = m_new

    for hd in range(N_HEADS):
        q2_ref = q2_all.at[hd]
        qb = q_ref[0, :, hd * LANES:(hd + 1) * LANES]
        zero = jnp.zeros_like(qb)
        q_maps = (jnp.where(first, qb, zero), jnp.where(first, zero, qb))
        qa = qaug_ref[hd]
        for part in range(2):
            rows = slice(part * half, (part + 1) * half)
            for m in range(2):
                dst = slice((2 * part + m) * half, (2 * part + m + 1) * half)
                q2_ref[dst, 0:LANES] = q_maps[m][rows]
                q2_ref[dst, LANES:] = qa[rows]

    for hd in range(N_HEADS):
        step(hd, 0, 2 * half, qi * TQ, TK, True, True)
        step(hd, 2 * half, 2 * half, qi * TQ, TK, False, True)
    for hd in range(N_HEADS):
        step(hd, 2 * half, 2 * half, qi * TQ + TK, TK, True, False)

    def body(j, carry):
        for hd in range(N_HEADS):
            step(hd, 0, 4 * half, 2 * j * TQ, 2 * TQ, False, False)
        return carry

    lax.fori_loop(0, qi // 2, body, 0)

    @pl.when(qi % 2 == 1)
    def _():
        for hd in range(N_HEADS):
            step(hd, 0, 4 * half, (qi - 1) * TQ, TQ, False, False)

    for hd in range(N_HEADS):
        l_ref, acc_ref = l_all.at[hd], acc_all.at[hd]
        for part in range(2):
            r = 2 * part * half
            o = (acc_ref[r:r + half] / l_ref[r:r + half]
                 - lam * (acc_ref[r + half:r + 2 * half] / l_ref[r + half:r + 2 * half]))
            o = _rms(o, sg_ref[...]) * (1.0 - lam_init)
            o_ref[0, part * half:(part + 1) * half, hd * LANES:(hd + 1) * LANES] = o.astype(BF16)


def _alibi_aug(s):
    start = 2.0 ** (-8.0 / N_HEADS)
    slopes = np.array([start ** (i + 1) for i in range(N_HEADS)], dtype=np.float32)
    c = math.log2(math.e) * slopes.astype(np.float64)[:, None] * np.arange(s)[None, :]
    pieces = []
    rem = c
    for _ in range(3):
        piece = rem.astype(jnp.bfloat16)
        pieces.append(piece)
        rem = rem - piece.astype(np.float64)
    one = np.ones_like(pieces[0])
    zero = np.zeros((N_HEADS, s, LANES - 6), dtype=jnp.bfloat16)
    kaug = np.concatenate([np.stack(pieces + [one] * 3, axis=-1), zero], axis=-1)
    qaug = np.concatenate([np.stack([one] * 3 + [-p for p in pieces], axis=-1), zero], axis=-1)
    return jnp.asarray(qaug), jnp.asarray(kaug)


def _attention(q, k, v, qaug, kaug, lq1, lk1, lq2, lk2, sub_g, lam_init):
    bsz, s, _ = q.shape
    nq = s // TQ
    fixed2 = lambda b_, i: (0, 0)
    seq = lambda b_, i: (b_, 0, 0)
    vec = pl.BlockSpec((1, HEAD_DIM), fixed2)
    return pl.pallas_call(
        functools.partial(_attn_kernel, lam_init=lam_init),
        grid=(bsz, nq),
        in_specs=[pl.BlockSpec((1, TQ, C_ATTN), lambda b_, i: (b_, i, 0)),
                  pl.BlockSpec((1, s, C_ATTN), seq),
                  pl.BlockSpec((1, s, C_ATTN), seq),
                  pl.BlockSpec((N_HEADS, TQ, LANES), lambda b_, i: (0, i, 0)),
                  pl.BlockSpec((N_HEADS, s, LANES), lambda b_, i: (0, 0, 0)),
                  vec, vec, vec, vec,
                  pl.BlockSpec((1, V_DIM), fixed2)],
        out_specs=pl.BlockSpec((1, TQ, C_ATTN), lambda b_, i: (b_, i, 0)),
        out_shape=jax.ShapeDtypeStruct((bsz, s, C_ATTN), BF16),
        scratch_shapes=[pltpu.VMEM((N_HEADS, s, 2 * LANES), BF16),
                        pltpu.VMEM((N_HEADS, 2 * TQ, 2 * LANES), BF16),
                        pltpu.VMEM((N_HEADS, 2 * TQ, LANES), F32),
                        pltpu.VMEM((N_HEADS, 2 * TQ, LANES), F32),
                        pltpu.VMEM((N_HEADS, 2 * TQ, V_DIM), F32)],
        compiler_params=_params(("parallel", "arbitrary")),
        name="diff_attention",
    )(q, k, v, qaug, kaug, lq1, lk1, lq2, lk2, sub_g)


def _split_bf16(a):
    hi = a.astype(BF16)
    lo = (a - hi.astype(F32)).astype(BF16)
    return hi, lo


def _out_proj_kernel(x_ref, yc_ref, yp_ref, ya_ref, w_ref, g_ref, rw_ref, rb_ref,
                     xm_ref, meta_t_ref, counts_ref, cnt_ref):
    mix = jnp.concatenate([yc_ref[...], yp_ref[...], ya_ref[...]], axis=1)
    x = _load_tokens(x_ref) + jnp.dot(mix, w_ref[...], preferred_element_type=F32)
    tm = x.shape[0]
    _store_tokens(xm_ref, x, XM_ROWS)

    hf_hi, hf_lo = _split_bf16(_rms(x, g_ref[...]))

    both = jnp.dot(hf_hi, rw_ref[...], preferred_element_type=F32)
    logits = (both[:, 0:LANES] + both[:, LANES:]
              + jnp.dot(hf_lo, rw_ref[:, 0:LANES], preferred_element_type=F32)) + rb_ref[...]
    lane = lax.broadcasted_iota(jnp.int32, (tm, LANES), 1)
    lane_f = lane.astype(F32)
    neg = -jnp.inf

    def first_argmax(val, vmax):
        first = jnp.min(jnp.where(val == vmax, lane_f, float(LANES)), axis=-1, keepdims=True)
        return first.astype(jnp.int32)

    lg = jnp.where(lane < N_GROUPS, logits, neg)
    g_max = jnp.max(lg, axis=-1, keepdims=True)
    g_idx = first_argmax(lg, g_max)
    g_p = 1.0 / jnp.sum(jnp.exp(lg - g_max), axis=-1, keepdims=True)

    e_lane = lane - N_GROUPS
    in_group = (e_lane >= g_idx * PER_GROUP) & (e_lane < (g_idx + 1) * PER_GROUP)
    le = jnp.where(in_group, logits, neg)
    e_max = jnp.max(le, axis=-1, keepdims=True)
    pe = jnp.exp(le - e_max)
    pe = pe / jnp.sum(pe, axis=-1, keepdims=True)
    p1 = jnp.max(pe, axis=-1, keepdims=True)
    i1 = first_argmax(jnp.where(in_group, pe, neg), p1)
    pe2 = jnp.where(in_group & (lane != i1), pe, neg)
    p2 = jnp.max(pe2, axis=-1, keepdims=True)
    i2 = first_argmax(pe2, p2)
    denom = p1 + p2
    gate1 = g_p * (p1 / denom)
    gate2 = g_p * (p2 / denom)

    loc1 = i1 - N_GROUPS - g_idx * PER_GROUP
    loc2 = i2 - N_GROUPS - g_idx * PER_GROUP
    a = jnp.minimum(loc1, loc2)
    b = jnp.maximum(loc1, loc2)
    pair = jnp.where(a == 0, 0, jnp.where(a == 1, 3, 5)) + (b - a - 1)
    bucket = g_idx * N_PAIRS + pair
    gate_a = jnp.where(loc1 < loc2, gate1, gate2)
    gate_b = jnp.where(loc1 < loc2, gate2, gate1)

    @pl.when(pl.program_id(0) == 0)
    def _():
        cnt_ref[...] = jnp.zeros(cnt_ref.shape, F32)

    in_bucket = lane == bucket
    earlier = (lax.broadcasted_iota(jnp.int32, (tm, tm), 1)
               < lax.broadcasted_iota(jnp.int32, (tm, tm), 0)).astype(BF16)
    before = jnp.dot(earlier, in_bucket.astype(BF16), preferred_element_type=F32) + cnt_ref[...]
    rank = jnp.sum(jnp.where(in_bucket, before, 0.0), axis=-1, keepdims=True)
    cnt_ref[...] += jnp.sum(in_bucket.astype(F32), axis=0, keepdims=True)
    counts_ref[...] = cnt_ref[...]

    meta = jnp.where(lane == META_GATE_A, gate_a,
                     jnp.where(lane == META_GATE_B, gate_b,
                               jnp.where(lane == META_BUCKET, bucket.astype(F32),
                                         jnp.where(lane == META_RANK, rank, 0.0))))
    xm_ref[pl.ds(X_ROWS, tm, stride=XM_ROWS), :] = meta
    meta_t_ref[...] = meta.T[0:META_ROWS, :]


def _out_proj(x2, yc, yp, ya, w_bf, g, rw_hi_lo, rb):
    token_major = x2.shape[1] == LANES
    t = x2.shape[0] // X_ROWS if token_major else x2.shape[0]
    tm = TM_OUT
    row = lambda i: (i, 0)
    fixed = lambda i: (0, 0)
    x_spec = pl.BlockSpec((tm * X_ROWS, LANES) if token_major else (tm, D_MODEL), row)
    return pl.pallas_call(
        _out_proj_kernel,
        grid=(t // tm,),
        in_specs=[x_spec,
                  pl.BlockSpec((tm, C_CONV), row),
                  pl.BlockSpec((tm, C_POOL), row),
                  pl.BlockSpec((tm, C_ATTN), row),
                  pl.BlockSpec((D_MODEL, D_MODEL), fixed),
                  pl.BlockSpec((1, D_MODEL), fixed),
                  pl.BlockSpec((D_MODEL, 2 * LANES), fixed),
                  pl.BlockSpec((1, LANES), fixed)],
        out_specs=[pl.BlockSpec((tm * XM_ROWS, LANES), row),
                   pl.BlockSpec((META_ROWS, tm), row),
                   pl.BlockSpec((1, LANES), fixed)],
        out_shape=[jax.ShapeDtypeStruct((t * XM_ROWS, LANES), F32),
                   jax.ShapeDtypeStruct((t // tm * META_ROWS, tm), F32),
                   jax.ShapeDtypeStruct((1, LANES), F32)],
        scratch_shapes=[pltpu.VMEM((1, LANES), F32)],
        compiler_params=_params(("arbitrary",)),
        name="out_proj_router",
    )(x2, yc, yp, ya, w_bf, g, rw_hi_lo, rb)


def _for_chunk_rows(dest_ref, chunk, start_copy, wait_copy):
    base = pl.program_id(0) * chunk

    def body(blk, carry):
        rows = [blk * DMA_UNROLL + u for u in range(DMA_UNROLL)]
        slots = [dest_ref[base + r] for r in rows]
        for u, (r, slot) in enumerate(zip(rows, slots)):
            start_copy(r, slot, u % 2)
        return carry

    lax.fori_loop(0, chunk // DMA_UNROLL, body, 0)
    for piece in range(chunk // COPY_WAIT):
        wait_copy(piece)


def _dispatch_kernel(dest_ref, x_ref, out_hbm, sem):
    def start_copy(r, slot, priority):
        pltpu.make_async_copy(x_ref.at[pl.ds(r * XM_ROWS, XM_ROWS)],
                              out_hbm.at[pl.ds(slot * XM_ROWS, XM_ROWS)],
                              sem.at[0]).start(priority=priority)

    def wait_copy(piece):
        n = COPY_WAIT * XM_ROWS
        pltpu.make_async_copy(x_ref.at[pl.ds(piece * n, n)], out_hbm.at[pl.ds(0, n)],
                              sem.at[0]).wait()

    _for_chunk_rows(dest_ref, DISPATCH_CHUNK, start_copy, wait_copy)


def _dispatch(xm, dest):
    n_tokens = xm.shape[0] // XM_ROWS
    n_rows = dest.shape[0]
    chunk = DISPATCH_CHUNK
    assert n_tokens % chunk == 0 and n_rows % chunk == 0
    last = n_tokens // chunk - 1
    grid_spec = pltpu.PrefetchScalarGridSpec(
        num_scalar_prefetch=1,
        grid=(n_rows // chunk,),
        in_specs=[pl.BlockSpec((chunk * XM_ROWS, LANES),
                               lambda i, dest: (jnp.minimum(i, last), 0))],
        out_specs=pl.BlockSpec(memory_space=pl.ANY),
        scratch_shapes=[pltpu.SemaphoreType.DMA((1,))])
    return pl.pallas_call(
        _dispatch_kernel,
        grid_spec=grid_spec,
        out_shape=jax.ShapeDtypeStruct((n_rows * XM_ROWS, LANES), F32),
        compiler_params=_params(("arbitrary",)),
        name="moe_dispatch",
    )(dest, xm)


def _combine_kernel(dest_ref, ys_hbm, o_ref, *scratch, to_rows):
    buf = scratch[0] if to_rows else o_ref
    sem = scratch[-1]

    def start_copy(r, slot, priority):
        pltpu.make_async_copy(ys_hbm.at[pl.ds(slot * X_ROWS, X_ROWS)],
                              buf.at[pl.ds(r * X_ROWS, X_ROWS)],
                              sem.at[0]).start(priority=priority)

    def wait_copy(piece):
        n = COPY_WAIT * X_ROWS
        pltpu.make_async_copy(ys_hbm.at[pl.ds(0, n)], buf.at[pl.ds(piece * n, n)],
                              sem.at[0]).wait()

    _for_chunk_rows(dest_ref, COPY_CHUNK, start_copy, wait_copy)
    if to_rows:
        o_ref[...] = _load_tokens(buf)


def _combine(ys, dest, n_tokens, to_rows):
    assert n_tokens % COPY_CHUNK == 0
    chunk_rows = COPY_CHUNK * X_ROWS
    if to_rows:
        out_spec = pl.BlockSpec((COPY_CHUNK, D_MODEL), lambda i, dest: (i, 0))
        out_shape = jax.ShapeDtypeStruct((n_tokens, D_MODEL), F32)
        scratch = [pltpu.VMEM((chunk_rows, LANES), F32)]
    else:
        out_spec = pl.BlockSpec((chunk_rows, LANES), lambda i, dest: (i, 0))
        out_shape = jax.ShapeDtypeStruct((n_tokens * X_ROWS, LANES), F32)
        scratch = []
    grid_spec = pltpu.PrefetchScalarGridSpec(
        num_scalar_prefetch=1,
        grid=(n_tokens // COPY_CHUNK,),
        in_specs=[pl.BlockSpec(memory_space=pl.ANY)],
        out_specs=out_spec,
        scratch_shapes=scratch + [pltpu.SemaphoreType.DMA((1,))])
    return pl.pallas_call(
        functools.partial(_combine_kernel, to_rows=to_rows),
        grid_spec=grid_spec,
        out_shape=out_shape,
        compiler_params=_params(("arbitrary",)),
        name="moe_combine",
    )(dest, ys)


def _moe_kernel(te1_ref, te2_ref, nv_ref, used_ref,
                xs_hbm, g_ref, wg1_ref, wu1_ref, wd1_ref, wg2_ref, wu2_ref, wd2_ref, ys_ref,
                xbuf, sem):
    i = pl.program_id(0)
    n_used = used_ref[0]
    tile_rows = TM_MOE * XM_ROWS

    def tile_copy(tile):
        slot = tile % MOE_RING
        return pltpu.make_async_copy(
            xs_hbm.at[pl.ds(pl.multiple_of(tile * tile_rows, tile_rows), tile_rows)],
            xbuf.at[slot], sem.at[slot])

    @pl.when(i == 0)
    def _():
        tile_copy(0).start()

        @pl.when(n_used > 1)
        def _():
            tile_copy(1).start()

    @pl.when(i + 2 < n_used)
    def _():
        tile_copy(i + 2).start()

    @pl.when(nv_ref[i] > 0)
    def _():
        tile_copy(i).wait()
        xs_ref = xbuf.at[i % MOE_RING]
        x = _load_tokens(xs_ref, XM_ROWS)
        meta = xs_ref[pl.ds(X_ROWS, TM_MOE, stride=XM_ROWS), :]
        hf = _rms(x, g_ref[...]).astype(BF16)

        def expert(wg_ref, wu_ref, wd_ref, gate):
            a = jnp.dot(hf, wg_ref[0], preferred_element_type=F32)
            u = jnp.dot(hf, wu_ref[0], preferred_element_type=F32)
            act = a * jax.nn.sigmoid(a) * u * gate
            return jnp.dot(act.astype(BF16), wd_ref[0], preferred_element_type=F32)

        y = expert(wg1_ref, wu1_ref, wd1_ref, meta[:, META_GATE_A:META_GATE_A + 1])
        y = y + expert(wg2_ref, wu2_ref, wd2_ref, meta[:, META_GATE_B:META_GATE_B + 1])
        _store_tokens(ys_ref, x + y)

    @pl.when(nv_ref[i] == 0)
    def _():
        ys_ref[...] = jnp.zeros(ys_ref.shape, F32)


def _moe(xs, g, te1, te2, nv, used, wg_bf, wu_bf, wd_bf):
    tm = TM_MOE
    n_tiles = nv.shape[0]
    fixed = lambda i, te1, te2, nv, used: (0, 0)
    tile_out = lambda i, te1, te2, nv, used: (i, 0)
    expert1 = lambda i, te1, te2, nv, used: (te1[i], 0, 0)
    expert2 = lambda i, te1, te2, nv, used: (te2[i], 0, 0)
    w_in = lambda index_map: pl.BlockSpec((1, D_MODEL, D_EXPERT), index_map)
    w_out = lambda index_map: pl.BlockSpec((1, D_EXPERT, D_MODEL), index_map)
    grid_spec = pltpu.PrefetchScalarGridSpec(
        num_scalar_prefetch=4,
        grid=(n_tiles,),
        in_specs=[pl.BlockSpec(memory_space=pl.ANY),
                  pl.BlockSpec((1, D_MODEL), fixed),
                  w_in(expert1), w_in(expert1), w_out(expert1),
                  w_in(expert2), w_in(expert2), w_out(expert2)],
        out_specs=pl.BlockSpec((tm * X_ROWS, LANES), tile_out),
        scratch_shapes=[pltpu.VMEM((MOE_RING, tm * XM_ROWS, LANES), F32),
                        pltpu.SemaphoreType.DMA((MOE_RING,))])
    return pl.pallas_call(
        _moe_kernel,
        grid_spec=grid_spec,
        out_shape=jax.ShapeDtypeStruct((n_tiles * tm * X_ROWS, LANES), F32),
        compiler_params=_params(("arbitrary",)),
        name="moe",
    )(te1, te2, nv, used, xs, g, wg_bf, wu_bf, wd_bf, wg_bf, wu_bf, wd_bf)


def _route(meta_t, counts, t):
    tm = TM_MOE
    nb = N_GROUPS * N_PAIRS
    n_tiles = t // tm + nb
    meta = meta_t.reshape(t // TM_OUT, META_ROWS, TM_OUT)
    bucket = meta[:, META_BUCKET, :].reshape(t).astype(jnp.int32)
    rank = meta[:, META_RANK, :].reshape(t).astype(jnp.int32)
    counts = counts[0, :nb].astype(jnp.int32)
    ids = jnp.arange(nb, dtype=jnp.int32)

    tiles = (counts + tm - 1) // tm
    tile_end = jnp.cumsum(tiles)
    tile_start = tile_end - tiles
    used = tile_end[-1]
    row0 = tile_start * tm
    dest = rank + jnp.sum(jnp.where(bucket[:, None] == ids[None, :], row0[None, :], 0), axis=1)

    ti = jnp.arange(n_tiles, dtype=jnp.int32)
    tic = jnp.minimum(ti, used - 1)
    tb = jnp.sum((tile_end[None, :] <= tic[:, None]).astype(jnp.int32), axis=1)
    onehot_tb = tb[:, None] == ids[None, :]
    pick = lambda table: jnp.sum(jnp.where(onehot_tb, table[None, :], 0), axis=1)
    nv = jnp.where(ti < used, jnp.clip(pick(counts) - (tic - pick(tile_start)) * tm, 0, tm), 0)
    pair_a = jnp.asarray([0, 0, 0, 1, 1, 2], jnp.int32)
    pair_b = jnp.asarray([1, 2, 3, 2, 3, 3], jnp.int32)
    te1 = pick((ids // N_PAIRS) * PER_GROUP + pair_a[ids % N_PAIRS])
    te2 = pick((ids // N_PAIRS) * PER_GROUP + pair_b[ids % N_PAIRS])

    pads = tiles * tm - counts
    pad_end = jnp.cumsum(pads)
    k = jnp.arange(nb * tm, dtype=jnp.int32)
    kb = jnp.minimum(jnp.sum((pad_end[None, :] <= k[:, None]).astype(jnp.int32), axis=1), nb - 1)
    onehot_kb = kb[:, None] == ids[None, :]
    pick_k = lambda table: jnp.sum(jnp.where(onehot_kb, table[None, :], 0), axis=1)
    in_pad = pick_k(row0 + counts) + (k - pick_k(pad_end - pads))
    free_rows = jnp.where(k < pad_end[-1], in_pad, used * tm + (k - pad_end[-1]))
    i32 = lambda a: a.astype(jnp.int32)
    return (i32(jnp.concatenate([dest, free_rows])), i32(te1), i32(te2), i32(nv),
            i32(used).reshape(1))


def _block_diag(w):
    n = w.shape[0]
    eye = jnp.eye(n, dtype=w.dtype)
    return jnp.einsum("gcd,gh->gchd", w, eye).reshape(n * POOL_GROUP, n * POOL_GROUP)


def kernel(x, attn_norm_g, w_in, conv_w, conv_b, conv_ln_g, conv_ln_b, conv_pw_w, conv_pw_b,
           pool_w, pool_scale, q_norm_g, k_norm_g, lambda_q1, lambda_k1, lambda_q2, lambda_k2,
           attn_sub_norm_g, w_out, ffn_norm_g, router_g_w, router_g_b, router_e_w, router_e_b,
           w_gate, w_up, w_down):
    bsz, s, d = x.shape
    depth = w_in.shape[0]
    t = bsz * s
    qaug, kaug = _alibi_aug(s)
    row = lambda a: a.reshape(1, -1)
    all_experts = lambda w: w.astype(BF16).reshape((depth * N_EXPERTS,) + w.shape[2:])
    wg_all, wu_all, wd_all = all_experts(w_gate), all_experts(w_up), all_experts(w_down)

    x2 = x.reshape(t, d)
    for l in range(depth):
        lam_init = 0.8 - 0.6 * math.exp(-0.3 * l)
        y_conv, y_pool, q, k, v = _in_proj(
            x2, s, row(attn_norm_g[l]), w_in[l].astype(BF16),
            row(jnp.tile(q_norm_g[l], 2) * (HEAD_DIM ** 0.5 * HEAD_DIM ** -0.5 * math.log2(math.e))),
            row(jnp.tile(k_norm_g[l], 2) * HEAD_DIM ** 0.5),
            conv_w[l], row(conv_b[l]), row(conv_ln_g[l]), row(conv_ln_b[l]),
            conv_pw_w[l].astype(BF16), row(conv_pw_b[l]),
            _block_diag(pool_w[l]).astype(BF16), row(pool_scale[l]))
        y_attn = _attention(q.reshape(bsz, s, -1), k.reshape(bsz, s, -1), v.reshape(bsz, s, -1),
                            qaug, kaug, row(lambda_q1[l]), row(lambda_k1[l]), row(lambda_q2[l]),
                            row(lambda_k2[l]), row(attn_sub_norm_g[l]), lam_init)

        pad = LANES - N_GROUPS - N_EXPERTS
        rw = jnp.pad(jnp.concatenate([router_g_w[l], router_e_w[l]], axis=1), ((0, 0), (0, pad)))
        rb = jnp.pad(jnp.concatenate([router_g_b[l], router_e_b[l]]), (0, pad))
        rw_hi = rw.astype(BF16)
        rw_lo = (rw - rw_hi.astype(F32)).astype(BF16)
        rw_hi_lo = jnp.concatenate([rw_hi, rw_lo], axis=1)
        xm, meta_t, counts = _out_proj(
            x2, y_conv, y_pool, y_attn.reshape(t, -1),
            w_out[l].astype(BF16), row(ffn_norm_g[l]), rw_hi_lo, row(rb))

        dest, te1, te2, nv, used = _route(meta_t, counts, t)
        xs = _dispatch(xm, dest)
        ys = _moe(xs, row(ffn_norm_g[l]), te1 + l * N_EXPERTS, te2 + l * N_EXPERTS, nv, used,
                  wg_all, wu_all, wd_all)
        x2 = _combine(ys, dest, t, to_rows=l == depth - 1)
    return x2.reshape(bsz, s, d)
```

```python
import functools
import math

import jax
import jax.numpy as jnp
import numpy as np
from jax import lax
from jax.experimental import pallas as pl
from jax.experimental.pallas import tpu as pltpu

D_MODEL = 1024
C_CONV = 256
C_POOL = 256
C_ATTN = 512
N_HEADS = 4
HEAD_DIM = 64
V_DIM = 2 * HEAD_DIM
CONV_WIDTH = 31
POOL_WINDOWS = (2, 4, 8, 16)
POOL_GROUP = C_POOL // len(POOL_WINDOWS)
N_IN = 2 * C_CONV + C_POOL + 3 * C_ATTN
N_GROUPS = 4
PER_GROUP = 4
N_EXPERTS = N_GROUPS * PER_GROUP
D_EXPERT = 256
EPS = 1e-6

LANES = 128
SUBLANES = 8
VMEM_LIMIT = 48 * 1024 * 1024

TM_PROJ = 512
TM_OUT = 512
HALO = 32
TQ = 512
TK = 256
TM_MOE = 512
MOE_RING = 3
N_PAIRS = 6
X_ROWS = D_MODEL // LANES
XM_ROWS = X_ROWS + 1
META_GATE_A, META_GATE_B, META_BUCKET, META_RANK = 0, 1, 2, 3
META_ROWS = 8
DMA_UNROLL = 8
DISPATCH_CHUNK = 4096
COPY_CHUNK = 2048
COPY_WAIT = 512

F32 = jnp.float32
BF16 = jnp.bfloat16


def _params(sem):
    return pltpu.CompilerParams(dimension_semantics=sem, vmem_limit_bytes=VMEM_LIMIT)


def _rms(x, g):
    return x * lax.rsqrt(jnp.mean(x * x, axis=-1, keepdims=True) + EPS) * g


def _load_tokens(x_ref, rows_per_token=X_ROWS):
    if x_ref.shape[-1] != LANES:
        return x_ref[...]
    tm = x_ref.shape[0] // rows_per_token
    return jnp.concatenate(
        [x_ref[pl.ds(c, tm, stride=rows_per_token), :] for c in range(X_ROWS)], axis=1)


def _store_tokens(o_ref, x, rows_per_token=X_ROWS):
    tm = x.shape[0]
    for c in range(X_ROWS):
        o_ref[pl.ds(c, tm, stride=rows_per_token), :] = x[:, c * LANES:(c + 1) * LANES]


def _in_proj_kernel(x_ref, g_ref, w_ref, qg_ref, kg_ref,
                    cw_ref, cb_ref, lng_ref, lnb_ref, pw_ref, pwb_ref, plw_ref, pls_ref,
                    yc_ref, yp_ref, q_ref, k_ref, v_ref, cext_ref, pext_ref, *, tiles_per_seq):
    ts = yc_ref.shape[0]
    seq_tile = pl.program_id(0) % tiles_per_seq
    h = _rms(_load_tokens(x_ref), g_ref[...]).astype(BF16)

    def proj(c0, width):
        return jnp.dot(h, w_ref[:, c0:c0 + width], preferred_element_type=F32)

    @pl.when(seq_tile == 0)
    def _():
        cext_ref[0, 0:HALO, :] = jnp.zeros((HALO, C_CONV), F32)
        pext_ref[0, 0:HALO, :] = jnp.zeros((HALO, C_POOL), F32)

    z = proj(0, 2 * C_CONV)
    cext_ref[0, HALO:, :] = z[:, :C_CONV] * jax.nn.sigmoid(z[:, C_CONV:])
    pext_ref[0, HALO:, :] = proj(2 * C_CONV, C_POOL)

    lane = lax.broadcasted_iota(jnp.int32, (1, LANES), 1)
    first = lane < HEAD_DIM

    def qk_norm(z, gain_ref, out_ref):
        gain = gain_ref[...]
        for hd in range(N_HEADS):
            blk = z[:, hd * LANES:(hd + 1) * LANES]
            sq = blk * blk
            s_all = jnp.sum(sq, axis=-1, keepdims=True)
            s_lo = jnp.sum(jnp.where(first, sq, 0.0), axis=-1, keepdims=True)
            r_lo = lax.rsqrt(s_lo + HEAD_DIM * EPS)
            r_hi = lax.rsqrt((s_all - s_lo) + HEAD_DIM * EPS)
            r = jnp.where(first, r_lo, r_hi)
            out_ref[:, hd * LANES:(hd + 1) * LANES] = (blk * r * gain).astype(BF16)

    c_q = 2 * C_CONV + C_POOL
    qk_norm(proj(c_q, C_ATTN), qg_ref, q_ref)
    qk_norm(proj(c_q + C_ATTN, C_ATTN), kg_ref, k_ref)
    v_ref[...] = proj(c_q + 2 * C_ATTN, C_ATTN).astype(BF16)

    _conv_group(cext_ref, ts, cw_ref, cb_ref, lng_ref, lnb_ref, pw_ref, pwb_ref, yc_ref)
    _pool_group(pext_ref, ts, seq_tile * ts, plw_ref, pls_ref, yp_ref)

    cext_ref[0, 0:HALO, :] = cext_ref[0, ts:ts + HALO, :]
    pext_ref[0, 0:HALO, :] = pext_ref[0, ts:ts + HALO, :]


def _in_proj(x2, seq_len, g, w_bf, qg, kg, conv_w, conv_b, ln_g, ln_b, pw_bf, pw_b,
             pool_w_bf, pool_scale):
    token_major = x2.shape[1] == LANES
    t = x2.shape[0] // X_ROWS if token_major else x2.shape[0]
    tm = TM_PROJ
    assert seq_len % tm == 0
    row = lambda i: (i, 0)
    fixed = lambda i: (0, 0)
    full = lambda a: pl.BlockSpec(a.shape, fixed)
    x_spec = pl.BlockSpec((tm * X_ROWS, LANES) if token_major else (tm, D_MODEL), row)
    small = [g, w_bf, qg, kg, conv_w, conv_b, ln_g, ln_b, pw_bf, pw_b, pool_w_bf, pool_scale]
    widths = [C_CONV, C_POOL, C_ATTN, C_ATTN, C_ATTN]
    return pl.pallas_call(
        functools.partial(_in_proj_kernel, tiles_per_seq=seq_len // tm),
        grid=(t // tm,),
        in_specs=[x_spec] + [full(a) for a in small],
        out_specs=[pl.BlockSpec((tm, w), row) for w in widths],
        out_shape=[jax.ShapeDtypeStruct((t, w), BF16) for w in widths],
        scratch_shapes=[pltpu.VMEM((SUBLANES, tm + HALO, C_CONV), F32),
                        pltpu.VMEM((4, tm + HALO, C_POOL), F32)],
        compiler_params=_params(("arbitrary",)),
        name="in_proj_mix",
    )(x2, *small)


def _conv_group(ext_ref, ts, w_ref, b_ref, lng_ref, lnb_ref, pw_ref, pwb_ref, o_ref):
    n_shifted = ts + HALO - SUBLANES
    for s in range(1, SUBLANES):
        ext_ref[s, 0:n_shifted, :] = ext_ref[0, s:s + n_shifted, :]

    base = HALO - (CONV_WIDTH - 1)
    rows = 128
    for c in range(ts // rows):
        acc = jnp.broadcast_to(b_ref[...], (rows, C_CONV))
        for j in range(CONV_WIDTH):
            phase = (base + j) % SUBLANES
            r0 = c * rows + base + j - phase
            acc = acc + ext_ref[phase, r0:r0 + rows, :] * w_ref[j:j + 1, :]
        mu = jnp.mean(acc, axis=-1, keepdims=True)
        xc = acc - mu
        y = xc * lax.rsqrt(jnp.mean(xc * xc, axis=-1, keepdims=True) + EPS)
        y = y * lng_ref[...] + lnb_ref[...]
        y = y * jax.nn.sigmoid(y)
        out = jnp.dot(y.astype(BF16), pw_ref[...], preferred_element_type=F32) + pwb_ref[...]
        o_ref[c * rows:(c + 1) * rows, :] = out.astype(BF16)


def _pool_group(ext_ref, ts, pos0, w_ref, scale_ref, o_ref):
    n = ts + HALO
    assert POOL_WINDOWS == (2, 4, 8, 16) and HALO >= 4 * SUBLANES
    sums = []
    for k in range(1, 5):
        lo = SUBLANES * k
        shift = 1 << (k - 1)
        level = ext_ref[k - 1, lo:n, :] + ext_ref[k - 1, lo - shift:n - shift, :]
        sums.append(level[HALO - lo:, :])
        if k < 4:
            ext_ref[k, lo:n, :] = level

    lane = lax.broadcasted_iota(jnp.int32, (1, C_POOL), 1)
    group = lane // POOL_GROUP
    win = jnp.left_shift(2, group)
    acc = jnp.where(group == 0, sums[0],
                    jnp.where(group == 1, sums[1], jnp.where(group == 2, sums[2], sums[3])))
    pos = pos0 + lax.broadcasted_iota(jnp.int32, (ts, 1), 0)
    count = jnp.minimum(pos + 1, win).astype(F32)
    diff = acc / count - ext_ref[0, HALO:, :]
    out = jnp.dot(diff.astype(BF16), w_ref[...], preferred_element_type=F32) * scale_ref[...]
    o_ref[...] = out.astype(BF16)


def _attn_kernel(q_ref, k_ref, v_ref, qaug_ref, kaug_ref, lq1_ref, lk1_ref, lq2_ref, lk2_ref,
                 sg_ref, o_ref, kx_ref, q2_all, m_all, l_all, acc_all, *, lam_init):
    qi = pl.program_id(1)
    half = TQ // 2
    assert half == TK

    @pl.when(qi == 0)
    def _():
        for hd in range(N_HEADS):
            kx_ref[hd, :, 0:LANES] = k_ref[0, :, hd * LANES:(hd + 1) * LANES]
            kx_ref[hd, :, LANES:] = kaug_ref[hd]

    lam = (jnp.exp(jnp.sum(lq1_ref[...] * lk1_ref[...], axis=-1, keepdims=True))
           - jnp.exp(jnp.sum(lq2_ref[...] * lk2_ref[...], axis=-1, keepdims=True))
           + lam_init)
    first = lax.broadcasted_iota(jnp.int32, (1, LANES), 1) < HEAD_DIM
    row = lax.broadcasted_iota(jnp.int32, (2 * half, TK), 0)
    col = lax.broadcasted_iota(jnp.int32, (2 * half, TK), 1)
    tri = col <= (row & (half - 1))

    def step(hd, r0, nr, k0, width, masked, first_block):
        q2_ref, m_ref, l_ref, acc_ref = q2_all.at[hd], m_all.at[hd], l_all.at[hd], acc_all.at[hd]
        kv_rows = pl.ds(pl.multiple_of(k0, TK), width)
        kb = kx_ref[hd, kv_rows, :]
        vb = v_ref[0, kv_rows, hd * LANES:(hd + 1) * LANES]
        s = lax.dot_general(q2_ref[r0:r0 + nr, :], kb, (((1,), (1,)), ((), ())),
                            preferred_element_type=F32)
        if masked:
            s = jnp.where(tri, s, -jnp.inf)
        s_max = jnp.max(s, axis=-1, keepdims=True)
        if first_block:
            m_new = jnp.broadcast_to(s_max, (nr, LANES))
        else:
            m_prev = m_ref[r0:r0 + nr]
            m_new = jnp.maximum(m_prev, s_max)
            alpha = jnp.exp2(m_prev - m_new)
        p = jnp.exp2(s - jnp.concatenate([m_new] * (width // LANES), axis=1))
        l_new = jnp.sum(p, axis=-1, keepdims=True)
        acc_new = jnp.dot(p.astype(BF16), vb, preferred_element_type=F32)
        if first_block:
            l_ref[r0:r0 + nr] = jnp.broadcast_to(l_new, (nr, LANES))
            acc_ref[r0:r0 + nr] = acc_new
        else:
            l_ref[r0:r0 + nr] = alpha * l_ref[r0:r0 + nr] + l_new
            acc_ref[r0:r0 + nr] = alpha * acc_ref[r0:r0 + nr] + acc_new
        m_ref[r0:r0 + nr] = m_new

    for hd in range(N_HEADS):
        q2_ref = q2_all.at[hd]
        qb = q_ref[0, :, hd * LANES:(hd + 1) * LANES]
        zero = jnp.zeros_like(qb)
        q_maps = (jnp.where(first, qb, zero), jnp.where(first, zero, qb))
        qa = qaug_ref[hd]
        for part in range(2):
            rows = slice(part * half, (part + 1) * half)
            for m in range(2):
                dst = slice((2 * part + m) * half, (2 * part + m + 1) * half)
                q2_ref[dst, 0:LANES] = q_maps[m][rows]
                q2_ref[dst, LANES:] = qa[rows]

    for hd in range(N_HEADS):
        step(hd, 0, 2 * half, qi * TQ, TK, True, True)
        step(hd, 2 * half, 2 * half, qi * TQ, TK, False, True)
    for hd in range(N_HEADS):
        step(hd, 2 * half, 2 * half, qi * TQ + TK, TK, True, False)

    def body(j, carry):
        for hd in range(N_HEADS):
            step(hd, 0, 4 * half, 2 * j * TQ, 2 * TQ, False, False)
        return carry

    lax.fori_loop(0, qi // 2, body, 0)

    @pl.when(qi % 2 == 1)
    def _():
        for hd in range(N_HEADS):
            step(hd, 0, 4 * half, (qi - 1) * TQ, TQ, False, False)

    for hd in range(N_HEADS):
        l_ref, acc_ref = l_all.at[hd], acc_all.at[hd]
        for part in range(2):
            r = 2 * part * half
            o = (acc_ref[r:r + half] / l_ref[r:r + half]
                 - lam * (acc_ref[r + half:r + 2 * half] / l_ref[r + half:r + 2 * half]))
            o = _rms(o, sg_ref[...]) * (1.0 - lam_init)
            o_ref[0, part * half:(part + 1) * half, hd * LANES:(hd + 1) * LANES] = o.astype(BF16)


def _alibi_aug(s):
    start = 2.0 ** (-8.0 / N_HEADS)
    slopes = np.array([start ** (i + 1) for i in range(N_HEADS)], dtype=np.float32)
    c = math.log2(math.e) * slopes.astype(np.float64)[:, None] * np.arange(s)[None, :]
    pieces = []
    rem = c
    for _ in range(3):
        piece = rem.astype(jnp.bfloat16)
        pieces.append(piece)
        rem = rem - piece.astype(np.float64)
    one = np.ones_like(pieces[0])
    zero = np.zeros((N_HEADS, s, LANES - 6), dtype=jnp.bfloat16)
    kaug = np.concatenate([np.stack(pieces + [one] * 3, axis=-1), zero], axis=-1)
    qaug = np.concatenate([np.stack([one] * 3 + [-p for p in pieces], axis=-1), zero], axis=-1)
    return jnp.asarray(qaug), jnp.asarray(kaug)


def _attention(q, k, v, qaug, kaug, lq1, lk1, lq2, lk2, sub_g, lam_init):
    bsz, s, _ = q.shape
    nq = s // TQ
    fixed2 = lambda b_, i: (0, 0)
    seq = lambda b_, i: (b_, 0, 0)
    vec = pl.BlockSpec((1, HEAD_DIM), fixed2)
    return pl.pallas_call(
        functools.partial(_attn_kernel, lam_init=lam_init),
        grid=(bsz, nq),
        in_specs=[pl.BlockSpec((1, TQ, C_ATTN), lambda b_, i: (b_, i, 0)),
                  pl.BlockSpec((1, s, C_ATTN), seq),
                  pl.BlockSpec((1, s, C_ATTN), seq),
                  pl.BlockSpec((N_HEADS, TQ, LANES), lambda b_, i: (0, i, 0)),
                  pl.BlockSpec((N_HEADS, s, LANES), lambda b_, i: (0, 0, 0)),
                  vec, vec, vec, vec,
                  pl.BlockSpec((1, V_DIM), fixed2)],
        out_specs=pl.BlockSpec((1, TQ, C_ATTN), lambda b_, i: (b_, i, 0)),
        out_shape=jax.ShapeDtypeStruct((bsz, s, C_ATTN), BF16),
        scratch_shapes=[pltpu.VMEM((N_HEADS, s, 2 * LANES), BF16),
                        pltpu.VMEM((N_HEADS, 2 * TQ, 2 * LANES), BF16),
                        pltpu.VMEM((N_HEADS, 2 * TQ, LANES), F32),
                        pltpu.VMEM((N_HEADS, 2 * TQ, LANES), F32),
                        pltpu.VMEM((N_HEADS, 2 * TQ, V_DIM), F32)],
        compiler_params=_params(("parallel", "arbitrary")),
        name="diff_attention",
    )(q, k, v, qaug, kaug, lq1, lk1, lq2, lk2, sub_g)


def _split_bf16(a):
    hi = a.astype(BF16)
    lo = (a - hi.astype(F32)).astype(BF16)
    return hi, lo


def _out_proj_kernel(x_ref, yc_ref, yp_ref, ya_ref, w_ref, g_ref, rw_ref, rb_ref,
                     xm_ref, meta_t_ref, counts_ref, cnt_ref):
    mix = jnp.concatenate([yc_ref[...], yp_ref[...], ya_ref[...]], axis=1)
    x = _load_tokens(x_ref) + jnp.dot(mix, w_ref[...], preferred_element_type=F32)
    tm = x.shape[0]
    _store_tokens(xm_ref, x, XM_ROWS)

    hf_hi, hf_lo = _split_bf16(_rms(x, g_ref[...]))

    both = jnp.dot(hf_hi, rw_ref[...], preferred_element_type=F32)
    logits = (both[:, 0:LANES] + both[:, LANES:]
              + jnp.dot(hf_lo, rw_ref[:, 0:LANES], preferred_element_type=F32)) + rb_ref[...]
    lane = lax.broadcasted_iota(jnp.int32, (tm, LANES), 1)
    lane_f = lane.astype(F32)
    neg = -jnp.inf

    def first_argmax(val, vmax):
        first = jnp.min(jnp.where(val == vmax, lane_f, float(LANES)), axis=-1, keepdims=True)
        return first.astype(jnp.int32)

    lg = jnp.where(lane < N_GROUPS, logits, neg)
    g_max = jnp.max(lg, axis=-1, keepdims=True)
    g_idx = first_argmax(lg, g_max)
    g_p = 1.0 / jnp.sum(jnp.exp(lg - g_max), axis=-1, keepdims=True)

    e_lane = lane - N_GROUPS
    in_group = (e_lane >= g_idx * PER_GROUP) & (e_lane < (g_idx + 1) * PER_GROUP)
    le = jnp.where(in_group, logits, neg)
    e_max = jnp.max(le, axis=-1, keepdims=True)
    pe = jnp.exp(le - e_max)
    pe = pe / jnp.sum(pe, axis=-1, keepdims=True)
    p1 = jnp.max(pe, axis=-1, keepdims=True)
    i1 = first_argmax(jnp.where(in_group, pe, neg), p1)
    pe2 = jnp.where(in_group & (lane != i1), pe, neg)
    p2 = jnp.max(pe2, axis=-1, keepdims=True)
    i2 = first_argmax(pe2, p2)
    denom = p1 + p2
    gate1 = g_p * (p1 / denom)
    gate2 = g_p * (p2 / denom)

    loc1 = i1 - N_GROUPS - g_idx * PER_GROUP
    loc2 = i2 - N_GROUPS - g_idx * PER_GROUP
    a = jnp.minimum(loc1, loc2)
    b = jnp.maximum(loc1, loc2)
    pair = jnp.where(a == 0, 0, jnp.where(a == 1, 3, 5)) + (b - a - 1)
    bucket = g_idx * N_PAIRS + pair
    gate_a = jnp.where(loc1 < loc2, gate1, gate2)
    gate_b = jnp.where(loc1 < loc2, gate2, gate1)

    @pl.when(pl.program_id(0) == 0)
    def _():
        cnt_ref[...] = jnp.zeros(cnt_ref.shape, F32)

    in_bucket = lane == bucket
    earlier = (lax.broadcasted_iota(jnp.int32, (tm, tm), 1)
               < lax.broadcasted_iota(jnp.int32, (tm, tm), 0)).astype(BF16)
    before = jnp.dot(earlier, in_bucket.astype(BF16), preferred_element_type=F32) + cnt_ref[...]
    rank = jnp.sum(jnp.where(in_bucket, before, 0.0), axis=-1, keepdims=True)
    cnt_ref[...] += jnp.sum(in_bucket.astype(F32), axis=0, keepdims=True)
    counts_ref[...] = cnt_ref[...]

    meta = jnp.where(lane == META_GATE_A, gate_a,
                     jnp.where(lane == META_GATE_B, gate_b,
                               jnp.where(lane == META_BUCKET, bucket.astype(F32),
                                         jnp.where(lane == META_RANK, rank, 0.0))))
    xm_ref[pl.ds(X_ROWS, tm, stride=XM_ROWS), :] = meta
    meta_t_ref[...] = meta.T[0:META_ROWS, :]


def _out_proj(x2, yc, yp, ya, w_bf, g, rw_hi_lo, rb):
    token_major = x2.shape[1] == LANES
    t = x2.shape[0] // X_ROWS if token_major else x2.shape[0]
    tm = TM_OUT
    row = lambda i: (i, 0)
    fixed = lambda i: (0, 0)
    x_spec = pl.BlockSpec((tm * X_ROWS, LANES) if token_major else (tm, D_MODEL), row)
    return pl.pallas_call(
        _out_proj_kernel,
        grid=(t // tm,),
        in_specs=[x_spec,
                  pl.BlockSpec((tm, C_CONV), row),
                  pl.BlockSpec((tm, C_POOL), row),
                  pl.BlockSpec((tm, C_ATTN), row),
                  pl.BlockSpec((D_MODEL, D_MODEL), fixed),
                  pl.BlockSpec((1, D_MODEL), fixed),
                  pl.BlockSpec((D_MODEL, 2 * LANES), fixed),
                  pl.BlockSpec((1, LANES), fixed)],
        out_specs=[pl.BlockSpec((tm * XM_ROWS, LANES), row),
                   pl.BlockSpec((META_ROWS, tm), row),
                   pl.BlockSpec((1, LANES), fixed)],
        out_shape=[jax.ShapeDtypeStruct((t * XM_ROWS, LANES), F32),
                   jax.ShapeDtypeStruct((t // tm * META_ROWS, tm), F32),
                   jax.ShapeDtypeStruct((1, LANES), F32)],
        scratch_shapes=[pltpu.VMEM((1, LANES), F32)],
        compiler_params=_params(("arbitrary",)),
        name="out_proj_router",
    )(x2, yc, yp, ya, w_bf, g, rw_hi_lo, rb)


def _for_chunk_rows(dest_ref, chunk, start_copy, wait_copy):
    base = pl.program_id(0) * chunk

    def body(blk, carry):
        rows = [blk * DMA_UNROLL + u for u in range(DMA_UNROLL)]
        slots = [dest_ref[base + r] for r in rows]
        for u, (r, slot) in enumerate(zip(rows, slots)):
            start_copy(r, slot, u % 2)
        return carry

    lax.fori_loop(0, chunk // DMA_UNROLL, body, 0)
    for piece in range(chunk // COPY_WAIT):
        wait_copy(piece)


def _dispatch_kernel(dest_ref, x_ref, out_hbm, sem):
    def start_copy(r, slot, priority):
        pltpu.make_async_copy(x_ref.at[pl.ds(r * XM_ROWS, XM_ROWS)],
                              out_hbm.at[pl.ds(slot * XM_ROWS, XM_ROWS)],
                              sem.at[0]).start(priority=priority)

    def wait_copy(piece):
        n = COPY_WAIT * XM_ROWS
        pltpu.make_async_copy(x_ref.at[pl.ds(piece * n, n)], out_hbm.at[pl.ds(0, n)],
                              sem.at[0]).wait()

    _for_chunk_rows(dest_ref, DISPATCH_CHUNK, start_copy, wait_copy)


def _dispatch(xm, dest):
    n_tokens = xm.shape[0] // XM_ROWS
    n_rows = dest.shape[0]
    chunk = DISPATCH_CHUNK
    assert n_tokens % chunk == 0 and n_rows % chunk == 0
    last = n_tokens // chunk - 1
    grid_spec = pltpu.PrefetchScalarGridSpec(
        num_scalar_prefetch=1,
        grid=(n_rows // chunk,),
        in_specs=[pl.BlockSpec((chunk * XM_ROWS, LANES),
                               lambda i, dest: (jnp.minimum(i, last), 0))],
        out_specs=pl.BlockSpec(memory_space=pl.ANY),
        scratch_shapes=[pltpu.SemaphoreType.DMA((1,))])
    return pl.pallas_call(
        _dispatch_kernel,
        grid_spec=grid_spec,
        out_shape=jax.ShapeDtypeStruct((n_rows * XM_ROWS, LANES), F32),
        compiler_params=_params(("arbitrary",)),
        name="moe_dispatch",
    )(dest, xm)


def _combine_kernel(dest_ref, ys_hbm, o_ref, *scratch, to_rows):
    buf = scratch[0] if to_rows else o_ref
    sem = scratch[-1]

    def start_copy(r, slot, priority):
        pltpu.make_async_copy(ys_hbm.at[pl.ds(slot * X_ROWS, X_ROWS)],
                              buf.at[pl.ds(r * X_ROWS, X_ROWS)],
                              sem.at[0]).start(priority=priority)

    def wait_copy(piece):
        n = COPY_WAIT * X_ROWS
        pltpu.make_async_copy(ys_hbm.at[pl.ds(0, n)], buf.at[pl.ds(piece * n, n)],
                              sem.at[0]).wait()

    _for_chunk_rows(dest_ref, COPY_CHUNK, start_copy, wait_copy)
    if to_rows:
        o_ref[...] = _load_tokens(buf)


def _combine(ys, dest, n_tokens, to_rows):
    assert n_tokens % COPY_CHUNK == 0
    chunk_rows = COPY_CHUNK * X_ROWS
    if to_rows:
        out_spec = pl.BlockSpec((COPY_CHUNK, D_MODEL), lambda i, dest: (i, 0))
        out_shape = jax.ShapeDtypeStruct((n_tokens, D_MODEL), F32)
        scratch = [pltpu.VMEM((chunk_rows, LANES), F32)]
    else:
        out_spec = pl.BlockSpec((chunk_rows, LANES), lambda i, dest: (i, 0))
        out_shape = jax.ShapeDtypeStruct((n_tokens * X_ROWS, LANES), F32)
        scratch = []
    grid_spec = pltpu.PrefetchScalarGridSpec(
        num_scalar_prefetch=1,
        grid=(n_tokens // COPY_CHUNK,),
        in_specs=[pl.BlockSpec(memory_space=pl.ANY)],
        out_specs=out_spec,
        scratch_shapes=scratch + [pltpu.SemaphoreType.DMA((1,))])
    return pl.pallas_call(
        functools.partial(_combine_kernel, to_rows=to_rows),
        grid_spec=grid_spec,
        out_shape=out_shape,
        compiler_params=_params(("arbitrary",)),
        name="moe_combine",
    )(dest, ys)


def _moe_kernel(te1_ref, te2_ref, nv_ref, used_ref,
                xs_hbm, g_ref, wg1_ref, wu1_ref, wd1_ref, wg2_ref, wu2_ref, wd2_ref, ys_ref,
                xbuf, sem):
    i = pl.program_id(0)
    n_used = used_ref[0]
    tile_rows = TM_MOE * XM_ROWS

    def tile_copy(tile):
        slot = tile % MOE_RING
        return pltpu.make_async_copy(
            xs_hbm.at[pl.ds(pl.multiple_of(tile * tile_rows, tile_rows), tile_rows)],
            xbuf.at[slot], sem.at[slot])

    @pl.when(i == 0)
    def _():
        tile_copy(0).start()

        @pl.when(n_used > 1)
        def _():
            tile_copy(1).start()

    @pl.when(i + 2 < n_used)
    def _():
        tile_copy(i + 2).start(priority=1)

    @pl.when(nv_ref[i] > 0)
    def _():
        tile_copy(i).wait()
        xs_ref = xbuf.at[i % MOE_RING]
        x = _load_tokens(xs_ref, XM_ROWS)
        meta = xs_ref[pl.ds(X_ROWS, TM_MOE, stride=XM_ROWS), :]
        hf = _rms(x, g_ref[...]).astype(BF16)

        def expert(wg_ref, wu_ref, wd_ref, gate):
            a = jnp.dot(hf, wg_ref[0], preferred_element_type=F32)
            u = jnp.dot(hf, wu_ref[0], preferred_element_type=F32)
            act = a * jax.nn.sigmoid(a) * u * gate
            return jnp.dot(act.astype(BF16), wd_ref[0], preferred_element_type=F32)

        y = expert(wg1_ref, wu1_ref, wd1_ref, meta[:, META_GATE_A:META_GATE_A + 1])
        y = y + expert(wg2_ref, wu2_ref, wd2_ref, meta[:, META_GATE_B:META_GATE_B + 1])
        _store_tokens(ys_ref, x + y)

    @pl.when(nv_ref[i] == 0)
    def _():
        ys_ref[...] = jnp.zeros(ys_ref.shape, F32)


def _moe(xs, g, te1, te2, nv, used, wg_bf, wu_bf, wd_bf):
    tm = TM_MOE
    n_tiles = nv.shape[0]
    fixed = lambda i, te1, te2, nv, used: (0, 0)
    tile_out = lambda i, te1, te2, nv, used: (i, 0)
    expert1 = lambda i, te1, te2, nv, used: (te1[i], 0, 0)
    expert2 = lambda i, te1, te2, nv, used: (te2[i], 0, 0)
    w_in = lambda index_map: pl.BlockSpec((1, D_MODEL, D_EXPERT), index_map)
    w_out = lambda index_map: pl.BlockSpec((1, D_EXPERT, D_MODEL), index_map)
    grid_spec = pltpu.PrefetchScalarGridSpec(
        num_scalar_prefetch=4,
        grid=(n_tiles,),
        in_specs=[pl.BlockSpec(memory_space=pl.ANY),
                  pl.BlockSpec((1, D_MODEL), fixed),
                  w_in(expert1), w_in(expert1), w_out(expert1),
                  w_in(expert2), w_in(expert2), w_out(expert2)],
        out_specs=pl.BlockSpec((tm * X_ROWS, LANES), tile_out),
        scratch_shapes=[pltpu.VMEM((MOE_RING, tm * XM_ROWS, LANES), F32),
                        pltpu.SemaphoreType.DMA((MOE_RING,))])
    return pl.pallas_call(
        _moe_kernel,
        grid_spec=grid_spec,
        out_shape=jax.ShapeDtypeStruct((n_tiles * tm * X_ROWS, LANES), F32),
        compiler_params=_params(("arbitrary",)),
        name="moe",
    )(te1, te2, nv, used, xs, g, wg_bf, wu_bf, wd_bf, wg_bf, wu_bf, wd_bf)


def _route(meta_t, counts, t):
    tm = TM_MOE
    nb = N_GROUPS * N_PAIRS
    n_tiles = t // tm + nb
    meta = meta_t.reshape(t // TM_OUT, META_ROWS, TM_OUT)
    bucket = meta[:, META_BUCKET, :].reshape(t).astype(jnp.int32)
    rank = meta[:, META_RANK, :].reshape(t).astype(jnp.int32)
    counts = counts[0, :nb].astype(jnp.int32)
    ids = jnp.arange(nb, dtype=jnp.int32)

    tiles = (counts + tm - 1) // tm
    tile_end = jnp.cumsum(tiles)
    tile_start = tile_end - tiles
    used = tile_end[-1]
    row0 = tile_start * tm
    dest = rank + jnp.sum(jnp.where(bucket[:, None] == ids[None, :], row0[None, :], 0), axis=1)

    ti = jnp.arange(n_tiles, dtype=jnp.int32)
    tic = jnp.minimum(ti, used - 1)
    tb = jnp.sum((tile_end[None, :] <= tic[:, None]).astype(jnp.int32), axis=1)
    onehot_tb = tb[:, None] == ids[None, :]
    pick = lambda table: jnp.sum(jnp.where(onehot_tb, table[None, :], 0), axis=1)
    nv = jnp.where(ti < used, jnp.clip(pick(counts) - (tic - pick(tile_start)) * tm, 0, tm), 0)
    pair_a = jnp.asarray([0, 0, 0, 1, 1, 2], jnp.int32)
    pair_b = jnp.asarray([1, 2, 3, 2, 3, 3], jnp.int32)
    te1 = pick((ids // N_PAIRS) * PER_GROUP + pair_a[ids % N_PAIRS])
    te2 = pick((ids // N_PAIRS) * PER_GROUP + pair_b[ids % N_PAIRS])

    pads = tiles * tm - counts
    pad_end = jnp.cumsum(pads)
    k = jnp.arange(nb * tm, dtype=jnp.int32)
    kb = jnp.minimum(jnp.sum((pad_end[None, :] <= k[:, None]).astype(jnp.int32), axis=1), nb - 1)
    onehot_kb = kb[:, None] == ids[None, :]
    pick_k = lambda table: jnp.sum(jnp.where(onehot_kb, table[None, :], 0), axis=1)
    in_pad = pick_k(row0 + counts) + (k - pick_k(pad_end - pads))
    free_rows = jnp.where(k < pad_end[-1], in_pad, used * tm + (k - pad_end[-1]))
    i32 = lambda a: a.astype(jnp.int32)
    return (i32(jnp.concatenate([dest, free_rows])), i32(te1), i32(te2), i32(nv),
            i32(used).reshape(1))


def _block_diag(w):
    n = w.shape[0]
    eye = jnp.eye(n, dtype=w.dtype)
    return jnp.einsum("gcd,gh->gchd", w, eye).reshape(n * POOL_GROUP, n * POOL_GROUP)


def kernel(x, attn_norm_g, w_in, conv_w, conv_b, conv_ln_g, conv_ln_b, conv_pw_w, conv_pw_b,
           pool_w, pool_scale, q_norm_g, k_norm_g, lambda_q1, lambda_k1, lambda_q2, lambda_k2,
           attn_sub_norm_g, w_out, ffn_norm_g, router_g_w, router_g_b, router_e_w, router_e_b,
           w_gate, w_up, w_down):
    bsz, s, d = x.shape
    depth = w_in.shape[0]
    t = bsz * s
    qaug, kaug = _alibi_aug(s)
    row = lambda a: a.reshape(1, -1)
    all_experts = lambda w: w.astype(BF16).reshape((depth * N_EXPERTS,) + w.shape[2:])
    wg_all, wu_all, wd_all = all_experts(w_gate), all_experts(w_up), all_experts(w_down)

    x2 = x.reshape(t, d)
    for l in range(depth):
        lam_init = 0.8 - 0.6 * math.exp(-0.3 * l)
        y_conv, y_pool, q, k, v = _in_proj(
            x2, s, row(attn_norm_g[l]), w_in[l].astype(BF16),
            row(jnp.tile(q_norm_g[l], 2) * (HEAD_DIM ** 0.5 * HEAD_DIM ** -0.5 * math.log2(math.e))),
            row(jnp.tile(k_norm_g[l], 2) * HEAD_DIM ** 0.5),
            conv_w[l], row(conv_b[l]), row(conv_ln_g[l]), row(conv_ln_b[l]),
            conv_pw_w[l].astype(BF16), row(conv_pw_b[l]),
            _block_diag(pool_w[l]).astype(BF16), row(pool_scale[l]))
        y_attn = _attention(q.reshape(bsz, s, -1), k.reshape(bsz, s, -1), v.reshape(bsz, s, -1),
                            qaug, kaug, row(lambda_q1[l]), row(lambda_k1[l]), row(lambda_q2[l]),
                            row(lambda_k2[l]), row(attn_sub_norm_g[l]), lam_init)

        pad = LANES - N_GROUPS - N_EXPERTS
        rw = jnp.pad(jnp.concatenate([router_g_w[l], router_e_w[l]], axis=1), ((0, 0), (0, pad)))
        rb = jnp.pad(jnp.concatenate([router_g_b[l], router_e_b[l]]), (0, pad))
        rw_hi = rw.astype(BF16)
        rw_lo = (rw - rw_hi.astype(F32)).astype(BF16)
        rw_hi_lo = jnp.concatenate([rw_hi, rw_lo], axis=1)
        xm, meta_t, counts = _out_proj(
            x2, y_conv, y_pool, y_attn.reshape(t, -1),
            w_out[l].astype(BF16), row(ffn_norm_g[l]), rw_hi_lo, row(rb))

        dest, te1, te2, nv, used = _route(meta_t, counts, t)
        xs = _dispatch(xm, dest)
        ys = _moe(xs, row(ffn_norm_g[l]), te1 + l * N_EXPERTS, te2 + l * N_EXPERTS, nv, used,
                  wg_all, wu_all, wd_all)
        x2 = _combine(ys, dest, t, to_rows=l == depth - 1)
    return x2.reshape(bsz, s, d)
```
